```python
import math
import jax, jax.numpy as jnp
from jax import lax
import numpy as np

D_MODEL = 2048
BATCH = 8
SEQ = 4096
DEPTH = 4

CHUNK = 64
N_MIXERS = 2
N_A = (DEPTH + 1) // 2
N_B = DEPTH // 2

GM_BLOCK = 128
GM_HALF = 3 * D_MODEL
GM_GROUPS = 8
GM_GROUP_DIM = GM_HALF // GM_GROUPS

MLA_HEADS = 16
Q_RANK = 512
KV_RANK = 512
NOPE_DIM = 128
ROPE_DIM = 64
V_DIM = 128
ROPE_THETA = 10000.0
Q_BLOCK = 128
SM_SCALE = (NOPE_DIM + ROPE_DIM) ** -0.5

D_FF = 5504
CONV_W = 3

ALPHA = (2 * DEPTH) ** 0.25
BETA = (8 * DEPTH) ** -0.25
LN_EPS = 1e-5
RMS_EPS = 1e-6

kernel_name = "interleaved_gmlp_mla_convffn_deepnorm"


def layer_norm(x, g, b):
    xf = x.astype(jnp.float32)
    mu = jnp.mean(xf, axis=-1, keepdims=True)
    var = jnp.mean(jnp.square(xf - mu), axis=-1, keepdims=True)
    y = (xf - mu) * lax.rsqrt(var + LN_EPS)
    return (y * g.astype(jnp.float32) + b.astype(jnp.float32)).astype(x.dtype)


def rms_norm(x, g):
    xf = x.astype(jnp.float32)
    y = xf * lax.rsqrt(jnp.mean(jnp.square(xf), axis=-1, keepdims=True) + RMS_EPS)
    return (y * g.astype(jnp.float32)).astype(x.dtype)


def rope_tables(seq_len):
    half = ROPE_DIM // 2
    inv_freq = ROPE_THETA ** (-jnp.arange(half, dtype=jnp.float32) / half)
    pos = jnp.arange(seq_len, dtype=jnp.float32)
    ang = pos[:, None] * inv_freq[None, :]
    return jnp.cos(ang), jnp.sin(ang)


def apply_rope(x, cos, sin):
    x1, x2 = jnp.split(x, 2, axis=-1)
    cos = cos.astype(x.dtype)
    sin = sin.astype(x.dtype)
    return jnp.concatenate([x1 * cos - x2 * sin, x2 * cos + x1 * sin], axis=-1)


def gmlp_mixer(x, w_in, ln_g, ln_b, w_s, b_s, w_out):
    B, S, _ = x.shape
    z = jax.nn.gelu(x @ w_in)
    u, v = jnp.split(z, 2, axis=-1)
    v = layer_norm(v, ln_g, ln_b)
    v = v.reshape(B, S // GM_BLOCK, GM_BLOCK, GM_GROUPS, GM_GROUP_DIM)
    idx = jnp.arange(GM_BLOCK) // CHUNK
    mask = (idx[None, :] <= idx[:, None]).astype(w_s.dtype)
    w = w_s * mask[None]
    s = jnp.einsum('gij,bnjgd->bnigd', w, v) + jnp.transpose(b_s)[None, None, :, :, None]
    return (u * s.reshape(B, S, GM_HALF)) @ w_out


def chunk_causal_attention(q_nope, q_rope, k_nope, k_rope, v):
    S = q_nope.shape[1]
    outs = []
    for qb in range(S // Q_BLOCK):
        q0 = qb * Q_BLOCK
        k_end = q0 + Q_BLOCK
        s = (jnp.einsum('bqhd,bkhd->bhqk', q_nope[:, q0:k_end], k_nope[:, :k_end])
             + jnp.einsum('bqhr,bkr->bhqk', q_rope[:, q0:k_end], k_rope[:, :k_end]))
        s = s.astype(jnp.float32) * SM_SCALE
        qpos = jnp.arange(q0, k_end)
        kpos = jnp.arange(k_end)
        allowed = (kpos[None, :] // CHUNK) <= (qpos[:, None] // CHUNK)
        s = jnp.where(allowed[None, None], s, -jnp.inf)
        p = jax.nn.softmax(s, axis=-1).astype(v.dtype)
        outs.append(jnp.einsum('bhqk,bkhd->bqhd', p, v[:, :k_end]))
    return jnp.concatenate(outs, axis=1)


def mla_mixer(x, w_in, q_norm_g, kv_norm_g, w_q_b, w_kv_b, w_out, cos, sin):
    B, S, _ = x.shape
    h = x @ w_in
    c_q, c_kv, k_rope = jnp.split(h, [Q_RANK, Q_RANK + KV_RANK], axis=-1)
    q = (rms_norm(c_q, q_norm_g) @ w_q_b).reshape(B, S, MLA_HEADS, NOPE_DIM + ROPE_DIM)
    q_nope, q_rope = jnp.split(q, [NOPE_DIM], axis=-1)
    q_rope = apply_rope(q_rope, cos[:, None, :], sin[:, None, :])
    k_rope = apply_rope(k_rope, cos, sin)
    kv = (rms_norm(c_kv, kv_norm_g) @ w_kv_b).reshape(B, S, MLA_HEADS, NOPE_DIM + V_DIM)
    k_nope, v = jnp.split(kv, [NOPE_DIM], axis=-1)
    o = chunk_causal_attention(q_nope, q_rope, k_nope, k_rope, v)
    return o.reshape(B, S, MLA_HEADS * V_DIM) @ w_out


def conv_ffn(x, w_up, conv_w, conv_b, w_down):
    h = x @ w_up
    C = h.shape[-1]
    h = lax.conv_general_dilated(h, conv_w[:, None, :], window_strides=(1,),
                                 padding=[(CONV_W - 1, 0)],
                                 dimension_numbers=('NWC', 'WIO', 'NWC'),
                                 feature_group_count=C) + conv_b
    a, g = jnp.split(h, 2, axis=-1)
    return (jax.nn.silu(g) * a) @ w_down


def _fwd_setup_inputs(seed: int = 0) -> dict:
    key = jax.random.key(seed)
    ks = jax.random.split(key, 24)

    def nrm(k, shape, scale):
        return jax.random.normal(k, shape, jnp.float32) * scale

    x = nrm(ks[0], (BATCH, SEQ, D_MODEL), 1.0)
    gm_w_in = nrm(ks[1], (N_A, D_MODEL, 2 * GM_HALF), D_MODEL ** -0.5)
    gm_ln_g = 1.0 + nrm(ks[2], (N_A, GM_HALF), 0.02)
    gm_ln_b = nrm(ks[3], (N_A, GM_HALF), 0.02)
    gm_w_s = nrm(ks[4], (N_A, GM_GROUPS, GM_BLOCK, GM_BLOCK), GM_BLOCK ** -0.5)
    gm_b_s = 1.0 + nrm(ks[5], (N_A, GM_GROUPS, GM_BLOCK), 0.1)
    gm_w_out = nrm(ks[6], (N_A, GM_HALF, D_MODEL), GM_HALF ** -0.5 * BETA)
    mla_w_in = nrm(ks[7], (N_B, D_MODEL, Q_RANK + KV_RANK + ROPE_DIM), D_MODEL ** -0.5)
    mla_q_norm_g = 1.0 + nrm(ks[8], (N_B, Q_RANK), 0.02)
    mla_kv_norm_g = 1.0 + nrm(ks[9], (N_B, KV_RANK), 0.02)
    mla_w_q_b = nrm(ks[10], (N_B, Q_RANK, MLA_HEADS * (NOPE_DIM + ROPE_DIM)), Q_RANK ** -0.5)
    mla_w_kv_b = nrm(ks[11], (N_B, KV_RANK, MLA_HEADS * (NOPE_DIM + V_DIM)), KV_RANK ** -0.5)
    mla_w_out = nrm(ks[12], (N_B, MLA_HEADS * V_DIM, D_MODEL), (MLA_HEADS * V_DIM) ** -0.5 * BETA)
    ffn_w_up = nrm(ks[13], (DEPTH, D_MODEL, 2 * D_FF), D_MODEL ** -0.5)
    ffn_conv_w = nrm(ks[14], (DEPTH, CONV_W, 2 * D_FF), CONV_W ** -0.5)
    ffn_conv_b = nrm(ks[15], (DEPTH, 2 * D_FF), 0.02)
    ffn_w_down = nrm(ks[16], (DEPTH, D_FF, D_MODEL), D_FF ** -0.5 * BETA)
    ln_mix_g = 1.0 + nrm(ks[17], (DEPTH, D_MODEL), 0.02)
    ln_mix_b = nrm(ks[18], (DEPTH, D_MODEL), 0.02)
    ln_ffn_g = 1.0 + nrm(ks[19], (DEPTH, D_MODEL), 0.02)
    ln_ffn_b = nrm(ks[20], (DEPTH, D_MODEL), 0.02)
    return {
        "x": x,
        "gm_w_in": gm_w_in, "gm_ln_g": gm_ln_g, "gm_ln_b": gm_ln_b,
        "gm_w_s": gm_w_s, "gm_b_s": gm_b_s, "gm_w_out": gm_w_out,
        "mla_w_in": mla_w_in, "mla_q_norm_g": mla_q_norm_g, "mla_kv_norm_g": mla_kv_norm_g,
        "mla_w_q_b": mla_w_q_b, "mla_w_kv_b": mla_w_kv_b, "mla_w_out": mla_w_out,
        "ffn_w_up": ffn_w_up, "ffn_conv_w": ffn_conv_w, "ffn_conv_b": ffn_conv_b,
        "ffn_w_down": ffn_w_down,
        "ln_mix_g": ln_mix_g, "ln_mix_b": ln_mix_b, "ln_ffn_g": ln_ffn_g, "ln_ffn_b": ln_ffn_b,
    }


def _fwd_reference(x, gm_w_in, gm_ln_g, gm_ln_b, gm_w_s, gm_b_s, gm_w_out,
              mla_w_in, mla_q_norm_g, mla_kv_norm_g, mla_w_q_b, mla_w_kv_b, mla_w_out,
              ffn_w_up, ffn_conv_w, ffn_conv_b, ffn_w_down,
              ln_mix_g, ln_mix_b, ln_ffn_g, ln_ffn_b):
    cos, sin = rope_tables(x.shape[1])
    for i in range(DEPTH):
        slot = i // N_MIXERS
        if i % N_MIXERS == 0:
            m = gmlp_mixer(x, gm_w_in[slot], gm_ln_g[slot], gm_ln_b[slot],
                           gm_w_s[slot], gm_b_s[slot], gm_w_out[slot])
        else:
            m = mla_mixer(x, mla_w_in[slot], mla_q_norm_g[slot], mla_kv_norm_g[slot],
                          mla_w_q_b[slot], mla_w_kv_b[slot], mla_w_out[slot], cos, sin)
        x = layer_norm(ALPHA * x + m, ln_mix_g[i], ln_mix_b[i])
        f = conv_ffn(x, ffn_w_up[i], ffn_conv_w[i], ffn_conv_b[i], ffn_w_down[i])
        x = layer_norm(ALPHA * x + f, ln_ffn_g[i], ln_ffn_b[i])
    return x


import jax as _jax
import jax.numpy as _jnp

TWIN_FORMAT = 'train_step'
FWD_PARAMS = ['x', 'gm_w_in', 'gm_ln_g', 'gm_ln_b', 'gm_w_s', 'gm_b_s', 'gm_w_out', 'mla_w_in', 'mla_q_norm_g', 'mla_kv_norm_g', 'mla_w_q_b', 'mla_w_kv_b', 'mla_w_out', 'ffn_w_up', 'ffn_conv_w', 'ffn_conv_b', 'ffn_w_down', 'ln_mix_g', 'ln_mix_b', 'ln_ffn_g', 'ln_ffn_b']
TWIN_WEIGHTS = ['gm_w_in', 'gm_ln_g', 'gm_ln_b', 'gm_w_s', 'gm_b_s', 'gm_w_out', 'mla_w_in', 'mla_q_norm_g', 'mla_kv_norm_g', 'mla_w_q_b', 'mla_w_kv_b', 'mla_w_out', 'ffn_w_up', 'ffn_conv_w', 'ffn_conv_b', 'ffn_w_down', 'ln_mix_g', 'ln_mix_b', 'ln_ffn_g', 'ln_ffn_b']
TWIN_DIFF_INPUT = 'x'
TWIN_INPUTS = ['x', 'gm_w_in', 'gm_ln_g', 'gm_ln_b', 'gm_w_s', 'gm_b_s', 'gm_w_out', 'mla_w_in', 'mla_q_norm_g', 'mla_kv_norm_g', 'mla_w_q_b', 'mla_w_kv_b', 'mla_w_out', 'ffn_w_up', 'ffn_conv_w', 'ffn_conv_b', 'ffn_w_down', 'ln_mix_g', 'ln_mix_b', 'ln_ffn_g', 'ln_ffn_b', 'loss_target', 'm_gm_w_in', 'm_gm_ln_g', 'm_gm_ln_b', 'm_gm_w_s', 'm_gm_b_s', 'm_gm_w_out', 'm_mla_w_in', 'm_mla_q_norm_g', 'm_mla_kv_norm_g', 'm_mla_w_q_b', 'm_mla_w_kv_b', 'm_mla_w_out', 'm_ffn_w_up', 'm_ffn_conv_w', 'm_ffn_conv_b', 'm_ffn_w_down', 'm_ln_mix_g', 'm_ln_mix_b', 'm_ln_ffn_g', 'm_ln_ffn_b', 'v_gm_w_in', 'v_gm_ln_g', 'v_gm_ln_b', 'v_gm_w_s', 'v_gm_b_s', 'v_gm_w_out', 'v_mla_w_in', 'v_mla_q_norm_g', 'v_mla_kv_norm_g', 'v_mla_w_q_b', 'v_mla_w_kv_b', 'v_mla_w_out', 'v_ffn_w_up', 'v_ffn_conv_w', 'v_ffn_conv_b', 'v_ffn_w_down', 'v_ln_mix_g', 'v_ln_mix_b', 'v_ln_ffn_g', 'v_ln_ffn_b']
TWIN_OUTPUTS = ['loss', 'grad_x', 'grad_gm_w_in', 'grad_gm_ln_g', 'grad_gm_ln_b', 'grad_gm_w_s', 'grad_gm_b_s', 'grad_gm_w_out', 'grad_mla_w_in', 'grad_mla_q_norm_g', 'grad_mla_kv_norm_g', 'grad_mla_w_q_b', 'grad_mla_w_kv_b', 'grad_mla_w_out', 'grad_ffn_w_up', 'grad_ffn_conv_w', 'grad_ffn_conv_b', 'grad_ffn_w_down', 'grad_ln_mix_g', 'grad_ln_mix_b', 'grad_ln_ffn_g', 'grad_ln_ffn_b', 'delta_gm_w_in', 'delta_gm_ln_g', 'delta_gm_ln_b', 'delta_gm_w_s', 'delta_gm_b_s', 'delta_gm_w_out', 'delta_mla_w_in', 'delta_mla_q_norm_g', 'delta_mla_kv_norm_g', 'delta_mla_w_q_b', 'delta_mla_w_kv_b', 'delta_mla_w_out', 'delta_ffn_w_up', 'delta_ffn_conv_w', 'delta_ffn_conv_b', 'delta_ffn_w_down', 'delta_ln_mix_g', 'delta_ln_mix_b', 'delta_ln_ffn_g', 'delta_ln_ffn_b', 'new_m_gm_w_in', 'new_m_gm_ln_g', 'new_m_gm_ln_b', 'new_m_gm_w_s', 'new_m_gm_b_s', 'new_m_gm_w_out', 'new_m_mla_w_in', 'new_m_mla_q_norm_g', 'new_m_mla_kv_norm_g', 'new_m_mla_w_q_b', 'new_m_mla_w_kv_b', 'new_m_mla_w_out', 'new_m_ffn_w_up', 'new_m_ffn_conv_w', 'new_m_ffn_conv_b', 'new_m_ffn_w_down', 'new_m_ln_mix_g', 'new_m_ln_mix_b', 'new_m_ln_ffn_g', 'new_m_ln_ffn_b', 'new_v_gm_w_in', 'new_v_gm_ln_g', 'new_v_gm_ln_b', 'new_v_gm_w_s', 'new_v_gm_b_s', 'new_v_gm_w_out', 'new_v_mla_w_in', 'new_v_mla_q_norm_g', 'new_v_mla_kv_norm_g', 'new_v_mla_w_q_b', 'new_v_mla_w_kv_b', 'new_v_mla_w_out', 'new_v_ffn_w_up', 'new_v_ffn_conv_w', 'new_v_ffn_conv_b', 'new_v_ffn_w_down', 'new_v_ln_mix_g', 'new_v_ln_mix_b', 'new_v_ln_ffn_g', 'new_v_ln_ffn_b']
TWIN_LEAF_KINDS = {'loss': 'loss', 'grad_x': 'grad_x', 'grad_gm_w_in': 'grad_w', 'grad_gm_ln_g': 'grad_w', 'grad_gm_ln_b': 'grad_w', 'grad_gm_w_s': 'grad_w', 'grad_gm_b_s': 'grad_w', 'grad_gm_w_out': 'grad_w', 'grad_mla_w_in': 'grad_w', 'grad_mla_q_norm_g': 'grad_w', 'grad_mla_kv_norm_g': 'grad_w', 'grad_mla_w_q_b': 'grad_w', 'grad_mla_w_kv_b': 'grad_w', 'grad_mla_w_out': 'grad_w', 'grad_ffn_w_up': 'grad_w', 'grad_ffn_conv_w': 'grad_w', 'grad_ffn_conv_b': 'grad_w', 'grad_ffn_w_down': 'grad_w', 'grad_ln_mix_g': 'grad_w', 'grad_ln_mix_b': 'grad_w', 'grad_ln_ffn_g': 'grad_w', 'grad_ln_ffn_b': 'grad_w', 'delta_gm_w_in': 'delta_w', 'delta_gm_ln_g': 'delta_w', 'delta_gm_ln_b': 'delta_w', 'delta_gm_w_s': 'delta_w', 'delta_gm_b_s': 'delta_w', 'delta_gm_w_out': 'delta_w', 'delta_mla_w_in': 'delta_w', 'delta_mla_q_norm_g': 'delta_w', 'delta_mla_kv_norm_g': 'delta_w', 'delta_mla_w_q_b': 'delta_w', 'delta_mla_w_kv_b': 'delta_w', 'delta_mla_w_out': 'delta_w', 'delta_ffn_w_up': 'delta_w', 'delta_ffn_conv_w': 'delta_w', 'delta_ffn_conv_b': 'delta_w', 'delta_ffn_w_down': 'delta_w', 'delta_ln_mix_g': 'delta_w', 'delta_ln_mix_b': 'delta_w', 'delta_ln_ffn_g': 'delta_w', 'delta_ln_ffn_b': 'delta_w', 'new_m_gm_w_in': 'new_m', 'new_m_gm_ln_g': 'new_m', 'new_m_gm_ln_b': 'new_m', 'new_m_gm_w_s': 'new_m', 'new_m_gm_b_s': 'new_m', 'new_m_gm_w_out': 'new_m', 'new_m_mla_w_in': 'new_m', 'new_m_mla_q_norm_g': 'new_m', 'new_m_mla_kv_norm_g': 'new_m', 'new_m_mla_w_q_b': 'new_m', 'new_m_mla_w_kv_b': 'new_m', 'new_m_mla_w_out': 'new_m', 'new_m_ffn_w_up': 'new_m', 'new_m_ffn_conv_w': 'new_m', 'new_m_ffn_conv_b': 'new_m', 'new_m_ffn_w_down': 'new_m', 'new_m_ln_mix_g': 'new_m', 'new_m_ln_mix_b': 'new_m', 'new_m_ln_ffn_g': 'new_m', 'new_m_ln_ffn_b': 'new_m', 'new_v_gm_w_in': 'new_v', 'new_v_gm_ln_g': 'new_v', 'new_v_gm_ln_b': 'new_v', 'new_v_gm_w_s': 'new_v', 'new_v_gm_b_s': 'new_v', 'new_v_gm_w_out': 'new_v', 'new_v_mla_w_in': 'new_v', 'new_v_mla_q_norm_g': 'new_v', 'new_v_mla_kv_norm_g': 'new_v', 'new_v_mla_w_q_b': 'new_v', 'new_v_mla_w_kv_b': 'new_v', 'new_v_mla_w_out': 'new_v', 'new_v_ffn_w_up': 'new_v', 'new_v_ffn_conv_w': 'new_v', 'new_v_ffn_conv_b': 'new_v', 'new_v_ffn_w_down': 'new_v', 'new_v_ln_mix_g': 'new_v', 'new_v_ln_mix_b': 'new_v', 'new_v_ln_ffn_g': 'new_v', 'new_v_ln_ffn_b': 'new_v'}


def _forward(args):
    return _fwd_reference(*[args[k] for k in FWD_PARAMS])


def _output_shape():
    def fwd():
        inp = _fwd_setup_inputs(0)
        return _fwd_reference(*[inp[k] for k in FWD_PARAMS])
    out = _jax.eval_shape(fwd)
    return out.shape, out.dtype

N_MICROBATCH = 1
ADAM_LR = 0.001
ADAM_B1 = 0.9
ADAM_B2 = 0.999
ADAM_EPS = 1e-08
ADAM_WD = 0.01
ADAM_STEP = 10
PER_EXAMPLE_BATCH_AXIS = {'x': 0, 'loss_target': 0}
SHARED_INPUTS = []
_WEIGHT_DTYPES = {'gm_w_in': _jnp.float32, 'gm_ln_g': _jnp.float32, 'gm_ln_b': _jnp.float32, 'gm_w_s': _jnp.float32, 'gm_b_s': _jnp.float32, 'gm_w_out': _jnp.float32, 'mla_w_in': _jnp.float32, 'mla_q_norm_g': _jnp.float32, 'mla_kv_norm_g': _jnp.float32, 'mla_w_q_b': _jnp.float32, 'mla_w_kv_b': _jnp.float32, 'mla_w_out': _jnp.float32, 'ffn_w_up': _jnp.float32, 'ffn_conv_w': _jnp.float32, 'ffn_conv_b': _jnp.float32, 'ffn_w_down': _jnp.float32, 'ln_mix_g': _jnp.float32, 'ln_mix_b': _jnp.float32, 'ln_ffn_g': _jnp.float32, 'ln_ffn_b': _jnp.float32}
MOMENT_SCALE = {'gm_w_in': 1.017709e-02, 'gm_ln_g': 7.432186e-03, 'gm_ln_b': 7.370322e-03, 'gm_w_s': 1.798209e-02, 'gm_b_s': 2.099396e-02, 'gm_w_out': 6.075454e-02, 'mla_w_in': 8.714969e-03, 'mla_q_norm_g': 6.498495e-03, 'mla_kv_norm_g': 1.161643e-02, 'mla_w_q_b': 2.629744e-03, 'mla_w_kv_b': 3.755456e-03, 'mla_w_out': 1.268011e-02, 'ffn_w_up': 8.478076e-03, 'ffn_conv_w': 8.512072e-03, 'ffn_conv_b': 1.025628e-02, 'ffn_w_down': 3.257850e-02, 'ln_mix_g': 5.543769e-01, 'ln_mix_b': 2.988535e-01, 'ln_ffn_g': 8.043382e+00, 'ln_ffn_b': 9.305971e-01}


def _to_microbatches(a, axis):
    t = _jnp.moveaxis(a, axis, 0)
    t = t.reshape((N_MICROBATCH, t.shape[0] // N_MICROBATCH) + t.shape[1:])
    return _jnp.moveaxis(t, 1, axis + 1)


def setup_inputs(seed: int = 0) -> dict:
    inp = _fwd_setup_inputs(seed)
    key = _jax.random.fold_in(_jax.random.key(seed), 7919)
    shape, _ = _output_shape()
    out = dict(inp)
    out["loss_target"] = _jax.random.normal(_jax.random.fold_in(key, 0), shape, _jnp.float32)
    for i, name in enumerate(TWIN_WEIGHTS):
        w = inp[name].astype(_jnp.float32)
        if MOMENT_SCALE is None:
            s = _jnp.sqrt(_jnp.mean(_jnp.square(w)) + 1e-30)
        else:
            s = MOMENT_SCALE[name]
        km, kv = _jax.random.split(_jax.random.fold_in(key, i + 1))
        out[name] = w
        out["m_" + name] = s * _jax.random.normal(km, w.shape, _jnp.float32)
        out["v_" + name] = (s * s) * _jax.random.uniform(kv, w.shape, _jnp.float32, 0.5, 1.5)
    if N_MICROBATCH > 1:
        for name, axis in PER_EXAMPLE_BATCH_AXIS.items():
            out[name] = _to_microbatches(out[name], axis)
    return {'x': out['x'], 'gm_w_in': out['gm_w_in'], 'gm_ln_g': out['gm_ln_g'], 'gm_ln_b': out['gm_ln_b'], 'gm_w_s': out['gm_w_s'], 'gm_b_s': out['gm_b_s'], 'gm_w_out': out['gm_w_out'], 'mla_w_in': out['mla_w_in'], 'mla_q_norm_g': out['mla_q_norm_g'], 'mla_kv_norm_g': out['mla_kv_norm_g'], 'mla_w_q_b': out['mla_w_q_b'], 'mla_w_kv_b': out['mla_w_kv_b'], 'mla_w_out': out['mla_w_out'], 'ffn_w_up': out['ffn_w_up'], 'ffn_conv_w': out['ffn_conv_w'], 'ffn_conv_b': out['ffn_conv_b'], 'ffn_w_down': out['ffn_w_down'], 'ln_mix_g': out['ln_mix_g'], 'ln_mix_b': out['ln_mix_b'], 'ln_ffn_g': out['ln_ffn_g'], 'ln_ffn_b': out['ln_ffn_b'], 'loss_target': out['loss_target'], 'm_gm_w_in': out['m_gm_w_in'], 'm_gm_ln_g': out['m_gm_ln_g'], 'm_gm_ln_b': out['m_gm_ln_b'], 'm_gm_w_s': out['m_gm_w_s'], 'm_gm_b_s': out['m_gm_b_s'], 'm_gm_w_out': out['m_gm_w_out'], 'm_mla_w_in': out['m_mla_w_in'], 'm_mla_q_norm_g': out['m_mla_q_norm_g'], 'm_mla_kv_norm_g': out['m_mla_kv_norm_g'], 'm_mla_w_q_b': out['m_mla_w_q_b'], 'm_mla_w_kv_b': out['m_mla_w_kv_b'], 'm_mla_w_out': out['m_mla_w_out'], 'm_ffn_w_up': out['m_ffn_w_up'], 'm_ffn_conv_w': out['m_ffn_conv_w'], 'm_ffn_conv_b': out['m_ffn_conv_b'], 'm_ffn_w_down': out['m_ffn_w_down'], 'm_ln_mix_g': out['m_ln_mix_g'], 'm_ln_mix_b': out['m_ln_mix_b'], 'm_ln_ffn_g': out['m_ln_ffn_g'], 'm_ln_ffn_b': out['m_ln_ffn_b'], 'v_gm_w_in': out['v_gm_w_in'], 'v_gm_ln_g': out['v_gm_ln_g'], 'v_gm_ln_b': out['v_gm_ln_b'], 'v_gm_w_s': out['v_gm_w_s'], 'v_gm_b_s': out['v_gm_b_s'], 'v_gm_w_out': out['v_gm_w_out'], 'v_mla_w_in': out['v_mla_w_in'], 'v_mla_q_norm_g': out['v_mla_q_norm_g'], 'v_mla_kv_norm_g': out['v_mla_kv_norm_g'], 'v_mla_w_q_b': out['v_mla_w_q_b'], 'v_mla_w_kv_b': out['v_mla_w_kv_b'], 'v_mla_w_out': out['v_mla_w_out'], 'v_ffn_w_up': out['v_ffn_w_up'], 'v_ffn_conv_w': out['v_ffn_conv_w'], 'v_ffn_conv_b': out['v_ffn_conv_b'], 'v_ffn_w_down': out['v_ffn_w_down'], 'v_ln_mix_g': out['v_ln_mix_g'], 'v_ln_mix_b': out['v_ln_mix_b'], 'v_ln_ffn_g': out['v_ln_ffn_g'], 'v_ln_ffn_b': out['v_ln_ffn_b']}


def _loss(weights, diff, rest, loss_target):
    with _jax.named_scope("forward"):
        args = {**rest, TWIN_DIFF_INPUT: diff, **{k: w.astype(_WEIGHT_DTYPES[k]) for k, w in weights.items()}}
        y = _forward(args)
    with _jax.named_scope("loss_head"):
        err = _jnp.square(y.astype(_jnp.float32) - loss_target)
        return 0.5 * _jnp.sum(_jnp.mean(err, axis=-1)) if err.ndim else 0.5 * err


def _adamw(w, g, m, v):
    m = ADAM_B1 * m + (1.0 - ADAM_B1) * g
    v = ADAM_B2 * v + (1.0 - ADAM_B2) * _jnp.square(g)
    m_hat = m / (1.0 - ADAM_B1 ** ADAM_STEP)
    v_hat = v / (1.0 - ADAM_B2 ** ADAM_STEP)
    delta = -ADAM_LR * (m_hat / (_jnp.sqrt(v_hat) + ADAM_EPS) + ADAM_WD * w)
    return delta, m, v


def reference(x, gm_w_in, gm_ln_g, gm_ln_b, gm_w_s, gm_b_s, gm_w_out, mla_w_in, mla_q_norm_g, mla_kv_norm_g, mla_w_q_b, mla_w_kv_b, mla_w_out, ffn_w_up, ffn_conv_w, ffn_conv_b, ffn_w_down, ln_mix_g, ln_mix_b, ln_ffn_g, ln_ffn_b, loss_target, m_gm_w_in, m_gm_ln_g, m_gm_ln_b, m_gm_w_s, m_gm_b_s, m_gm_w_out, m_mla_w_in, m_mla_q_norm_g, m_mla_kv_norm_g, m_mla_w_q_b, m_mla_w_kv_b, m_mla_w_out, m_ffn_w_up, m_ffn_conv_w, m_ffn_conv_b, m_ffn_w_down, m_ln_mix_g, m_ln_mix_b, m_ln_ffn_g, m_ln_ffn_b, v_gm_w_in, v_gm_ln_g, v_gm_ln_b, v_gm_w_s, v_gm_b_s, v_gm_w_out, v_mla_w_in, v_mla_q_norm_g, v_mla_kv_norm_g, v_mla_w_q_b, v_mla_w_kv_b, v_mla_w_out, v_ffn_w_up, v_ffn_conv_w, v_ffn_conv_b, v_ffn_w_down, v_ln_mix_g, v_ln_mix_b, v_ln_ffn_g, v_ln_ffn_b):
    given = dict(x=x, gm_w_in=gm_w_in, gm_ln_g=gm_ln_g, gm_ln_b=gm_ln_b, gm_w_s=gm_w_s, gm_b_s=gm_b_s, gm_w_out=gm_w_out, mla_w_in=mla_w_in, mla_q_norm_g=mla_q_norm_g, mla_kv_norm_g=mla_kv_norm_g, mla_w_q_b=mla_w_q_b, mla_w_kv_b=mla_w_kv_b, mla_w_out=mla_w_out, ffn_w_up=ffn_w_up, ffn_conv_w=ffn_conv_w, ffn_conv_b=ffn_conv_b, ffn_w_down=ffn_w_down, ln_mix_g=ln_mix_g, ln_mix_b=ln_mix_b, ln_ffn_g=ln_ffn_g, ln_ffn_b=ln_ffn_b, loss_target=loss_target, m_gm_w_in=m_gm_w_in, m_gm_ln_g=m_gm_ln_g, m_gm_ln_b=m_gm_ln_b, m_gm_w_s=m_gm_w_s, m_gm_b_s=m_gm_b_s, m_gm_w_out=m_gm_w_out, m_mla_w_in=m_mla_w_in, m_mla_q_norm_g=m_mla_q_norm_g, m_mla_kv_norm_g=m_mla_kv_norm_g, m_mla_w_q_b=m_mla_w_q_b, m_mla_w_kv_b=m_mla_w_kv_b, m_mla_w_out=m_mla_w_out, m_ffn_w_up=m_ffn_w_up, m_ffn_conv_w=m_ffn_conv_w, m_ffn_conv_b=m_ffn_conv_b, m_ffn_w_down=m_ffn_w_down, m_ln_mix_g=m_ln_mix_g, m_ln_mix_b=m_ln_mix_b, m_ln_ffn_g=m_ln_ffn_g, m_ln_ffn_b=m_ln_ffn_b, v_gm_w_in=v_gm_w_in, v_gm_ln_g=v_gm_ln_g, v_gm_ln_b=v_gm_ln_b, v_gm_w_s=v_gm_w_s, v_gm_b_s=v_gm_b_s, v_gm_w_out=v_gm_w_out, v_mla_w_in=v_mla_w_in, v_mla_q_norm_g=v_mla_q_norm_g, v_mla_kv_norm_g=v_mla_kv_norm_g, v_mla_w_q_b=v_mla_w_q_b, v_mla_w_kv_b=v_mla_w_kv_b, v_mla_w_out=v_mla_w_out, v_ffn_w_up=v_ffn_w_up, v_ffn_conv_w=v_ffn_conv_w, v_ffn_conv_b=v_ffn_conv_b, v_ffn_w_down=v_ffn_w_down, v_ln_mix_g=v_ln_mix_g, v_ln_mix_b=v_ln_mix_b, v_ln_ffn_g=v_ln_ffn_g, v_ln_ffn_b=v_ln_ffn_b)
    weights = {n: given[n] for n in TWIN_WEIGHTS}
    shared = {n: given[n] for n in SHARED_INPUTS}
    per_example = {n: given[n] for n in ['x']}
    grad_fn = _jax.value_and_grad(_loss, argnums=(0, 1))

    def one_microbatch(ex, loss_target):
        ex = dict(ex)
        diff = ex.pop(TWIN_DIFF_INPUT)
        return grad_fn(weights, diff, {**shared, **ex}, loss_target)

    if N_MICROBATCH == 1:
        loss, (grad_w, grad_x) = one_microbatch(per_example, given["loss_target"])
    else:
        def body(carry, xs):
            loss_sum, grad_sum = carry
            l_k, (gw_k, gx_k) = one_microbatch(xs[0], xs[1])
            with _jax.named_scope("update"):
                return (loss_sum + l_k, _jax.tree.map(_jnp.add, grad_sum, gw_k)), gx_k

        init = (_jnp.zeros((), _jnp.float32), _jax.tree.map(_jnp.zeros_like, weights))
        (loss, grad_w), grad_x = _jax.lax.scan(body, init, (per_example, given["loss_target"]))
    with _jax.named_scope("update"):
        delta_w, new_m, new_v = {}, {}, {}
        for n in TWIN_WEIGHTS:
            delta_w[n], new_m[n], new_v[n] = _adamw(weights[n], grad_w[n], given["m_" + n], given["v_" + n])
    return (loss, grad_x, *[grad_w[n] for n in TWIN_WEIGHTS], *[delta_w[n] for n in TWIN_WEIGHTS],
            *[new_m[n] for n in TWIN_WEIGHTS], *[new_v[n] for n in TWIN_WEIGHTS])
```

```python
import jax
import jax.numpy as jnp
from jax import lax
from jax.experimental import pallas as pl
from jax.experimental.pallas import tpu as pltpu

F32, BF16 = jnp.float32, jnp.bfloat16

DEPTH = 4
CHUNK = 64
GM_BLOCK = 128
GM_GROUPS = 8
HEADS = 16
NOPE, ROPE, VDIM = 128, 64, 128
QRANK, KVRANK = 512, 512
ROPE_THETA = 10000.0
SM_SCALE = (NOPE + ROPE) ** -0.5
ALPHA = (2 * DEPTH) ** 0.25
LN_EPS = 1e-5
RMS_EPS = 1e-6
ADAM_LR, ADAM_B1, ADAM_B2, ADAM_EPS, ADAM_WD, ADAM_STEP = 0.001, 0.9, 0.999, 1e-08, 0.01, 10

N_DEV = 8
LANES = 128
VMEM_LIMIT = 56 * 1024 * 1024
MESH = pl.DeviceIdType.MESH
HBM_SPEC = pl.BlockSpec(memory_space=pltpu.HBM)


def _call(body, *, name, out_shape, in_specs, out_specs, grid=None, scratch=(), dims=None, grid_spec=None):
    kw = dict(vmem_limit_bytes=VMEM_LIMIT)
    if dims is not None:
        kw["dimension_semantics"] = dims
    cp = pltpu.CompilerParams(**kw)
    if grid_spec is not None:
        return pl.pallas_call(body, name=name, grid_spec=grid_spec, out_shape=out_shape, compiler_params=cp, interpret=False)
    extra = {} if grid is None else {"grid": grid}
    return pl.pallas_call(body, name=name, in_specs=in_specs, out_specs=out_specs, out_shape=out_shape,
                          scratch_shapes=list(scratch), compiler_params=cp, interpret=False, **extra)


def _tile(n, pref, mult=LANES):
    if n <= pref:
        return n
    t = (pref // mult) * mult
    while t >= mult:
        if n % t == 0:
            return t
        t -= mult
    return n


def _sds(shape, dtype):
    return jax.ShapeDtypeStruct(tuple(shape), dtype)


_DN = {"nn": (((1,), (0,)), ((), ())), "nt": (((1,), (1,)), ((), ())), "tn": (((0,), (0,)), ((), ()))}


def _mm(a, b, mode, *, name, out_dtype, tm=1024, tn=1024, tk=2048, res=None, res_scale=1.0):
    if mode == "nn":
        (M, K), (K2, N) = a.shape, b.shape
    elif mode == "nt":
        (M, K), (N, K2) = a.shape, b.shape
    else:
        (K, M), (K2, N) = a.shape, b.shape
    assert K == K2, (name, a.shape, b.shape)
    tm, tn, tk = _tile(M, tm), _tile(N, tn), _tile(K, tk)
    nk = K // tk
    a_spec = pl.BlockSpec((tk, tm), lambda i, j, k: (k, i)) if mode == "tn" else pl.BlockSpec((tm, tk), lambda i, j, k: (i, k))
    b_spec = pl.BlockSpec((tn, tk), lambda i, j, k: (j, k)) if mode == "nt" else pl.BlockSpec((tk, tn), lambda i, j, k: (k, j))
    in_specs = [a_spec, b_spec]
    args = [a, b]
    if res is not None:
        in_specs.append(pl.BlockSpec((tm, tn), lambda i, j, k: (i, j)))
        args.append(res)
    dn = _DN[mode]
    has_res = res is not None

    def body(*refs):
        a_ref, b_ref = refs[0], refs[1]
        r_ref = refs[2] if has_res else None
        o_ref = refs[2 + has_res]
        part = lax.dot_general(a_ref[...], b_ref[...], dn, preferred_element_type=F32)

        def finish(acc):
            if has_res:
                acc = acc + res_scale * r_ref[...]
            o_ref[...] = acc.astype(o_ref.dtype)

        if nk == 1:
            finish(part)
        else:
            acc_ref = refs[3 + has_res]
            k = pl.program_id(2)

            @pl.when(k == 0)
            def _():
                acc_ref[...] = part

            @pl.when(k > 0)
            def _():
                acc_ref[...] += part

            @pl.when(k == nk - 1)
            def _():
                finish(acc_ref[...])

    scratch = [pltpu.VMEM((tm, tn), F32)] if nk > 1 else []
    return _call(body, name=name, grid=(M // tm, N // tn, nk), in_specs=in_specs,
                 out_specs=pl.BlockSpec((tm, tn), lambda i, j, k: (i, j)), out_shape=_sds((M, N), out_dtype),
                 scratch=scratch, dims=("parallel", "parallel", "arbitrary"))(*args)


def _ln_fwd(x, m, g, b, *, name):
    S, D = x.shape
    tr = _tile(S, 256, 8)

    def body(x_ref, m_ref, g_ref, b_ref, y_ref, yb_ref, xh_ref, rs_ref):
        r = ALPHA * x_ref[...] + m_ref[...]
        mu = jnp.mean(r, axis=-1, keepdims=True)
        d = r - mu
        var = jnp.mean(d * d, axis=-1, keepdims=True)
        rstd = lax.rsqrt(var + LN_EPS)
        xh = d * rstd
        y = xh * g_ref[...] + b_ref[...]
        y_ref[...] = y
        yb_ref[...] = y.astype(BF16)
        xh_ref[...] = xh
        rs_ref[...] = rstd

    row = pl.BlockSpec((tr, D), lambda i: (i, 0))
    vec = pl.BlockSpec((1, D), lambda i: (0, 0))
    return _call(body, name=name, grid=(S // tr,), in_specs=[row, row, vec, vec],
                 out_specs=[row, row, row, pl.BlockSpec((tr, 1), lambda i: (i, 0))],
                 out_shape=[_sds((S, D), F32), _sds((S, D), BF16), _sds((S, D), F32), _sds((S, 1), F32)],
                 dims=("parallel",))(x, m, g, b)


def _ln_bwd(dy, xh, rstd, g, *, name):
    S, D = dy.shape
    tr = _tile(S, 256, 8)

    def body(dy_ref, xh_ref, rs_ref, g_ref, dr_ref, drb_ref, dg_ref, db_ref):
        @pl.when(pl.program_id(0) == 0)
        def _():
            dg_ref[...] = jnp.zeros_like(dg_ref)
            db_ref[...] = jnp.zeros_like(db_ref)

        dyv = dy_ref[...]
        xhv = xh_ref[...]
        dxh = dyv * g_ref[...]
        m1 = jnp.mean(dxh, axis=-1, keepdims=True)
        m2 = jnp.mean(dxh * xhv, axis=-1, keepdims=True)
        dr = rs_ref[...] * (dxh - m1 - xhv * m2)
        dr_ref[...] = dr
        drb_ref[...] = dr.astype(BF16)
        dg_ref[...] += jnp.sum(dyv * xhv, axis=0, keepdims=True)
        db_ref[...] += jnp.sum(dyv, axis=0, keepdims=True)

    row = pl.BlockSpec((tr, D), lambda i: (i, 0))
    vec = pl.BlockSpec((1, D), lambda i: (0, 0))
    return _call(body, name=name, grid=(S // tr,), in_specs=[row, row, pl.BlockSpec((tr, 1), lambda i: (i, 0)), vec],
                 out_specs=[row, row, vec, vec],
                 out_shape=[_sds((S, D), F32), _sds((S, D), BF16), _sds((1, D), F32), _sds((1, D), F32)],
                 dims=("arbitrary",))(dy, xh, rstd, g)


def _loss_kernel(y, t, *, name):
    S, D = y.shape
    tr = _tile(S, 256, 8)

    def body(y_ref, t_ref, lp_ref, dy_ref):
        @pl.when(pl.program_id(0) == 0)
        def _():
            lp_ref[...] = jnp.zeros_like(lp_ref)

        e = y_ref[...] - t_ref[...]
        dy_ref[...] = e / D
        lp_ref[...] += jnp.sum(e * e, axis=0, keepdims=True)

    row = pl.BlockSpec((tr, D), lambda i: (i, 0))
    vec = pl.BlockSpec((1, D), lambda i: (0, 0))
    return _call(body, name=name, grid=(S // tr,), in_specs=[row, row], out_specs=[vec, row],
                 out_shape=[_sds((1, D), F32), _sds((S, D), F32)], dims=("arbitrary",))(y, t)


_GELU_C = 0.7978845608028654
_GELU_A = 0.044715


def _gelu(x):
    return 0.5 * x * (1.0 + jnp.tanh(_GELU_C * (x + _GELU_A * x * x * x)))


def _gelu_grad(x):
    x2 = x * x
    t = jnp.tanh(_GELU_C * (x + _GELU_A * x * x2))
    return 0.5 * (1.0 + t) + 0.5 * x * (1.0 - t * t) * (_GELU_C * (1.0 + 3.0 * _GELU_A * x2))


def _masked_ws(w):
    i = lax.broadcasted_iota(jnp.int32, w.shape, 0) // CHUNK
    j = lax.broadcasted_iota(jnp.int32, w.shape, 1) // CHUNK
    return jnp.where(j <= i, w, 0.0)


def _gm_mid_fwd(zp, ln_g, ln_b, w_s, b_st, *, name):
    S, H2 = zp.shape
    H = H2 // 2
    gd = H // GM_GROUPS
    nb = S // GM_BLOCK

    def body(zu_ref, zv_ref, g_ref, b_ref, w_ref, bs_ref, p_ref, s_ref, vh_ref, rs_ref):
        v = _gelu(zv_ref[...].astype(F32))
        mu = jnp.mean(v, axis=-1, keepdims=True)
        d = v - mu
        var = jnp.mean(d * d, axis=-1, keepdims=True)
        rstd = lax.rsqrt(var + LN_EPS)
        vh = d * rstd
        vh_ref[...] = vh.astype(BF16)
        rs_ref[...] = rstd
        vn = (vh * g_ref[...] + b_ref[...]).astype(BF16)
        bs = bs_ref[...]
        for gi in range(GM_GROUPS):
            cs = slice(gi * gd, (gi + 1) * gd)
            wm = _masked_ws(w_ref[gi]).astype(BF16)
            s = jnp.dot(wm, vn[:, cs], preferred_element_type=F32) + bs[:, gi:gi + 1]
            u = _gelu(zu_ref[:, cs].astype(F32))
            s_ref[:, cs] = s.astype(BF16)
            p_ref[:, cs] = (u * s).astype(BF16)

    blk = lambda c: pl.BlockSpec((GM_BLOCK, H), lambda n, c=c: (n, c))
    vec = pl.BlockSpec((1, H), lambda n: (0, 0))
    return _call(body, name=name, grid=(nb,),
                 in_specs=[blk(0), blk(1), vec, vec, pl.BlockSpec((GM_GROUPS, GM_BLOCK, GM_BLOCK), lambda n: (0, 0, 0)),
                           pl.BlockSpec((GM_BLOCK, GM_GROUPS), lambda n: (0, 0))],
                 out_specs=[blk(0), blk(0), blk(0), pl.BlockSpec((GM_BLOCK, 1), lambda n: (n, 0))],
                 out_shape=[_sds((S, H), BF16), _sds((S, H), BF16), _sds((S, H), BF16), _sds((S, 1), F32)],
                 dims=("parallel",))(zp, zp, ln_g, ln_b, w_s, b_st)


def _gm_mid_bwd(dp, zp, s, vhat, rstd, ln_g, ln_b, w_s, *, name):
    S, H2 = zp.shape
    H = H2 // 2
    gd = H // GM_GROUPS
    nb = S // GM_BLOCK

    def body(dp_ref, zu_ref, zv_ref, s_ref, vh_ref, rs_ref, g_ref, b_ref, w_ref,
             dz_ref, dw_ref, dbs_ref, dg_ref, db_ref, dvh_ref):
        @pl.when(pl.program_id(0) == 0)
        def _():
            dw_ref[...] = jnp.zeros_like(dw_ref)
            dbs_ref[...] = jnp.zeros_like(dbs_ref)
            dg_ref[...] = jnp.zeros_like(dg_ref)
            db_ref[...] = jnp.zeros_like(db_ref)

        m1 = jnp.zeros((GM_BLOCK, 1), F32)
        m2 = jnp.zeros((GM_BLOCK, 1), F32)
        for gi in range(GM_GROUPS):
            cs = slice(gi * gd, (gi + 1) * gd)
            dpg = dp_ref[:, cs].astype(F32)
            zu = zu_ref[:, cs].astype(F32)
            u = _gelu(zu)
            ds = dpg * u
            du = dpg * s_ref[:, cs].astype(F32)
            dz_ref[:, cs] = (du * _gelu_grad(zu)).astype(BF16)
            dsb = ds.astype(BF16)
            vh = vh_ref[:, cs].astype(F32)
            lg = g_ref[:, cs]
            vn = (vh * lg + b_ref[:, cs]).astype(BF16)
            wm = _masked_ws(w_ref[gi]).astype(BF16)
            dvn = lax.dot_general(wm, dsb, _DN["tn"], preferred_element_type=F32)
            dw_ref[gi] += lax.dot_general(dsb, vn, _DN["nt"], preferred_element_type=F32)
            dbs_ref[gi] += jnp.sum(ds, axis=1, keepdims=True)
            dg_ref[:, cs] += jnp.sum(dvn * vh, axis=0, keepdims=True)
            db_ref[:, cs] += jnp.sum(dvn, axis=0, keepdims=True)
            dvh = dvn * lg
            dvh_ref[:, cs] = dvh
            m1 = m1 + jnp.sum(dvh, axis=1, keepdims=True)
            m2 = m2 + jnp.sum(dvh * vh, axis=1, keepdims=True)
        m1 = m1 / H
        m2 = m2 / H
        rs = rs_ref[...]
        for gi in range(GM_GROUPS):
            cs = slice(gi * gd, (gi + 1) * gd)
            vh = vh_ref[:, cs].astype(F32)
            dv = rs * (dvh_ref[:, cs] - m1 - vh * m2)
            zv = zv_ref[:, cs].astype(F32)
            dz_ref[:, H + gi * gd:H + (gi + 1) * gd] = (dv * _gelu_grad(zv)).astype(BF16)

        @pl.when(pl.program_id(0) == nb - 1)
        def _():
            for gi in range(GM_GROUPS):
                dw_ref[gi] = _masked_ws(dw_ref[gi])

    blk = lambda c: pl.BlockSpec((GM_BLOCK, H), lambda n, c=c: (n, c))
    vec = pl.BlockSpec((1, H), lambda n: (0, 0))
    wspec = pl.BlockSpec((GM_GROUPS, GM_BLOCK, GM_BLOCK), lambda n: (0, 0, 0))
    return _call(body, name=name, grid=(nb,),
                 in_specs=[blk(0), blk(0), blk(1), blk(0), blk(0), pl.BlockSpec((GM_BLOCK, 1), lambda n: (n, 0)), vec, vec, wspec],
                 out_specs=[pl.BlockSpec((GM_BLOCK, H2), lambda n: (n, 0)), wspec,
                            pl.BlockSpec((GM_GROUPS, GM_BLOCK, 1), lambda n: (0, 0, 0)), vec, vec],
                 out_shape=[_sds((S, H2), BF16), _sds((GM_GROUPS, GM_BLOCK, GM_BLOCK), F32),
                            _sds((GM_GROUPS, GM_BLOCK, 1), F32), _sds((1, H), F32), _sds((1, H), F32)],
                 scratch=[pltpu.VMEM((GM_BLOCK, H), F32)], dims=("arbitrary",))(dp, zp, zp, s, vhat, rstd, ln_g, ln_b, w_s)


def _shift_down(x, k):
    rows = lax.broadcasted_iota(jnp.int32, x.shape, 0)
    return jnp.where(rows >= k, pltpu.roll(x, k, 0), 0.0)


def _shift_up(x, k):
    n = x.shape[0]
    rows = lax.broadcasted_iota(jnp.int32, x.shape, 0)
    return jnp.where(rows < n - k, pltpu.roll(x, n - k, 0), 0.0)


def _conv(h, w, b):
    return w[0:1, :] * _shift_down(h, 2) + w[1:2, :] * _shift_down(h, 1) + w[2:3, :] * h + b


def _ffn_mid_fwd(h, cw, cb, *, name):
    S, F2 = h.shape
    F = F2 // 2
    nf = F // LANES

    def body(ha_ref, hg_ref, wa_ref, wg_ref, ba_ref, bg_ref, act_ref):
        a = _conv(ha_ref[...].astype(F32), wa_ref[...], ba_ref[...])
        g = _conv(hg_ref[...].astype(F32), wg_ref[...], bg_ref[...])
        act_ref[...] = (g * jax.nn.sigmoid(g) * a).astype(BF16)

    sl = lambda off, r: pl.BlockSpec((r, LANES), lambda j, off=off: (0, j + off))
    return _call(body, name=name, grid=(nf,),
                 in_specs=[sl(0, S), sl(nf, S), sl(0, 3), sl(nf, 3), sl(0, 1), sl(nf, 1)],
                 out_specs=sl(0, S), out_shape=_sds((S, F), BF16), dims=("parallel",))(h, h, cw, cw, cb, cb)


def _ffn_mid_bwd(dact, h, cw, cb, *, name):
    S, F2 = h.shape
    F = F2 // 2
    nf = F // LANES

    def body(da_ref, ha_ref, hg_ref, wa_ref, wg_ref, ba_ref, bg_ref, dh_ref, dw_ref, db_ref):
        half = pl.program_id(0)
        dact_v = da_ref[...].astype(F32)
        hg = hg_ref[...].astype(F32)
        g = _conv(hg, wg_ref[...], bg_ref[...])
        sg = jax.nn.sigmoid(g)

        def emit(dc, hin, w):
            dh_ref[...] = (w[2:3, :] * dc + w[1:2, :] * _shift_up(dc, 1) + w[0:1, :] * _shift_up(dc, 2)).astype(BF16)
            dw_ref[0:1, :] = jnp.sum(dc * _shift_down(hin, 2), axis=0, keepdims=True)
            dw_ref[1:2, :] = jnp.sum(dc * _shift_down(hin, 1), axis=0, keepdims=True)
            dw_ref[2:3, :] = jnp.sum(dc * hin, axis=0, keepdims=True)
            db_ref[...] = jnp.sum(dc, axis=0, keepdims=True)

        @pl.when(half == 0)
        def _():
            emit(dact_v * (g * sg), ha_ref[...].astype(F32), wa_ref[...])

        @pl.when(half == 1)
        def _():
            a = _conv(ha_ref[...].astype(F32), wa_ref[...], ba_ref[...])
            emit(dact_v * a * (sg * (1.0 + g * (1.0 - sg))), hg, wg_ref[...])

    sl = lambda off, r: pl.BlockSpec((r, LANES), lambda hf, j, off=off: (0, j + off))
    out = lambda r: pl.BlockSpec((r, LANES), lambda hf, j: (0, hf * nf + j))
    return _call(body, name=name, grid=(2, nf),
                 in_specs=[sl(0, S), sl(0, S), sl(nf, S), sl(0, 3), sl(nf, 3), sl(0, 1), sl(nf, 1)],
                 out_specs=[out(S), out(3), out(1)],
                 out_shape=[_sds((S, F2), BF16), _sds((3, F2), F32), _sds((1, F2), F32)],
                 dims=("parallel", "parallel"))(dact, h, h, cw, cw, cb, cb)


def _swap_half(x):
    lane = lax.broadcasted_iota(jnp.int32, x.shape, 1)
    return jnp.where((lane % ROPE) < ROPE // 2, pltpu.roll(x, LANES - ROPE // 2, 1), pltpu.roll(x, ROPE // 2, 1))


def _rope(x, cos, sin_s):
    return x * cos + _swap_half(x) * sin_s


def _mla_prep(h, gq, gkv, cos, sin_s, *, name):
    S, W = h.shape
    tr = _tile(S, 512, 8)

    def body(h_ref, gq_ref, gkv_ref, cos_ref, sin_ref, cq_ref, ckv_ref, kr_ref, rq_ref, rkv_ref):
        cq = h_ref[:, 0:QRANK]
        rq = lax.rsqrt(jnp.mean(cq * cq, axis=-1, keepdims=True) + RMS_EPS)
        cq_ref[...] = (cq * rq * gq_ref[...]).astype(BF16)
        rq_ref[...] = rq
        ckv = h_ref[:, QRANK:QRANK + KVRANK]
        rkv = lax.rsqrt(jnp.mean(ckv * ckv, axis=-1, keepdims=True) + RMS_EPS)
        ckv_ref[...] = (ckv * rkv * gkv_ref[...]).astype(BF16)
        rkv_ref[...] = rkv
        kr = _rope(h_ref[:, QRANK + KVRANK:W], cos_ref[...], sin_ref[...])
        lane = lax.broadcasted_iota(jnp.int32, kr.shape, 1)
        kr = jnp.where(lane < ROPE, kr, 0.0)
        kr_ref[...] = (kr + pltpu.roll(kr, ROPE, 1)).astype(BF16)

    row = lambda w: pl.BlockSpec((tr, w), lambda i: (i, 0))
    vec = lambda w: pl.BlockSpec((1, w), lambda i: (0, 0))
    return _call(body, name=name, grid=(S // tr,),
                 in_specs=[row(W), vec(QRANK), vec(KVRANK), row(LANES), row(LANES)],
                 out_specs=[row(QRANK), row(KVRANK), row(LANES), row(1), row(1)],
                 out_shape=[_sds((S, QRANK), BF16), _sds((S, KVRANK), BF16), _sds((S, LANES), BF16), _sds((S, 1), F32), _sds((S, 1), F32)],
                 dims=("parallel",))(h, gq, gkv, cos, sin_s)


def _mla_prep_bwd(dcqn, dckvn, dkr2, h, rq, rkv, gq, gkv, cos, sin_neg, *, name):
    S, W = h.shape
    tr = _tile(S, 512, 8)

    def rms_bwd(dy, c, r, g):
        n = c * r
        dn = dy * g
        return r * (dn - n * jnp.mean(dn * n, axis=-1, keepdims=True)), jnp.sum(dy * n, axis=0, keepdims=True)

    def body(dq_ref, dkv_ref, dkr_ref, h_ref, rq_ref, rkv_ref, gq_ref, gkv_ref, cos_ref, sin_ref, dh_ref, dgq_ref, dgkv_ref):
        @pl.when(pl.program_id(0) == 0)
        def _():
            dgq_ref[...] = jnp.zeros_like(dgq_ref)
            dgkv_ref[...] = jnp.zeros_like(dgkv_ref)

        dcq, dg = rms_bwd(dq_ref[...], h_ref[:, 0:QRANK], rq_ref[...], gq_ref[...])
        dgq_ref[...] += dg
        dh_ref[:, 0:QRANK] = dcq.astype(BF16)
        dckv, dg = rms_bwd(dkv_ref[...], h_ref[:, QRANK:QRANK + KVRANK], rkv_ref[...], gkv_ref[...])
        dgkv_ref[...] += dg
        dh_ref[:, QRANK:QRANK + KVRANK] = dckv.astype(BF16)
        dk = dkr_ref[...]
        dk = dk + pltpu.roll(dk, ROPE, 1)
        dk = _rope(dk, cos_ref[...], sin_ref[...])
        lane = lax.broadcasted_iota(jnp.int32, dk.shape, 1)
        dh_ref[:, QRANK + KVRANK:W] = jnp.where(lane < ROPE, dk, 0.0).astype(BF16)

    row = lambda w: pl.BlockSpec((tr, w), lambda i: (i, 0))
    vec = lambda w: pl.BlockSpec((1, w), lambda i: (0, 0))
    return _call(body, name=name, grid=(S // tr,),
                 in_specs=[row(QRANK), row(KVRANK), row(LANES), row(W), row(1), row(1), vec(QRANK), vec(KVRANK), row(LANES), row(LANES)],
                 out_specs=[row(W), vec(QRANK), vec(KVRANK)],
                 out_shape=[_sds((S, W), BF16), _sds((1, QRANK), F32), _sds((1, KVRANK), F32)],
                 dims=("arbitrary",))(dcqn, dckvn, dkr2, h, rq, rkv, gq, gkv, cos, sin_neg)


QPAIR = 2 * NOPE + 2 * ROPE
KVPAIR = 2 * (NOPE + VDIM)


def _rope_q(q, cos, sin_s, *, name):
    S, Wq = q.shape
    tr = _tile(S, 512, 8)

    def body(q_ref, cos_ref, sin_ref, o_ref):
        o_ref[:, 0:2 * NOPE] = q_ref[:, 0:2 * NOPE].astype(BF16)
        o_ref[:, 2 * NOPE:QPAIR] = _rope(q_ref[:, 2 * NOPE:QPAIR], cos_ref[...], sin_ref[...]).astype(BF16)

    blk = pl.BlockSpec((tr, QPAIR), lambda i, p: (i, p))
    tab = pl.BlockSpec((tr, LANES), lambda i, p: (i, 0))
    return _call(body, name=name, grid=(S // tr, Wq // QPAIR), in_specs=[blk, tab, tab], out_specs=blk,
                 out_shape=_sds((S, Wq), BF16), dims=("parallel", "parallel"))(q, cos, sin_s)


ATT_T = 512


def _att_scores(qn, rm, kn, kr, q0, k0, masked):
    s = (lax.dot_general(qn, kn, _DN["nt"], preferred_element_type=F32)
         + lax.dot_general(rm, kr, _DN["nt"], preferred_element_type=F32)) * SM_SCALE
    if masked:
        qpos = q0 + lax.broadcasted_iota(jnp.int32, s.shape, 0)
        kpos = k0 + lax.broadcasted_iota(jnp.int32, s.shape, 1)
        s = jnp.where((kpos // CHUNK) <= (qpos // CHUNK), s, -jnp.inf)
    return s


def _head_mask(r, hh):
    lane = lax.broadcasted_iota(jnp.int32, r.shape, 1)
    keep = (lane < ROPE) if hh == 0 else (lane >= ROPE)
    return jnp.where(keep, r, jnp.zeros_like(r))


def _attn_fwd(qb, kv, kr2, *, name):
    S = qb.shape[0]
    t = _tile(S, ATT_T, 8)
    nq = S // t
    npair = HEADS // 2

    def body(q_ref, kv_ref, kr_ref, o_ref, lse_ref):
        i = pl.program_id(1)
        q0 = i * t
        r = q_ref[:, 2 * NOPE:QPAIR]
        for hh in range(2):
            qn = q_ref[:, hh * NOPE:(hh + 1) * NOPE]
            rm = _head_mask(r, hh)
            kc = hh * (NOPE + VDIM)

            def step(j, carry, masked):
                m, l, acc = carry
                k0 = pl.multiple_of(j * t, t)
                kn = kv_ref[pl.ds(k0, t), kc:kc + NOPE]
                v = kv_ref[pl.ds(k0, t), kc + NOPE:kc + NOPE + VDIM]
                kr = kr_ref[pl.ds(k0, t), :]
                s = _att_scores(qn, rm, kn, kr, q0, k0, masked)
                m_new = jnp.maximum(m, jnp.max(s, axis=1, keepdims=True))
                p = jnp.exp(s - m_new)
                a = jnp.exp(m - m_new)
                l = a * l + jnp.sum(p, axis=1, keepdims=True)
                acc = a * acc + jnp.dot(p.astype(BF16), v, preferred_element_type=F32)
                return m_new, l, acc

            init = (jnp.full((t, 1), -jnp.inf, F32), jnp.zeros((t, 1), F32), jnp.zeros((t, VDIM), F32))
            carry = lax.fori_loop(0, i, lambda j, c: step(j, c, False), init)
            m, l, acc = step(i, carry, True)
            o_ref[:, hh * VDIM:(hh + 1) * VDIM] = (acc / l).astype(BF16)
            lse_ref[:, hh * VDIM:(hh + 1) * VDIM] = jnp.broadcast_to(m + jnp.log(l), (t, VDIM))

    return _call(body, name=name, grid=(npair, nq),
                 in_specs=[pl.BlockSpec((t, QPAIR), lambda p, i: (i, p)), pl.BlockSpec((S, KVPAIR), lambda p, i: (0, p)),
                           pl.BlockSpec((S, LANES), lambda p, i: (0, 0))],
                 out_specs=[pl.BlockSpec((t, 2 * VDIM), lambda p, i: (i, p)), pl.BlockSpec((t, 2 * VDIM), lambda p, i: (i, p))],
                 out_shape=[_sds((S, HEADS * VDIM), BF16), _sds((S, HEADS * VDIM), F32)],
                 dims=("parallel", "arbitrary"))(qb, kv, kr2)


def _attn_bwd(qb, kv, kr2, o, do, lse, *, name):
    S = qb.shape[0]
    t = _tile(S, ATT_T, 8)
    nq = S // t
    npair = HEADS // 2

    def body(q_ref, kv_ref, kr_ref, o_ref, do_ref, lse_ref, dq_ref, dkv_ref, dkr_ref, acc_ref):
        pid = pl.program_id(0)
        i = pl.program_id(1)
        q0 = i * t

        @pl.when(i == 0)
        def _():
            acc_ref[...] = jnp.zeros_like(acc_ref)

        @pl.when((i == 0) & (pid == 0))
        def _():
            dkr_ref[...] = jnp.zeros_like(dkr_ref)

        r = q_ref[:, 2 * NOPE:QPAIR]
        dr_tot = jnp.zeros((t, LANES), F32)
        for hh in range(2):
            qn = q_ref[:, hh * NOPE:(hh + 1) * NOPE]
            rm = _head_mask(r, hh)
            kc = hh * (NOPE + VDIM)
            dov = do_ref[:, hh * VDIM:(hh + 1) * VDIM]
            dsum = jnp.sum(dov.astype(F32) * o_ref[:, hh * VDIM:(hh + 1) * VDIM].astype(F32), axis=1, keepdims=True)
            lse_v = lse_ref[:, hh * VDIM:hh * VDIM + 1]

            def step(j, carry, masked):
                dqn, drr = carry
                k0 = pl.multiple_of(j * t, t)
                rows = pl.ds(k0, t)
                kn = kv_ref[rows, kc:kc + NOPE]
                v = kv_ref[rows, kc + NOPE:kc + NOPE + VDIM]
                kr = kr_ref[rows, :]
                s = _att_scores(qn, rm, kn, kr, q0, k0, masked)
                p = jnp.exp(s - lse_v)
                dp = lax.dot_general(dov, v, _DN["nt"], preferred_element_type=F32)
                ds = (p * (dp - dsum) * SM_SCALE).astype(BF16)
                acc_ref[rows, kc + NOPE:kc + NOPE + VDIM] += lax.dot_general(p.astype(BF16), dov, _DN["tn"], preferred_element_type=F32)
                acc_ref[rows, kc:kc + NOPE] += lax.dot_general(ds, qn, _DN["tn"], preferred_element_type=F32)
                dkr_ref[rows, :] += lax.dot_general(ds, rm, _DN["tn"], preferred_element_type=F32)
                dqn = dqn + jnp.dot(ds, kn, preferred_element_type=F32)
                drr = drr + jnp.dot(ds, kr, preferred_element_type=F32)
                return dqn, drr

            init = (jnp.zeros((t, NOPE), F32), jnp.zeros((t, LANES), F32))
            carry = lax.fori_loop(0, i, lambda j, c: step(j, c, False), init)
            dqn, drr = step(i, carry, True)
            dq_ref[:, hh * NOPE:(hh + 1) * NOPE] = dqn
            dr_tot = dr_tot + _head_mask(drr, hh)
        dq_ref[:, 2 * NOPE:QPAIR] = dr_tot

        @pl.when(i == nq - 1)
        def _():
            dkv_ref[...] = acc_ref[...].astype(BF16)

    qspec = pl.BlockSpec((t, QPAIR), lambda p, i: (i, p))
    hspec = pl.BlockSpec((t, 2 * VDIM), lambda p, i: (i, p))
    kvspec = pl.BlockSpec((S, KVPAIR), lambda p, i: (0, p))
    krspec = pl.BlockSpec((S, LANES), lambda p, i: (0, 0))
    return _call(body, name=name, grid=(npair, nq),
                 in_specs=[qspec, kvspec, krspec, hspec, hspec, hspec],
                 out_specs=[qspec, kvspec, krspec],
                 out_shape=[_sds(qb.shape, F32), _sds(kv.shape, BF16), _sds((S, LANES), F32)],
                 scratch=[pltpu.VMEM((S, KVPAIR), F32)], dims=("arbitrary", "arbitrary"))(qb, kv, kr2, o, do, lse)


def _coords():
    return lax.axis_index("x"), lax.axis_index("y"), lax.axis_index("c")


def _all_gather(arrs, *, name):
    n = len(arrs)

    def body(*refs):
        ins, outs = refs[:n], refs[n:2 * n]
        send_sems, recv_sems, local_sems = refs[2 * n:]
        x, y, c = _coords()
        me, sib = (x, y, c), (x, y, 1 - c)
        chips = [(1 - x, y), (x, 1 - y), (1 - x, 1 - y)]

        def copy(a, k, block, to, src=None):
            dst = outs[a].at[4 * block[0] + 2 * block[1] + block[2]]
            return pltpu.make_async_remote_copy(src_ref=dst if src is None else src, dst_ref=dst,
                                                send_sem=send_sems.at[a, k], recv_sem=recv_sems.at[a, k],
                                                device_id=to, device_id_type=MESH)

        mine = [pltpu.make_async_copy(ins[a], outs[a].at[4 * x + 2 * y + c], local_sems.at[a]) for a in range(n)]
        for cp in mine:
            cp.start()
        sends = []
        for a in range(n):
            sends.append(copy(a, 0, me, sib, src=ins[a]))
            sends += [copy(a, 1 + j, me, (*chip, c), src=ins[a]) for j, chip in enumerate(chips)]
        for cp in sends:
            cp.start()
        for j, chip in enumerate(chips):
            for a in range(n):
                copy(a, 1 + j, (*chip, c), me).wait_recv()
                fwd = copy(a, 4 + j, (*chip, c), sib)
                fwd.start()
                sends.append(fwd)
        for a in range(n):
            copy(a, 0, sib, me).wait_recv()
            for j, chip in enumerate(chips):
                copy(a, 4 + j, (*chip, 1 - c), me).wait_recv()
        for cp in sends:
            cp.wait_send()
        for cp in mine:
            cp.wait()

    outs = _call(body, name=name, in_specs=[HBM_SPEC] * n, out_specs=[HBM_SPEC] * n,
                 out_shape=[_sds((N_DEV,) + a.shape, a.dtype) for a in arrs],
                 scratch=[pltpu.SemaphoreType.DMA((n, 7)), pltpu.SemaphoreType.DMA((n, 7)), pltpu.SemaphoreType.DMA((n,))])(*arrs)
    return list(outs)


def _pair_exchange(gs, *, name):
    n = len(gs)

    def body(*refs):
        ins, outs = refs[:n], refs[n:2 * n]
        send_sems, recv_sems = refs[2 * n:]
        x, y, c = _coords()
        cps = []
        for a in range(n):
            for j in range(4):
                cps.append(pltpu.make_async_remote_copy(src_ref=ins[a].at[2 * j + (1 - c)], dst_ref=outs[a].at[j],
                                                        send_sem=send_sems.at[a, j], recv_sem=recv_sems.at[a, j],
                                                        device_id=(x, y, 1 - c), device_id_type=MESH))
        for cp in cps:
            cp.start()
        for cp in cps:
            cp.wait()

    outs = _call(body, name=name, in_specs=[HBM_SPEC] * n, out_specs=[HBM_SPEC] * n,
                 out_shape=[_sds((4,) + g.shape[1:], g.dtype) for g in gs],
                 scratch=[pltpu.SemaphoreType.DMA((n, 4)), pltpu.SemaphoreType.DMA((n, 4))])(*gs)
    return list(outs)


def _chip_exchange(ps, *, name):
    n = len(ps)

    def body(*refs):
        ins, outs = refs[:n], refs[n:2 * n]
        send_sems, recv_sems = refs[2 * n:]
        x, y, c = _coords()
        chips = [(1 - x, y), (x, 1 - y), (1 - x, 1 - y)]
        cps = []
        for a in range(n):
            for r, (px, py) in enumerate(chips):
                cps.append(pltpu.make_async_remote_copy(src_ref=ins[a].at[2 * px + py], dst_ref=outs[a].at[r],
                                                        send_sem=send_sems.at[a, r], recv_sem=recv_sems.at[a, r],
                                                        device_id=(px, py, c), device_id_type=MESH))
        for cp in cps:
            cp.start()
        for cp in cps:
            cp.wait()

    outs = _call(body, name=name, in_specs=[HBM_SPEC] * n, out_specs=[HBM_SPEC] * n,
                 out_shape=[_sds((3,) + p.shape[1:], p.dtype) for p in ps],
                 scratch=[pltpu.SemaphoreType.DMA((n, 3)), pltpu.SemaphoreType.DMA((n, 3))])(*ps)
    return list(outs)


def _row_tile(R, C, budget=1 << 20, mult=16):
    want = max(mult, budget // (4 * C))
    if R <= want:
        return R
    t = (want // mult) * mult
    while t >= mult:
        if R % t == 0:
            return t
        t -= mult
    return R


def _pair_sum(g, l1, c_idx, *, name):
    _, R, C = g.shape
    tr = _row_tile(R, C)
    g4 = g.reshape(4, 2, R, C)

    def body(c_ref, g_ref, l_ref, o_ref):
        o_ref[...] = (g_ref[...].astype(F32) + l_ref[...].astype(F32)).astype(o_ref.dtype)

    gs = pltpu.PrefetchScalarGridSpec(
        num_scalar_prefetch=1, grid=(4, R // tr),
        in_specs=[pl.BlockSpec((None, None, tr, C), lambda j, r, c_ref: (j, c_ref[0], r, 0)),
                  pl.BlockSpec((None, tr, C), lambda j, r, c_ref: (j, r, 0))],
        out_specs=pl.BlockSpec((None, tr, C), lambda j, r, c_ref: (j, r, 0)))
    return _call(body, name=name, grid_spec=gs, in_specs=None, out_specs=None, out_shape=_sds((4, R, C), g.dtype))(c_idx, g4, l1)


def _chip_sum(p, l2, chip_idx, *, name):
    _, R, C = p.shape
    tr = _row_tile(R, C)

    def body(j_ref, p_ref, l_ref, o_ref):
        o_ref[...] = ((p_ref[...].astype(F32) + l_ref[0].astype(F32)) + l_ref[1].astype(F32)) + l_ref[2].astype(F32)

    gs = pltpu.PrefetchScalarGridSpec(
        num_scalar_prefetch=1, grid=(R // tr,),
        in_specs=[pl.BlockSpec((None, tr, C), lambda r, j_ref: (j_ref[0], r, 0)),
                  pl.BlockSpec((3, tr, C), lambda r, j_ref: (0, r, 0))],
        out_specs=pl.BlockSpec((tr, C), lambda r, j_ref: (r, 0)))
    return _call(body, name=name, grid_spec=gs, in_specs=None, out_specs=None, out_shape=_sds((R, C), F32))(chip_idx, p, l2)


def _sum8(a, *, name):
    _, R, C = a.shape
    tr = _row_tile(R, C, budget=1 << 18, mult=8)

    def body(a_ref, o_ref):
        acc = a_ref[0]
        for k in range(1, N_DEV):
            acc = acc + a_ref[k]
        o_ref[...] = acc

    return _call(body, name=name, grid=(R // tr,), in_specs=[pl.BlockSpec((N_DEV, tr, C), lambda r: (0, r, 0))],
                 out_specs=pl.BlockSpec((tr, C), lambda r: (r, 0)), out_shape=_sds((R, C), F32), dims=("parallel",))(a)


def _adamw(w, g, m, v, *, name):
    R, C = w.shape
    tr = _row_tile(R, C, budget=1 << 20, mult=8)
    c1 = 1.0 - ADAM_B1 ** ADAM_STEP
    c2 = 1.0 - ADAM_B2 ** ADAM_STEP

    def body(w_ref, g_ref, m_ref, v_ref, d_ref, mo_ref, vo_ref):
        gv = g_ref[...]
        mn = ADAM_B1 * m_ref[...] + (1.0 - ADAM_B1) * gv
        vn = ADAM_B2 * v_ref[...] + (1.0 - ADAM_B2) * (gv * gv)
        mo_ref[...] = mn
        vo_ref[...] = vn
        d_ref[...] = -ADAM_LR * ((mn / c1) / (jnp.sqrt(vn / c2) + ADAM_EPS) + ADAM_WD * w_ref[...])

    blk = pl.BlockSpec((tr, C), lambda r: (r, 0))
    return _call(body, name=name, grid=(R // tr,), in_specs=[blk] * 4, out_specs=[blk] * 3,
                 out_shape=[_sds((R, C), F32)] * 3, dims=("parallel",))(w, g, m, v)


def _gmlp_fwd(xb, W, sp, tag):
    zp = _mm(xb, W["w_in_t"], "nt", name=f"gm_zp_{tag}", out_dtype=BF16)
    p, s, vhat, rstd = _gm_mid_fwd(zp, sp["ln_g"], sp["ln_b"], sp["w_s"], sp["b_st"], name=f"gm_mid_{tag}")
    m = _mm(p, W["w_out"], "nn", name=f"gm_out_{tag}", out_dtype=F32)
    return m, dict(xb=xb, zp=zp, p=p, s=s, vhat=vhat, rstd=rstd)


def _gmlp_bwd(drb, dr, W, sp, sv, tag):
    dp = _mm(drb, W["w_out"], "nt", name=f"gm_dp_{tag}", out_dtype=BF16)
    d_w_out = _mm(sv["p"], drb, "tn", name=f"gm_dwout_{tag}", out_dtype=BF16, tk=1024)
    dzp, dws, dbs, dlg, dlb = _gm_mid_bwd(dp, sv["zp"], sv["s"], sv["vhat"], sv["rstd"], sp["ln_g"], sp["ln_b"], sp["w_s"],
                                          name=f"gm_midb_{tag}")
    dx = _mm(dzp, W["w_in_t"], "nn", name=f"gm_dx_{tag}", out_dtype=F32, res=dr, res_scale=ALPHA)
    d_w_in_t = _mm(dzp, sv["xb"], "tn", name=f"gm_dwin_{tag}", out_dtype=BF16, tk=1024)
    return dx, dict(w_in_t=d_w_in_t, w_out=d_w_out), dict(w_s=dws, b_s=dbs, ln_g=dlg, ln_b=dlb)


def _mla_fwd(xb, W, sp, rope, tag):
    cos, sin_s, _ = rope
    h = _mm(xb, W["w_in"], "nn", name=f"mla_h_{tag}", out_dtype=F32, tn=W["w_in"].shape[1])
    cqn, ckvn, kr2, rq, rkv = _mla_prep(h, sp["gq"], sp["gkv"], cos, sin_s, name=f"mla_prep_{tag}")
    q = _mm(cqn, W["w_qb_t"], "nt", name=f"mla_q_{tag}", out_dtype=F32)
    qb = _rope_q(q, cos, sin_s, name=f"mla_ropeq_{tag}")
    kv = _mm(ckvn, W["w_kvb_t"], "nt", name=f"mla_kv_{tag}", out_dtype=BF16)
    o, lse = _attn_fwd(qb, kv, kr2, name=f"mla_attn_{tag}")
    m = _mm(o, W["w_out"], "nn", name=f"mla_out_{tag}", out_dtype=F32)
    return m, dict(xb=xb, h=h, cqn=cqn, ckvn=ckvn, kr2=kr2, rq=rq, rkv=rkv, qb=qb, kv=kv, o=o, lse=lse)


def _mla_bwd(drb, dr, W, sp, rope, sv, tag):
    cos, _, sin_neg = rope
    do = _mm(drb, W["w_out"], "nt", name=f"mla_do_{tag}", out_dtype=BF16)
    d_w_out = _mm(sv["o"], drb, "tn", name=f"mla_dwout_{tag}", out_dtype=BF16, tk=1024)
    dq, dkv, dkr2 = _attn_bwd(sv["qb"], sv["kv"], sv["kr2"], sv["o"], do, sv["lse"], name=f"mla_attnb_{tag}")
    dqb = _rope_q(dq, cos, sin_neg, name=f"mla_ropedq_{tag}")
    dcqn = _mm(dqb, W["w_qb_t"], "nn", name=f"mla_dcq_{tag}", out_dtype=F32, tk=dqb.shape[1])
    d_w_qb_t = _mm(dqb, sv["cqn"], "tn", name=f"mla_dwqb_{tag}", out_dtype=BF16, tk=1024)
    dckvn = _mm(dkv, W["w_kvb_t"], "nn", name=f"mla_dckv_{tag}", out_dtype=F32, tk=dkv.shape[1])
    d_w_kvb_t = _mm(dkv, sv["ckvn"], "tn", name=f"mla_dwkvb_{tag}", out_dtype=BF16, tk=1024)
    dh, dgq, dgkv = _mla_prep_bwd(dcqn, dckvn, dkr2, sv["h"], sv["rq"], sv["rkv"], sp["gq"], sp["gkv"], cos, sin_neg,
                                  name=f"mla_prepb_{tag}")
    dx = _mm(dh, W["w_in"], "nt", name=f"mla_dx_{tag}", out_dtype=F32, tk=dh.shape[1], res=dr, res_scale=ALPHA)
    d_w_in = _mm(sv["xb"], dh, "tn", name=f"mla_dwin_{tag}", out_dtype=BF16, tn=dh.shape[1], tk=1024)
    return dx, dict(w_in=d_w_in, w_qb_t=d_w_qb_t, w_kvb_t=d_w_kvb_t, w_out=d_w_out), dict(gq=dgq, gkv=dgkv)


def _ffn_fwd(xb, W, sp, tag):
    S = xb.shape[0]
    F = W["w_down"].shape[0]
    h = _mm(xb, W["w_up_t"], "nt", name=f"ffn_h_{tag}", out_dtype=BF16, tm=2048, tn=256)
    act = _ffn_mid_fwd(h, sp["cw"], sp["cb"], name=f"ffn_mid_{tag}")
    f = _mm(act, W["w_down"], "nn", name=f"ffn_out_{tag}", out_dtype=F32, tm=512, tn=1024, tk=F)
    return f, dict(xb=xb, h=h, act=act)


def _ffn_bwd(drb, dr, W, sp, sv, tag):
    F = W["w_down"].shape[0]
    dact = _mm(drb, W["w_down"], "nt", name=f"ffn_dact_{tag}", out_dtype=BF16, tm=512, tn=F, tk=512)
    d_w_down = _mm(sv["act"], drb, "tn", name=f"ffn_dwdown_{tag}", out_dtype=BF16, tm=F, tn=512, tk=512)
    dh, dcw, dcb = _ffn_mid_bwd(dact, sv["h"], sp["cw"], sp["cb"], name=f"ffn_midb_{tag}")
    dx = _mm(dh, W["w_up_t"], "nn", name=f"ffn_dx_{tag}", out_dtype=F32, tm=512, tn=1024, tk=F, res=dr, res_scale=ALPHA)
    d_w_up_t = _mm(dh, sv["xb"], "tn", name=f"ffn_dwup_{tag}", out_dtype=BF16, tm=F, tn=512, tk=512)
    return dx, dict(w_up_t=d_w_up_t, w_down=d_w_down), dict(cw=dcw, cb=dcb)


def _rope_tables(S):
    half = ROPE // 2
    inv_freq = ROPE_THETA ** (-jnp.arange(half, dtype=F32) / half)
    ang = jnp.arange(S, dtype=F32)[:, None] * inv_freq[None, :]
    cos, sin = jnp.cos(ang), jnp.sin(ang)
    cos128 = jnp.concatenate([cos] * 4, axis=1)
    sin128 = jnp.concatenate([-sin, sin, -sin, sin], axis=1)
    return cos128, sin128, -sin128


def _fwd_bwd(x, tgt, Wm, Wf, spm, spf, ln):
    S, D = x.shape
    rope = _rope_tables(S)
    xf, xb = x, x.astype(BF16)
    saved = []
    for i in range(DEPTH):
        if i % 2 == 0:
            m, svm = _gmlp_fwd(xb, Wm[i], spm[i], f"l{i}")
        else:
            m, svm = _mla_fwd(xb, Wm[i], spm[i], rope, f"l{i}")
        y, yb, xh1, rs1 = _ln_fwd(xf, m, ln["mix_g"][i], ln["mix_b"][i], name=f"ln_mix_{i}")
        f, svf = _ffn_fwd(yb, Wf[i], spf[i], f"l{i}")
        z, zb, xh2, rs2 = _ln_fwd(y, f, ln["ffn_g"][i], ln["ffn_b"][i], name=f"ln_ffn_{i}")
        saved.append((svm, xh1, rs1, svf, xh2, rs2))
        xf, xb = z, zb
    lp, dy = _loss_kernel(xf, tgt, name="loss")
    gm_big, gm_small, gf_big, gf_small, gln = [None] * DEPTH, [None] * DEPTH, [None] * DEPTH, [None] * DEPTH, [None] * DEPTH
    for i in reversed(range(DEPTH)):
        svm, xh1, rs1, svf, xh2, rs2 = saved[i]
        dr, drb, dg2, db2 = _ln_bwd(dy, xh2, rs2, ln["ffn_g"][i], name=f"lnb_ffn_{i}")
        dy, gf_big[i], gf_small[i] = _ffn_bwd(drb, dr, Wf[i], spf[i], svf, f"l{i}")
        dr, drb, dg1, db1 = _ln_bwd(dy, xh1, rs1, ln["mix_g"][i], name=f"lnb_mix_{i}")
        if i % 2 == 0:
            dy, gm_big[i], gm_small[i] = _gmlp_bwd(drb, dr, Wm[i], spm[i], svm, f"l{i}")
        else:
            dy, gm_big[i], gm_small[i] = _mla_bwd(drb, dr, Wm[i], spm[i], rope, svm, f"l{i}")
        gln[i] = dict(mix_g=dg1, mix_b=db1, ffn_g=dg2, ffn_b=db2)
    return lp, dy, gm_big, gm_small, gf_big, gf_small, gln


def _perm_q_rows(wt):
    hd = NOPE + ROPE
    return jnp.concatenate([wt[0:NOPE], wt[hd:hd + NOPE], wt[NOPE:hd], wt[hd + NOPE:2 * hd]], axis=0)


def _unperm_q_rows(wt):
    return jnp.concatenate([wt[0:NOPE], wt[2 * NOPE:2 * NOPE + ROPE], wt[NOPE:2 * NOPE], wt[2 * NOPE + ROPE:]], axis=0)


def _pad_cols(w, to):
    return jnp.pad(w, ((0, 0), (0, to - w.shape[1])))


def _pack(parts, rows_mult=8):
    flat = jnp.concatenate([p.reshape(-1).astype(F32) for p in parts])
    n = flat.shape[0]
    per = LANES * rows_mult
    tot = ((n + per - 1) // per) * per
    return jnp.pad(flat, (0, tot - n)).reshape(tot // LANES, LANES)


def _unpack(buf, shapes):
    flat = buf.reshape(-1)
    out, off = [], 0
    for s in shapes:
        n = 1
        for d in s:
            n *= d
        out.append(flat[off:off + n].reshape(s))
        off += n
    return out


def kernel(x, gm_w_in, gm_ln_g, gm_ln_b, gm_w_s, gm_b_s, gm_w_out, mla_w_in, mla_q_norm_g, mla_kv_norm_g, mla_w_q_b, mla_w_kv_b, mla_w_out, ffn_w_up, ffn_conv_w, ffn_conv_b, ffn_w_down, ln_mix_g, ln_mix_b, ln_ffn_g, ln_ffn_b, loss_target, m_gm_w_in, m_gm_ln_g, m_gm_ln_b, m_gm_w_s, m_gm_b_s, m_gm_w_out, m_mla_w_in, m_mla_q_norm_g, m_mla_kv_norm_g, m_mla_w_q_b, m_mla_w_kv_b, m_mla_w_out, m_ffn_w_up, m_ffn_conv_w, m_ffn_conv_b, m_ffn_w_down, m_ln_mix_g, m_ln_mix_b, m_ln_ffn_g, m_ln_ffn_b, v_gm_w_in, v_gm_ln_g, v_gm_ln_b, v_gm_w_s, v_gm_b_s, v_gm_w_out, v_mla_w_in, v_mla_q_norm_g, v_mla_kv_norm_g, v_mla_w_q_b, v_mla_w_kv_b, v_mla_w_out, v_ffn_w_up, v_ffn_conv_w, v_ffn_conv_b, v_ffn_w_down, v_ln_mix_g, v_ln_mix_b, v_ln_ffn_g, v_ln_ffn_b):
    S, D = x.shape[1], x.shape[2]
    xi, yi, ci = _coords()
    dev = 4 * xi + 2 * yi + ci
    c_idx = jnp.reshape(ci, (1,)).astype(jnp.int32)
    chip_idx = jnp.reshape(2 * xi + yi, (1,)).astype(jnp.int32)
    w_in_cols = mla_w_in.shape[2]
    w_in_pad = ((w_in_cols + LANES - 1) // LANES) * LANES
    n_gm, n_mla = gm_w_in.shape[0], mla_w_in.shape[0]

    cw_l = ffn_conv_w.shape[2]
    small_in = _pack([mla_q_norm_g, mla_kv_norm_g, ffn_conv_w])
    (small_all,) = _all_gather([small_in], name="ag_small")
    gq_parts, gkv_parts, cw_parts = [], [], []
    for k in range(N_DEV):
        a, b, c_ = _unpack(small_all[k], [mla_q_norm_g.shape, mla_kv_norm_g.shape, ffn_conv_w.shape])
        gq_parts.append(a)
        gkv_parts.append(b)
        cw_parts.append(c_)
    gq_full = jnp.concatenate(gq_parts, axis=1)
    gkv_full = jnp.concatenate(gkv_parts, axis=1)
    cw_full = jnp.concatenate(cw_parts, axis=2)

    def gather(shards, name):
        full = _all_gather(list(shards.values()), name=name)
        return {k: f.reshape((-1, f.shape[2])) for k, f in zip(shards.keys(), full)}

    Wm, Wf = [None] * DEPTH, [None] * DEPTH
    for i in range(DEPTH):
        s = i // 2
        if i % 2 == 0:
            Wm[i] = gather(dict(w_in_t=gm_w_in[s].T.astype(BF16), w_out=gm_w_out[s].astype(BF16)), f"ag_gm_{i}")
        else:
            Wm[i] = gather(dict(w_in=_pad_cols(mla_w_in[s], w_in_pad).astype(BF16),
                                w_qb_t=_perm_q_rows(mla_w_q_b[s].T).astype(BF16),
                                w_kvb_t=mla_w_kv_b[s].T.astype(BF16), w_out=mla_w_out[s].astype(BF16)), f"ag_mla_{i}")
        Wf[i] = gather(dict(w_up_t=ffn_w_up[i].T.astype(BF16), w_down=ffn_w_down[i].astype(BF16)), f"ag_ffn_{i}")

    spm, spf = [None] * DEPTH, [None] * DEPTH
    for i in range(DEPTH):
        s = i // 2
        if i % 2 == 0:
            spm[i] = dict(ln_g=gm_ln_g[s][None], ln_b=gm_ln_b[s][None], w_s=gm_w_s[s], b_st=gm_b_s[s].T)
        else:
            spm[i] = dict(gq=gq_full[s][None], gkv=gkv_full[s][None])
        spf[i] = dict(cw=cw_full[i], cb=ffn_conv_b[i][None])
    ln = dict(mix_g=ln_mix_g[:, None], mix_b=ln_mix_b[:, None], ffn_g=ln_ffn_g[:, None], ffn_b=ln_ffn_b[:, None])

    lp, grad_x, gm_big, gm_small, gf_big, gf_small, gln = _fwd_bwd(x[0], loss_target[0], Wm, Wf, spm, spf, ln)
    loss = lax.psum(0.5 * jnp.sum(lp) / D, ("x", "y", "c"))

    def reduce_scatter(gd, tag):
        keys = list(gd.keys())
        g8 = [gd[k].reshape((N_DEV, gd[k].shape[0] // N_DEV, gd[k].shape[1])) for k in keys]
        l1 = _pair_exchange(g8, name=f"rs_pair_{tag}")
        ps = [_pair_sum(g, l, c_idx, name=f"rs_psum_{tag}_{k}") for g, l, k in zip(g8, l1, keys)]
        l2 = _chip_exchange(ps, name=f"rs_chip_{tag}")
        return {k: _chip_sum(p, l, chip_idx, name=f"rs_csum_{tag}_{k}") for p, l, k in zip(ps, l2, keys)}

    rm, rf = [None] * DEPTH, [None] * DEPTH
    for i in reversed(range(DEPTH)):
        rf[i] = reduce_scatter(gf_big[i], f"ffn{i}")
        rm[i] = reduce_scatter(gm_big[i], f"mix{i}")

    g_gm_w_in = jnp.stack([rm[2 * s]["w_in_t"].T for s in range(n_gm)])
    g_gm_w_out = jnp.stack([rm[2 * s]["w_out"] for s in range(n_gm)])
    g_mla_w_in = jnp.stack([rm[2 * s + 1]["w_in"][:, :w_in_cols] for s in range(n_mla)])
    g_mla_w_q_b = jnp.stack([_unperm_q_rows(rm[2 * s + 1]["w_qb_t"]).T for s in range(n_mla)])
    g_mla_w_kv_b = jnp.stack([rm[2 * s + 1]["w_kvb_t"].T for s in range(n_mla)])
    g_mla_w_out = jnp.stack([rm[2 * s + 1]["w_out"] for s in range(n_mla)])
    g_ffn_w_up = jnp.stack([rf[i]["w_up_t"].T for i in range(DEPTH)])
    g_ffn_w_down = jnp.stack([rf[i]["w_down"] for i in range(DEPTH)])

    small_g = [
        jnp.stack([gm_small[2 * s]["ln_g"][0] for s in range(n_gm)]),
        jnp.stack([gm_small[2 * s]["ln_b"][0] for s in range(n_gm)]),
        jnp.stack([gm_small[2 * s]["w_s"] for s in range(n_gm)]),
        jnp.stack([gm_small[2 * s]["b_s"][:, :, 0] for s in range(n_gm)]),
        jnp.stack([gf_small[i]["cb"][0] for i in range(DEPTH)]),
        jnp.stack([gln[i]["mix_g"][0] for i in range(DEPTH)]),
        jnp.stack([gln[i]["mix_b"][0] for i in range(DEPTH)]),
        jnp.stack([gln[i]["ffn_g"][0] for i in range(DEPTH)]),
        jnp.stack([gln[i]["ffn_b"][0] for i in range(DEPTH)]),
        jnp.stack([gm_small[2 * s + 1]["gq"][0] for s in range(n_mla)]),
        jnp.stack([gm_small[2 * s + 1]["gkv"][0] for s in range(n_mla)]),
        jnp.stack([gf_small[i]["cw"] for i in range(DEPTH)]),
    ]
    small_shapes = [g.shape for g in small_g]
    (small_g_all,) = _all_gather([_pack(small_g)], name="ag_small_grads")
    small_sum = _unpack(_sum8(small_g_all, name="small_grad_sum"), small_shapes)
    (g_gm_ln_g, g_gm_ln_b, g_gm_w_s, g_gm_b_s, g_ffn_conv_b, g_ln_mix_g, g_ln_mix_b, g_ln_ffn_g, g_ln_ffn_b,
     gq_all, gkv_all, cw_all) = small_sum
    qn_l = mla_q_norm_g.shape[1]
    g_mla_q_norm_g = lax.dynamic_slice_in_dim(gq_all, dev * qn_l, qn_l, axis=1)
    g_mla_kv_norm_g = lax.dynamic_slice_in_dim(gkv_all, dev * qn_l, qn_l, axis=1)
    g_ffn_conv_w = lax.dynamic_slice_in_dim(cw_all, dev * cw_l, cw_l, axis=2)

    def adam_big(w, g, m, v, tag):
        sh = w.shape
        two = lambda a: a.reshape((-1, sh[-1]))
        d, mn, vn = _adamw(two(w), two(g), two(m), two(v), name=f"adamw_{tag}")
        return d.reshape(sh), mn.reshape(sh), vn.reshape(sh)

    big = [("gm_w_in", gm_w_in, g_gm_w_in, m_gm_w_in, v_gm_w_in), ("gm_w_out", gm_w_out, g_gm_w_out, m_gm_w_out, v_gm_w_out),
           ("mla_w_in", mla_w_in, g_mla_w_in, m_mla_w_in, v_mla_w_in), ("mla_w_q_b", mla_w_q_b, g_mla_w_q_b, m_mla_w_q_b, v_mla_w_q_b),
           ("mla_w_kv_b", mla_w_kv_b, g_mla_w_kv_b, m_mla_w_kv_b, v_mla_w_kv_b), ("mla_w_out", mla_w_out, g_mla_w_out, m_mla_w_out, v_mla_w_out),
           ("ffn_w_up", ffn_w_up, g_ffn_w_up, m_ffn_w_up, v_ffn_w_up), ("ffn_w_down", ffn_w_down, g_ffn_w_down, m_ffn_w_down, v_ffn_w_down)]
    res = {}
    for tag, w, g, m, v in big:
        res[tag] = (g,) + adam_big(w, g, m, v, tag)

    small = [("gm_ln_g", gm_ln_g, g_gm_ln_g, m_gm_ln_g, v_gm_ln_g), ("gm_ln_b", gm_ln_b, g_gm_ln_b, m_gm_ln_b, v_gm_ln_b),
             ("gm_w_s", gm_w_s, g_gm_w_s, m_gm_w_s, v_gm_w_s), ("gm_b_s", gm_b_s, g_gm_b_s, m_gm_b_s, v_gm_b_s),
             ("mla_q_norm_g", mla_q_norm_g, g_mla_q_norm_g, m_mla_q_norm_g, v_mla_q_norm_g),
             ("mla_kv_norm_g", mla_kv_norm_g, g_mla_kv_norm_g, m_mla_kv_norm_g, v_mla_kv_norm_g),
             ("ffn_conv_w", ffn_conv_w, g_ffn_conv_w, m_ffn_conv_w, v_ffn_conv_w), ("ffn_conv_b", ffn_conv_b, g_ffn_conv_b, m_ffn_conv_b, v_ffn_conv_b),
             ("ln_mix_g", ln_mix_g, g_ln_mix_g, m_ln_mix_g, v_ln_mix_g), ("ln_mix_b", ln_mix_b, g_ln_mix_b, m_ln_mix_b, v_ln_mix_b),
             ("ln_ffn_g", ln_ffn_g, g_ln_ffn_g, m_ln_ffn_g, v_ln_ffn_g), ("ln_ffn_b", ln_ffn_b, g_ln_ffn_b, m_ln_ffn_b, v_ln_ffn_b)]
    shapes = [t[1].shape for t in small]
    d_s, m_s, v_s = _adamw(_pack([t[1] for t in small]), _pack([t[2] for t in small]), _pack([t[3] for t in small]),
                           _pack([t[4] for t in small]), name="adamw_small")
    d_l, m_l, v_l = _unpack(d_s, shapes), _unpack(m_s, shapes), _unpack(v_s, shapes)
    for (tag, _, g, _, _), d, mn, vn in zip(small, d_l, m_l, v_l):
        res[tag] = (g, d, mn, vn)

    order = ["gm_w_in", "gm_ln_g", "gm_ln_b", "gm_w_s", "gm_b_s", "gm_w_out", "mla_w_in", "mla_q_norm_g", "mla_kv_norm_g",
             "mla_w_q_b", "mla_w_kv_b", "mla_w_out", "ffn_w_up", "ffn_conv_w", "ffn_conv_b", "ffn_w_down",
             "ln_mix_g", "ln_mix_b", "ln_ffn_g", "ln_ffn_b"]
    out = [loss, grad_x[None]]
    for q in range(4):
        out += [res[k][q] for k in order]
    return tuple(out)
```

```python
import jax
import jax.numpy as jnp
from jax import lax
from jax.experimental import pallas as pl
from jax.experimental.pallas import tpu as pltpu
from jax.experimental.pallas import tpu_sc as plsc

F32, BF16 = jnp.float32, jnp.bfloat16

DEPTH = 4
CHUNK = 64
GM_BLOCK = 128
GM_GROUPS = 8
HEADS = 16
NOPE, ROPE, VDIM = 128, 64, 128
QRANK, KVRANK = 512, 512
ROPE_THETA = 10000.0
SM_SCALE = (NOPE + ROPE) ** -0.5
ALPHA = (2 * DEPTH) ** 0.25
LN_EPS = 1e-5
RMS_EPS = 1e-6
ADAM_LR, ADAM_B1, ADAM_B2, ADAM_EPS, ADAM_WD, ADAM_STEP = 0.001, 0.9, 0.999, 1e-08, 0.01, 10

N_DEV = 8
LANES = 128
VMEM_LIMIT = 56 * 1024 * 1024
MESH = pl.DeviceIdType.MESH
HBM_SPEC = pl.BlockSpec(memory_space=pltpu.HBM)


def _call(body, *, name, out_shape, in_specs, out_specs, grid=None, scratch=(), dims=None, grid_spec=None):
    kw = dict(vmem_limit_bytes=VMEM_LIMIT)
    if dims is not None:
        kw["dimension_semantics"] = dims
    cp = pltpu.CompilerParams(**kw)
    if grid_spec is not None:
        return pl.pallas_call(body, name=name, grid_spec=grid_spec, out_shape=out_shape, compiler_params=cp, interpret=False)
    extra = {} if grid is None else {"grid": grid}
    return pl.pallas_call(body, name=name, in_specs=in_specs, out_specs=out_specs, out_shape=out_shape,
                          scratch_shapes=list(scratch), compiler_params=cp, interpret=False, **extra)


def _tile(n, pref, mult=LANES):
    if n <= pref:
        return n
    t = (pref // mult) * mult
    while t >= mult:
        if n % t == 0:
            return t
        t -= mult
    return n


def _sds(shape, dtype):
    return jax.ShapeDtypeStruct(tuple(shape), dtype)


_DN = {"nn": (((1,), (0,)), ((), ())), "nt": (((1,), (1,)), ((), ())), "tn": (((0,), (0,)), ((), ()))}


def _mm(a, b, mode, *, name, out_dtype, tm=1024, tn=1024, tk=2048, res=None, res_scale=1.0):
    if mode == "nn":
        (M, K), (K2, N) = a.shape, b.shape
    elif mode == "nt":
        (M, K), (N, K2) = a.shape, b.shape
    else:
        (K, M), (K2, N) = a.shape, b.shape
    assert K == K2, (name, a.shape, b.shape)
    tm, tn, tk = _tile(M, tm), _tile(N, tn), _tile(K, tk)
    nk = K // tk
    a_spec = pl.BlockSpec((tk, tm), lambda i, j, k: (k, i)) if mode == "tn" else pl.BlockSpec((tm, tk), lambda i, j, k: (i, k))
    b_spec = pl.BlockSpec((tn, tk), lambda i, j, k: (j, k)) if mode == "nt" else pl.BlockSpec((tk, tn), lambda i, j, k: (k, j))
    in_specs = [a_spec, b_spec]
    args = [a, b]
    if res is not None:
        in_specs.append(pl.BlockSpec((tm, tn), lambda i, j, k: (i, j)))
        args.append(res)
    dn = _DN[mode]
    has_res = res is not None

    def body(*refs):
        a_ref, b_ref = refs[0], refs[1]
        r_ref = refs[2] if has_res else None
        o_ref = refs[2 + has_res]
        part = lax.dot_general(a_ref[...], b_ref[...], dn, preferred_element_type=F32)

        def finish(acc):
            if has_res:
                acc = acc + res_scale * r_ref[...]
            o_ref[...] = acc.astype(o_ref.dtype)

        if nk == 1:
            finish(part)
        else:
            acc_ref = refs[3 + has_res]
            k = pl.program_id(2)

            @pl.when(k == 0)
            def _():
                acc_ref[...] = part

            @pl.when(k > 0)
            def _():
                acc_ref[...] += part

            @pl.when(k == nk - 1)
            def _():
                finish(acc_ref[...])

    scratch = [pltpu.VMEM((tm, tn), F32)] if nk > 1 else []
    return _call(body, name=name, grid=(M // tm, N // tn, nk), in_specs=in_specs,
                 out_specs=pl.BlockSpec((tm, tn), lambda i, j, k: (i, j)), out_shape=_sds((M, N), out_dtype),
                 scratch=scratch, dims=("parallel", "parallel", "arbitrary"))(*args)


def _ln_fwd(x, m, g, b, *, name):
    S, D = x.shape
    tr = _tile(S, 256, 8)

    def body(x_ref, m_ref, g_ref, b_ref, y_ref, yb_ref, xh_ref, rs_ref):
        r = ALPHA * x_ref[...] + m_ref[...]
        mu = jnp.mean(r, axis=-1, keepdims=True)
        d = r - mu
        var = jnp.mean(d * d, axis=-1, keepdims=True)
        rstd = lax.rsqrt(var + LN_EPS)
        xh = d * rstd
        y = xh * g_ref[...] + b_ref[...]
        y_ref[...] = y
        yb_ref[...] = y.astype(BF16)
        xh_ref[...] = xh
        rs_ref[...] = rstd

    row = pl.BlockSpec((tr, D), lambda i: (i, 0))
    vec = pl.BlockSpec((1, D), lambda i: (0, 0))
    return _call(body, name=name, grid=(S // tr,), in_specs=[row, row, vec, vec],
                 out_specs=[row, row, row, pl.BlockSpec((tr, 1), lambda i: (i, 0))],
                 out_shape=[_sds((S, D), F32), _sds((S, D), BF16), _sds((S, D), F32), _sds((S, 1), F32)],
                 dims=("parallel",))(x, m, g, b)


def _ln_bwd(dy, xh, rstd, g, *, name):
    S, D = dy.shape
    tr = _tile(S, 256, 8)

    def body(dy_ref, xh_ref, rs_ref, g_ref, dr_ref, drb_ref, dg_ref, db_ref):
        @pl.when(pl.program_id(0) == 0)
        def _():
            dg_ref[...] = jnp.zeros_like(dg_ref)
            db_ref[...] = jnp.zeros_like(db_ref)

        dyv = dy_ref[...]
        xhv = xh_ref[...]
        dxh = dyv * g_ref[...]
        m1 = jnp.mean(dxh, axis=-1, keepdims=True)
        m2 = jnp.mean(dxh * xhv, axis=-1, keepdims=True)
        dr = rs_ref[...] * (dxh - m1 - xhv * m2)
        dr_ref[...] = dr
        drb_ref[...] = dr.astype(BF16)
        dg_ref[...] += jnp.sum(dyv * xhv, axis=0, keepdims=True)
        db_ref[...] += jnp.sum(dyv, axis=0, keepdims=True)

    row = pl.BlockSpec((tr, D), lambda i: (i, 0))
    vec = pl.BlockSpec((1, D), lambda i: (0, 0))
    return _call(body, name=name, grid=(S // tr,), in_specs=[row, row, pl.BlockSpec((tr, 1), lambda i: (i, 0)), vec],
                 out_specs=[row, row, vec, vec],
                 out_shape=[_sds((S, D), F32), _sds((S, D), BF16), _sds((1, D), F32), _sds((1, D), F32)],
                 dims=("arbitrary",))(dy, xh, rstd, g)


def _loss_kernel(y, t, *, name):
    S, D = y.shape
    tr = _tile(S, 256, 8)

    def body(y_ref, t_ref, lp_ref, dy_ref):
        @pl.when(pl.program_id(0) == 0)
        def _():
            lp_ref[...] = jnp.zeros_like(lp_ref)

        e = y_ref[...] - t_ref[...]
        dy_ref[...] = e / D
        lp_ref[...] += jnp.sum(e * e, axis=0, keepdims=True)

    row = pl.BlockSpec((tr, D), lambda i: (i, 0))
    vec = pl.BlockSpec((1, D), lambda i: (0, 0))
    return _call(body, name=name, grid=(S // tr,), in_specs=[row, row], out_specs=[vec, row],
                 out_shape=[_sds((1, D), F32), _sds((S, D), F32)], dims=("arbitrary",))(y, t)


_GELU_C = 0.7978845608028654
_GELU_A = 0.044715


def _gelu(x):
    return 0.5 * x * (1.0 + jnp.tanh(_GELU_C * (x + _GELU_A * x * x * x)))


def _gelu_grad(x):
    x2 = x * x
    t = jnp.tanh(_GELU_C * (x + _GELU_A * x * x2))
    return 0.5 * (1.0 + t) + 0.5 * x * (1.0 - t * t) * (_GELU_C * (1.0 + 3.0 * _GELU_A * x2))


def _masked_ws(w):
    i = lax.broadcasted_iota(jnp.int32, w.shape, 0) // CHUNK
    j = lax.broadcasted_iota(jnp.int32, w.shape, 1) // CHUNK
    return jnp.where(j <= i, w, 0.0)


def _gm_mid_fwd(zp, ln_g, ln_b, w_s, b_st, *, name):
    S, H2 = zp.shape
    H = H2 // 2
    gd = H // GM_GROUPS
    nb = S // GM_BLOCK

    def body(zu_ref, zv_ref, g_ref, b_ref, w_ref, bs_ref, p_ref, s_ref, vh_ref, rs_ref):
        v = _gelu(zv_ref[...].astype(F32))
        mu = jnp.mean(v, axis=-1, keepdims=True)
        d = v - mu
        var = jnp.mean(d * d, axis=-1, keepdims=True)
        rstd = lax.rsqrt(var + LN_EPS)
        vh = d * rstd
        vh_ref[...] = vh.astype(BF16)
        rs_ref[...] = rstd
        vn = (vh * g_ref[...] + b_ref[...]).astype(BF16)
        bs = bs_ref[...]
        for gi in range(GM_GROUPS):
            cs = slice(gi * gd, (gi + 1) * gd)
            wm = _masked_ws(w_ref[gi]).astype(BF16)
            s = jnp.dot(wm, vn[:, cs], preferred_element_type=F32) + bs[:, gi:gi + 1]
            u = _gelu(zu_ref[:, cs].astype(F32))
            s_ref[:, cs] = s.astype(BF16)
            p_ref[:, cs] = (u * s).astype(BF16)

    blk = lambda c: pl.BlockSpec((GM_BLOCK, H), lambda n, c=c: (n, c))
    vec = pl.BlockSpec((1, H), lambda n: (0, 0))
    return _call(body, name=name, grid=(nb,),
                 in_specs=[blk(0), blk(1), vec, vec, pl.BlockSpec((GM_GROUPS, GM_BLOCK, GM_BLOCK), lambda n: (0, 0, 0)),
                           pl.BlockSpec((GM_BLOCK, GM_GROUPS), lambda n: (0, 0))],
                 out_specs=[blk(0), blk(0), blk(0), pl.BlockSpec((GM_BLOCK, 1), lambda n: (n, 0))],
                 out_shape=[_sds((S, H), BF16), _sds((S, H), BF16), _sds((S, H), BF16), _sds((S, 1), F32)],
                 dims=("parallel",))(zp, zp, ln_g, ln_b, w_s, b_st)


def _gm_mid_bwd(dp, zp, s, vhat, rstd, ln_g, ln_b, w_s, *, name):
    S, H2 = zp.shape
    H = H2 // 2
    gd = H // GM_GROUPS
    nb = S // GM_BLOCK

    def body(dp_ref, zu_ref, zv_ref, s_ref, vh_ref, rs_ref, g_ref, b_ref, w_ref,
             dz_ref, dw_ref, dbs_ref, dg_ref, db_ref, dvh_ref):
        @pl.when(pl.program_id(0) == 0)
        def _():
            dw_ref[...] = jnp.zeros_like(dw_ref)
            dbs_ref[...] = jnp.zeros_like(dbs_ref)
            dg_ref[...] = jnp.zeros_like(dg_ref)
            db_ref[...] = jnp.zeros_like(db_ref)

        m1 = jnp.zeros((GM_BLOCK, 1), F32)
        m2 = jnp.zeros((GM_BLOCK, 1), F32)
        for gi in range(GM_GROUPS):
            cs = slice(gi * gd, (gi + 1) * gd)
            dpg = dp_ref[:, cs].astype(F32)
            zu = zu_ref[:, cs].astype(F32)
            u = _gelu(zu)
            ds = dpg * u
            du = dpg * s_ref[:, cs].astype(F32)
            dz_ref[:, cs] = (du * _gelu_grad(zu)).astype(BF16)
            dsb = ds.astype(BF16)
            vh = vh_ref[:, cs].astype(F32)
            lg = g_ref[:, cs]
            vn = (vh * lg + b_ref[:, cs]).astype(BF16)
            wm = _masked_ws(w_ref[gi]).astype(BF16)
            dvn = lax.dot_general(wm, dsb, _DN["tn"], preferred_element_type=F32)
            dw_ref[gi] += lax.dot_general(dsb, vn, _DN["nt"], preferred_element_type=F32)
            dbs_ref[gi] += jnp.sum(ds, axis=1, keepdims=True)
            dg_ref[:, cs] += jnp.sum(dvn * vh, axis=0, keepdims=True)
            db_ref[:, cs] += jnp.sum(dvn, axis=0, keepdims=True)
            dvh = dvn * lg
            dvh_ref[:, cs] = dvh
            m1 = m1 + jnp.sum(dvh, axis=1, keepdims=True)
            m2 = m2 + jnp.sum(dvh * vh, axis=1, keepdims=True)
        m1 = m1 / H
        m2 = m2 / H
        rs = rs_ref[...]
        for gi in range(GM_GROUPS):
            cs = slice(gi * gd, (gi + 1) * gd)
            vh = vh_ref[:, cs].astype(F32)
            dv = rs * (dvh_ref[:, cs] - m1 - vh * m2)
            zv = zv_ref[:, cs].astype(F32)
            dz_ref[:, H + gi * gd:H + (gi + 1) * gd] = (dv * _gelu_grad(zv)).astype(BF16)

        @pl.when(pl.program_id(0) == nb - 1)
        def _():
            for gi in range(GM_GROUPS):
                dw_ref[gi] = _masked_ws(dw_ref[gi])

    blk = lambda c: pl.BlockSpec((GM_BLOCK, H), lambda n, c=c: (n, c))
    vec = pl.BlockSpec((1, H), lambda n: (0, 0))
    wspec = pl.BlockSpec((GM_GROUPS, GM_BLOCK, GM_BLOCK), lambda n: (0, 0, 0))
    return _call(body, name=name, grid=(nb,),
                 in_specs=[blk(0), blk(0), blk(1), blk(0), blk(0), pl.BlockSpec((GM_BLOCK, 1), lambda n: (n, 0)), vec, vec, wspec],
                 out_specs=[pl.BlockSpec((GM_BLOCK, H2), lambda n: (n, 0)), wspec,
                            pl.BlockSpec((GM_GROUPS, GM_BLOCK, 1), lambda n: (0, 0, 0)), vec, vec],
                 out_shape=[_sds((S, H2), BF16), _sds((GM_GROUPS, GM_BLOCK, GM_BLOCK), F32),
                            _sds((GM_GROUPS, GM_BLOCK, 1), F32), _sds((1, H), F32), _sds((1, H), F32)],
                 scratch=[pltpu.VMEM((GM_BLOCK, H), F32)], dims=("arbitrary",))(dp, zp, zp, s, vhat, rstd, ln_g, ln_b, w_s)


def _shift_down(x, k):
    rows = lax.broadcasted_iota(jnp.int32, x.shape, 0)
    return jnp.where(rows >= k, pltpu.roll(x, k, 0), 0.0)


def _shift_up(x, k):
    n = x.shape[0]
    rows = lax.broadcasted_iota(jnp.int32, x.shape, 0)
    return jnp.where(rows < n - k, pltpu.roll(x, n - k, 0), 0.0)


def _conv(h, w, b):
    return w[0:1, :] * _shift_down(h, 2) + w[1:2, :] * _shift_down(h, 1) + w[2:3, :] * h + b


def _ffn_mid_fwd(h, cw, cb, *, name):
    S, F2 = h.shape
    F = F2 // 2
    nf = F // LANES

    def body(ha_ref, hg_ref, wa_ref, wg_ref, ba_ref, bg_ref, act_ref):
        a = _conv(ha_ref[...].astype(F32), wa_ref[...], ba_ref[...])
        g = _conv(hg_ref[...].astype(F32), wg_ref[...], bg_ref[...])
        act_ref[...] = (g * jax.nn.sigmoid(g) * a).astype(BF16)

    sl = lambda off, r: pl.BlockSpec((r, LANES), lambda j, off=off: (0, j + off))
    return _call(body, name=name, grid=(nf,),
                 in_specs=[sl(0, S), sl(nf, S), sl(0, 3), sl(nf, 3), sl(0, 1), sl(nf, 1)],
                 out_specs=sl(0, S), out_shape=_sds((S, F), BF16), dims=("parallel",))(h, h, cw, cw, cb, cb)


def _ffn_mid_bwd(dact, h, cw, cb, *, name):
    S, F2 = h.shape
    F = F2 // 2
    nf = F // LANES

    def body(da_ref, ha_ref, hg_ref, wa_ref, wg_ref, ba_ref, bg_ref, dh_ref, dw_ref, db_ref):
        half = pl.program_id(0)
        dact_v = da_ref[...].astype(F32)
        hg = hg_ref[...].astype(F32)
        g = _conv(hg, wg_ref[...], bg_ref[...])
        sg = jax.nn.sigmoid(g)

        def emit(dc, hin, w):
            dh_ref[...] = (w[2:3, :] * dc + w[1:2, :] * _shift_up(dc, 1) + w[0:1, :] * _shift_up(dc, 2)).astype(BF16)
            dw_ref[0:1, :] = jnp.sum(dc * _shift_down(hin, 2), axis=0, keepdims=True)
            dw_ref[1:2, :] = jnp.sum(dc * _shift_down(hin, 1), axis=0, keepdims=True)
            dw_ref[2:3, :] = jnp.sum(dc * hin, axis=0, keepdims=True)
            db_ref[...] = jnp.sum(dc, axis=0, keepdims=True)

        @pl.when(half == 0)
        def _():
            emit(dact_v * (g * sg), ha_ref[...].astype(F32), wa_ref[...])

        @pl.when(half == 1)
        def _():
            a = _conv(ha_ref[...].astype(F32), wa_ref[...], ba_ref[...])
            emit(dact_v * a * (sg * (1.0 + g * (1.0 - sg))), hg, wg_ref[...])

    sl = lambda off, r: pl.BlockSpec((r, LANES), lambda hf, j, off=off: (0, j + off))
    out = lambda r: pl.BlockSpec((r, LANES), lambda hf, j: (0, hf * nf + j))
    return _call(body, name=name, grid=(2, nf),
                 in_specs=[sl(0, S), sl(0, S), sl(nf, S), sl(0, 3), sl(nf, 3), sl(0, 1), sl(nf, 1)],
                 out_specs=[out(S), out(3), out(1)],
                 out_shape=[_sds((S, F2), BF16), _sds((3, F2), F32), _sds((1, F2), F32)],
                 dims=("parallel", "parallel"))(dact, h, h, cw, cw, cb, cb)


def _swap_half(x):
    lane = lax.broadcasted_iota(jnp.int32, x.shape, 1)
    return jnp.where((lane % ROPE) < ROPE // 2, pltpu.roll(x, LANES - ROPE // 2, 1), pltpu.roll(x, ROPE // 2, 1))


def _rope(x, cos, sin_s):
    return x * cos + _swap_half(x) * sin_s


def _mla_prep(h, gq, gkv, cos, sin_s, *, name):
    S, W = h.shape
    tr = _tile(S, 512, 8)

    def body(h_ref, gq_ref, gkv_ref, cos_ref, sin_ref, cq_ref, ckv_ref, kr_ref, rq_ref, rkv_ref):
        cq = h_ref[:, 0:QRANK]
        rq = lax.rsqrt(jnp.mean(cq * cq, axis=-1, keepdims=True) + RMS_EPS)
        cq_ref[...] = (cq * rq * gq_ref[...]).astype(BF16)
        rq_ref[...] = rq
        ckv = h_ref[:, QRANK:QRANK + KVRANK]
        rkv = lax.rsqrt(jnp.mean(ckv * ckv, axis=-1, keepdims=True) + RMS_EPS)
        ckv_ref[...] = (ckv * rkv * gkv_ref[...]).astype(BF16)
        rkv_ref[...] = rkv
        kr = _rope(h_ref[:, QRANK + KVRANK:W], cos_ref[...], sin_ref[...])
        lane = lax.broadcasted_iota(jnp.int32, kr.shape, 1)
        kr = jnp.where(lane < ROPE, kr, 0.0)
        kr_ref[...] = (kr + pltpu.roll(kr, ROPE, 1)).astype(BF16)

    row = lambda w: pl.BlockSpec((tr, w), lambda i: (i, 0))
    vec = lambda w: pl.BlockSpec((1, w), lambda i: (0, 0))
    return _call(body, name=name, grid=(S // tr,),
                 in_specs=[row(W), vec(QRANK), vec(KVRANK), row(LANES), row(LANES)],
                 out_specs=[row(QRANK), row(KVRANK), row(LANES), row(1), row(1)],
                 out_shape=[_sds((S, QRANK), BF16), _sds((S, KVRANK), BF16), _sds((S, LANES), BF16), _sds((S, 1), F32), _sds((S, 1), F32)],
                 dims=("parallel",))(h, gq, gkv, cos, sin_s)


def _mla_prep_bwd(dcqn, dckvn, dkr2, h, rq, rkv, gq, gkv, cos, sin_neg, *, name):
    S, W = h.shape
    tr = _tile(S, 512, 8)

    def rms_bwd(dy, c, r, g):
        n = c * r
        dn = dy * g
        return r * (dn - n * jnp.mean(dn * n, axis=-1, keepdims=True)), jnp.sum(dy * n, axis=0, keepdims=True)

    def body(dq_ref, dkv_ref, dkr_ref, h_ref, rq_ref, rkv_ref, gq_ref, gkv_ref, cos_ref, sin_ref, dh_ref, dgq_ref, dgkv_ref):
        @pl.when(pl.program_id(0) == 0)
        def _():
            dgq_ref[...] = jnp.zeros_like(dgq_ref)
            dgkv_ref[...] = jnp.zeros_like(dgkv_ref)

        dcq, dg = rms_bwd(dq_ref[...], h_ref[:, 0:QRANK], rq_ref[...], gq_ref[...])
        dgq_ref[...] += dg
        dh_ref[:, 0:QRANK] = dcq.astype(BF16)
        dckv, dg = rms_bwd(dkv_ref[...], h_ref[:, QRANK:QRANK + KVRANK], rkv_ref[...], gkv_ref[...])
        dgkv_ref[...] += dg
        dh_ref[:, QRANK:QRANK + KVRANK] = dckv.astype(BF16)
        dk = dkr_ref[...]
        dk = dk + pltpu.roll(dk, ROPE, 1)
        dk = _rope(dk, cos_ref[...], sin_ref[...])
        lane = lax.broadcasted_iota(jnp.int32, dk.shape, 1)
        dh_ref[:, QRANK + KVRANK:W] = jnp.where(lane < ROPE, dk, 0.0).astype(BF16)

    row = lambda w: pl.BlockSpec((tr, w), lambda i: (i, 0))
    vec = lambda w: pl.BlockSpec((1, w), lambda i: (0, 0))
    return _call(body, name=name, grid=(S // tr,),
                 in_specs=[row(QRANK), row(KVRANK), row(LANES), row(W), row(1), row(1), vec(QRANK), vec(KVRANK), row(LANES), row(LANES)],
                 out_specs=[row(W), vec(QRANK), vec(KVRANK)],
                 out_shape=[_sds((S, W), BF16), _sds((1, QRANK), F32), _sds((1, KVRANK), F32)],
                 dims=("arbitrary",))(dcqn, dckvn, dkr2, h, rq, rkv, gq, gkv, cos, sin_neg)


QPAIR = 2 * NOPE + 2 * ROPE
KVPAIR = 2 * (NOPE + VDIM)


def _rope_q(q, cos, sin_s, *, name):
    S, Wq = q.shape
    tr = _tile(S, 512, 8)

    def body(q_ref, cos_ref, sin_ref, o_ref):
        o_ref[:, 0:2 * NOPE] = q_ref[:, 0:2 * NOPE].astype(BF16)
        o_ref[:, 2 * NOPE:QPAIR] = _rope(q_ref[:, 2 * NOPE:QPAIR], cos_ref[...], sin_ref[...]).astype(BF16)

    blk = pl.BlockSpec((tr, QPAIR), lambda i, p: (i, p))
    tab = pl.BlockSpec((tr, LANES), lambda i, p: (i, 0))
    return _call(body, name=name, grid=(S // tr, Wq // QPAIR), in_specs=[blk, tab, tab], out_specs=blk,
                 out_shape=_sds((S, Wq), BF16), dims=("parallel", "parallel"))(q, cos, sin_s)


ATT_T = 512


def _att_scores(qn, rm, kn, kr, q0, k0, masked):
    s = (lax.dot_general(qn, kn, _DN["nt"], preferred_element_type=F32)
         + lax.dot_general(rm, kr, _DN["nt"], preferred_element_type=F32)) * SM_SCALE
    if masked:
        qpos = q0 + lax.broadcasted_iota(jnp.int32, s.shape, 0)
        kpos = k0 + lax.broadcasted_iota(jnp.int32, s.shape, 1)
        s = jnp.where((kpos // CHUNK) <= (qpos // CHUNK), s, -jnp.inf)
    return s


def _head_mask(r, hh):
    lane = lax.broadcasted_iota(jnp.int32, r.shape, 1)
    keep = (lane < ROPE) if hh == 0 else (lane >= ROPE)
    return jnp.where(keep, r, jnp.zeros_like(r))


def _attn_fwd(qb, kv, kr2, *, name):
    S = qb.shape[0]
    t = _tile(S, ATT_T, 8)
    nq = S // t
    npair = HEADS // 2

    def body(q_ref, kv_ref, kr_ref, o_ref, lse_ref):
        i = pl.program_id(1)
        q0 = i * t
        r = q_ref[:, 2 * NOPE:QPAIR]
        for hh in range(2):
            qn = q_ref[:, hh * NOPE:(hh + 1) * NOPE]
            rm = _head_mask(r, hh)
            kc = hh * (NOPE + VDIM)

            def step(j, carry, masked):
                m, l, acc = carry
                k0 = pl.multiple_of(j * t, t)
                kn = kv_ref[pl.ds(k0, t), kc:kc + NOPE]
                v = kv_ref[pl.ds(k0, t), kc + NOPE:kc + NOPE + VDIM]
                kr = kr_ref[pl.ds(k0, t), :]
                s = _att_scores(qn, rm, kn, kr, q0, k0, masked)
                m_new = jnp.maximum(m, jnp.max(s, axis=1, keepdims=True))
                p = jnp.exp(s - m_new)
                a = jnp.exp(m - m_new)
                l = a * l + jnp.sum(p, axis=1, keepdims=True)
                acc = a * acc + jnp.dot(p.astype(BF16), v, preferred_element_type=F32)
                return m_new, l, acc

            init = (jnp.full((t, 1), -jnp.inf, F32), jnp.zeros((t, 1), F32), jnp.zeros((t, VDIM), F32))
            carry = lax.fori_loop(0, i, lambda j, c: step(j, c, False), init)
            m, l, acc = step(i, carry, True)
            o_ref[:, hh * VDIM:(hh + 1) * VDIM] = (acc / l).astype(BF16)
            lse_ref[:, hh * VDIM:(hh + 1) * VDIM] = jnp.broadcast_to(m + jnp.log(l), (t, VDIM))

    return _call(body, name=name, grid=(npair, nq),
                 in_specs=[pl.BlockSpec((t, QPAIR), lambda p, i: (i, p)), pl.BlockSpec((S, KVPAIR), lambda p, i: (0, p)),
                           pl.BlockSpec((S, LANES), lambda p, i: (0, 0))],
                 out_specs=[pl.BlockSpec((t, 2 * VDIM), lambda p, i: (i, p)), pl.BlockSpec((t, 2 * VDIM), lambda p, i: (i, p))],
                 out_shape=[_sds((S, HEADS * VDIM), BF16), _sds((S, HEADS * VDIM), F32)],
                 dims=("parallel", "arbitrary"))(qb, kv, kr2)


def _attn_bwd(qb, kv, kr2, o, do, lse, *, name):
    S = qb.shape[0]
    t = _tile(S, ATT_T, 8)
    nq = S // t
    npair = HEADS // 2

    def body(q_ref, kv_ref, kr_ref, o_ref, do_ref, lse_ref, dq_ref, dkv_ref, dkr_ref, acc_ref):
        pid = pl.program_id(0)
        i = pl.program_id(1)
        q0 = i * t

        @pl.when(i == 0)
        def _():
            acc_ref[...] = jnp.zeros_like(acc_ref)

        @pl.when((i == 0) & (pid == 0))
        def _():
            dkr_ref[...] = jnp.zeros_like(dkr_ref)

        r = q_ref[:, 2 * NOPE:QPAIR]
        dr_tot = jnp.zeros((t, LANES), F32)
        for hh in range(2):
            qn = q_ref[:, hh * NOPE:(hh + 1) * NOPE]
            rm = _head_mask(r, hh)
            kc = hh * (NOPE + VDIM)
            dov = do_ref[:, hh * VDIM:(hh + 1) * VDIM]
            dsum = jnp.sum(dov.astype(F32) * o_ref[:, hh * VDIM:(hh + 1) * VDIM].astype(F32), axis=1, keepdims=True)
            lse_v = lse_ref[:, hh * VDIM:hh * VDIM + 1]

            def step(j, carry, masked):
                dqn, drr = carry
                k0 = pl.multiple_of(j * t, t)
                rows = pl.ds(k0, t)
                kn = kv_ref[rows, kc:kc + NOPE]
                v = kv_ref[rows, kc + NOPE:kc + NOPE + VDIM]
                kr = kr_ref[rows, :]
                s = _att_scores(qn, rm, kn, kr, q0, k0, masked)
                p = jnp.exp(s - lse_v)
                dp = lax.dot_general(dov, v, _DN["nt"], preferred_element_type=F32)
                ds = (p * (dp - dsum) * SM_SCALE).astype(BF16)
                acc_ref[rows, kc + NOPE:kc + NOPE + VDIM] += lax.dot_general(p.astype(BF16), dov, _DN["tn"], preferred_element_type=F32)
                acc_ref[rows, kc:kc + NOPE] += lax.dot_general(ds, qn, _DN["tn"], preferred_element_type=F32)
                dkr_ref[rows, :] += lax.dot_general(ds, rm, _DN["tn"], preferred_element_type=F32)
                dqn = dqn + jnp.dot(ds, kn, preferred_element_type=F32)
                drr = drr + jnp.dot(ds, kr, preferred_element_type=F32)
                return dqn, drr

            init = (jnp.zeros((t, NOPE), F32), jnp.zeros((t, LANES), F32))
            carry = lax.fori_loop(0, i, lambda j, c: step(j, c, False), init)
            dqn, drr = step(i, carry, True)
            dq_ref[:, hh * NOPE:(hh + 1) * NOPE] = dqn
            dr_tot = dr_tot + _head_mask(drr, hh)
        dq_ref[:, 2 * NOPE:QPAIR] = dr_tot

        @pl.when(i == nq - 1)
        def _():
            dkv_ref[...] = acc_ref[...].astype(BF16)

    qspec = pl.BlockSpec((t, QPAIR), lambda p, i: (i, p))
    hspec = pl.BlockSpec((t, 2 * VDIM), lambda p, i: (i, p))
    kvspec = pl.BlockSpec((S, KVPAIR), lambda p, i: (0, p))
    krspec = pl.BlockSpec((S, LANES), lambda p, i: (0, 0))
    return _call(body, name=name, grid=(npair, nq),
                 in_specs=[qspec, kvspec, krspec, hspec, hspec, hspec],
                 out_specs=[qspec, kvspec, krspec],
                 out_shape=[_sds(qb.shape, F32), _sds(kv.shape, BF16), _sds((S, LANES), F32)],
                 scratch=[pltpu.VMEM((S, KVPAIR), F32)], dims=("arbitrary", "arbitrary"))(qb, kv, kr2, o, do, lse)


def _coords():
    return lax.axis_index("x"), lax.axis_index("y"), lax.axis_index("c")


def _all_gather_body(n, shake):
    def body(*refs):
        ins, outs = refs[:n], refs[n:2 * n]
        send_sems, recv_sems, local_sems = refs[2 * n:]
        x, y, c = _coords()
        me, sib = (x, y, c), (x, y, 1 - c)
        chips = [(1 - x, y), (x, 1 - y), (1 - x, 1 - y)]
        if shake:
            _handshake([sib] + [(*chip, c) for chip in chips])

        def copy(a, k, block, to, src=None):
            dst = outs[a].at[4 * block[0] + 2 * block[1] + block[2]]
            return pltpu.make_async_remote_copy(src_ref=dst if src is None else src, dst_ref=dst,
                                                send_sem=send_sems.at[a, k], recv_sem=recv_sems.at[a, k],
                                                device_id=to, device_id_type=MESH)

        mine = [pltpu.make_async_copy(ins[a], outs[a].at[4 * x + 2 * y + c], local_sems.at[a]) for a in range(n)]
        for cp in mine:
            cp.start()
        sends = []
        for a in range(n):
            sends.append(copy(a, 0, me, sib, src=ins[a]))
            sends += [copy(a, 1 + j, me, (*chip, c), src=ins[a]) for j, chip in enumerate(chips)]
        for cp in sends:
            cp.start()
        for j, chip in enumerate(chips):
            for a in range(n):
                copy(a, 1 + j, (*chip, c), me).wait_recv()
                fwd = copy(a, 4 + j, (*chip, c), sib)
                fwd.start()
                sends.append(fwd)
        for a in range(n):
            copy(a, 0, sib, me).wait_recv()
            for j, chip in enumerate(chips):
                copy(a, 4 + j, (*chip, 1 - c), me).wait_recv()
        for cp in sends:
            cp.wait_send()
        for cp in mine:
            cp.wait()

    return body


def _all_gather(arrs, *, name):
    n = len(arrs)
    outs = _call(_all_gather_body(n, False), name=name, in_specs=[HBM_SPEC] * n, out_specs=[HBM_SPEC] * n,
                 out_shape=[_sds((N_DEV,) + a.shape, a.dtype) for a in arrs],
                 scratch=[pltpu.SemaphoreType.DMA((n, 7)), pltpu.SemaphoreType.DMA((n, 7)), pltpu.SemaphoreType.DMA((n,))])(*arrs)
    return list(outs)


def _sequencer_call(body, *, name, out_type, n_sems, collective_id):
    return pl.kernel(body, out_type=out_type, mesh=plsc.ScalarSubcoreMesh(axis_name="sq", num_cores=1), name=name,
                     scratch_types=[pltpu.SemaphoreType.DMA(n_sems), pltpu.SemaphoreType.DMA(n_sems), pltpu.SemaphoreType.DMA((n_sems[0],))],
                     compiler_params=pltpu.CompilerParams(collective_id=collective_id))


def _handshake(peers):
    barrier = pltpu.get_barrier_semaphore()
    for p in peers:
        pl.semaphore_signal(barrier, inc=1, device_id=p, device_id_type=MESH)
    pl.semaphore_wait(barrier, len(peers))


AG_ID, PAIR_ID, CHIP_ID = 1, 2, 3


def _all_gather_sc(arrs, *, name):
    n = len(arrs)
    outs = _sequencer_call(_all_gather_body(n, True), name=name, out_type=[_sds((N_DEV,) + a.shape, a.dtype) for a in arrs],
                           n_sems=(n, 7), collective_id=AG_ID)(*arrs)
    return list(outs)


def _pair_exchange_sc(gs, *, name):
    n = len(gs)

    def body(*refs):
        ins, outs = refs[:n], refs[n:2 * n]
        send_sems, recv_sems, _ = refs[2 * n:]
        x, y, c = _coords()
        sib = (x, y, 1 - c)
        _handshake([sib])
        cps = []
        for a in range(n):
            for j in range(4):
                cps.append(pltpu.make_async_remote_copy(src_ref=ins[a].at[2 * j + (1 - c)], dst_ref=outs[a].at[j],
                                                        send_sem=send_sems.at[a, j], recv_sem=recv_sems.at[a, j],
                                                        device_id=sib, device_id_type=MESH))
        for cp in cps:
            cp.start()
        for cp in cps:
            cp.wait()

    outs = _sequencer_call(body, name=name, out_type=[_sds((4,) + g.shape[1:], g.dtype) for g in gs],
                           n_sems=(n, 4), collective_id=PAIR_ID)(*gs)
    return list(outs)


def _chip_exchange_sc(ps, *, name):
    n = len(ps)

    def body(*refs):
        ins, outs = refs[:n], refs[n:2 * n]
        send_sems, recv_sems, _ = refs[2 * n:]
        x, y, c = _coords()
        chips = [(1 - x, y), (x, 1 - y), (1 - x, 1 - y)]
        _handshake([(*chip, c) for chip in chips])
        cps = []
        for a in range(n):
            for r, (px, py) in enumerate(chips):
                cps.append(pltpu.make_async_remote_copy(src_ref=ins[a].at[2 * px + py], dst_ref=outs[a].at[r],
                                                        send_sem=send_sems.at[a, r], recv_sem=recv_sems.at[a, r],
                                                        device_id=(px, py, c), device_id_type=MESH))
        for cp in cps:
            cp.start()
        for cp in cps:
            cp.wait()

    outs = _sequencer_call(body, name=name, out_type=[_sds((3,) + p.shape[1:], p.dtype) for p in ps],
                           n_sems=(n, 3), collective_id=CHIP_ID)(*ps)
    return list(outs)


def _row_tile(R, C, budget=1 << 20, mult=16):
    want = max(mult, budget // (4 * C))
    if R <= want:
        return R
    t = (want // mult) * mult
    while t >= mult:
        if R % t == 0:
            return t
        t -= mult
    return R


def _pair_sum(g, l1, c_idx, *, name):
    _, R, C = g.shape
    tr = _row_tile(R, C)
    g4 = g.reshape(4, 2, R, C)

    def body(c_ref, g_ref, l_ref, o_ref):
        o_ref[...] = (g_ref[...].astype(F32) + l_ref[...].astype(F32)).astype(o_ref.dtype)

    gs = pltpu.PrefetchScalarGridSpec(
        num_scalar_prefetch=1, grid=(4, R // tr),
        in_specs=[pl.BlockSpec((None, None, tr, C), lambda j, r, c_ref: (j, c_ref[0], r, 0)),
                  pl.BlockSpec((None, tr, C), lambda j, r, c_ref: (j, r, 0))],
        out_specs=pl.BlockSpec((None, tr, C), lambda j, r, c_ref: (j, r, 0)))
    return _call(body, name=name, grid_spec=gs, in_specs=None, out_specs=None, out_shape=_sds((4, R, C), g.dtype))(c_idx, g4, l1)


def _chip_sum(p, l2, chip_idx, *, name):
    _, R, C = p.shape
    tr = _row_tile(R, C)

    def body(j_ref, p_ref, l_ref, o_ref):
        o_ref[...] = ((p_ref[...].astype(F32) + l_ref[0].astype(F32)) + l_ref[1].astype(F32)) + l_ref[2].astype(F32)

    gs = pltpu.PrefetchScalarGridSpec(
        num_scalar_prefetch=1, grid=(R // tr,),
        in_specs=[pl.BlockSpec((None, tr, C), lambda r, j_ref: (j_ref[0], r, 0)),
                  pl.BlockSpec((3, tr, C), lambda r, j_ref: (0, r, 0))],
        out_specs=pl.BlockSpec((tr, C), lambda r, j_ref: (r, 0)))
    return _call(body, name=name, grid_spec=gs, in_specs=None, out_specs=None, out_shape=_sds((R, C), F32))(chip_idx, p, l2)


def _sum8(a, *, name):
    _, R, C = a.shape
    tr = _row_tile(R, C, budget=1 << 18, mult=8)

    def body(a_ref, o_ref):
        acc = a_ref[0]
        for k in range(1, N_DEV):
            acc = acc + a_ref[k]
        o_ref[...] = acc

    return _call(body, name=name, grid=(R // tr,), in_specs=[pl.BlockSpec((N_DEV, tr, C), lambda r: (0, r, 0))],
                 out_specs=pl.BlockSpec((tr, C), lambda r: (r, 0)), out_shape=_sds((R, C), F32), dims=("parallel",))(a)


def _adamw(w, g, m, v, *, name):
    R, C = w.shape
    tr = _row_tile(R, C, budget=1 << 20, mult=8)
    c1 = 1.0 - ADAM_B1 ** ADAM_STEP
    c2 = 1.0 - ADAM_B2 ** ADAM_STEP

    def body(w_ref, g_ref, m_ref, v_ref, d_ref, mo_ref, vo_ref):
        gv = g_ref[...]
        mn = ADAM_B1 * m_ref[...] + (1.0 - ADAM_B1) * gv
        vn = ADAM_B2 * v_ref[...] + (1.0 - ADAM_B2) * (gv * gv)
        mo_ref[...] = mn
        vo_ref[...] = vn
        d_ref[...] = -ADAM_LR * ((mn / c1) / (jnp.sqrt(vn / c2) + ADAM_EPS) + ADAM_WD * w_ref[...])

    blk = pl.BlockSpec((tr, C), lambda r: (r, 0))
    return _call(body, name=name, grid=(R // tr,), in_specs=[blk] * 4, out_specs=[blk] * 3,
                 out_shape=[_sds((R, C), F32)] * 3, dims=("parallel",))(w, g, m, v)


def _gmlp_fwd(xb, W, sp, tag):
    zp = _mm(xb, W["w_in_t"], "nt", name=f"gm_zp_{tag}", out_dtype=BF16)
    p, s, vhat, rstd = _gm_mid_fwd(zp, sp["ln_g"], sp["ln_b"], sp["w_s"], sp["b_st"], name=f"gm_mid_{tag}")
    m = _mm(p, W["w_out"], "nn", name=f"gm_out_{tag}", out_dtype=F32)
    return m, dict(xb=xb, zp=zp, p=p, s=s, vhat=vhat, rstd=rstd)


def _gmlp_bwd(drb, dr, W, sp, sv, tag):
    dp = _mm(drb, W["w_out"], "nt", name=f"gm_dp_{tag}", out_dtype=BF16)
    d_w_out = _mm(sv["p"], drb, "tn", name=f"gm_dwout_{tag}", out_dtype=BF16, tk=1024)
    dzp, dws, dbs, dlg, dlb = _gm_mid_bwd(dp, sv["zp"], sv["s"], sv["vhat"], sv["rstd"], sp["ln_g"], sp["ln_b"], sp["w_s"],
                                          name=f"gm_midb_{tag}")
    dx = _mm(dzp, W["w_in_t"], "nn", name=f"gm_dx_{tag}", out_dtype=F32, res=dr, res_scale=ALPHA)
    d_w_in_t = _mm(dzp, sv["xb"], "tn", name=f"gm_dwin_{tag}", out_dtype=BF16, tk=1024)
    return dx, dict(w_in_t=d_w_in_t, w_out=d_w_out), dict(w_s=dws, b_s=dbs, ln_g=dlg, ln_b=dlb)


def _mla_fwd(xb, W, sp, rope, tag):
    cos, sin_s, _ = rope
    h = _mm(xb, W["w_in"], "nn", name=f"mla_h_{tag}", out_dtype=F32, tn=W["w_in"].shape[1])
    cqn, ckvn, kr2, rq, rkv = _mla_prep(h, sp["gq"], sp["gkv"], cos, sin_s, name=f"mla_prep_{tag}")
    q = _mm(cqn, W["w_qb_t"], "nt", name=f"mla_q_{tag}", out_dtype=F32)
    qb = _rope_q(q, cos, sin_s, name=f"mla_ropeq_{tag}")
    kv = _mm(ckvn, W["w_kvb_t"], "nt", name=f"mla_kv_{tag}", out_dtype=BF16)
    o, lse = _attn_fwd(qb, kv, kr2, name=f"mla_attn_{tag}")
    m = _mm(o, W["w_out"], "nn", name=f"mla_out_{tag}", out_dtype=F32)
    return m, dict(xb=xb, h=h, cqn=cqn, ckvn=ckvn, kr2=kr2, rq=rq, rkv=rkv, qb=qb, kv=kv, o=o, lse=lse)


def _mla_bwd(drb, dr, W, sp, rope, sv, tag):
    cos, _, sin_neg = rope
    do = _mm(drb, W["w_out"], "nt", name=f"mla_do_{tag}", out_dtype=BF16)
    d_w_out = _mm(sv["o"], drb, "tn", name=f"mla_dwout_{tag}", out_dtype=BF16, tk=1024)
    dq, dkv, dkr2 = _attn_bwd(sv["qb"], sv["kv"], sv["kr2"], sv["o"], do, sv["lse"], name=f"mla_attnb_{tag}")
    dqb = _rope_q(dq, cos, sin_neg, name=f"mla_ropedq_{tag}")
    dcqn = _mm(dqb, W["w_qb_t"], "nn", name=f"mla_dcq_{tag}", out_dtype=F32, tk=dqb.shape[1])
    d_w_qb_t = _mm(dqb, sv["cqn"], "tn", name=f"mla_dwqb_{tag}", out_dtype=BF16, tk=1024)
    dckvn = _mm(dkv, W["w_kvb_t"], "nn", name=f"mla_dckv_{tag}", out_dtype=F32, tk=dkv.shape[1])
    d_w_kvb_t = _mm(dkv, sv["ckvn"], "tn", name=f"mla_dwkvb_{tag}", out_dtype=BF16, tk=1024)
    dh, dgq, dgkv = _mla_prep_bwd(dcqn, dckvn, dkr2, sv["h"], sv["rq"], sv["rkv"], sp["gq"], sp["gkv"], cos, sin_neg,
                                  name=f"mla_prepb_{tag}")
    dx = _mm(dh, W["w_in"], "nt", name=f"mla_dx_{tag}", out_dtype=F32, tk=dh.shape[1], res=dr, res_scale=ALPHA)
    d_w_in = _mm(sv["xb"], dh, "tn", name=f"mla_dwin_{tag}", out_dtype=BF16, tn=dh.shape[1], tk=1024)
    return dx, dict(w_in=d_w_in, w_qb_t=d_w_qb_t, w_kvb_t=d_w_kvb_t, w_out=d_w_out), dict(gq=dgq, gkv=dgkv)


def _ffn_fwd(xb, W, sp, tag):
    S = xb.shape[0]
    F = W["w_down"].shape[0]
    h = _mm(xb, W["w_up_t"], "nt", name=f"ffn_h_{tag}", out_dtype=BF16, tm=2048, tn=256)
    act = _ffn_mid_fwd(h, sp["cw"], sp["cb"], name=f"ffn_mid_{tag}")
    f = _mm(act, W["w_down"], "nn", name=f"ffn_out_{tag}", out_dtype=F32, tm=512, tn=1024, tk=F)
    return f, dict(xb=xb, h=h, act=act)


def _ffn_bwd(drb, dr, W, sp, sv, tag):
    F = W["w_down"].shape[0]
    dact = _mm(drb, W["w_down"], "nt", name=f"ffn_dact_{tag}", out_dtype=BF16, tm=512, tn=F, tk=512)
    d_w_down = _mm(sv["act"], drb, "tn", name=f"ffn_dwdown_{tag}", out_dtype=BF16, tm=F, tn=512, tk=512)
    dh, dcw, dcb = _ffn_mid_bwd(dact, sv["h"], sp["cw"], sp["cb"], name=f"ffn_midb_{tag}")
    dx = _mm(dh, W["w_up_t"], "nn", name=f"ffn_dx_{tag}", out_dtype=F32, tm=512, tn=1024, tk=F, res=dr, res_scale=ALPHA)
    d_w_up_t = _mm(dh, sv["xb"], "tn", name=f"ffn_dwup_{tag}", out_dtype=BF16, tm=F, tn=512, tk=512)
    return dx, dict(w_up_t=d_w_up_t, w_down=d_w_down), dict(cw=dcw, cb=dcb)


def _rope_tables(S):
    half = ROPE // 2
    inv_freq = ROPE_THETA ** (-jnp.arange(half, dtype=F32) / half)
    ang = jnp.arange(S, dtype=F32)[:, None] * inv_freq[None, :]
    cos, sin = jnp.cos(ang), jnp.sin(ang)
    cos128 = jnp.concatenate([cos] * 4, axis=1)
    sin128 = jnp.concatenate([-sin, sin, -sin, sin], axis=1)
    return cos128, sin128, -sin128


def _fwd_bwd(x, tgt, Wm, Wf, spm, spf, ln):
    S, D = x.shape
    rope = _rope_tables(S)
    xf, xb = x, x.astype(BF16)
    saved = []
    for i in range(DEPTH):
        if i % 2 == 0:
            m, svm = _gmlp_fwd(xb, Wm[i], spm[i], f"l{i}")
        else:
            m, svm = _mla_fwd(xb, Wm[i], spm[i], rope, f"l{i}")
        y, yb, xh1, rs1 = _ln_fwd(xf, m, ln["mix_g"][i], ln["mix_b"][i], name=f"ln_mix_{i}")
        f, svf = _ffn_fwd(yb, Wf[i], spf[i], f"l{i}")
        z, zb, xh2, rs2 = _ln_fwd(y, f, ln["ffn_g"][i], ln["ffn_b"][i], name=f"ln_ffn_{i}")
        saved.append((svm, xh1, rs1, svf, xh2, rs2))
        xf, xb = z, zb
    lp, dy = _loss_kernel(xf, tgt, name="loss")
    gm_big, gm_small, gf_big, gf_small, gln = [None] * DEPTH, [None] * DEPTH, [None] * DEPTH, [None] * DEPTH, [None] * DEPTH
    for i in reversed(range(DEPTH)):
        svm, xh1, rs1, svf, xh2, rs2 = saved[i]
        dr, drb, dg2, db2 = _ln_bwd(dy, xh2, rs2, ln["ffn_g"][i], name=f"lnb_ffn_{i}")
        dy, gf_big[i], gf_small[i] = _ffn_bwd(drb, dr, Wf[i], spf[i], svf, f"l{i}")
        dr, drb, dg1, db1 = _ln_bwd(dy, xh1, rs1, ln["mix_g"][i], name=f"lnb_mix_{i}")
        if i % 2 == 0:
            dy, gm_big[i], gm_small[i] = _gmlp_bwd(drb, dr, Wm[i], spm[i], svm, f"l{i}")
        else:
            dy, gm_big[i], gm_small[i] = _mla_bwd(drb, dr, Wm[i], spm[i], rope, svm, f"l{i}")
        gln[i] = dict(mix_g=dg1, mix_b=db1, ffn_g=dg2, ffn_b=db2)
    return lp, dy, gm_big, gm_small, gf_big, gf_small, gln


def _perm_q_rows(wt):
    hd = NOPE + ROPE
    return jnp.concatenate([wt[0:NOPE], wt[hd:hd + NOPE], wt[NOPE:hd], wt[hd + NOPE:2 * hd]], axis=0)


def _unperm_q_rows(wt):
    return jnp.concatenate([wt[0:NOPE], wt[2 * NOPE:2 * NOPE + ROPE], wt[NOPE:2 * NOPE], wt[2 * NOPE + ROPE:]], axis=0)


def _pad_cols(w, to):
    return jnp.pad(w, ((0, 0), (0, to - w.shape[1])))


def _pack(parts, rows_mult=8):
    flat = jnp.concatenate([p.reshape(-1).astype(F32) for p in parts])
    n = flat.shape[0]
    per = LANES * rows_mult
    tot = ((n + per - 1) // per) * per
    return jnp.pad(flat, (0, tot - n)).reshape(tot // LANES, LANES)


def _unpack(buf, shapes):
    flat = buf.reshape(-1)
    out, off = [], 0
    for s in shapes:
        n = 1
        for d in s:
            n *= d
        out.append(flat[off:off + n].reshape(s))
        off += n
    return out


def kernel(x, gm_w_in, gm_ln_g, gm_ln_b, gm_w_s, gm_b_s, gm_w_out, mla_w_in, mla_q_norm_g, mla_kv_norm_g, mla_w_q_b, mla_w_kv_b, mla_w_out, ffn_w_up, ffn_conv_w, ffn_conv_b, ffn_w_down, ln_mix_g, ln_mix_b, ln_ffn_g, ln_ffn_b, loss_target, m_gm_w_in, m_gm_ln_g, m_gm_ln_b, m_gm_w_s, m_gm_b_s, m_gm_w_out, m_mla_w_in, m_mla_q_norm_g, m_mla_kv_norm_g, m_mla_w_q_b, m_mla_w_kv_b, m_mla_w_out, m_ffn_w_up, m_ffn_conv_w, m_ffn_conv_b, m_ffn_w_down, m_ln_mix_g, m_ln_mix_b, m_ln_ffn_g, m_ln_ffn_b, v_gm_w_in, v_gm_ln_g, v_gm_ln_b, v_gm_w_s, v_gm_b_s, v_gm_w_out, v_mla_w_in, v_mla_q_norm_g, v_mla_kv_norm_g, v_mla_w_q_b, v_mla_w_kv_b, v_mla_w_out, v_ffn_w_up, v_ffn_conv_w, v_ffn_conv_b, v_ffn_w_down, v_ln_mix_g, v_ln_mix_b, v_ln_ffn_g, v_ln_ffn_b):
    S, D = x.shape[1], x.shape[2]
    xi, yi, ci = _coords()
    dev = 4 * xi + 2 * yi + ci
    c_idx = jnp.reshape(ci, (1,)).astype(jnp.int32)
    chip_idx = jnp.reshape(2 * xi + yi, (1,)).astype(jnp.int32)
    w_in_cols = mla_w_in.shape[2]
    w_in_pad = ((w_in_cols + LANES - 1) // LANES) * LANES
    n_gm, n_mla = gm_w_in.shape[0], mla_w_in.shape[0]

    cw_l = ffn_conv_w.shape[2]
    small_in = _pack([mla_q_norm_g, mla_kv_norm_g, ffn_conv_w])
    (small_all,) = _all_gather([small_in], name="ag_small")
    gq_parts, gkv_parts, cw_parts = [], [], []
    for k in range(N_DEV):
        a, b, c_ = _unpack(small_all[k], [mla_q_norm_g.shape, mla_kv_norm_g.shape, ffn_conv_w.shape])
        gq_parts.append(a)
        gkv_parts.append(b)
        cw_parts.append(c_)
    gq_full = jnp.concatenate(gq_parts, axis=1)
    gkv_full = jnp.concatenate(gkv_parts, axis=1)
    cw_full = jnp.concatenate(cw_parts, axis=2)

    def gather(shards, name):
        full = _all_gather_sc(list(shards.values()), name=name)
        return {k: f.reshape((-1, f.shape[2])) for k, f in zip(shards.keys(), full)}

    Wm, Wf = [None] * DEPTH, [None] * DEPTH
    for i in range(DEPTH):
        s = i // 2
        if i % 2 == 0:
            Wm[i] = gather(dict(w_in_t=gm_w_in[s].T.astype(BF16), w_out=gm_w_out[s].astype(BF16)), f"ag_gm_{i}")
        else:
            Wm[i] = gather(dict(w_in=_pad_cols(mla_w_in[s], w_in_pad).astype(BF16),
                                w_qb_t=_perm_q_rows(mla_w_q_b[s].T).astype(BF16),
                                w_kvb_t=mla_w_kv_b[s].T.astype(BF16), w_out=mla_w_out[s].astype(BF16)), f"ag_mla_{i}")
        Wf[i] = gather(dict(w_up_t=ffn_w_up[i].T.astype(BF16), w_down=ffn_w_down[i].astype(BF16)), f"ag_ffn_{i}")

    spm, spf = [None] * DEPTH, [None] * DEPTH
    for i in range(DEPTH):
        s = i // 2
        if i % 2 == 0:
            spm[i] = dict(ln_g=gm_ln_g[s][None], ln_b=gm_ln_b[s][None], w_s=gm_w_s[s], b_st=gm_b_s[s].T)
        else:
            spm[i] = dict(gq=gq_full[s][None], gkv=gkv_full[s][None])
        spf[i] = dict(cw=cw_full[i], cb=ffn_conv_b[i][None])
    ln = dict(mix_g=ln_mix_g[:, None], mix_b=ln_mix_b[:, None], ffn_g=ln_ffn_g[:, None], ffn_b=ln_ffn_b[:, None])

    lp, grad_x, gm_big, gm_small, gf_big, gf_small, gln = _fwd_bwd(x[0], loss_target[0], Wm, Wf, spm, spf, ln)
    loss = lax.psum(0.5 * jnp.sum(lp) / D, ("x", "y", "c"))

    def reduce_scatter(gd, tag):
        keys = list(gd.keys())
        g8 = [gd[k].reshape((N_DEV, gd[k].shape[0] // N_DEV, gd[k].shape[1])) for k in keys]
        l1 = _pair_exchange_sc(g8, name=f"rs_pair_{tag}")
        ps = [_pair_sum(g, l, c_idx, name=f"rs_psum_{tag}_{k}") for g, l, k in zip(g8, l1, keys)]
        l2 = _chip_exchange_sc(ps, name=f"rs_chip_{tag}")
        return {k: _chip_sum(p, l, chip_idx, name=f"rs_csum_{tag}_{k}") for p, l, k in zip(ps, l2, keys)}

    rm, rf = [None] * DEPTH, [None] * DEPTH
    for i in reversed(range(DEPTH)):
        rf[i] = reduce_scatter(gf_big[i], f"ffn{i}")
        rm[i] = reduce_scatter(gm_big[i], f"mix{i}")

    g_gm_w_in = jnp.stack([rm[2 * s]["w_in_t"].T for s in range(n_gm)])
    g_gm_w_out = jnp.stack([rm[2 * s]["w_out"] for s in range(n_gm)])
    g_mla_w_in = jnp.stack([rm[2 * s + 1]["w_in"][:, :w_in_cols] for s in range(n_mla)])
    g_mla_w_q_b = jnp.stack([_unperm_q_rows(rm[2 * s + 1]["w_qb_t"]).T for s in range(n_mla)])
    g_mla_w_kv_b = jnp.stack([rm[2 * s + 1]["w_kvb_t"].T for s in range(n_mla)])
    g_mla_w_out = jnp.stack([rm[2 * s + 1]["w_out"] for s in range(n_mla)])
    g_ffn_w_up = jnp.stack([rf[i]["w_up_t"].T for i in range(DEPTH)])
    g_ffn_w_down = jnp.stack([rf[i]["w_down"] for i in range(DEPTH)])

    small_g = [
        jnp.stack([gm_small[2 * s]["ln_g"][0] for s in range(n_gm)]),
        jnp.stack([gm_small[2 * s]["ln_b"][0] for s in range(n_gm)]),
        jnp.stack([gm_small[2 * s]["w_s"] for s in range(n_gm)]),
        jnp.stack([gm_small[2 * s]["b_s"][:, :, 0] for s in range(n_gm)]),
        jnp.stack([gf_small[i]["cb"][0] for i in range(DEPTH)]),
        jnp.stack([gln[i]["mix_g"][0] for i in range(DEPTH)]),
        jnp.stack([gln[i]["mix_b"][0] for i in range(DEPTH)]),
        jnp.stack([gln[i]["ffn_g"][0] for i in range(DEPTH)]),
        jnp.stack([gln[i]["ffn_b"][0] for i in range(DEPTH)]),
        jnp.stack([gm_small[2 * s + 1]["gq"][0] for s in range(n_mla)]),
        jnp.stack([gm_small[2 * s + 1]["gkv"][0] for s in range(n_mla)]),
        jnp.stack([gf_small[i]["cw"] for i in range(DEPTH)]),
    ]
    small_shapes = [g.shape for g in small_g]
    (small_g_all,) = _all_gather([_pack(small_g)], name="ag_small_grads")
    small_sum = _unpack(_sum8(small_g_all, name="small_grad_sum"), small_shapes)
    (g_gm_ln_g, g_gm_ln_b, g_gm_w_s, g_gm_b_s, g_ffn_conv_b, g_ln_mix_g, g_ln_mix_b, g_ln_ffn_g, g_ln_ffn_b,
     gq_all, gkv_all, cw_all) = small_sum
    qn_l = mla_q_norm_g.shape[1]
    g_mla_q_norm_g = lax.dynamic_slice_in_dim(gq_all, dev * qn_l, qn_l, axis=1)
    g_mla_kv_norm_g = lax.dynamic_slice_in_dim(gkv_all, dev * qn_l, qn_l, axis=1)
    g_ffn_conv_w = lax.dynamic_slice_in_dim(cw_all, dev * cw_l, cw_l, axis=2)

    def adam_big(w, g, m, v, tag):
        sh = w.shape
        two = lambda a: a.reshape((-1, sh[-1]))
        d, mn, vn = _adamw(two(w), two(g), two(m), two(v), name=f"adamw_{tag}")
        return d.reshape(sh), mn.reshape(sh), vn.reshape(sh)

    big = [("gm_w_in", gm_w_in, g_gm_w_in, m_gm_w_in, v_gm_w_in), ("gm_w_out", gm_w_out, g_gm_w_out, m_gm_w_out, v_gm_w_out),
           ("mla_w_in", mla_w_in, g_mla_w_in, m_mla_w_in, v_mla_w_in), ("mla_w_q_b", mla_w_q_b, g_mla_w_q_b, m_mla_w_q_b, v_mla_w_q_b),
           ("mla_w_kv_b", mla_w_kv_b, g_mla_w_kv_b, m_mla_w_kv_b, v_mla_w_kv_b), ("mla_w_out", mla_w_out, g_mla_w_out, m_mla_w_out, v_mla_w_out),
           ("ffn_w_up", ffn_w_up, g_ffn_w_up, m_ffn_w_up, v_ffn_w_up), ("ffn_w_down", ffn_w_down, g_ffn_w_down, m_ffn_w_down, v_ffn_w_down)]
    res = {}
    for tag, w, g, m, v in big:
        res[tag] = (g,) + adam_big(w, g, m, v, tag)

    small = [("gm_ln_g", gm_ln_g, g_gm_ln_g, m_gm_ln_g, v_gm_ln_g), ("gm_ln_b", gm_ln_b, g_gm_ln_b, m_gm_ln_b, v_gm_ln_b),
             ("gm_w_s", gm_w_s, g_gm_w_s, m_gm_w_s, v_gm_w_s), ("gm_b_s", gm_b_s, g_gm_b_s, m_gm_b_s, v_gm_b_s),
             ("mla_q_norm_g", mla_q_norm_g, g_mla_q_norm_g, m_mla_q_norm_g, v_mla_q_norm_g),
             ("mla_kv_norm_g", mla_kv_norm_g, g_mla_kv_norm_g, m_mla_kv_norm_g, v_mla_kv_norm_g),
             ("ffn_conv_w", ffn_conv_w, g_ffn_conv_w, m_ffn_conv_w, v_ffn_conv_w), ("ffn_conv_b", ffn_conv_b, g_ffn_conv_b, m_ffn_conv_b, v_ffn_conv_b),
             ("ln_mix_g", ln_mix_g, g_ln_mix_g, m_ln_mix_g, v_ln_mix_g), ("ln_mix_b", ln_mix_b, g_ln_mix_b, m_ln_mix_b, v_ln_mix_b),
             ("ln_ffn_g", ln_ffn_g, g_ln_ffn_g, m_ln_ffn_g, v_ln_ffn_g), ("ln_ffn_b", ln_ffn_b, g_ln_ffn_b, m_ln_ffn_b, v_ln_ffn_b)]
    shapes = [t[1].shape for t in small]
    d_s, m_s, v_s = _adamw(_pack([t[1] for t in small]), _pack([t[2] for t in small]), _pack([t[3] for t in small]),
                           _pack([t[4] for t in small]), name="adamw_small")
    d_l, m_l, v_l = _unpack(d_s, shapes), _unpack(m_s, shapes), _unpack(v_s, shapes)
    for (tag, _, g, _, _), d, mn, vn in zip(small, d_l, m_l, v_l):
        res[tag] = (g, d, mn, vn)

    order = ["gm_w_in", "gm_ln_g", "gm_ln_b", "gm_w_s", "gm_b_s", "gm_w_out", "mla_w_in", "mla_q_norm_g", "mla_kv_norm_g",
             "mla_w_q_b", "mla_w_kv_b", "mla_w_out", "ffn_w_up", "ffn_conv_w", "ffn_conv_b", "ffn_w_down",
             "ln_mix_g", "ln_mix_b", "ln_ffn_g", "ln_ffn_b"]
    out = [loss, grad_x[None]]
    for q in range(4):
        out += [res[k][q] for k in order]
    return tuple(out)
```

```python
import jax
import jax.numpy as jnp
from jax import lax
from jax.experimental import pallas as pl
from jax.experimental.pallas import tpu as pltpu
from jax.experimental.pallas import tpu_sc as plsc

F32, BF16 = jnp.float32, jnp.bfloat16

DEPTH = 4
CHUNK = 64
GM_BLOCK = 128
GM_GROUPS = 8
HEADS = 16
NOPE, ROPE, VDIM = 128, 64, 128
QRANK, KVRANK = 512, 512
ROPE_THETA = 10000.0
SM_SCALE = (NOPE + ROPE) ** -0.5
ALPHA = (2 * DEPTH) ** 0.25
LN_EPS = 1e-5
RMS_EPS = 1e-6
ADAM_LR, ADAM_B1, ADAM_B2, ADAM_EPS, ADAM_WD, ADAM_STEP = 0.001, 0.9, 0.999, 1e-08, 0.01, 10

N_DEV = 8
LANES = 128
SUBLANES = 8
VMEM_LIMIT = 56 * 1024 * 1024
MESH = pl.DeviceIdType.MESH
HBM_SPEC = pl.BlockSpec(memory_space=pltpu.HBM)


class _Schedule:
    def __init__(self):
        self.last = None
        self.reducer = None
        self.clock = 0.0

    def tick(self, us):
        self.clock += us
        if self.reducer is not None:
            self.reducer.advance()


_SCHED = _Schedule()


def _ticked(out, us):
    _SCHED.tick(us)
    return out


def _call(body, *, name, out_shape, in_specs, out_specs, grid=None, scratch=(), dims=None, n_prefetch=0):
    kw = dict(vmem_limit_bytes=VMEM_LIMIT)
    if dims is not None:
        kw["dimension_semantics"] = dims
    cp = pltpu.CompilerParams(**kw)
    in_specs = list(in_specs)
    token = _SCHED.last
    if token is not None:
        pos = n_prefetch + len(in_specs)
        in_specs.append(pl.BlockSpec(memory_space=pl.ANY))
        inner = body

        def body(*refs):
            return inner(*refs[:pos], *refs[pos + 1:])

    if n_prefetch:
        gs = pltpu.PrefetchScalarGridSpec(num_scalar_prefetch=n_prefetch, grid=grid, in_specs=in_specs, out_specs=out_specs,
                                          scratch_shapes=list(scratch))
        fn = pl.pallas_call(body, name=name, grid_spec=gs, out_shape=out_shape, compiler_params=cp, interpret=False)
    else:
        extra = {} if grid is None else {"grid": grid}
        fn = pl.pallas_call(body, name=name, in_specs=in_specs, out_specs=out_specs, out_shape=out_shape,
                            scratch_shapes=list(scratch), compiler_params=cp, interpret=False, **extra)

    def run(*args):
        out = fn(*args, token) if token is not None else fn(*args)
        _SCHED.last = out[0] if isinstance(out, (list, tuple)) else out
        return out

    return run


def _tile(n, pref, mult=LANES):
    if n <= pref:
        return n
    t = (pref // mult) * mult
    while t >= mult:
        if n % t == 0:
            return t
        t -= mult
    return n


def _sds(shape, dtype):
    return jax.ShapeDtypeStruct(tuple(shape), dtype)


MXU_FLOPS_PER_US = 8e8
TOKENS_K = 4096
_DN = {"nn": (((1,), (0,)), ((), ())), "nt": (((1,), (1,)), ((), ())), "tn": (((0,), (0,)), ((), ()))}


def _mm(a, b, mode, *, name, out_dtype, tm=1024, tn=1024, tk=2048, res=None, res_scale=1.0, a_parts=1, b_resident=False):
    if a_parts > 1:
        S_, F_ = a.shape[0] // a_parts, a.shape[1]
        a_shape = (S_, a_parts * F_)
    else:
        a_shape = a.shape
    if mode == "nn":
        (M, K), (K2, N) = a_shape, b.shape
    elif mode == "nt":
        (M, K), (N, K2) = a_shape, b.shape
    else:
        (K, M), (K2, N) = a_shape, b.shape
    assert K == K2, (name, a.shape, b.shape)
    tm, tn, tk = _tile(M, tm), _tile(N, tn), _tile(K, tk)
    nk = K // tk
    if a_parts > 1 and mode == "nn":
        assert tk == F_
        a_spec = pl.BlockSpec((tm, tk), lambda i, j, k: (k * (S_ // tm) + i, 0))
    elif a_parts > 1:
        assert mode == "tn" and tm == F_
        a_spec = pl.BlockSpec((tk, tm), lambda i, j, k: (i * (S_ // tk) + k, 0))
    elif mode == "tn":
        a_spec = pl.BlockSpec((tk, tm), lambda i, j, k: (k, i))
    else:
        a_spec = pl.BlockSpec((tm, tk), lambda i, j, k: (i, k))
    b_kw = {"pipeline_mode": pl.Buffered(1)} if b_resident else {}
    b_spec = (pl.BlockSpec((tn, tk), lambda i, j, k: (j, k), **b_kw) if mode == "nt"
              else pl.BlockSpec((tk, tn), lambda i, j, k: (k, j), **b_kw))
    in_specs = [a_spec, b_spec]
    args = [a, b]
    if res is not None:
        in_specs.append(pl.BlockSpec((tm, tn), lambda i, j, k: (i, j)))
        args.append(res)
    dn = _DN[mode]
    has_res = res is not None

    def body(*refs):
        a_ref, b_ref = refs[0], refs[1]
        r_ref = refs[2] if has_res else None
        o_ref = refs[2 + has_res]
        part = lax.dot_general(a_ref[...], b_ref[...], dn, preferred_element_type=F32)

        def finish(acc):
            if has_res:
                acc = acc + res_scale * r_ref[...]
            o_ref[...] = acc.astype(o_ref.dtype)

        if nk == 1:
            finish(part)
        else:
            acc_ref = refs[3 + has_res]
            k = pl.program_id(2)

            @pl.when(k == 0)
            def _():
                acc_ref[...] = part

            @pl.when(k > 0)
            def _():
                acc_ref[...] += part

            @pl.when(k == nk - 1)
            def _():
                finish(acc_ref[...])

    scratch = [pltpu.VMEM((tm, tn), F32)] if nk > 1 else []
    out = _call(body, name=name, grid=(M // tm, N // tn, nk), in_specs=in_specs,
                out_specs=pl.BlockSpec((tm, tn), lambda i, j, k: (i, j)), out_shape=_sds((M, N), out_dtype),
                scratch=scratch, dims=("parallel", "parallel", "arbitrary"))(*args)
    _SCHED.tick(2.0 * M * N * K / MXU_FLOPS_PER_US)
    return out


def _ln_fwd(x, m, g, b, *, name):
    S, D = x.shape
    tr = _tile(S, 256, 8)

    def body(x_ref, m_ref, g_ref, b_ref, y_ref, yb_ref, xh_ref, rs_ref):
        r = ALPHA * x_ref[...] + m_ref[...]
        mu = jnp.mean(r, axis=-1, keepdims=True)
        d = r - mu
        var = jnp.mean(d * d, axis=-1, keepdims=True)
        rstd = lax.rsqrt(var + LN_EPS)
        xh = d * rstd
        y = xh * g_ref[...] + b_ref[...]
        y_ref[...] = y
        yb_ref[...] = y.astype(BF16)
        xh_ref[...] = xh
        rs_ref[...] = rstd

    row = pl.BlockSpec((tr, D), lambda i: (i, 0))
    vec = pl.BlockSpec((1, D), lambda i: (0, 0))
    return _call(body, name=name, grid=(S // tr,), in_specs=[row, row, vec, vec],
                 out_specs=[row, row, row, pl.BlockSpec((tr, 1), lambda i: (i, 0))],
                 out_shape=[_sds((S, D), F32), _sds((S, D), BF16), _sds((S, D), F32), _sds((S, 1), F32)],
                 dims=("parallel",))(x, m, g, b)


def _ln_bwd(dy, xh, rstd, g, *, name):
    S, D = dy.shape
    tr = _tile(S, 256, 8)

    def body(dy_ref, xh_ref, rs_ref, g_ref, dr_ref, drb_ref, dg_ref, db_ref):
        @pl.when(pl.program_id(0) == 0)
        def _():
            dg_ref[...] = jnp.zeros_like(dg_ref)
            db_ref[...] = jnp.zeros_like(db_ref)

        dyv = dy_ref[...]
        xhv = xh_ref[...]
        dxh = dyv * g_ref[...]
        m1 = jnp.mean(dxh, axis=-1, keepdims=True)
        m2 = jnp.mean(dxh * xhv, axis=-1, keepdims=True)
        dr = rs_ref[...] * (dxh - m1 - xhv * m2)
        dr_ref[...] = dr
        drb_ref[...] = dr.astype(BF16)
        dg_ref[...] += jnp.sum(dyv * xhv, axis=0, keepdims=True)
        db_ref[...] += jnp.sum(dyv, axis=0, keepdims=True)

    row = pl.BlockSpec((tr, D), lambda i: (i, 0))
    vec = pl.BlockSpec((1, D), lambda i: (0, 0))
    return _ticked(_call(body, name=name, grid=(S // tr,), in_specs=[row, row, pl.BlockSpec((tr, 1), lambda i: (i, 0)), vec],
                         out_specs=[row, row, vec, vec],
                         out_shape=[_sds((S, D), F32), _sds((S, D), BF16), _sds((1, D), F32), _sds((1, D), F32)],
                         dims=("arbitrary",))(dy, xh, rstd, g), 45.0 * S / 4096)


def _loss_kernel(y, t, *, name):
    S, D = y.shape
    tr = _tile(S, 256, 8)

    def body(y_ref, t_ref, lp_ref, dy_ref):
        @pl.when(pl.program_id(0) == 0)
        def _():
            lp_ref[...] = jnp.zeros_like(lp_ref)

        e = y_ref[...] - t_ref[...]
        dy_ref[...] = e / D
        lp_ref[...] += jnp.sum(e * e, axis=0, keepdims=True)

    row = pl.BlockSpec((tr, D), lambda i: (i, 0))
    vec = pl.BlockSpec((1, D), lambda i: (0, 0))
    return _call(body, name=name, grid=(S // tr,), in_specs=[row, row], out_specs=[vec, row],
                 out_shape=[_sds((1, D), F32), _sds((S, D), F32)], dims=("arbitrary",))(y, t)


_GELU_C = 0.7978845608028654
_GELU_A = 0.044715


def _gelu(x):
    return 0.5 * x * (1.0 + jnp.tanh(_GELU_C * (x + _GELU_A * x * x * x)))


def _gelu_grad(x):
    x2 = x * x
    t = jnp.tanh(_GELU_C * (x + _GELU_A * x * x2))
    return 0.5 * (1.0 + t) + 0.5 * x * (1.0 - t * t) * (_GELU_C * (1.0 + 3.0 * _GELU_A * x2))


def _masked_ws(w):
    i = lax.broadcasted_iota(jnp.int32, w.shape, 0) // CHUNK
    j = lax.broadcasted_iota(jnp.int32, w.shape, 1) // CHUNK
    return jnp.where(j <= i, w, 0.0)


def _gm_mid_fwd(zp, ln_g, ln_b, w_s, b_st, *, name):
    S, H2 = zp.shape
    H = H2 // 2
    gd = H // GM_GROUPS
    nb = S // GM_BLOCK

    def body(zu_ref, zv_ref, g_ref, b_ref, w_ref, bs_ref, p_ref, s_ref, vh_ref, rs_ref):
        v = _gelu(zv_ref[...].astype(F32))
        mu = jnp.mean(v, axis=-1, keepdims=True)
        d = v - mu
        var = jnp.mean(d * d, axis=-1, keepdims=True)
        rstd = lax.rsqrt(var + LN_EPS)
        vh = d * rstd
        vh_ref[...] = vh.astype(BF16)
        rs_ref[...] = rstd
        vn = (vh * g_ref[...] + b_ref[...]).astype(BF16)
        bs = bs_ref[...]
        for gi in range(GM_GROUPS):
            cs = slice(gi * gd, (gi + 1) * gd)
            wm = _masked_ws(w_ref[gi]).astype(BF16)
            s = jnp.dot(wm, vn[:, cs], preferred_element_type=F32) + bs[:, gi:gi + 1]
            u = _gelu(zu_ref[:, cs].astype(F32))
            s_ref[:, cs] = s.astype(BF16)
            p_ref[:, cs] = (u * s).astype(BF16)

    blk = lambda c: pl.BlockSpec((GM_BLOCK, H), lambda n, c=c: (n, c))
    vec = pl.BlockSpec((1, H), lambda n: (0, 0))
    return _call(body, name=name, grid=(nb,),
                 in_specs=[blk(0), blk(1), vec, vec, pl.BlockSpec((GM_GROUPS, GM_BLOCK, GM_BLOCK), lambda n: (0, 0, 0)),
                           pl.BlockSpec((GM_BLOCK, GM_GROUPS), lambda n: (0, 0))],
                 out_specs=[blk(0), blk(0), blk(0), pl.BlockSpec((GM_BLOCK, 1), lambda n: (n, 0))],
                 out_shape=[_sds((S, H), BF16), _sds((S, H), BF16), _sds((S, H), BF16), _sds((S, 1), F32)],
                 dims=("parallel",))(zp, zp, ln_g, ln_b, w_s, b_st)


def _gm_mid_bwd(dp, zp, s, vhat, rstd, ln_g, ln_b, w_s, *, name):
    S, H2 = zp.shape
    H = H2 // 2
    gd = H // GM_GROUPS
    nb = S // GM_BLOCK

    def body(dp_ref, zu_ref, zv_ref, s_ref, vh_ref, rs_ref, g_ref, b_ref, w_ref,
             dz_ref, dw_ref, dbs_ref, dg_ref, db_ref, dvh_ref):
        @pl.when(pl.program_id(0) == 0)
        def _():
            dw_ref[...] = jnp.zeros_like(dw_ref)
            dbs_ref[...] = jnp.zeros_like(dbs_ref)
            dg_ref[...] = jnp.zeros_like(dg_ref)
            db_ref[...] = jnp.zeros_like(db_ref)

        m1 = jnp.zeros((GM_BLOCK, 1), F32)
        m2 = jnp.zeros((GM_BLOCK, 1), F32)
        for gi in range(GM_GROUPS):
            cs = slice(gi * gd, (gi + 1) * gd)
            dpg = dp_ref[:, cs].astype(F32)
            zu = zu_ref[:, cs].astype(F32)
            u = _gelu(zu)
            ds = dpg * u
            du = dpg * s_ref[:, cs].astype(F32)
            dz_ref[:, cs] = (du * _gelu_grad(zu)).astype(BF16)
            dsb = ds.astype(BF16)
            vh = vh_ref[:, cs].astype(F32)
            lg = g_ref[:, cs]
            vn = (vh * lg + b_ref[:, cs]).astype(BF16)
            wm = _masked_ws(w_ref[gi]).astype(BF16)
            dvn = lax.dot_general(wm, dsb, _DN["tn"], preferred_element_type=F32)
            dw_ref[gi] += lax.dot_general(dsb, vn, _DN["nt"], preferred_element_type=F32)
            dbs_ref[gi] += jnp.sum(ds, axis=1, keepdims=True)
            dg_ref[:, cs] += jnp.sum(dvn * vh, axis=0, keepdims=True)
            db_ref[:, cs] += jnp.sum(dvn, axis=0, keepdims=True)
            dvh = dvn * lg
            dvh_ref[:, cs] = dvh
            m1 = m1 + jnp.sum(dvh, axis=1, keepdims=True)
            m2 = m2 + jnp.sum(dvh * vh, axis=1, keepdims=True)
        m1 = m1 / H
        m2 = m2 / H
        rs = rs_ref[...]
        for gi in range(GM_GROUPS):
            cs = slice(gi * gd, (gi + 1) * gd)
            vh = vh_ref[:, cs].astype(F32)
            dv = rs * (dvh_ref[:, cs] - m1 - vh * m2)
            zv = zv_ref[:, cs].astype(F32)
            dz_ref[:, H + gi * gd:H + (gi + 1) * gd] = (dv * _gelu_grad(zv)).astype(BF16)

        @pl.when(pl.program_id(0) == nb - 1)
        def _():
            for gi in range(GM_GROUPS):
                dw_ref[gi] = _masked_ws(dw_ref[gi])

    blk = lambda c: pl.BlockSpec((GM_BLOCK, H), lambda n, c=c: (n, c))
    vec = pl.BlockSpec((1, H), lambda n: (0, 0))
    wspec = pl.BlockSpec((GM_GROUPS, GM_BLOCK, GM_BLOCK), lambda n: (0, 0, 0))
    return _ticked(_call(body, name=name, grid=(nb,),
                 in_specs=[blk(0), blk(0), blk(1), blk(0), blk(0), pl.BlockSpec((GM_BLOCK, 1), lambda n: (n, 0)), vec, vec, wspec],
                 out_specs=[pl.BlockSpec((GM_BLOCK, H2), lambda n: (n, 0)), wspec,
                            pl.BlockSpec((GM_GROUPS, GM_BLOCK, 1), lambda n: (0, 0, 0)), vec, vec],
                 out_shape=[_sds((S, H2), BF16), _sds((GM_GROUPS, GM_BLOCK, GM_BLOCK), F32),
                            _sds((GM_GROUPS, GM_BLOCK, 1), F32), _sds((1, H), F32), _sds((1, H), F32)],
                 scratch=[pltpu.VMEM((GM_BLOCK, H), F32)], dims=("arbitrary",))(dp, zp, zp, s, vhat, rstd, ln_g, ln_b, w_s), 280.0 * S / 4096)


def _shift_down(x, k):
    r = pltpu.roll(x, k, 0)
    rows = lax.broadcasted_iota(jnp.int32, (SUBLANES, x.shape[1]), 0)
    return jnp.concatenate([jnp.where(rows >= k, r[:SUBLANES], 0.0), r[SUBLANES:]], axis=0)


def _shift_up(x, k):
    n = x.shape[0]
    r = pltpu.roll(x, n - k, 0)
    rows = lax.broadcasted_iota(jnp.int32, (SUBLANES, x.shape[1]), 0)
    return jnp.concatenate([r[:n - SUBLANES], jnp.where(rows < SUBLANES - k, r[n - SUBLANES:], 0.0)], axis=0)


def _conv(h, w, b):
    return w[0:1, :] * _shift_down(h, 2) + w[1:2, :] * _shift_down(h, 1) + w[2:3, :] * h + b


def _ffn_mid_fwd(h, cw, cb, *, name):
    S, F2 = h.shape
    F = F2 // 2
    nf = F // LANES

    def body(ha_ref, hg_ref, wa_ref, wg_ref, ba_ref, bg_ref, act_ref):
        a = _conv(ha_ref[...].astype(F32), wa_ref[...], ba_ref[...])
        g = _conv(hg_ref[...].astype(F32), wg_ref[...], bg_ref[...])
        act_ref[...] = (g * jax.nn.sigmoid(g) * a).astype(BF16)

    sl = lambda off, r: pl.BlockSpec((r, LANES), lambda j, off=off: (0, j + off))
    return _call(body, name=name, grid=(nf,),
                 in_specs=[sl(0, S), sl(nf, S), sl(0, 3), sl(nf, 3), sl(0, 1), sl(nf, 1)],
                 out_specs=sl(0, S), out_shape=_sds((S, F), BF16), dims=("parallel",))(h, h, cw, cw, cb, cb)


def _ffn_mid_bwd(dact, h, cw, cb, *, name):
    S, F2 = h.shape
    F = F2 // 2
    nf = F // LANES

    def body(da_ref, ha_ref, hg_ref, wa_ref, wg_ref, ba_ref, bg_ref, dh_ref, dw_ref, db_ref):
        dact_v = da_ref[...].astype(F32)
        ha = ha_ref[...].astype(F32)
        hg = hg_ref[...].astype(F32)
        a = _conv(ha, wa_ref[...], ba_ref[...])
        g = _conv(hg, wg_ref[...], bg_ref[...])
        sg = jax.nn.sigmoid(g)

        def emit(p, dc, hin, w):
            up1, up2 = _shift_up(dc, 1), _shift_up(dc, 2)
            dh_ref[p] = (w[2:3, :] * dc + w[1:2, :] * up1 + w[0:1, :] * up2).astype(BF16)
            dw_ref[p, 0:1, :] = jnp.sum(up2 * hin, axis=0, keepdims=True)
            dw_ref[p, 1:2, :] = jnp.sum(up1 * hin, axis=0, keepdims=True)
            dw_ref[p, 2:3, :] = jnp.sum(dc * hin, axis=0, keepdims=True)
            db_ref[p] = jnp.sum(dc, axis=0, keepdims=True)

        emit(0, dact_v * (g * sg), ha, wa_ref[...])
        emit(1, dact_v * a * (sg * (1.0 + g * (1.0 - sg))), hg, wg_ref[...])

    sl = lambda off, r: pl.BlockSpec((r, LANES), lambda j, off=off: (0, j + off))
    out = lambda r: pl.BlockSpec((2, r, LANES), lambda j: (0, 0, j))
    return _ticked(_call(body, name=name, grid=(nf,),
                 in_specs=[sl(0, S), sl(0, S), sl(nf, S), sl(0, 3), sl(nf, 3), sl(0, 1), sl(nf, 1)],
                 out_specs=[out(S), out(3), out(1)],
                 out_shape=[_sds((2, S, F), BF16), _sds((2, 3, F), F32), _sds((2, 1, F), F32)],
                 dims=("parallel",))(dact, h, h, cw, cw, cb, cb), 320.0 * S / 4096)


def _swap_half(x):
    lane = lax.broadcasted_iota(jnp.int32, x.shape, 1)
    return jnp.where((lane % ROPE) < ROPE // 2, pltpu.roll(x, LANES - ROPE // 2, 1), pltpu.roll(x, ROPE // 2, 1))


def _rope(x, cos, sin_s):
    return x * cos + _swap_half(x) * sin_s


def _mla_prep(h, gq, gkv, cos, sin_s, *, name):
    S, W = h.shape
    tr = _tile(S, 512, 8)

    def body(h_ref, gq_ref, gkv_ref, cos_ref, sin_ref, cq_ref, ckv_ref, kr_ref, rq_ref, rkv_ref):
        cq = h_ref[:, 0:QRANK]
        rq = lax.rsqrt(jnp.mean(cq * cq, axis=-1, keepdims=True) + RMS_EPS)
        cq_ref[...] = (cq * rq * gq_ref[...]).astype(BF16)
        rq_ref[...] = rq
        ckv = h_ref[:, QRANK:QRANK + KVRANK]
        rkv = lax.rsqrt(jnp.mean(ckv * ckv, axis=-1, keepdims=True) + RMS_EPS)
        ckv_ref[...] = (ckv * rkv * gkv_ref[...]).astype(BF16)
        rkv_ref[...] = rkv
        kr = _rope(h_ref[:, QRANK + KVRANK:W], cos_ref[...], sin_ref[...])
        lane = lax.broadcasted_iota(jnp.int32, kr.shape, 1)
        kr = jnp.where(lane < ROPE, kr, 0.0)
        kr_ref[...] = (kr + pltpu.roll(kr, ROPE, 1)).astype(BF16)

    row = lambda w: pl.BlockSpec((tr, w), lambda i: (i, 0))
    vec = lambda w: pl.BlockSpec((1, w), lambda i: (0, 0))
    return _call(body, name=name, grid=(S // tr,),
                 in_specs=[row(W), vec(QRANK), vec(KVRANK), row(LANES), row(LANES)],
                 out_specs=[row(QRANK), row(KVRANK), row(LANES), row(1), row(1)],
                 out_shape=[_sds((S, QRANK), BF16), _sds((S, KVRANK), BF16), _sds((S, LANES), BF16), _sds((S, 1), F32), _sds((S, 1), F32)],
                 dims=("parallel",))(h, gq, gkv, cos, sin_s)


def _mla_prep_bwd(dcqn, dckvn, dkr2, h, rq, rkv, gq, gkv, cos, sin_neg, *, name):
    S, W = h.shape
    tr = _tile(S, 512, 8)

    def rms_bwd(dy, c, r, g):
        n = c * r
        dn = dy * g
        return r * (dn - n * jnp.mean(dn * n, axis=-1, keepdims=True)), jnp.sum(dy * n, axis=0, keepdims=True)

    def body(dq_ref, dkv_ref, dkr_ref, h_ref, rq_ref, rkv_ref, gq_ref, gkv_ref, cos_ref, sin_ref, dh_ref, dgq_ref, dgkv_ref):
        @pl.when(pl.program_id(0) == 0)
        def _():
            dgq_ref[...] = jnp.zeros_like(dgq_ref)
            dgkv_ref[...] = jnp.zeros_like(dgkv_ref)

        dcq, dg = rms_bwd(dq_ref[...], h_ref[:, 0:QRANK], rq_ref[...], gq_ref[...])
        dgq_ref[...] += dg
        dh_ref[:, 0:QRANK] = dcq.astype(BF16)
        dckv, dg = rms_bwd(dkv_ref[...], h_ref[:, QRANK:QRANK + KVRANK], rkv_ref[...], gkv_ref[...])
        dgkv_ref[...] += dg
        dh_ref[:, QRANK:QRANK + KVRANK] = dckv.astype(BF16)
        dk = dkr_ref[...]
        dk = dk + pltpu.roll(dk, ROPE, 1)
        dk = _rope(dk, cos_ref[...], sin_ref[...])
        lane = lax.broadcasted_iota(jnp.int32, dk.shape, 1)
        dh_ref[:, QRANK + KVRANK:W] = jnp.where(lane < ROPE, dk, 0.0).astype(BF16)

    row = lambda w: pl.BlockSpec((tr, w), lambda i: (i, 0))
    vec = lambda w: pl.BlockSpec((1, w), lambda i: (0, 0))
    return _call(body, name=name, grid=(S // tr,),
                 in_specs=[row(QRANK), row(KVRANK), row(LANES), row(W), row(1), row(1), vec(QRANK), vec(KVRANK), row(LANES), row(LANES)],
                 out_specs=[row(W), vec(QRANK), vec(KVRANK)],
                 out_shape=[_sds((S, W), BF16), _sds((1, QRANK), F32), _sds((1, KVRANK), F32)],
                 dims=("arbitrary",))(dcqn, dckvn, dkr2, h, rq, rkv, gq, gkv, cos, sin_neg)


QPAIR = 2 * NOPE + 2 * ROPE
KVPAIR = 2 * (NOPE + VDIM)


def _rope_q(q, cos, sin_s, *, name):
    S, Wq = q.shape
    tr = _tile(S, 512, 8)

    def body(q_ref, cos_ref, sin_ref, o_ref):
        o_ref[:, 0:2 * NOPE] = q_ref[:, 0:2 * NOPE].astype(BF16)
        o_ref[:, 2 * NOPE:QPAIR] = _rope(q_ref[:, 2 * NOPE:QPAIR], cos_ref[...], sin_ref[...]).astype(BF16)

    blk = pl.BlockSpec((tr, QPAIR), lambda i, p: (i, p))
    tab = pl.BlockSpec((tr, LANES), lambda i, p: (i, 0))
    return _call(body, name=name, grid=(S // tr, Wq // QPAIR), in_specs=[blk, tab, tab], out_specs=blk,
                 out_shape=_sds((S, Wq), BF16), dims=("parallel", "parallel"))(q, cos, sin_s)


ATT_T = 512


def _att_scores(qc, kc, allowed):
    s = lax.dot_general(qc, kc, _DN["nt"], preferred_element_type=F32) * SM_SCALE
    return s if allowed is None else jnp.where(allowed, s, -jnp.inf)


def _att_allowed(q0, k0, t):
    qpos = q0 + lax.broadcasted_iota(jnp.int32, (t, t), 0)
    kpos = k0 + lax.broadcasted_iota(jnp.int32, (t, t), 1)
    return (kpos // CHUNK) <= (qpos // CHUNK)


def _head_mask(r, hh):
    lane = lax.broadcasted_iota(jnp.int32, r.shape, 1)
    keep = (lane < ROPE) if hh == 0 else (lane >= ROPE)
    return jnp.where(keep, r, jnp.zeros_like(r))


def _att_q(q_ref):
    r = q_ref[:, 2 * NOPE:QPAIR]
    return [jnp.concatenate([q_ref[:, hh * NOPE:(hh + 1) * NOPE], _head_mask(r, hh)], axis=1) for hh in range(2)]


def _attn_fwd(qb, kv, kr2, *, name):
    S = qb.shape[0]
    t = _tile(S, ATT_T, 8)
    nq = S // t
    npair = HEADS // 2

    def body(q_ref, kv_ref, kr_ref, o_ref, lse_ref):
        i = pl.program_id(1)
        q0 = i * t
        qc = _att_q(q_ref)

        def step(j, carry, masked):
            k0 = pl.multiple_of(j * t, t)
            rows = pl.ds(k0, t)
            kr = kr_ref[rows, :]
            allowed = _att_allowed(q0, k0, t) if masked else None
            out = []
            for hh in range(2):
                m, l, acc = carry[hh]
                c0 = hh * (NOPE + VDIM)
                kc = jnp.concatenate([kv_ref[rows, c0:c0 + NOPE], kr], axis=1)
                v = kv_ref[rows, c0 + NOPE:c0 + NOPE + VDIM]
                s = _att_scores(qc[hh], kc, allowed)
                m_new = jnp.maximum(m, jnp.max(s, axis=1, keepdims=True))
                p = jnp.exp(s - m_new)
                a = jnp.exp(m - m_new)
                l = a * l + jnp.sum(p, axis=1, keepdims=True)
                acc = a * acc + jnp.dot(p.astype(BF16), v, preferred_element_type=F32)
                out.append((m_new, l, acc))
            return tuple(out)

        one = (jnp.full((t, 1), -jnp.inf, F32), jnp.zeros((t, 1), F32), jnp.zeros((t, VDIM), F32))
        carry = lax.fori_loop(0, i, lambda j, c: step(j, c, False), (one, one))
        carry = step(i, carry, True)
        for hh in range(2):
            m, l, acc = carry[hh]
            o_ref[:, hh * VDIM:(hh + 1) * VDIM] = (acc / l).astype(BF16)
            lse_ref[:, hh * VDIM:(hh + 1) * VDIM] = jnp.broadcast_to(m + jnp.log(l), (t, VDIM))

    return _call(body, name=name, grid=(npair, nq),
                 in_specs=[pl.BlockSpec((t, QPAIR), lambda p, i: (i, p)), pl.BlockSpec((S, KVPAIR), lambda p, i: (0, p)),
                           pl.BlockSpec((S, LANES), lambda p, i: (0, 0))],
                 out_specs=[pl.BlockSpec((t, 2 * VDIM), lambda p, i: (i, p)), pl.BlockSpec((t, 2 * VDIM), lambda p, i: (i, p))],
                 out_shape=[_sds((S, HEADS * VDIM), BF16), _sds((S, HEADS * VDIM), F32)],
                 dims=("parallel", "arbitrary"))(qb, kv, kr2)


def _attn_bwd(qb, kv, kr2, o, do, lse, *, name):
    S = qb.shape[0]
    t = _tile(S, ATT_T, 8)
    nq = S // t
    npair = HEADS // 2

    def body(q_ref, kv_ref, kr_ref, o_ref, do_ref, lse_ref, dq_ref, dkv_ref, dkr_ref, acc_ref):
        pid = pl.program_id(0)
        i = pl.program_id(1)
        q0 = i * t

        @pl.when(i == 0)
        def _():
            acc_ref[...] = jnp.zeros_like(acc_ref)

        @pl.when((i == 0) & (pid == 0))
        def _():
            dkr_ref[...] = jnp.zeros_like(dkr_ref)

        qc = _att_q(q_ref)
        dov = [do_ref[:, hh * VDIM:(hh + 1) * VDIM] for hh in range(2)]
        dsum = [jnp.sum(dov[hh].astype(F32) * o_ref[:, hh * VDIM:(hh + 1) * VDIM].astype(F32), axis=1, keepdims=True)
                for hh in range(2)]
        lse_v = [lse_ref[:, hh * VDIM:hh * VDIM + 1] for hh in range(2)]

        def step(j, carry, masked):
            k0 = pl.multiple_of(j * t, t)
            rows = pl.ds(k0, t)
            kr = kr_ref[rows, :]
            allowed = _att_allowed(q0, k0, t) if masked else None
            out = []
            dkr = jnp.zeros((t, LANES), F32)
            for hh in range(2):
                c0 = hh * (NOPE + VDIM)
                kc = jnp.concatenate([kv_ref[rows, c0:c0 + NOPE], kr], axis=1)
                v = kv_ref[rows, c0 + NOPE:c0 + NOPE + VDIM]
                p = jnp.exp(_att_scores(qc[hh], kc, allowed) - lse_v[hh])
                dp = lax.dot_general(dov[hh], v, _DN["nt"], preferred_element_type=F32)
                ds = (p * (dp - dsum[hh]) * SM_SCALE).astype(BF16)
                acc_ref[rows, c0 + NOPE:c0 + NOPE + VDIM] += lax.dot_general(p.astype(BF16), dov[hh], _DN["tn"], preferred_element_type=F32)
                dkc = lax.dot_general(ds, qc[hh], _DN["tn"], preferred_element_type=F32)
                acc_ref[rows, c0:c0 + NOPE] += dkc[:, 0:NOPE]
                dkr = dkr + dkc[:, NOPE:]
                out.append(carry[hh] + jnp.dot(ds, kc, preferred_element_type=F32))
            dkr_ref[rows, :] += dkr
            return tuple(out)

        zero = jnp.zeros((t, NOPE + LANES), F32)
        carry = lax.fori_loop(0, i, lambda j, c: step(j, c, False), (zero, zero))
        dqc = step(i, carry, True)
        for hh in range(2):
            dq_ref[:, hh * NOPE:(hh + 1) * NOPE] = dqc[hh][:, 0:NOPE]
        dq_ref[:, 2 * NOPE:QPAIR] = _head_mask(dqc[0][:, NOPE:], 0) + _head_mask(dqc[1][:, NOPE:], 1)

        @pl.when(i == nq - 1)
        def _():
            dkv_ref[...] = acc_ref[...].astype(BF16)

    qspec = pl.BlockSpec((t, QPAIR), lambda p, i: (i, p))
    hspec = pl.BlockSpec((t, 2 * VDIM), lambda p, i: (i, p))
    kvspec = pl.BlockSpec((S, KVPAIR), lambda p, i: (0, p))
    krspec = pl.BlockSpec((S, LANES), lambda p, i: (0, 0))
    return _ticked(_call(body, name=name, grid=(npair, nq),
                 in_specs=[qspec, kvspec, krspec, hspec, hspec, hspec],
                 out_specs=[qspec, kvspec, krspec],
                 out_shape=[_sds(qb.shape, F32), _sds(kv.shape, BF16), _sds((S, LANES), F32)],
                 scratch=[pltpu.VMEM((S, KVPAIR), F32)], dims=("arbitrary", "arbitrary"))(qb, kv, kr2, o, do, lse),
                   640.0 * (S / 4096) ** 2)


def _coords():
    return lax.axis_index("x"), lax.axis_index("y"), lax.axis_index("c")


def _all_gather_body(n, shake):
    def body(*refs):
        ins, outs = refs[:n], refs[n:2 * n]
        send_sems, recv_sems, local_sems = refs[2 * n:]
        x, y, c = _coords()
        me, sib = (x, y, c), (x, y, 1 - c)
        chips = [(1 - x, y), (x, 1 - y), (1 - x, 1 - y)]
        if shake:
            _handshake([sib] + [(*chip, c) for chip in chips])

        def copy(a, k, block, to, src=None):
            dst = outs[a].at[4 * block[0] + 2 * block[1] + block[2]]
            return pltpu.make_async_remote_copy(src_ref=dst if src is None else src, dst_ref=dst,
                                                send_sem=send_sems.at[a, k], recv_sem=recv_sems.at[a, k],
                                                device_id=to, device_id_type=MESH)

        mine = [pltpu.make_async_copy(ins[a], outs[a].at[4 * x + 2 * y + c], local_sems.at[a]) for a in range(n)]
        for cp in mine:
            cp.start()
        sends = []
        for a in range(n):
            sends.append(copy(a, 0, me, sib, src=ins[a]))
            sends += [copy(a, 1 + j, me, (*chip, c), src=ins[a]) for j, chip in enumerate(chips)]
        for cp in sends:
            cp.start()
        for j, chip in enumerate(chips):
            for a in range(n):
                copy(a, 1 + j, (*chip, c), me).wait_recv()
                fwd = copy(a, 4 + j, (*chip, c), sib)
                fwd.start()
                sends.append(fwd)
        for a in range(n):
            copy(a, 0, sib, me).wait_recv()
            for j, chip in enumerate(chips):
                copy(a, 4 + j, (*chip, 1 - c), me).wait_recv()
        for cp in sends:
            cp.wait_send()
        for cp in mine:
            cp.wait()

    return body


def _all_gather(arrs, *, name):
    n = len(arrs)
    outs = _call(_all_gather_body(n, False), name=name, in_specs=[HBM_SPEC] * n, out_specs=[HBM_SPEC] * n,
                 out_shape=[_sds((N_DEV,) + a.shape, a.dtype) for a in arrs],
                 scratch=[pltpu.SemaphoreType.DMA((n, 7)), pltpu.SemaphoreType.DMA((n, 7)), pltpu.SemaphoreType.DMA((n,))])(*arrs)
    return list(outs)


def _sequencer_call(body, *, name, out_type, n_sems, collective_id):
    return pl.kernel(body, out_type=out_type, mesh=plsc.ScalarSubcoreMesh(axis_name="sq", num_cores=1), name=name,
                     scratch_types=[pltpu.SemaphoreType.DMA(n_sems), pltpu.SemaphoreType.DMA(n_sems), pltpu.SemaphoreType.DMA((n_sems[0],))],
                     compiler_params=pltpu.CompilerParams(collective_id=collective_id))


def _handshake(peers):
    barrier = pltpu.get_barrier_semaphore()
    for p in peers:
        pl.semaphore_signal(barrier, inc=1, device_id=p, device_id_type=MESH)
    pl.semaphore_wait(barrier, len(peers))


AG_ID, PAIR_ID, CHIP_ID = 1, 2, 3


def _all_gather_sc(arrs, *, name):
    n = len(arrs)
    outs = _sequencer_call(_all_gather_body(n, True), name=name, out_type=[_sds((N_DEV,) + a.shape, a.dtype) for a in arrs],
                           n_sems=(n, 7), collective_id=AG_ID)(*arrs)
    return list(outs)


def _pair_exchange_sc(gs, *, name):
    n = len(gs)

    def body(*refs):
        ins, outs = refs[:n], refs[n:2 * n]
        send_sems, recv_sems, _ = refs[2 * n:]
        x, y, c = _coords()
        sib = (x, y, 1 - c)
        _handshake([sib])
        cps = []
        for a in range(n):
            for j in range(4):
                cps.append(pltpu.make_async_remote_copy(src_ref=ins[a].at[2 * j + (1 - c)], dst_ref=outs[a].at[j],
                                                        send_sem=send_sems.at[a, j], recv_sem=recv_sems.at[a, j],
                                                        device_id=sib, device_id_type=MESH))
        for cp in cps:
            cp.start()
        for cp in cps:
            cp.wait()

    outs = _sequencer_call(body, name=name, out_type=[_sds((4,) + g.shape[1:], g.dtype) for g in gs],
                           n_sems=(n, 4), collective_id=PAIR_ID)(*gs)
    return list(outs)


def _chip_exchange_sc(ps, *, name):
    n = len(ps)

    def body(*refs):
        ins, outs = refs[:n], refs[n:2 * n]
        send_sems, recv_sems, _ = refs[2 * n:]
        x, y, c = _coords()
        chips = [(1 - x, y), (x, 1 - y), (1 - x, 1 - y)]
        _handshake([(*chip, c) for chip in chips])
        cps = []
        for a in range(n):
            for r, (px, py) in enumerate(chips):
                cps.append(pltpu.make_async_remote_copy(src_ref=ins[a].at[2 * px + py], dst_ref=outs[a].at[r],
                                                        send_sem=send_sems.at[a, r], recv_sem=recv_sems.at[a, r],
                                                        device_id=(px, py, c), device_id_type=MESH))
        for cp in cps:
            cp.start()
        for cp in cps:
            cp.wait()

    outs = _sequencer_call(body, name=name, out_type=[_sds((3,) + p.shape[1:], p.dtype) for p in ps],
                           n_sems=(n, 3), collective_id=CHIP_ID)(*ps)
    return list(outs)


def _row_tile(R, C, budget=1 << 20, mult=16):
    want = max(mult, budget // (4 * C))
    if R <= want:
        return R
    t = (want // mult) * mult
    while t >= mult:
        if R % t == 0:
            return t
        t -= mult
    return R


RS_TILE_BYTES = 6 << 20


def _pair_sum(g, l1, c_idx, *, name):
    _, R, C = g.shape
    tr = _row_tile(R, C, budget=RS_TILE_BYTES)
    g4 = g.reshape(4, 2, R, C)

    def body(c_ref, g_ref, l_ref, o_ref):
        o_ref[...] = (g_ref[...].astype(F32) + l_ref[...].astype(F32)).astype(o_ref.dtype)

    return _call(body, name=name, n_prefetch=1, grid=(4, R // tr),
                 in_specs=[pl.BlockSpec((None, None, tr, C), lambda j, r, c_ref: (j, c_ref[0], r, 0)),
                           pl.BlockSpec((None, tr, C), lambda j, r, c_ref: (j, r, 0))],
                 out_specs=pl.BlockSpec((None, tr, C), lambda j, r, c_ref: (j, r, 0)),
                 out_shape=_sds((4, R, C), g.dtype))(c_idx, g4, l1)


def _chip_sum(p, l2, chip_idx, *, name):
    _, R, C = p.shape
    tr = _row_tile(R, C, budget=RS_TILE_BYTES)

    def body(j_ref, p_ref, l_ref, o_ref):
        o_ref[...] = ((p_ref[...].astype(F32) + l_ref[0].astype(F32)) + l_ref[1].astype(F32)) + l_ref[2].astype(F32)

    return _call(body, name=name, n_prefetch=1, grid=(R // tr,),
                 in_specs=[pl.BlockSpec((None, tr, C), lambda r, j_ref: (j_ref[0], r, 0)),
                           pl.BlockSpec((3, tr, C), lambda r, j_ref: (0, r, 0))],
                 out_specs=pl.BlockSpec((tr, C), lambda r, j_ref: (r, 0)), out_shape=_sds((R, C), F32))(chip_idx, p, l2)


class _Reducer:
    PAIR_US = (12.0, 7.0)
    CHIP_US = (15.0, 43.0)

    def __init__(self, c_idx, chip_idx):
        self.c_idx, self.chip_idx = c_idx, chip_idx
        self.units, self.out, self.sc_free = [], {}, 0.0

    def _sequencer_done(self, cost):
        self.sc_free = max(self.sc_free, _SCHED.clock) + cost[0] + cost[1] * self._mb
        return self.sc_free

    def submit(self, name, g):
        R = g.shape[0] // N_DEV
        g8 = g.reshape((N_DEV, R, g.shape[1]))
        self._mb = R * g.shape[1] * g.dtype.itemsize / 1e6
        (l1,) = _pair_exchange_sc([g8], name=f"rs_pair_{name}")
        self.units.append(dict(name=name, g8=g8, l1=l1, mb=self._mb, stage=1, ready=self._sequencer_done(self.PAIR_US)))

    def advance(self, force=False):
        for u in self.units:
            if not (force or _SCHED.clock >= u["ready"]):
                continue
            if u["stage"] == 1:
                p = _pair_sum(u.pop("g8"), u.pop("l1"), self.c_idx, name=f"rs_psum_{u['name']}")
                (l2,) = _chip_exchange_sc([p], name=f"rs_chip_{u['name']}")
                self._mb = u["mb"]
                u.update(p=p, l2=l2, stage=2, ready=self._sequencer_done(self.CHIP_US))
            elif u["stage"] == 2:
                self.out[u["name"]] = _chip_sum(u.pop("p"), u.pop("l2"), self.chip_idx, name=f"rs_csum_{u['name']}")
                u["stage"] = 3

    def finish(self):
        self.advance(force=True)
        self.advance(force=True)
        return self.out


def _grad_ready(name, g):
    if _SCHED.reducer is not None:
        _SCHED.reducer.submit(name, g)
    return g


def _sum8(a, *, name):
    _, R, C = a.shape
    tr = _row_tile(R, C, budget=1 << 18, mult=8)

    def body(a_ref, o_ref):
        acc = a_ref[0]
        for k in range(1, N_DEV):
            acc = acc + a_ref[k]
        o_ref[...] = acc

    return _call(body, name=name, grid=(R // tr,), in_specs=[pl.BlockSpec((N_DEV, tr, C), lambda r: (0, r, 0))],
                 out_specs=pl.BlockSpec((tr, C), lambda r: (r, 0)), out_shape=_sds((R, C), F32), dims=("parallel",))(a)


def _adamw(w, g, m, v, *, name):
    R, C = w.shape
    tr = _row_tile(R, C, budget=1 << 20, mult=8)
    c1 = 1.0 - ADAM_B1 ** ADAM_STEP
    c2 = 1.0 - ADAM_B2 ** ADAM_STEP

    def body(w_ref, g_ref, m_ref, v_ref, d_ref, mo_ref, vo_ref):
        gv = g_ref[...]
        mn = ADAM_B1 * m_ref[...] + (1.0 - ADAM_B1) * gv
        vn = ADAM_B2 * v_ref[...] + (1.0 - ADAM_B2) * (gv * gv)
        mo_ref[...] = mn
        vo_ref[...] = vn
        d_ref[...] = -ADAM_LR * ((mn / c1) / (jnp.sqrt(vn / c2) + ADAM_EPS) + ADAM_WD * w_ref[...])

    blk = pl.BlockSpec((tr, C), lambda r: (r, 0))
    return _call(body, name=name, grid=(R // tr,), in_specs=[blk] * 4, out_specs=[blk] * 3,
                 out_shape=[_sds((R, C), F32)] * 3, dims=("parallel",))(w, g, m, v)


def _gmlp_fwd(xb, W, sp, tag):
    zp = _mm(xb, W["w_in_t"], "nt", name=f"gm_zp_{tag}", out_dtype=BF16)
    p, s, vhat, rstd = _gm_mid_fwd(zp, sp["ln_g"], sp["ln_b"], sp["w_s"], sp["b_st"], name=f"gm_mid_{tag}")
    m = _mm(p, W["w_out"], "nn", name=f"gm_out_{tag}", out_dtype=F32)
    return m, dict(xb=xb, zp=zp, p=p, s=s, vhat=vhat, rstd=rstd)


def _gmlp_bwd(drb, dr, W, sp, sv, tag):
    d_w_out = _grad_ready(f"{tag}_w_out", _mm(sv["p"], drb, "tn", name=f"gm_dwout_{tag}", out_dtype=BF16, tk=TOKENS_K))
    dp = _mm(drb, W["w_out"], "nt", name=f"gm_dp_{tag}", out_dtype=BF16)
    dzp, dws, dbs, dlg, dlb = _gm_mid_bwd(dp, sv["zp"], sv["s"], sv["vhat"], sv["rstd"], sp["ln_g"], sp["ln_b"], sp["w_s"],
                                          name=f"gm_midb_{tag}")
    d_w_in_t = _grad_ready(f"{tag}_w_in_t", _mm(dzp, sv["xb"], "tn", name=f"gm_dwin_{tag}", out_dtype=BF16, tk=TOKENS_K))
    dx = _mm(dzp, W["w_in_t"], "nn", name=f"gm_dx_{tag}", out_dtype=F32, res=dr, res_scale=ALPHA)
    return dx, dict(w_in_t=d_w_in_t, w_out=d_w_out), dict(w_s=dws, b_s=dbs, ln_g=dlg, ln_b=dlb)


def _mla_fwd(xb, W, sp, rope, tag):
    cos, sin_s, _ = rope
    h = _mm(xb, W["w_in"], "nn", name=f"mla_h_{tag}", out_dtype=F32, tn=W["w_in"].shape[1])
    cqn, ckvn, kr2, rq, rkv = _mla_prep(h, sp["gq"], sp["gkv"], cos, sin_s, name=f"mla_prep_{tag}")
    q = _mm(cqn, W["w_qb_t"], "nt", name=f"mla_q_{tag}", out_dtype=F32)
    qb = _rope_q(q, cos, sin_s, name=f"mla_ropeq_{tag}")
    kv = _mm(ckvn, W["w_kvb_t"], "nt", name=f"mla_kv_{tag}", out_dtype=BF16)
    o, lse = _attn_fwd(qb, kv, kr2, name=f"mla_attn_{tag}")
    m = _mm(o, W["w_out"], "nn", name=f"mla_out_{tag}", out_dtype=F32)
    return m, dict(xb=xb, h=h, cqn=cqn, ckvn=ckvn, kr2=kr2, rq=rq, rkv=rkv, qb=qb, kv=kv, o=o, lse=lse)


def _mla_bwd(drb, dr, W, sp, rope, sv, tag):
    cos, _, sin_neg = rope
    d_w_out = _grad_ready(f"{tag}_w_out", _mm(sv["o"], drb, "tn", name=f"mla_dwout_{tag}", out_dtype=BF16, tk=TOKENS_K))
    do = _mm(drb, W["w_out"], "nt", name=f"mla_do_{tag}", out_dtype=BF16)
    dq, dkv, dkr2 = _attn_bwd(sv["qb"], sv["kv"], sv["kr2"], sv["o"], do, sv["lse"], name=f"mla_attnb_{tag}")
    dqb = _rope_q(dq, cos, sin_neg, name=f"mla_ropedq_{tag}")
    d_w_qb_t = _grad_ready(f"{tag}_w_qb_t", _mm(dqb, sv["cqn"], "tn", name=f"mla_dwqb_{tag}", out_dtype=BF16, tk=TOKENS_K))
    d_w_kvb_t = _grad_ready(f"{tag}_w_kvb_t", _mm(dkv, sv["ckvn"], "tn", name=f"mla_dwkvb_{tag}", out_dtype=BF16, tk=TOKENS_K))
    dcqn = _mm(dqb, W["w_qb_t"], "nn", name=f"mla_dcq_{tag}", out_dtype=F32, tk=dqb.shape[1])
    dckvn = _mm(dkv, W["w_kvb_t"], "nn", name=f"mla_dckv_{tag}", out_dtype=F32, tk=dkv.shape[1])
    dh, dgq, dgkv = _mla_prep_bwd(dcqn, dckvn, dkr2, sv["h"], sv["rq"], sv["rkv"], sp["gq"], sp["gkv"], cos, sin_neg,
                                  name=f"mla_prepb_{tag}")
    d_w_in = _grad_ready(f"{tag}_w_in", _mm(sv["xb"], dh, "tn", name=f"mla_dwin_{tag}", out_dtype=BF16, tn=dh.shape[1], tk=TOKENS_K))
    dx = _mm(dh, W["w_in"], "nt", name=f"mla_dx_{tag}", out_dtype=F32, tk=dh.shape[1], res=dr, res_scale=ALPHA)
    return dx, dict(w_in=d_w_in, w_qb_t=d_w_qb_t, w_kvb_t=d_w_kvb_t, w_out=d_w_out), dict(gq=dgq, gkv=dgkv)


def _ffn_fwd(xb, W, sp, tag):
    S = xb.shape[0]
    F = W["w_down"].shape[0]
    h = _mm(xb, W["w_up_t"], "nt", name=f"ffn_h_{tag}", out_dtype=BF16, tm=2048, tn=256)
    act = _ffn_mid_fwd(h, sp["cw"], sp["cb"], name=f"ffn_mid_{tag}")
    f = _mm(act, W["w_down"], "nn", name=f"ffn_out_{tag}", out_dtype=F32, tm=512, tn=1024, tk=F)
    return f, dict(xb=xb, h=h, act=act)


def _ffn_bwd(drb, dr, W, sp, sv, tag):
    F = W["w_down"].shape[0]
    S, D = drb.shape
    d_w_down = _grad_ready(f"{tag}_w_down", _mm(sv["act"], drb, "tn", name=f"ffn_dwdown_{tag}", out_dtype=BF16, tm=F, tn=512, tk=1024))
    dact = _mm(drb, W["w_down"], "nt", name=f"ffn_dact_{tag}", out_dtype=BF16, tm=512, tn=F, tk=D, b_resident=True)
    dh, dcw, dcb = _ffn_mid_bwd(dact, sv["h"], sp["cw"], sp["cb"], name=f"ffn_midb_{tag}")
    dh = dh.reshape(2 * S, F)
    d_w_up_t = _grad_ready(f"{tag}_w_up_t", _mm(dh, sv["xb"], "tn", name=f"ffn_dwup_{tag}", out_dtype=BF16, tm=F, tn=512, tk=1024, a_parts=2))
    dx = _mm(dh, W["w_up_t"], "nn", name=f"ffn_dx_{tag}", out_dtype=F32, tm=512, tn=1024, tk=F, res=dr, res_scale=ALPHA, a_parts=2)
    dcw = jnp.concatenate([dcw[0], dcw[1]], axis=1)
    dcb = jnp.concatenate([dcb[0], dcb[1]], axis=1)
    return dx, dict(w_up_t=d_w_up_t, w_down=d_w_down), dict(cw=dcw, cb=dcb)


def _rope_tables(S):
    half = ROPE // 2
    inv_freq = ROPE_THETA ** (-jnp.arange(half, dtype=F32) / half)
    ang = jnp.arange(S, dtype=F32)[:, None] * inv_freq[None, :]
    cos, sin = jnp.cos(ang), jnp.sin(ang)
    cos128 = jnp.concatenate([cos] * 4, axis=1)
    sin128 = jnp.concatenate([-sin, sin, -sin, sin], axis=1)
    return cos128, sin128, -sin128


def _fwd_bwd(x, tgt, Wm, Wf, spm, spf, ln):
    S, D = x.shape
    _SCHED.last, _SCHED.clock = None, 0.0
    rope = _rope_tables(S)
    xf, xb = x, x.astype(BF16)
    saved = []
    for i in range(DEPTH):
        if i % 2 == 0:
            m, svm = _gmlp_fwd(xb, Wm[i], spm[i], f"l{i}")
        else:
            m, svm = _mla_fwd(xb, Wm[i], spm[i], rope, f"l{i}")
        y, yb, xh1, rs1 = _ln_fwd(xf, m, ln["mix_g"][i], ln["mix_b"][i], name=f"ln_mix_{i}")
        f, svf = _ffn_fwd(yb, Wf[i], spf[i], f"l{i}")
        z, zb, xh2, rs2 = _ln_fwd(y, f, ln["ffn_g"][i], ln["ffn_b"][i], name=f"ln_ffn_{i}")
        saved.append((svm, xh1, rs1, svf, xh2, rs2))
        xf, xb = z, zb
    lp, dy = _loss_kernel(xf, tgt, name="loss")
    gm_big, gm_small, gf_big, gf_small, gln = [None] * DEPTH, [None] * DEPTH, [None] * DEPTH, [None] * DEPTH, [None] * DEPTH
    for i in reversed(range(DEPTH)):
        svm, xh1, rs1, svf, xh2, rs2 = saved[i]
        dr, drb, dg2, db2 = _ln_bwd(dy, xh2, rs2, ln["ffn_g"][i], name=f"lnb_ffn_{i}")
        dy, gf_big[i], gf_small[i] = _ffn_bwd(drb, dr, Wf[i], spf[i], svf, f"l{i}")
        dr, drb, dg1, db1 = _ln_bwd(dy, xh1, rs1, ln["mix_g"][i], name=f"lnb_mix_{i}")
        if i % 2 == 0:
            dy, gm_big[i], gm_small[i] = _gmlp_bwd(drb, dr, Wm[i], spm[i], svm, f"l{i}")
        else:
            dy, gm_big[i], gm_small[i] = _mla_bwd(drb, dr, Wm[i], spm[i], rope, svm, f"l{i}")
        gln[i] = dict(mix_g=dg1, mix_b=db1, ffn_g=dg2, ffn_b=db2)
    return lp, dy, gm_big, gm_small, gf_big, gf_small, gln


def _perm_q_rows(wt):
    hd = NOPE + ROPE
    return jnp.concatenate([wt[0:NOPE], wt[hd:hd + NOPE], wt[NOPE:hd], wt[hd + NOPE:2 * hd]], axis=0)


def _unperm_q_rows(wt):
    return jnp.concatenate([wt[0:NOPE], wt[2 * NOPE:2 * NOPE + ROPE], wt[NOPE:2 * NOPE], wt[2 * NOPE + ROPE:]], axis=0)


def _pad_cols(w, to):
    return jnp.pad(w, ((0, 0), (0, to - w.shape[1])))


def _pack(parts, rows_mult=8):
    flat = jnp.concatenate([p.reshape(-1).astype(F32) for p in parts])
    n = flat.shape[0]
    per = LANES * rows_mult
    tot = ((n + per - 1) // per) * per
    return jnp.pad(flat, (0, tot - n)).reshape(tot // LANES, LANES)


def _unpack(buf, shapes):
    flat = buf.reshape(-1)
    out, off = [], 0
    for s in shapes:
        n = 1
        for d in s:
            n *= d
        out.append(flat[off:off + n].reshape(s))
        off += n
    return out


def kernel(x, gm_w_in, gm_ln_g, gm_ln_b, gm_w_s, gm_b_s, gm_w_out, mla_w_in, mla_q_norm_g, mla_kv_norm_g, mla_w_q_b, mla_w_kv_b, mla_w_out, ffn_w_up, ffn_conv_w, ffn_conv_b, ffn_w_down, ln_mix_g, ln_mix_b, ln_ffn_g, ln_ffn_b, loss_target, m_gm_w_in, m_gm_ln_g, m_gm_ln_b, m_gm_w_s, m_gm_b_s, m_gm_w_out, m_mla_w_in, m_mla_q_norm_g, m_mla_kv_norm_g, m_mla_w_q_b, m_mla_w_kv_b, m_mla_w_out, m_ffn_w_up, m_ffn_conv_w, m_ffn_conv_b, m_ffn_w_down, m_ln_mix_g, m_ln_mix_b, m_ln_ffn_g, m_ln_ffn_b, v_gm_w_in, v_gm_ln_g, v_gm_ln_b, v_gm_w_s, v_gm_b_s, v_gm_w_out, v_mla_w_in, v_mla_q_norm_g, v_mla_kv_norm_g, v_mla_w_q_b, v_mla_w_kv_b, v_mla_w_out, v_ffn_w_up, v_ffn_conv_w, v_ffn_conv_b, v_ffn_w_down, v_ln_mix_g, v_ln_mix_b, v_ln_ffn_g, v_ln_ffn_b):
    S, D = x.shape[1], x.shape[2]
    xi, yi, ci = _coords()
    dev = 4 * xi + 2 * yi + ci
    c_idx = jnp.reshape(ci, (1,)).astype(jnp.int32)
    chip_idx = jnp.reshape(2 * xi + yi, (1,)).astype(jnp.int32)
    w_in_cols = mla_w_in.shape[2]
    w_in_pad = ((w_in_cols + LANES - 1) // LANES) * LANES
    n_gm, n_mla = gm_w_in.shape[0], mla_w_in.shape[0]

    cw_l = ffn_conv_w.shape[2]
    small_in = _pack([mla_q_norm_g, mla_kv_norm_g, ffn_conv_w])

    def gather(shards, name):
        full = _all_gather_sc(list(shards.values()), name=name)
        return {k: f.reshape((-1, f.shape[2])) for k, f in zip(shards.keys(), full)}

    Wm, Wf = [None] * DEPTH, [None] * DEPTH
    for i in range(DEPTH):
        s = i // 2
        if i == 0:
            Wm[i] = gather(dict(w_in_t=gm_w_in[s].T.astype(BF16), w_out=gm_w_out[s].astype(BF16), small=small_in), f"ag_gm_{i}")
            small_all = Wm[i].pop("small").reshape((N_DEV,) + small_in.shape)
        elif i % 2 == 0:
            Wm[i] = gather(dict(w_in_t=gm_w_in[s].T.astype(BF16), w_out=gm_w_out[s].astype(BF16)), f"ag_gm_{i}")
        else:
            Wm[i] = gather(dict(w_in=_pad_cols(mla_w_in[s], w_in_pad).astype(BF16),
                                w_qb_t=_perm_q_rows(mla_w_q_b[s].T).astype(BF16),
                                w_kvb_t=mla_w_kv_b[s].T.astype(BF16), w_out=mla_w_out[s].astype(BF16)), f"ag_mla_{i}")
        Wf[i] = gather(dict(w_up_t=ffn_w_up[i].T.astype(BF16), w_down=ffn_w_down[i].astype(BF16)), f"ag_ffn_{i}")

    gq_parts, gkv_parts, cw_parts = [], [], []
    for k in range(N_DEV):
        a, b, c_ = _unpack(small_all[k], [mla_q_norm_g.shape, mla_kv_norm_g.shape, ffn_conv_w.shape])
        gq_parts.append(a)
        gkv_parts.append(b)
        cw_parts.append(c_)
    gq_full = jnp.concatenate(gq_parts, axis=1)
    gkv_full = jnp.concatenate(gkv_parts, axis=1)
    cw_full = jnp.concatenate(cw_parts, axis=2)

    spm, spf = [None] * DEPTH, [None] * DEPTH
    for i in range(DEPTH):
        s = i // 2
        if i % 2 == 0:
            spm[i] = dict(ln_g=gm_ln_g[s][None], ln_b=gm_ln_b[s][None], w_s=gm_w_s[s], b_st=gm_b_s[s].T)
        else:
            spm[i] = dict(gq=gq_full[s][None], gkv=gkv_full[s][None])
        spf[i] = dict(cw=cw_full[i], cb=ffn_conv_b[i][None])
    ln = dict(mix_g=ln_mix_g[:, None], mix_b=ln_mix_b[:, None], ffn_g=ln_ffn_g[:, None], ffn_b=ln_ffn_b[:, None])

    _SCHED.reducer = _Reducer(c_idx, chip_idx)
    lp, grad_x, _, gm_small, _, gf_small, gln = _fwd_bwd(x[0], loss_target[0], Wm, Wf, spm, spf, ln)
    red = _SCHED.reducer.finish()
    _SCHED.reducer = None
    rm = [{k: red[f"l{i}_{k}"] for k in Wm[i]} for i in range(DEPTH)]
    rf = [{k: red[f"l{i}_{k}"] for k in Wf[i]} for i in range(DEPTH)]

    g_gm_w_in = jnp.stack([rm[2 * s]["w_in_t"].T for s in range(n_gm)])
    g_gm_w_out = jnp.stack([rm[2 * s]["w_out"] for s in range(n_gm)])
    g_mla_w_in = jnp.stack([rm[2 * s + 1]["w_in"][:, :w_in_cols] for s in range(n_mla)])
    g_mla_w_q_b = jnp.stack([_unperm_q_rows(rm[2 * s + 1]["w_qb_t"]).T for s in range(n_mla)])
    g_mla_w_kv_b = jnp.stack([rm[2 * s + 1]["w_kvb_t"].T for s in range(n_mla)])
    g_mla_w_out = jnp.stack([rm[2 * s + 1]["w_out"] for s in range(n_mla)])
    g_ffn_w_up = jnp.stack([rf[i]["w_up_t"].T for i in range(DEPTH)])
    g_ffn_w_down = jnp.stack([rf[i]["w_down"] for i in range(DEPTH)])

    small_g = [
        jnp.stack([gm_small[2 * s]["ln_g"][0] for s in range(n_gm)]),
        jnp.stack([gm_small[2 * s]["ln_b"][0] for s in range(n_gm)]),
        jnp.stack([gm_small[2 * s]["w_s"] for s in range(n_gm)]),
        jnp.stack([gm_small[2 * s]["b_s"][:, :, 0] for s in range(n_gm)]),
        jnp.stack([gf_small[i]["cb"][0] for i in range(DEPTH)]),
        jnp.stack([gln[i]["mix_g"][0] for i in range(DEPTH)]),
        jnp.stack([gln[i]["mix_b"][0] for i in range(DEPTH)]),
        jnp.stack([gln[i]["ffn_g"][0] for i in range(DEPTH)]),
        jnp.stack([gln[i]["ffn_b"][0] for i in range(DEPTH)]),
        jnp.stack([gm_small[2 * s + 1]["gq"][0] for s in range(n_mla)]),
        jnp.stack([gm_small[2 * s + 1]["gkv"][0] for s in range(n_mla)]),
        jnp.stack([gf_small[i]["cw"] for i in range(DEPTH)]),
        jnp.reshape(0.5 * jnp.sum(lp) / D, (1,)),
    ]
    small_shapes = [g.shape for g in small_g]
    (small_g_all,) = _all_gather([_pack(small_g)], name="ag_small_grads")
    small_sum = _unpack(_sum8(small_g_all, name="small_grad_sum"), small_shapes)
    (g_gm_ln_g, g_gm_ln_b, g_gm_w_s, g_gm_b_s, g_ffn_conv_b, g_ln_mix_g, g_ln_mix_b, g_ln_ffn_g, g_ln_ffn_b,
     gq_all, gkv_all, cw_all, loss1) = small_sum
    loss = jnp.reshape(loss1, ())
    qn_l = mla_q_norm_g.shape[1]
    g_mla_q_norm_g = lax.dynamic_slice_in_dim(gq_all, dev * qn_l, qn_l, axis=1)
    g_mla_kv_norm_g = lax.dynamic_slice_in_dim(gkv_all, dev * qn_l, qn_l, axis=1)
    g_ffn_conv_w = lax.dynamic_slice_in_dim(cw_all, dev * cw_l, cw_l, axis=2)

    def adam_big(w, g, m, v, tag):
        sh = w.shape
        two = lambda a: a.reshape((-1, sh[-1]))
        d, mn, vn = _adamw(two(w), two(g), two(m), two(v), name=f"adamw_{tag}")
        return d.reshape(sh), mn.reshape(sh), vn.reshape(sh)

    big = [("gm_w_in", gm_w_in, g_gm_w_in, m_gm_w_in, v_gm_w_in), ("gm_w_out", gm_w_out, g_gm_w_out, m_gm_w_out, v_gm_w_out),
           ("mla_w_in", mla_w_in, g_mla_w_in, m_mla_w_in, v_mla_w_in), ("mla_w_q_b", mla_w_q_b, g_mla_w_q_b, m_mla_w_q_b, v_mla_w_q_b),
           ("mla_w_kv_b", mla_w_kv_b, g_mla_w_kv_b, m_mla_w_kv_b, v_mla_w_kv_b), ("mla_w_out", mla_w_out, g_mla_w_out, m_mla_w_out, v_mla_w_out),
           ("ffn_w_up", ffn_w_up, g_ffn_w_up, m_ffn_w_up, v_ffn_w_up), ("ffn_w_down", ffn_w_down, g_ffn_w_down, m_ffn_w_down, v_ffn_w_down)]
    res = {}
    for tag, w, g, m, v in big:
        res[tag] = (g,) + adam_big(w, g, m, v, tag)

    small = [("gm_ln_g", gm_ln_g, g_gm_ln_g, m_gm_ln_g, v_gm_ln_g), ("gm_ln_b", gm_ln_b, g_gm_ln_b, m_gm_ln_b, v_gm_ln_b),
             ("gm_w_s", gm_w_s, g_gm_w_s, m_gm_w_s, v_gm_w_s), ("gm_b_s", gm_b_s, g_gm_b_s, m_gm_b_s, v_gm_b_s),
             ("mla_q_norm_g", mla_q_norm_g, g_mla_q_norm_g, m_mla_q_norm_g, v_mla_q_norm_g),
             ("mla_kv_norm_g", mla_kv_norm_g, g_mla_kv_norm_g, m_mla_kv_norm_g, v_mla_kv_norm_g),
             ("ffn_conv_w", ffn_conv_w, g_ffn_conv_w, m_ffn_conv_w, v_ffn_conv_w), ("ffn_conv_b", ffn_conv_b, g_ffn_conv_b, m_ffn_conv_b, v_ffn_conv_b),
             ("ln_mix_g", ln_mix_g, g_ln_mix_g, m_ln_mix_g, v_ln_mix_g), ("ln_mix_b", ln_mix_b, g_ln_mix_b, m_ln_mix_b, v_ln_mix_b),
             ("ln_ffn_g", ln_ffn_g, g_ln_ffn_g, m_ln_ffn_g, v_ln_ffn_g), ("ln_ffn_b", ln_ffn_b, g_ln_ffn_b, m_ln_ffn_b, v_ln_ffn_b)]
    shapes = [t[1].shape for t in small]
    d_s, m_s, v_s = _adamw(_pack([t[1] for t in small]), _pack([t[2] for t in small]), _pack([t[3] for t in small]),
                           _pack([t[4] for t in small]), name="adamw_small")
    d_l, m_l, v_l = _unpack(d_s, shapes), _unpack(m_s, shapes), _unpack(v_s, shapes)
    for (tag, _, g, _, _), d, mn, vn in zip(small, d_l, m_l, v_l):
        res[tag] = (g, d, mn, vn)

    order = ["gm_w_in", "gm_ln_g", "gm_ln_b", "gm_w_s", "gm_b_s", "gm_w_out", "mla_w_in", "mla_q_norm_g", "mla_kv_norm_g",
             "mla_w_q_b", "mla_w_kv_b", "mla_w_out", "ffn_w_up", "ffn_conv_w", "ffn_conv_b", "ffn_w_down",
             "ln_mix_g", "ln_mix_b", "ln_ffn_g", "ln_ffn_b"]
    out = [loss, grad_x[None]]
    for q in range(4):
        out += [res[k][q] for k in order]
    return tuple(out)
```

```python
import jax
import jax.numpy as jnp
from jax import lax
from jax.experimental import pallas as pl
from jax.experimental.pallas import tpu as pltpu
from jax.experimental.pallas import tpu_sc as plsc

F32, BF16 = jnp.float32, jnp.bfloat16

DEPTH = 4
CHUNK = 64
GM_BLOCK = 128
GM_GROUPS = 8
HEADS = 16
NOPE, ROPE, VDIM = 128, 64, 128
QRANK, KVRANK = 512, 512
ROPE_THETA = 10000.0
SM_SCALE = (NOPE + ROPE) ** -0.5
ALPHA = (2 * DEPTH) ** 0.25
LN_EPS = 1e-5
RMS_EPS = 1e-6
ADAM_LR, ADAM_B1, ADAM_B2, ADAM_EPS, ADAM_WD, ADAM_STEP = 0.001, 0.9, 0.999, 1e-08, 0.01, 10

N_DEV = 8
LANES = 128
SUBLANES = 8
VMEM_LIMIT = 56 * 1024 * 1024
MESH = pl.DeviceIdType.MESH
HBM_SPEC = pl.BlockSpec(memory_space=pltpu.HBM)


class _Schedule:
    def __init__(self):
        self.last = None
        self.reducer = None
        self.clock = 0.0

    def tick(self, us):
        self.clock += us
        if self.reducer is not None:
            self.reducer.advance()


_SCHED = _Schedule()


def _ticked(out, us):
    _SCHED.tick(us)
    return out


def _call(body, *, name, out_shape, in_specs, out_specs, grid=None, scratch=(), dims=None, n_prefetch=0):
    kw = dict(vmem_limit_bytes=VMEM_LIMIT)
    if dims is not None:
        kw["dimension_semantics"] = dims
    cp = pltpu.CompilerParams(**kw)
    in_specs = list(in_specs)
    token = _SCHED.last
    if token is not None:
        pos = n_prefetch + len(in_specs)
        in_specs.append(pl.BlockSpec(memory_space=pl.ANY))
        inner = body

        def body(*refs):
            return inner(*refs[:pos], *refs[pos + 1:])

    if n_prefetch:
        gs = pltpu.PrefetchScalarGridSpec(num_scalar_prefetch=n_prefetch, grid=grid, in_specs=in_specs, out_specs=out_specs,
                                          scratch_shapes=list(scratch))
        fn = pl.pallas_call(body, name=name, grid_spec=gs, out_shape=out_shape, compiler_params=cp, interpret=False)
    else:
        extra = {} if grid is None else {"grid": grid}
        fn = pl.pallas_call(body, name=name, in_specs=in_specs, out_specs=out_specs, out_shape=out_shape,
                            scratch_shapes=list(scratch), compiler_params=cp, interpret=False, **extra)

    def run(*args):
        out = fn(*args, token) if token is not None else fn(*args)
        _SCHED.last = out[0] if isinstance(out, (list, tuple)) else out
        return out

    return run


def _tile(n, pref, mult=LANES):
    if n <= pref:
        return n
    t = (pref // mult) * mult
    while t >= mult:
        if n % t == 0:
            return t
        t -= mult
    return n


def _sds(shape, dtype):
    return jax.ShapeDtypeStruct(tuple(shape), dtype)


FFN_TILE = 1408
FFN_SHARD_MULT = 64
MXU_FLOPS_PER_US = 8e8
TOKENS_K = 4096
_DN = {"nn": (((1,), (0,)), ((), ())), "nt": (((1,), (1,)), ((), ())), "tn": (((0,), (0,)), ((), ()))}


def _mm(a, b, mode, *, name, out_dtype, tm=1024, tn=1024, tk=2048, res=None, res_scale=1.0, a_parts=1, b_resident=False):
    if a_parts > 1:
        S_, F_ = a.shape[0] // a_parts, a.shape[1]
        a_shape = (S_, a_parts * F_)
    else:
        a_shape = a.shape
    if mode == "nn":
        (M, K), (K2, N) = a_shape, b.shape
    elif mode == "nt":
        (M, K), (N, K2) = a_shape, b.shape
    else:
        (K, M), (K2, N) = a_shape, b.shape
    assert K == K2, (name, a.shape, b.shape)
    tm, tn, tk = _tile(M, tm), _tile(N, tn), _tile(K, tk)
    nk = K // tk
    if a_parts > 1 and mode == "nn":
        per = F_ // tk
        assert F_ % tk == 0
        a_spec = pl.BlockSpec((tm, tk), lambda i, j, k: ((k // per) * (S_ // tm) + i, k % per))
    elif a_parts > 1:
        per = F_ // tm
        assert mode == "tn" and F_ % tm == 0
        a_spec = pl.BlockSpec((tk, tm), lambda i, j, k: ((i // per) * (S_ // tk) + k, i % per))
    elif mode == "tn":
        a_spec = pl.BlockSpec((tk, tm), lambda i, j, k: (k, i))
    else:
        a_spec = pl.BlockSpec((tm, tk), lambda i, j, k: (i, k))
    b_kw = {"pipeline_mode": pl.Buffered(1)} if b_resident else {}
    b_spec = (pl.BlockSpec((tn, tk), lambda i, j, k: (j, k), **b_kw) if mode == "nt"
              else pl.BlockSpec((tk, tn), lambda i, j, k: (k, j), **b_kw))
    in_specs = [a_spec, b_spec]
    args = [a, b]
    if res is not None:
        in_specs.append(pl.BlockSpec((tm, tn), lambda i, j, k: (i, j)))
        args.append(res)
    dn = _DN[mode]
    has_res = res is not None

    def body(*refs):
        a_ref, b_ref = refs[0], refs[1]
        r_ref = refs[2] if has_res else None
        o_ref = refs[2 + has_res]
        part = lax.dot_general(a_ref[...], b_ref[...], dn, preferred_element_type=F32)

        def finish(acc):
            if has_res:
                acc = acc + res_scale * r_ref[...]
            o_ref[...] = acc.astype(o_ref.dtype)

        if nk == 1:
            finish(part)
        else:
            acc_ref = refs[3 + has_res]
            k = pl.program_id(2)

            @pl.when(k == 0)
            def _():
                acc_ref[...] = part

            @pl.when(k > 0)
            def _():
                acc_ref[...] += part

            @pl.when(k == nk - 1)
            def _():
                finish(acc_ref[...])

    scratch = [pltpu.VMEM((tm, tn), F32)] if nk > 1 else []
    out = _call(body, name=name, grid=(M // tm, N // tn, nk), in_specs=in_specs,
                out_specs=pl.BlockSpec((tm, tn), lambda i, j, k: (i, j)), out_shape=_sds((M, N), out_dtype),
                scratch=scratch, dims=("parallel", "parallel", "arbitrary"))(*args)
    _SCHED.tick(2.0 * M * N * K / MXU_FLOPS_PER_US)
    return out


def _ln_fwd(x, m, g, b, *, name):
    S, D = x.shape
    tr = _tile(S, 256, 8)

    def body(x_ref, m_ref, g_ref, b_ref, y_ref, yb_ref, xh_ref, rs_ref):
        r = ALPHA * x_ref[...] + m_ref[...]
        mu = jnp.mean(r, axis=-1, keepdims=True)
        d = r - mu
        var = jnp.mean(d * d, axis=-1, keepdims=True)
        rstd = lax.rsqrt(var + LN_EPS)
        xh = d * rstd
        y = xh * g_ref[...] + b_ref[...]
        y_ref[...] = y
        yb_ref[...] = y.astype(BF16)
        xh_ref[...] = xh
        rs_ref[...] = rstd

    row = pl.BlockSpec((tr, D), lambda i: (i, 0))
    vec = pl.BlockSpec((1, D), lambda i: (0, 0))
    return _call(body, name=name, grid=(S // tr,), in_specs=[row, row, vec, vec],
                 out_specs=[row, row, row, pl.BlockSpec((tr, 1), lambda i: (i, 0))],
                 out_shape=[_sds((S, D), F32), _sds((S, D), BF16), _sds((S, D), F32), _sds((S, 1), F32)],
                 dims=("parallel",))(x, m, g, b)


def _ln_bwd(dy, xh, rstd, g, *, name):
    S, D = dy.shape
    tr = _tile(S, 256, 8)

    def body(dy_ref, xh_ref, rs_ref, g_ref, dr_ref, drb_ref, dg_ref, db_ref):
        @pl.when(pl.program_id(0) == 0)
        def _():
            dg_ref[...] = jnp.zeros_like(dg_ref)
            db_ref[...] = jnp.zeros_like(db_ref)

        dyv = dy_ref[...]
        xhv = xh_ref[...]
        dxh = dyv * g_ref[...]
        m1 = jnp.mean(dxh, axis=-1, keepdims=True)
        m2 = jnp.mean(dxh * xhv, axis=-1, keepdims=True)
        dr = rs_ref[...] * (dxh - m1 - xhv * m2)
        dr_ref[...] = dr
        drb_ref[...] = dr.astype(BF16)
        dg_ref[...] += jnp.sum(dyv * xhv, axis=0, keepdims=True)
        db_ref[...] += jnp.sum(dyv, axis=0, keepdims=True)

    row = pl.BlockSpec((tr, D), lambda i: (i, 0))
    vec = pl.BlockSpec((1, D), lambda i: (0, 0))
    return _ticked(_call(body, name=name, grid=(S // tr,), in_specs=[row, row, pl.BlockSpec((tr, 1), lambda i: (i, 0)), vec],
                         out_specs=[row, row, vec, vec],
                         out_shape=[_sds((S, D), F32), _sds((S, D), BF16), _sds((1, D), F32), _sds((1, D), F32)],
                         dims=("arbitrary",))(dy, xh, rstd, g), 45.0 * S / 4096)


def _loss_kernel(y, t, *, name):
    S, D = y.shape
    tr = _tile(S, 256, 8)

    def body(y_ref, t_ref, lp_ref, dy_ref):
        @pl.when(pl.program_id(0) == 0)
        def _():
            lp_ref[...] = jnp.zeros_like(lp_ref)

        e = y_ref[...] - t_ref[...]
        dy_ref[...] = e / D
        lp_ref[...] += jnp.sum(e * e, axis=0, keepdims=True)

    row = pl.BlockSpec((tr, D), lambda i: (i, 0))
    vec = pl.BlockSpec((1, D), lambda i: (0, 0))
    return _call(body, name=name, grid=(S // tr,), in_specs=[row, row], out_specs=[vec, row],
                 out_shape=[_sds((1, D), F32), _sds((S, D), F32)], dims=("arbitrary",))(y, t)


_GELU_C = 0.7978845608028654
_GELU_A = 0.044715


def _gelu(x):
    return 0.5 * x * (1.0 + jnp.tanh(_GELU_C * (x + _GELU_A * x * x * x)))


def _gelu_grad(x):
    x2 = x * x
    t = jnp.tanh(_GELU_C * (x + _GELU_A * x * x2))
    return 0.5 * (1.0 + t) + 0.5 * x * (1.0 - t * t) * (_GELU_C * (1.0 + 3.0 * _GELU_A * x2))


def _masked_ws(w):
    i = lax.broadcasted_iota(jnp.int32, w.shape, 0) // CHUNK
    j = lax.broadcasted_iota(jnp.int32, w.shape, 1) // CHUNK
    return jnp.where(j <= i, w, 0.0)


def _gm_mid_fwd(zp, ln_g, ln_b, w_s, b_st, *, name):
    S, H2 = zp.shape
    H = H2 // 2
    gd = H // GM_GROUPS
    nb = S // GM_BLOCK

    def body(zu_ref, zv_ref, g_ref, b_ref, w_ref, bs_ref, p_ref, s_ref, vh_ref, rs_ref):
        v = _gelu(zv_ref[...].astype(F32))
        mu = jnp.mean(v, axis=-1, keepdims=True)
        d = v - mu
        var = jnp.mean(d * d, axis=-1, keepdims=True)
        rstd = lax.rsqrt(var + LN_EPS)
        vh = d * rstd
        vh_ref[...] = vh.astype(BF16)
        rs_ref[...] = rstd
        vn = (vh * g_ref[...] + b_ref[...]).astype(BF16)
        bs = bs_ref[...]
        for gi in range(GM_GROUPS):
            cs = slice(gi * gd, (gi + 1) * gd)
            wm = _masked_ws(w_ref[gi]).astype(BF16)
            s = jnp.dot(wm, vn[:, cs], preferred_element_type=F32) + bs[:, gi:gi + 1]
            u = _gelu(zu_ref[:, cs].astype(F32))
            s_ref[:, cs] = s.astype(BF16)
            p_ref[:, cs] = (u * s).astype(BF16)

    blk = lambda c: pl.BlockSpec((GM_BLOCK, H), lambda n, c=c: (n, c))
    vec = pl.BlockSpec((1, H), lambda n: (0, 0))
    return _call(body, name=name, grid=(nb,),
                 in_specs=[blk(0), blk(1), vec, vec, pl.BlockSpec((GM_GROUPS, GM_BLOCK, GM_BLOCK), lambda n: (0, 0, 0)),
                           pl.BlockSpec((GM_BLOCK, GM_GROUPS), lambda n: (0, 0))],
                 out_specs=[blk(0), blk(0), blk(0), pl.BlockSpec((GM_BLOCK, 1), lambda n: (n, 0))],
                 out_shape=[_sds((S, H), BF16), _sds((S, H), BF16), _sds((S, H), BF16), _sds((S, 1), F32)],
                 dims=("parallel",))(zp, zp, ln_g, ln_b, w_s, b_st)


def _gm_mid_bwd(dp, zp, s, vhat, rstd, ln_g, ln_b, w_s, *, name):
    S, H2 = zp.shape
    H = H2 // 2
    gd = H // GM_GROUPS
    nb = S // GM_BLOCK

    def body(dp_ref, zu_ref, zv_ref, s_ref, vh_ref, rs_ref, g_ref, b_ref, w_ref,
             dz_ref, dw_ref, dbs_ref, dg_ref, db_ref, dvh_ref):
        @pl.when(pl.program_id(0) == 0)
        def _():
            dw_ref[...] = jnp.zeros_like(dw_ref)
            dbs_ref[...] = jnp.zeros_like(dbs_ref)
            dg_ref[...] = jnp.zeros_like(dg_ref)
            db_ref[...] = jnp.zeros_like(db_ref)

        m1 = jnp.zeros((GM_BLOCK, 1), F32)
        m2 = jnp.zeros((GM_BLOCK, 1), F32)
        for gi in range(GM_GROUPS):
            cs = slice(gi * gd, (gi + 1) * gd)
            dpg = dp_ref[:, cs].astype(F32)
            zu = zu_ref[:, cs].astype(F32)
            u = _gelu(zu)
            ds = dpg * u
            du = dpg * s_ref[:, cs].astype(F32)
            dz_ref[:, cs] = (du * _gelu_grad(zu)).astype(BF16)
            dsb = ds.astype(BF16)
            vh = vh_ref[:, cs].astype(F32)
            lg = g_ref[:, cs]
            vn = (vh * lg + b_ref[:, cs]).astype(BF16)
            wm = _masked_ws(w_ref[gi]).astype(BF16)
            dvn = lax.dot_general(wm, dsb, _DN["tn"], preferred_element_type=F32)
            dw_ref[gi] += lax.dot_general(dsb, vn, _DN["nt"], preferred_element_type=F32)
            dbs_ref[gi] += jnp.sum(ds, axis=1, keepdims=True)
            dg_ref[:, cs] += jnp.sum(dvn * vh, axis=0, keepdims=True)
            db_ref[:, cs] += jnp.sum(dvn, axis=0, keepdims=True)
            dvh = dvn * lg
            dvh_ref[:, cs] = dvh
            m1 = m1 + jnp.sum(dvh, axis=1, keepdims=True)
            m2 = m2 + jnp.sum(dvh * vh, axis=1, keepdims=True)
        m1 = m1 / H
        m2 = m2 / H
        rs = rs_ref[...]
        for gi in range(GM_GROUPS):
            cs = slice(gi * gd, (gi + 1) * gd)
            vh = vh_ref[:, cs].astype(F32)
            dv = rs * (dvh_ref[:, cs] - m1 - vh * m2)
            zv = zv_ref[:, cs].astype(F32)
            dz_ref[:, H + gi * gd:H + (gi + 1) * gd] = (dv * _gelu_grad(zv)).astype(BF16)

        @pl.when(pl.program_id(0) == nb - 1)
        def _():
            for gi in range(GM_GROUPS):
                dw_ref[gi] = _masked_ws(dw_ref[gi])

    blk = lambda c: pl.BlockSpec((GM_BLOCK, H), lambda n, c=c: (n, c))
    vec = pl.BlockSpec((1, H), lambda n: (0, 0))
    wspec = pl.BlockSpec((GM_GROUPS, GM_BLOCK, GM_BLOCK), lambda n: (0, 0, 0))
    return _ticked(_call(body, name=name, grid=(nb,),
                 in_specs=[blk(0), blk(0), blk(1), blk(0), blk(0), pl.BlockSpec((GM_BLOCK, 1), lambda n: (n, 0)), vec, vec, wspec],
                 out_specs=[pl.BlockSpec((GM_BLOCK, H2), lambda n: (n, 0)), wspec,
                            pl.BlockSpec((GM_GROUPS, GM_BLOCK, 1), lambda n: (0, 0, 0)), vec, vec],
                 out_shape=[_sds((S, H2), BF16), _sds((GM_GROUPS, GM_BLOCK, GM_BLOCK), F32),
                            _sds((GM_GROUPS, GM_BLOCK, 1), F32), _sds((1, H), F32), _sds((1, H), F32)],
                 scratch=[pltpu.VMEM((GM_BLOCK, H), F32)], dims=("arbitrary",))(dp, zp, zp, s, vhat, rstd, ln_g, ln_b, w_s), 280.0 * S / 4096)


def _shift_down(x, k):
    r = pltpu.roll(x, k, 0)
    rows = lax.broadcasted_iota(jnp.int32, (SUBLANES, x.shape[1]), 0)
    return jnp.concatenate([jnp.where(rows >= k, r[:SUBLANES], 0.0), r[SUBLANES:]], axis=0)


def _shift_up(x, k):
    n = x.shape[0]
    r = pltpu.roll(x, n - k, 0)
    rows = lax.broadcasted_iota(jnp.int32, (SUBLANES, x.shape[1]), 0)
    return jnp.concatenate([r[:n - SUBLANES], jnp.where(rows < SUBLANES - k, r[n - SUBLANES:], 0.0)], axis=0)


def _conv(h, w, b):
    return w[0:1, :] * _shift_down(h, 2) + w[1:2, :] * _shift_down(h, 1) + w[2:3, :] * h + b


def _ffn_mid_fwd(h, cw, cb, *, name):
    S, F2 = h.shape
    F = F2 // 2
    nf = F // LANES

    def body(ha_ref, hg_ref, wa_ref, wg_ref, ba_ref, bg_ref, act_ref):
        a = _conv(ha_ref[...].astype(F32), wa_ref[...], ba_ref[...])
        g = _conv(hg_ref[...].astype(F32), wg_ref[...], bg_ref[...])
        act_ref[...] = (g * jax.nn.sigmoid(g) * a).astype(BF16)

    sl = lambda off, r: pl.BlockSpec((r, LANES), lambda j, off=off: (0, j + off))
    return _call(body, name=name, grid=(nf,),
                 in_specs=[sl(0, S), sl(nf, S), sl(0, 3), sl(nf, 3), sl(0, 1), sl(nf, 1)],
                 out_specs=sl(0, S), out_shape=_sds((S, F), BF16), dims=("parallel",))(h, h, cw, cw, cb, cb)


def _ffn_mid_bwd(dact, h, cw, cb, *, name):
    S, F2 = h.shape
    F = F2 // 2
    nf = F // LANES

    def body(da_ref, ha_ref, hg_ref, wa_ref, wg_ref, ba_ref, bg_ref, dh_ref, dw_ref, db_ref):
        dact_v = da_ref[...].astype(F32)
        ha = ha_ref[...].astype(F32)
        hg = hg_ref[...].astype(F32)
        a = _conv(ha, wa_ref[...], ba_ref[...])
        g = _conv(hg, wg_ref[...], bg_ref[...])
        sg = jax.nn.sigmoid(g)

        def emit(p, dc, hin, w):
            up1, up2 = _shift_up(dc, 1), _shift_up(dc, 2)
            dh_ref[p] = (w[2:3, :] * dc + w[1:2, :] * up1 + w[0:1, :] * up2).astype(BF16)
            dw_ref[p, 0:1, :] = jnp.sum(up2 * hin, axis=0, keepdims=True)
            dw_ref[p, 1:2, :] = jnp.sum(up1 * hin, axis=0, keepdims=True)
            dw_ref[p, 2:3, :] = jnp.sum(dc * hin, axis=0, keepdims=True)
            db_ref[p] = jnp.sum(dc, axis=0, keepdims=True)

        emit(0, dact_v * (g * sg), ha, wa_ref[...])
        emit(1, dact_v * a * (sg * (1.0 + g * (1.0 - sg))), hg, wg_ref[...])

    sl = lambda off, r: pl.BlockSpec((r, LANES), lambda j, off=off: (0, j + off))
    out = lambda r: pl.BlockSpec((2, r, LANES), lambda j: (0, 0, j))
    return _ticked(_call(body, name=name, grid=(nf,),
                 in_specs=[sl(0, S), sl(0, S), sl(nf, S), sl(0, 3), sl(nf, 3), sl(0, 1), sl(nf, 1)],
                 out_specs=[out(S), out(3), out(1)],
                 out_shape=[_sds((2, S, F), BF16), _sds((2, 3, F), F32), _sds((2, 1, F), F32)],
                 dims=("parallel",))(dact, h, h, cw, cw, cb, cb), 320.0 * S / 4096)


def _swap_half(x):
    lane = lax.broadcasted_iota(jnp.int32, x.shape, 1)
    return jnp.where((lane % ROPE) < ROPE // 2, pltpu.roll(x, LANES - ROPE // 2, 1), pltpu.roll(x, ROPE // 2, 1))


def _rope(x, cos, sin_s):
    return x * cos + _swap_half(x) * sin_s


def _mla_prep(h, gq, gkv, cos, sin_s, *, name):
    S, W = h.shape
    tr = _tile(S, 512, 8)

    def body(h_ref, gq_ref, gkv_ref, cos_ref, sin_ref, cq_ref, ckv_ref, kr_ref, rq_ref, rkv_ref):
        cq = h_ref[:, 0:QRANK]
        rq = lax.rsqrt(jnp.mean(cq * cq, axis=-1, keepdims=True) + RMS_EPS)
        cq_ref[...] = (cq * rq * gq_ref[...]).astype(BF16)
        rq_ref[...] = rq
        ckv = h_ref[:, QRANK:QRANK + KVRANK]
        rkv = lax.rsqrt(jnp.mean(ckv * ckv, axis=-1, keepdims=True) + RMS_EPS)
        ckv_ref[...] = (ckv * rkv * gkv_ref[...]).astype(BF16)
        rkv_ref[...] = rkv
        kr = _rope(h_ref[:, QRANK + KVRANK:W], cos_ref[...], sin_ref[...])
        lane = lax.broadcasted_iota(jnp.int32, kr.shape, 1)
        kr = jnp.where(lane < ROPE, kr, 0.0)
        kr_ref[...] = (kr + pltpu.roll(kr, ROPE, 1)).astype(BF16)

    row = lambda w: pl.BlockSpec((tr, w), lambda i: (i, 0))
    vec = lambda w: pl.BlockSpec((1, w), lambda i: (0, 0))
    return _call(body, name=name, grid=(S // tr,),
                 in_specs=[row(W), vec(QRANK), vec(KVRANK), row(LANES), row(LANES)],
                 out_specs=[row(QRANK), row(KVRANK), row(LANES), row(1), row(1)],
                 out_shape=[_sds((S, QRANK), BF16), _sds((S, KVRANK), BF16), _sds((S, LANES), BF16), _sds((S, 1), F32), _sds((S, 1), F32)],
                 dims=("parallel",))(h, gq, gkv, cos, sin_s)


def _mla_prep_bwd(dcqn, dckvn, dkr2, h, rq, rkv, gq, gkv, cos, sin_neg, *, name):
    S, W = h.shape
    tr = _tile(S, 512, 8)

    def rms_bwd(dy, c, r, g):
        n = c * r
        dn = dy * g
        return r * (dn - n * jnp.mean(dn * n, axis=-1, keepdims=True)), jnp.sum(dy * n, axis=0, keepdims=True)

    def body(dq_ref, dkv_ref, dkr_ref, h_ref, rq_ref, rkv_ref, gq_ref, gkv_ref, cos_ref, sin_ref, dh_ref, dgq_ref, dgkv_ref):
        @pl.when(pl.program_id(0) == 0)
        def _():
            dgq_ref[...] = jnp.zeros_like(dgq_ref)
            dgkv_ref[...] = jnp.zeros_like(dgkv_ref)

        dcq, dg = rms_bwd(dq_ref[...], h_ref[:, 0:QRANK], rq_ref[...], gq_ref[...])
        dgq_ref[...] += dg
        dh_ref[:, 0:QRANK] = dcq.astype(BF16)
        dckv, dg = rms_bwd(dkv_ref[...], h_ref[:, QRANK:QRANK + KVRANK], rkv_ref[...], gkv_ref[...])
        dgkv_ref[...] += dg
        dh_ref[:, QRANK:QRANK + KVRANK] = dckv.astype(BF16)
        dk = dkr_ref[...]
        dk = dk + pltpu.roll(dk, ROPE, 1)
        dk = _rope(dk, cos_ref[...], sin_ref[...])
        lane = lax.broadcasted_iota(jnp.int32, dk.shape, 1)
        dh_ref[:, QRANK + KVRANK:W] = jnp.where(lane < ROPE, dk, 0.0).astype(BF16)

    row = lambda w: pl.BlockSpec((tr, w), lambda i: (i, 0))
    vec = lambda w: pl.BlockSpec((1, w), lambda i: (0, 0))
    return _call(body, name=name, grid=(S // tr,),
                 in_specs=[row(QRANK), row(KVRANK), row(LANES), row(W), row(1), row(1), vec(QRANK), vec(KVRANK), row(LANES), row(LANES)],
                 out_specs=[row(W), vec(QRANK), vec(KVRANK)],
                 out_shape=[_sds((S, W), BF16), _sds((1, QRANK), F32), _sds((1, KVRANK), F32)],
                 dims=("arbitrary",))(dcqn, dckvn, dkr2, h, rq, rkv, gq, gkv, cos, sin_neg)


QPAIR = 2 * NOPE + 2 * ROPE
KVPAIR = 2 * (NOPE + VDIM)


def _rope_q(q, cos, sin_s, *, name):
    S, Wq = q.shape
    tr = _tile(S, 512, 8)

    def body(q_ref, cos_ref, sin_ref, o_ref):
        o_ref[:, 0:2 * NOPE] = q_ref[:, 0:2 * NOPE].astype(BF16)
        o_ref[:, 2 * NOPE:QPAIR] = _rope(q_ref[:, 2 * NOPE:QPAIR], cos_ref[...], sin_ref[...]).astype(BF16)

    blk = pl.BlockSpec((tr, QPAIR), lambda i, p: (i, p))
    tab = pl.BlockSpec((tr, LANES), lambda i, p: (i, 0))
    return _call(body, name=name, grid=(S // tr, Wq // QPAIR), in_specs=[blk, tab, tab], out_specs=blk,
                 out_shape=_sds((S, Wq), BF16), dims=("parallel", "parallel"))(q, cos, sin_s)


ATT_T = 512
LOG2_E = 1.4426950408889634


def _att_scores(kc, qc, allowed):
    s = lax.dot_general(kc, qc, _DN["nt"], preferred_element_type=F32) * (SM_SCALE * LOG2_E)
    return s if allowed is None else jnp.where(allowed, s, -jnp.inf)


def _att_allowed(k0, q0, t):
    kpos = k0 + lax.broadcasted_iota(jnp.int32, (t, t), 0)
    qpos = q0 + lax.broadcasted_iota(jnp.int32, (t, t), 1)
    return (kpos // CHUNK) <= (qpos // CHUNK)


def _head_mask(r, hh):
    lane = lax.broadcasted_iota(jnp.int32, r.shape, 1)
    keep = (lane < ROPE) if hh == 0 else (lane >= ROPE)
    return jnp.where(keep, r, jnp.zeros_like(r))


def _att_q(q_ref):
    r = q_ref[:, 2 * NOPE:QPAIR]
    return [jnp.concatenate([q_ref[:, hh * NOPE:(hh + 1) * NOPE], _head_mask(r, hh)], axis=1) for hh in range(2)]


def _attn_fwd(qb, kv, kr2, *, name):
    S = qb.shape[0]
    t = _tile(S, ATT_T, 8)
    nq = S // t
    npair = HEADS // 2

    def body(q_ref, kv_ref, kr_ref, o_ref, lse_ref):
        i = pl.program_id(1)
        q0 = i * t
        qc = _att_q(q_ref)

        def step(j, carry, masked):
            k0 = pl.multiple_of(j * t, t)
            rows = pl.ds(k0, t)
            kr = kr_ref[rows, :]
            allowed = _att_allowed(k0, q0, t) if masked else None
            out = []
            for hh in range(2):
                m, l, acc = carry[hh]
                c0 = hh * (NOPE + VDIM)
                kc = jnp.concatenate([kv_ref[rows, c0:c0 + NOPE], kr], axis=1)
                v = kv_ref[rows, c0 + NOPE:c0 + NOPE + VDIM]
                s = _att_scores(kc, qc[hh], allowed)
                m_new = jnp.maximum(m, jnp.max(s, axis=0, keepdims=True))
                p = jnp.exp2(s - m_new)
                a = jnp.exp2(m - m_new)
                l = a * l + jnp.sum(p, axis=0, keepdims=True)
                acc = a * acc + lax.dot_general(v, p.astype(BF16), _DN["tn"], preferred_element_type=F32)
                out.append((m_new, l, acc))
            return tuple(out)

        one = (jnp.full((1, t), -jnp.inf, F32), jnp.zeros((1, t), F32), jnp.zeros((VDIM, t), F32))
        carry = lax.fori_loop(0, i, lambda j, c: step(j, c, False), (one, one))
        carry = step(i, carry, True)
        for hh in range(2):
            m, l, acc = carry[hh]
            o_ref[:, hh * VDIM:(hh + 1) * VDIM] = jnp.transpose(acc / l).astype(BF16)
            lse_ref[hh:hh + 1, :] = m + jnp.log2(l)

    return _call(body, name=name, grid=(npair, nq),
                 in_specs=[pl.BlockSpec((t, QPAIR), lambda p, i: (i, p)), pl.BlockSpec((S, KVPAIR), lambda p, i: (0, p)),
                           pl.BlockSpec((S, LANES), lambda p, i: (0, 0))],
                 out_specs=[pl.BlockSpec((t, 2 * VDIM), lambda p, i: (i, p)), pl.BlockSpec((None, 2, t), lambda p, i: (p, 0, i))],
                 out_shape=[_sds((S, HEADS * VDIM), BF16), _sds((npair, 2, S), F32)],
                 dims=("parallel", "arbitrary"))(qb, kv, kr2)


def _attn_bwd(qb, kv, kr2, o, do, lse, *, name):
    S = qb.shape[0]
    t = _tile(S, ATT_T, 8)
    nq = S // t
    npair = HEADS // 2

    def body(q_ref, kv_ref, kr_ref, o_ref, do_ref, lse_ref, dq_ref, dkv_ref, dkr_ref, acc_ref):
        pid = pl.program_id(0)
        i = pl.program_id(1)
        q0 = i * t

        @pl.when(i == 0)
        def _():
            acc_ref[...] = jnp.zeros_like(acc_ref)

        @pl.when((i == 0) & (pid == 0))
        def _():
            dkr_ref[...] = jnp.zeros_like(dkr_ref)

        qc = _att_q(q_ref)
        dov = [do_ref[:, hh * VDIM:(hh + 1) * VDIM] for hh in range(2)]
        dsum = [jnp.sum(jnp.transpose(dov[hh].astype(F32) * o_ref[:, hh * VDIM:(hh + 1) * VDIM].astype(F32)), axis=0, keepdims=True)
                for hh in range(2)]
        lse_v = [lse_ref[hh:hh + 1, :] for hh in range(2)]

        def step(j, carry, masked):
            k0 = pl.multiple_of(j * t, t)
            rows = pl.ds(k0, t)
            kr = kr_ref[rows, :]
            allowed = _att_allowed(k0, q0, t) if masked else None
            out = []
            dkr = jnp.zeros((t, LANES), F32)
            for hh in range(2):
                c0 = hh * (NOPE + VDIM)
                kc = jnp.concatenate([kv_ref[rows, c0:c0 + NOPE], kr], axis=1)
                v = kv_ref[rows, c0 + NOPE:c0 + NOPE + VDIM]
                p = jnp.exp2(_att_scores(kc, qc[hh], allowed) - lse_v[hh])
                dp = lax.dot_general(v, dov[hh], _DN["nt"], preferred_element_type=F32)
                ds = (p * (dp - dsum[hh]) * SM_SCALE).astype(BF16)
                acc_ref[rows, c0 + NOPE:c0 + NOPE + VDIM] += jnp.dot(p.astype(BF16), dov[hh], preferred_element_type=F32)
                dkc = jnp.dot(ds, qc[hh], preferred_element_type=F32)
                acc_ref[rows, c0:c0 + NOPE] += dkc[:, 0:NOPE]
                dkr = dkr + dkc[:, NOPE:]
                out.append(carry[hh] + lax.dot_general(ds, kc, _DN["tn"], preferred_element_type=F32))
            dkr_ref[rows, :] += dkr
            return tuple(out)

        zero = jnp.zeros((t, NOPE + LANES), F32)
        carry = lax.fori_loop(0, i, lambda j, c: step(j, c, False), (zero, zero))
        dqc = step(i, carry, True)
        for hh in range(2):
            dq_ref[:, hh * NOPE:(hh + 1) * NOPE] = dqc[hh][:, 0:NOPE]
        dq_ref[:, 2 * NOPE:QPAIR] = _head_mask(dqc[0][:, NOPE:], 0) + _head_mask(dqc[1][:, NOPE:], 1)

        @pl.when(i == nq - 1)
        def _():
            dkv_ref[...] = acc_ref[...].astype(BF16)

    qspec = pl.BlockSpec((t, QPAIR), lambda p, i: (i, p))
    hspec = pl.BlockSpec((t, 2 * VDIM), lambda p, i: (i, p))
    kvspec = pl.BlockSpec((S, KVPAIR), lambda p, i: (0, p))
    krspec = pl.BlockSpec((S, LANES), lambda p, i: (0, 0))
    return _ticked(_call(body, name=name, grid=(npair, nq),
                 in_specs=[qspec, kvspec, krspec, hspec, hspec, pl.BlockSpec((None, 2, t), lambda p, i: (p, 0, i))],
                 out_specs=[qspec, kvspec, krspec],
                 out_shape=[_sds(qb.shape, F32), _sds(kv.shape, BF16), _sds((S, LANES), F32)],
                 scratch=[pltpu.VMEM((S, KVPAIR), F32)], dims=("arbitrary", "arbitrary"))(qb, kv, kr2, o, do, lse),
                   640.0 * (S / 4096) ** 2)


def _coords():
    return lax.axis_index("x"), lax.axis_index("y"), lax.axis_index("c")


def _all_gather_body(n, shake):
    def body(*refs):
        ins, outs = refs[:n], refs[n:2 * n]
        send_sems, recv_sems, local_sems = refs[2 * n:]
        x, y, c = _coords()
        me, sib = (x, y, c), (x, y, 1 - c)
        chips = [(1 - x, y), (x, 1 - y), (1 - x, 1 - y)]
        if shake:
            _handshake([sib] + [(*chip, c) for chip in chips])

        def copy(a, k, block, to, src=None):
            dst = outs[a].at[4 * block[0] + 2 * block[1] + block[2]]
            return pltpu.make_async_remote_copy(src_ref=dst if src is None else src, dst_ref=dst,
                                                send_sem=send_sems.at[a, k], recv_sem=recv_sems.at[a, k],
                                                device_id=to, device_id_type=MESH)

        mine = [pltpu.make_async_copy(ins[a], outs[a].at[4 * x + 2 * y + c], local_sems.at[a]) for a in range(n)]
        for cp in mine:
            cp.start()
        sends = []
        for a in range(n):
            sends.append(copy(a, 0, me, sib, src=ins[a]))
            sends += [copy(a, 1 + j, me, (*chip, c), src=ins[a]) for j, chip in enumerate(chips)]
        for cp in sends:
            cp.start()
        for j, chip in enumerate(chips):
            for a in range(n):
                copy(a, 1 + j, (*chip, c), me).wait_recv()
                fwd = copy(a, 4 + j, (*chip, c), sib)
                fwd.start()
                sends.append(fwd)
        for a in range(n):
            copy(a, 0, sib, me).wait_recv()
            for j, chip in enumerate(chips):
                copy(a, 4 + j, (*chip, 1 - c), me).wait_recv()
        for cp in sends:
            cp.wait_send()
        for cp in mine:
            cp.wait()

    return body


def _all_gather(arrs, *, name):
    n = len(arrs)
    outs = _call(_all_gather_body(n, False), name=name, in_specs=[HBM_SPEC] * n, out_specs=[HBM_SPEC] * n,
                 out_shape=[_sds((N_DEV,) + a.shape, a.dtype) for a in arrs],
                 scratch=[pltpu.SemaphoreType.DMA((n, 7)), pltpu.SemaphoreType.DMA((n, 7)), pltpu.SemaphoreType.DMA((n,))])(*arrs)
    return list(outs)


def _sequencer_call(body, *, name, out_type, n_sems, collective_id):
    return pl.kernel(body, out_type=out_type, mesh=plsc.ScalarSubcoreMesh(axis_name="sq", num_cores=1), name=name,
                     scratch_types=[pltpu.SemaphoreType.DMA(n_sems), pltpu.SemaphoreType.DMA(n_sems), pltpu.SemaphoreType.DMA((n_sems[0],))],
                     compiler_params=pltpu.CompilerParams(collective_id=collective_id))


def _handshake(peers):
    barrier = pltpu.get_barrier_semaphore()
    for p in peers:
        pl.semaphore_signal(barrier, inc=1, device_id=p, device_id_type=MESH)
    pl.semaphore_wait(barrier, len(peers))


AG_ID, PAIR_ID, CHIP_ID = 1, 2, 3


def _all_gather_sc(arrs, *, name):
    n = len(arrs)
    outs = _sequencer_call(_all_gather_body(n, True), name=name, out_type=[_sds((N_DEV,) + a.shape, a.dtype) for a in arrs],
                           n_sems=(n, 7), collective_id=AG_ID)(*arrs)
    return list(outs)


def _pair_exchange_sc(gs, *, name):
    n = len(gs)

    def body(*refs):
        ins, outs = refs[:n], refs[n:2 * n]
        send_sems, recv_sems, _ = refs[2 * n:]
        x, y, c = _coords()
        sib = (x, y, 1 - c)
        _handshake([sib])
        cps = []
        for a in range(n):
            for j in range(4):
                cps.append(pltpu.make_async_remote_copy(src_ref=ins[a].at[2 * j + (1 - c)], dst_ref=outs[a].at[j],
                                                        send_sem=send_sems.at[a, j], recv_sem=recv_sems.at[a, j],
                                                        device_id=sib, device_id_type=MESH))
        for cp in cps:
            cp.start()
        for cp in cps:
            cp.wait()

    outs = _sequencer_call(body, name=name, out_type=[_sds((4,) + g.shape[1:], g.dtype) for g in gs],
                           n_sems=(n, 4), collective_id=PAIR_ID)(*gs)
    return list(outs)


def _chip_exchange_sc(ps, *, name):
    n = len(ps)

    def body(*refs):
        ins, outs = refs[:n], refs[n:2 * n]
        send_sems, recv_sems, _ = refs[2 * n:]
        x, y, c = _coords()
        chips = [(1 - x, y), (x, 1 - y), (1 - x, 1 - y)]
        _handshake([(*chip, c) for chip in chips])
        cps = []
        for a in range(n):
            for r, (px, py) in enumerate(chips):
                cps.append(pltpu.make_async_remote_copy(src_ref=ins[a].at[2 * px + py], dst_ref=outs[a].at[r],
                                                        send_sem=send_sems.at[a, r], recv_sem=recv_sems.at[a, r],
                                                        device_id=(px, py, c), device_id_type=MESH))
        for cp in cps:
            cp.start()
        for cp in cps:
            cp.wait()

    outs = _sequencer_call(body, name=name, out_type=[_sds((3,) + p.shape[1:], p.dtype) for p in ps],
                           n_sems=(n, 3), collective_id=CHIP_ID)(*ps)
    return list(outs)


def _row_tile(R, C, budget=1 << 20, mult=16):
    want = max(mult, budget // (4 * C))
    if R <= want:
        return R
    t = (want // mult) * mult
    while t >= mult:
        if R % t == 0:
            return t
        t -= mult
    return R


RS_TILE_BYTES = 6 << 20


def _pair_sum(g, l1, c_idx, *, name):
    _, R, C = g.shape
    tr = _row_tile(R, C, budget=RS_TILE_BYTES)
    g4 = g.reshape(4, 2, R, C)

    def body(c_ref, g_ref, l_ref, o_ref):
        o_ref[...] = (g_ref[...].astype(F32) + l_ref[...].astype(F32)).astype(o_ref.dtype)

    return _call(body, name=name, n_prefetch=1, grid=(4, R // tr),
                 in_specs=[pl.BlockSpec((None, None, tr, C), lambda j, r, c_ref: (j, c_ref[0], r, 0)),
                           pl.BlockSpec((None, tr, C), lambda j, r, c_ref: (j, r, 0))],
                 out_specs=pl.BlockSpec((None, tr, C), lambda j, r, c_ref: (j, r, 0)),
                 out_shape=_sds((4, R, C), g.dtype))(c_idx, g4, l1)


def _chip_sum(p, l2, chip_idx, *, name):
    _, R, C = p.shape
    tr = _row_tile(R, C, budget=RS_TILE_BYTES)

    def body(j_ref, p_ref, l_ref, o_ref):
        o_ref[...] = ((p_ref[...].astype(F32) + l_ref[0].astype(F32)) + l_ref[1].astype(F32)) + l_ref[2].astype(F32)

    return _call(body, name=name, n_prefetch=1, grid=(R // tr,),
                 in_specs=[pl.BlockSpec((None, tr, C), lambda r, j_ref: (j_ref[0], r, 0)),
                           pl.BlockSpec((3, tr, C), lambda r, j_ref: (0, r, 0))],
                 out_specs=pl.BlockSpec((tr, C), lambda r, j_ref: (r, 0)), out_shape=_sds((R, C), F32))(chip_idx, p, l2)


class _Reducer:
    PAIR_US = (12.0, 7.0)
    CHIP_US = (15.0, 43.0)

    def __init__(self, c_idx, chip_idx):
        self.c_idx, self.chip_idx = c_idx, chip_idx
        self.units, self.out, self.sc_free = [], {}, 0.0

    def _sequencer_done(self, cost):
        self.sc_free = max(self.sc_free, _SCHED.clock) + cost[0] + cost[1] * self._mb
        return self.sc_free

    def submit(self, name, g):
        R = g.shape[0] // N_DEV
        g8 = g.reshape((N_DEV, R, g.shape[1]))
        self._mb = R * g.shape[1] * g.dtype.itemsize / 1e6
        (l1,) = _pair_exchange_sc([g8], name=f"rs_pair_{name}")
        self.units.append(dict(name=name, g8=g8, l1=l1, mb=self._mb, stage=1, ready=self._sequencer_done(self.PAIR_US)))

    def advance(self, force=False):
        for u in self.units:
            if not (force or _SCHED.clock >= u["ready"]):
                continue
            if u["stage"] == 1:
                p = _pair_sum(u.pop("g8"), u.pop("l1"), self.c_idx, name=f"rs_psum_{u['name']}")
                (l2,) = _chip_exchange_sc([p], name=f"rs_chip_{u['name']}")
                self._mb = u["mb"]
                u.update(p=p, l2=l2, stage=2, ready=self._sequencer_done(self.CHIP_US))
            elif u["stage"] == 2:
                self.out[u["name"]] = _chip_sum(u.pop("p"), u.pop("l2"), self.chip_idx, name=f"rs_csum_{u['name']}")
                u["stage"] = 3

    def finish(self):
        self.advance(force=True)
        self.advance(force=True)
        return self.out


def _grad_ready(name, g):
    if _SCHED.reducer is not None:
        _SCHED.reducer.submit(name, g)
    return g


def _sum8(a, *, name):
    _, R, C = a.shape
    tr = _row_tile(R, C, budget=1 << 18, mult=8)

    def body(a_ref, o_ref):
        acc = a_ref[0]
        for k in range(1, N_DEV):
            acc = acc + a_ref[k]
        o_ref[...] = acc

    return _call(body, name=name, grid=(R // tr,), in_specs=[pl.BlockSpec((N_DEV, tr, C), lambda r: (0, r, 0))],
                 out_specs=pl.BlockSpec((tr, C), lambda r: (r, 0)), out_shape=_sds((R, C), F32), dims=("parallel",))(a)


def _adamw(w, g, m, v, *, name):
    R, C = w.shape
    tr = _row_tile(R, C, budget=1 << 20, mult=8)
    c1 = 1.0 - ADAM_B1 ** ADAM_STEP
    c2 = 1.0 - ADAM_B2 ** ADAM_STEP

    def body(w_ref, g_ref, m_ref, v_ref, d_ref, mo_ref, vo_ref):
        gv = g_ref[...]
        mn = ADAM_B1 * m_ref[...] + (1.0 - ADAM_B1) * gv
        vn = ADAM_B2 * v_ref[...] + (1.0 - ADAM_B2) * (gv * gv)
        mo_ref[...] = mn
        vo_ref[...] = vn
        d_ref[...] = -ADAM_LR * ((mn / c1) / (jnp.sqrt(vn / c2) + ADAM_EPS) + ADAM_WD * w_ref[...])

    blk = pl.BlockSpec((tr, C), lambda r: (r, 0))
    return _call(body, name=name, grid=(R // tr,), in_specs=[blk] * 4, out_specs=[blk] * 3,
                 out_shape=[_sds((R, C), F32)] * 3, dims=("parallel",))(w, g, m, v)


def _gmlp_fwd(xb, W, sp, tag):
    zp = _mm(xb, W["w_in_t"], "nt", name=f"gm_zp_{tag}", out_dtype=BF16)
    p, s, vhat, rstd = _gm_mid_fwd(zp, sp["ln_g"], sp["ln_b"], sp["w_s"], sp["b_st"], name=f"gm_mid_{tag}")
    m = _mm(p, W["w_out"], "nn", name=f"gm_out_{tag}", out_dtype=F32, tm=512, tk=p.shape[1])
    return m, dict(xb=xb, zp=zp, p=p, s=s, vhat=vhat, rstd=rstd)


def _gmlp_bwd(drb, dr, W, sp, sv, tag, on_small=None):
    d_w_out = _grad_ready(f"{tag}_w_out", _mm(sv["p"], drb, "tn", name=f"gm_dwout_{tag}", out_dtype=BF16, tk=TOKENS_K))
    dp = _mm(drb, W["w_out"], "nt", name=f"gm_dp_{tag}", out_dtype=BF16)
    dzp, dws, dbs, dlg, dlb = _gm_mid_bwd(dp, sv["zp"], sv["s"], sv["vhat"], sv["rstd"], sp["ln_g"], sp["ln_b"], sp["w_s"],
                                          name=f"gm_midb_{tag}")
    small = dict(w_s=dws, b_s=dbs, ln_g=dlg, ln_b=dlb)
    if on_small is not None:
        on_small(small)
    d_w_in_t = _grad_ready(f"{tag}_w_in_t", _mm(dzp, sv["xb"], "tn", name=f"gm_dwin_{tag}", out_dtype=BF16, tk=TOKENS_K))
    dx = _mm(dzp, W["w_in_t"], "nn", name=f"gm_dx_{tag}", out_dtype=F32, res=dr, res_scale=ALPHA)
    return dx, dict(w_in_t=d_w_in_t, w_out=d_w_out), small


def _mla_fwd(xb, W, sp, rope, tag):
    cos, sin_s, _ = rope
    h = _mm(xb, W["w_in"], "nn", name=f"mla_h_{tag}", out_dtype=F32, tn=W["w_in"].shape[1])
    cqn, ckvn, kr2, rq, rkv = _mla_prep(h, sp["gq"], sp["gkv"], cos, sin_s, name=f"mla_prep_{tag}")
    q = _mm(cqn, W["w_qb_t"], "nt", name=f"mla_q_{tag}", out_dtype=F32)
    qb = _rope_q(q, cos, sin_s, name=f"mla_ropeq_{tag}")
    kv = _mm(ckvn, W["w_kvb_t"], "nt", name=f"mla_kv_{tag}", out_dtype=BF16)
    o, lse = _attn_fwd(qb, kv, kr2, name=f"mla_attn_{tag}")
    m = _mm(o, W["w_out"], "nn", name=f"mla_out_{tag}", out_dtype=F32)
    return m, dict(xb=xb, h=h, cqn=cqn, ckvn=ckvn, kr2=kr2, rq=rq, rkv=rkv, qb=qb, kv=kv, o=o, lse=lse)


def _mla_bwd(drb, dr, W, sp, rope, sv, tag):
    cos, _, sin_neg = rope
    d_w_out = _grad_ready(f"{tag}_w_out", _mm(sv["o"], drb, "tn", name=f"mla_dwout_{tag}", out_dtype=BF16, tk=TOKENS_K))
    do = _mm(drb, W["w_out"], "nt", name=f"mla_do_{tag}", out_dtype=BF16)
    dq, dkv, dkr2 = _attn_bwd(sv["qb"], sv["kv"], sv["kr2"], sv["o"], do, sv["lse"], name=f"mla_attnb_{tag}")
    dqb = _rope_q(dq, cos, sin_neg, name=f"mla_ropedq_{tag}")
    d_w_qb_t = _grad_ready(f"{tag}_w_qb_t", _mm(dqb, sv["cqn"], "tn", name=f"mla_dwqb_{tag}", out_dtype=BF16, tk=TOKENS_K))
    d_w_kvb_t = _grad_ready(f"{tag}_w_kvb_t", _mm(dkv, sv["ckvn"], "tn", name=f"mla_dwkvb_{tag}", out_dtype=BF16, tk=TOKENS_K))
    dcqn = _mm(dqb, W["w_qb_t"], "nn", name=f"mla_dcq_{tag}", out_dtype=F32, tk=dqb.shape[1])
    dckvn = _mm(dkv, W["w_kvb_t"], "nn", name=f"mla_dckv_{tag}", out_dtype=F32, tk=dkv.shape[1])
    dh, dgq, dgkv = _mla_prep_bwd(dcqn, dckvn, dkr2, sv["h"], sv["rq"], sv["rkv"], sp["gq"], sp["gkv"], cos, sin_neg,
                                  name=f"mla_prepb_{tag}")
    d_w_in = _grad_ready(f"{tag}_w_in", _mm(sv["xb"], dh, "tn", name=f"mla_dwin_{tag}", out_dtype=BF16, tn=dh.shape[1], tk=TOKENS_K))
    dx = _mm(dh, W["w_in"], "nt", name=f"mla_dx_{tag}", out_dtype=F32, tk=dh.shape[1], res=dr, res_scale=ALPHA)
    return dx, dict(w_in=d_w_in, w_qb_t=d_w_qb_t, w_kvb_t=d_w_kvb_t, w_out=d_w_out), dict(gq=dgq, gkv=dgkv)


def _ffn_fwd(xb, W, sp, tag):
    S = xb.shape[0]
    F = W["w_down"].shape[0]
    h = _mm(xb, W["w_up_t"], "nt", name=f"ffn_h_{tag}", out_dtype=BF16)
    act = _ffn_mid_fwd(h, sp["cw"], sp["cb"], name=f"ffn_mid_{tag}")
    f = _mm(act, W["w_down"], "nn", name=f"ffn_out_{tag}", out_dtype=F32, tk=F // 2)
    return f, dict(xb=xb, h=h, act=act)


def _ffn_bwd(drb, dr, W, sp, sv, tag):
    F = W["w_down"].shape[0]
    S, D = drb.shape
    tf = _tile(F, FFN_TILE)
    d_w_down = _grad_ready(f"{tag}_w_down", _mm(sv["act"], drb, "tn", name=f"ffn_dwdown_{tag}", out_dtype=BF16, tm=tf, tn=512, tk=TOKENS_K))
    dact = _mm(drb, W["w_down"], "nt", name=f"ffn_dact_{tag}", out_dtype=BF16, tm=512, tn=F, tk=D, b_resident=True)
    dh, dcw, dcb = _ffn_mid_bwd(dact, sv["h"], sp["cw"], sp["cb"], name=f"ffn_midb_{tag}")
    dh = dh.reshape(2 * S, F)
    d_w_up_t = _grad_ready(f"{tag}_w_up_t", _mm(dh, sv["xb"], "tn", name=f"ffn_dwup_{tag}", out_dtype=BF16, tm=tf, tn=512, tk=TOKENS_K, a_parts=2))
    dx = _mm(dh, W["w_up_t"], "nn", name=f"ffn_dx_{tag}", out_dtype=F32, res=dr, res_scale=ALPHA, a_parts=2, tk=F // 2)
    dcw = jnp.concatenate([dcw[0], dcw[1]], axis=1)
    dcb = jnp.concatenate([dcb[0], dcb[1]], axis=1)
    return dx, dict(w_up_t=d_w_up_t, w_down=d_w_down), dict(cw=dcw, cb=dcb)


def _rope_tables(S):
    half = ROPE // 2
    inv_freq = ROPE_THETA ** (-jnp.arange(half, dtype=F32) / half)
    ang = jnp.arange(S, dtype=F32)[:, None] * inv_freq[None, :]
    cos, sin = jnp.cos(ang), jnp.sin(ang)
    cos128 = jnp.concatenate([cos] * 4, axis=1)
    sin128 = jnp.concatenate([-sin, sin, -sin, sin], axis=1)
    return cos128, sin128, -sin128


def _fwd_bwd(x, tgt, Wm, Wf, spm, spf, ln, on_small=None):
    S, D = x.shape
    _SCHED.last, _SCHED.clock = None, 0.0
    rope = _rope_tables(S)
    xf, xb = x, x.astype(BF16)
    saved = []
    for i in range(DEPTH):
        if i % 2 == 0:
            m, svm = _gmlp_fwd(xb, Wm[i], spm[i], f"l{i}")
        else:
            m, svm = _mla_fwd(xb, Wm[i], spm[i], rope, f"l{i}")
        y, yb, xh1, rs1 = _ln_fwd(xf, m, ln["mix_g"][i], ln["mix_b"][i], name=f"ln_mix_{i}")
        f, svf = _ffn_fwd(yb, Wf[i], spf[i], f"l{i}")
        z, zb, xh2, rs2 = _ln_fwd(y, f, ln["ffn_g"][i], ln["ffn_b"][i], name=f"ln_ffn_{i}")
        saved.append((svm, xh1, rs1, svf, xh2, rs2))
        xf, xb = z, zb
    lp, dy = _loss_kernel(xf, tgt, name="loss")
    gm_big, gm_small, gf_big, gf_small, gln = [None] * DEPTH, [None] * DEPTH, [None] * DEPTH, [None] * DEPTH, [None] * DEPTH
    for i in reversed(range(DEPTH)):
        svm, xh1, rs1, svf, xh2, rs2 = saved[i]
        dr, drb, dg2, db2 = _ln_bwd(dy, xh2, rs2, ln["ffn_g"][i], name=f"lnb_ffn_{i}")
        dy, gf_big[i], gf_small[i] = _ffn_bwd(drb, dr, Wf[i], spf[i], svf, f"l{i}")
        dr, drb, dg1, db1 = _ln_bwd(dy, xh1, rs1, ln["mix_g"][i], name=f"lnb_mix_{i}")
        gln[i] = dict(mix_g=dg1, mix_b=db1, ffn_g=dg2, ffn_b=db2)
        if i % 2 == 0:
            hook = None if (i > 0 or on_small is None) else (lambda ms: on_small(lp, [ms] + gm_small[1:], gf_small, gln))
            dy, gm_big[i], gm_small[i] = _gmlp_bwd(drb, dr, Wm[i], spm[i], svm, f"l{i}", hook)
        else:
            dy, gm_big[i], gm_small[i] = _mla_bwd(drb, dr, Wm[i], spm[i], rope, svm, f"l{i}")
    return lp, dy, gm_big, gm_small, gf_big, gf_small, gln


def _perm_q_rows(wt):
    hd = NOPE + ROPE
    return jnp.concatenate([wt[0:NOPE], wt[hd:hd + NOPE], wt[NOPE:hd], wt[hd + NOPE:2 * hd]], axis=0)


def _unperm_q_rows(wt):
    return jnp.concatenate([wt[0:NOPE], wt[2 * NOPE:2 * NOPE + ROPE], wt[NOPE:2 * NOPE], wt[2 * NOPE + ROPE:]], axis=0)


def _pad_cols(w, to):
    return jnp.pad(w, ((0, 0), (0, to - w.shape[1])))


def _pad_pieces(a, axis, piece, piece_pad):
    n = a.shape[axis] // piece
    sh = a.shape[:axis] + (n, piece) + a.shape[axis + 1:]
    pad = [(0, 0)] * (len(sh))
    pad[axis + 1] = (0, piece_pad - piece)
    return jnp.pad(a.reshape(sh), pad).reshape(a.shape[:axis] + (n * piece_pad,) + a.shape[axis + 1:])


def _unpad_pieces(a, axis, piece, piece_pad):
    n = a.shape[axis] // piece_pad
    sh = a.shape[:axis] + (n, piece_pad) + a.shape[axis + 1:]
    return lax.slice_in_dim(a.reshape(sh), 0, piece, axis=axis + 1).reshape(a.shape[:axis] + (n * piece,) + a.shape[axis + 1:])


def _pack(parts, rows_mult=8):
    flat = jnp.concatenate([p.reshape(-1).astype(F32) for p in parts])
    n = flat.shape[0]
    per = LANES * rows_mult
    tot = ((n + per - 1) // per) * per
    return jnp.pad(flat, (0, tot - n)).reshape(tot // LANES, LANES)


def _unpack(buf, shapes):
    flat = buf.reshape(-1)
    out, off = [], 0
    for s in shapes:
        n = 1
        for d in s:
            n *= d
        out.append(flat[off:off + n].reshape(s))
        off += n
    return out


def kernel(x, gm_w_in, gm_ln_g, gm_ln_b, gm_w_s, gm_b_s, gm_w_out, mla_w_in, mla_q_norm_g, mla_kv_norm_g, mla_w_q_b, mla_w_kv_b, mla_w_out, ffn_w_up, ffn_conv_w, ffn_conv_b, ffn_w_down, ln_mix_g, ln_mix_b, ln_ffn_g, ln_ffn_b, loss_target, m_gm_w_in, m_gm_ln_g, m_gm_ln_b, m_gm_w_s, m_gm_b_s, m_gm_w_out, m_mla_w_in, m_mla_q_norm_g, m_mla_kv_norm_g, m_mla_w_q_b, m_mla_w_kv_b, m_mla_w_out, m_ffn_w_up, m_ffn_conv_w, m_ffn_conv_b, m_ffn_w_down, m_ln_mix_g, m_ln_mix_b, m_ln_ffn_g, m_ln_ffn_b, v_gm_w_in, v_gm_ln_g, v_gm_ln_b, v_gm_w_s, v_gm_b_s, v_gm_w_out, v_mla_w_in, v_mla_q_norm_g, v_mla_kv_norm_g, v_mla_w_q_b, v_mla_w_kv_b, v_mla_w_out, v_ffn_w_up, v_ffn_conv_w, v_ffn_conv_b, v_ffn_w_down, v_ln_mix_g, v_ln_mix_b, v_ln_ffn_g, v_ln_ffn_b):
    S, D = x.shape[1], x.shape[2]
    xi, yi, ci = _coords()
    dev = 4 * xi + 2 * yi + ci
    c_idx = jnp.reshape(ci, (1,)).astype(jnp.int32)
    chip_idx = jnp.reshape(2 * xi + yi, (1,)).astype(jnp.int32)
    w_in_cols = mla_w_in.shape[2]
    w_in_pad = ((w_in_cols + LANES - 1) // LANES) * LANES
    n_gm, n_mla = gm_w_in.shape[0], mla_w_in.shape[0]

    cw_l = ffn_conv_w.shape[2]
    small_in = _pack([mla_q_norm_g, mla_kv_norm_g, ffn_conv_w])
    r_down = ffn_w_down.shape[1]
    r_pad = ((r_down + FFN_SHARD_MULT - 1) // FFN_SHARD_MULT) * FFN_SHARD_MULT

    def gather(shards, name):
        full = _all_gather_sc(list(shards.values()), name=name)
        return {k: f.reshape((-1, f.shape[2])) for k, f in zip(shards.keys(), full)}

    Wm, Wf = [None] * DEPTH, [None] * DEPTH
    for i in range(DEPTH):
        s = i // 2
        if i == 0:
            Wm[i] = gather(dict(w_in_t=gm_w_in[s].T.astype(BF16)), f"ag_gm_{i}_first")
            Wm[i].update(gather(dict(w_out=gm_w_out[s].astype(BF16), small=small_in), f"ag_gm_{i}"))
            small_all = Wm[i].pop("small").reshape((N_DEV,) + small_in.shape)
        elif i % 2 == 0:
            Wm[i] = gather(dict(w_in_t=gm_w_in[s].T.astype(BF16), w_out=gm_w_out[s].astype(BF16)), f"ag_gm_{i}")
        else:
            Wm[i] = gather(dict(w_in=_pad_cols(mla_w_in[s], w_in_pad).astype(BF16),
                                w_qb_t=_perm_q_rows(mla_w_q_b[s].T).astype(BF16),
                                w_kvb_t=mla_w_kv_b[s].T.astype(BF16), w_out=mla_w_out[s].astype(BF16)), f"ag_mla_{i}")
        Wf[i] = gather(dict(w_up_t=_pad_pieces(ffn_w_up[i].T, 0, r_down, r_pad).astype(BF16),
                            w_down=_pad_pieces(ffn_w_down[i], 0, r_down, r_pad).astype(BF16)), f"ag_ffn_{i}")

    gq_parts, gkv_parts, cw_parts = [], [], []
    for k in range(N_DEV):
        a, b, c_ = _unpack(small_all[k], [mla_q_norm_g.shape, mla_kv_norm_g.shape, ffn_conv_w.shape])
        gq_parts.append(a)
        gkv_parts.append(b)
        cw_parts.append(c_)
    gq_full = jnp.concatenate(gq_parts, axis=1)
    gkv_full = jnp.concatenate(gkv_parts, axis=1)
    cw_full = jnp.concatenate(cw_parts, axis=2)

    spm, spf = [None] * DEPTH, [None] * DEPTH
    for i in range(DEPTH):
        s = i // 2
        if i % 2 == 0:
            spm[i] = dict(ln_g=gm_ln_g[s][None], ln_b=gm_ln_b[s][None], w_s=gm_w_s[s], b_st=gm_b_s[s].T)
        else:
            spm[i] = dict(gq=gq_full[s][None], gkv=gkv_full[s][None])
        spf[i] = dict(cw=_pad_pieces(cw_full[i], 1, r_down, r_pad), cb=_pad_pieces(ffn_conv_b[i][None], 1, r_down, r_pad))
    ln = dict(mix_g=ln_mix_g[:, None], mix_b=ln_mix_b[:, None], ffn_g=ln_ffn_g[:, None], ffn_b=ln_ffn_b[:, None])

    small_box = {}

    def on_small(lp, gm_small, gf_small, gln):
        small_g = [
            jnp.stack([gm_small[2 * s]["ln_g"][0] for s in range(n_gm)]),
            jnp.stack([gm_small[2 * s]["ln_b"][0] for s in range(n_gm)]),
            jnp.stack([gm_small[2 * s]["w_s"] for s in range(n_gm)]),
            jnp.stack([gm_small[2 * s]["b_s"][:, :, 0] for s in range(n_gm)]),
            jnp.stack([_unpad_pieces(gf_small[i]["cb"], 1, r_down, r_pad)[0] for i in range(DEPTH)]),
            jnp.stack([gln[i]["mix_g"][0] for i in range(DEPTH)]),
            jnp.stack([gln[i]["mix_b"][0] for i in range(DEPTH)]),
            jnp.stack([gln[i]["ffn_g"][0] for i in range(DEPTH)]),
            jnp.stack([gln[i]["ffn_b"][0] for i in range(DEPTH)]),
            jnp.stack([gm_small[2 * s + 1]["gq"][0] for s in range(n_mla)]),
            jnp.stack([gm_small[2 * s + 1]["gkv"][0] for s in range(n_mla)]),
            jnp.stack([_unpad_pieces(gf_small[i]["cw"], 1, r_down, r_pad) for i in range(DEPTH)]),
        ]
        small_box["shapes"] = [g.shape for g in small_g]
        (small_box["all"],) = _all_gather_sc([_pack(small_g)], name="ag_small_grads")

    _SCHED.reducer = _Reducer(c_idx, chip_idx)
    lp, grad_x, _, _, _, _, _ = _fwd_bwd(x[0], loss_target[0], Wm, Wf, spm, spf, ln, on_small)
    red = _SCHED.reducer.finish()
    _SCHED.reducer = None
    rm = [{k: red[f"l{i}_{k}"] for k in Wm[i]} for i in range(DEPTH)]
    rf = [{k: red[f"l{i}_{k}"] for k in Wf[i]} for i in range(DEPTH)]

    g_gm_w_in = jnp.stack([rm[2 * s]["w_in_t"].T for s in range(n_gm)])
    g_gm_w_out = jnp.stack([rm[2 * s]["w_out"] for s in range(n_gm)])
    g_mla_w_in = jnp.stack([rm[2 * s + 1]["w_in"][:, :w_in_cols] for s in range(n_mla)])
    g_mla_w_q_b = jnp.stack([_unperm_q_rows(rm[2 * s + 1]["w_qb_t"]).T for s in range(n_mla)])
    g_mla_w_kv_b = jnp.stack([rm[2 * s + 1]["w_kvb_t"].T for s in range(n_mla)])
    g_mla_w_out = jnp.stack([rm[2 * s + 1]["w_out"] for s in range(n_mla)])
    g_ffn_w_up = jnp.stack([_unpad_pieces(rf[i]["w_up_t"], 0, r_down, r_pad).T for i in range(DEPTH)])
    g_ffn_w_down = jnp.stack([_unpad_pieces(rf[i]["w_down"], 0, r_down, r_pad) for i in range(DEPTH)])

    small_sum = _unpack(_sum8(small_box["all"], name="small_grad_sum"), small_box["shapes"])
    (g_gm_ln_g, g_gm_ln_b, g_gm_w_s, g_gm_b_s, g_ffn_conv_b, g_ln_mix_g, g_ln_mix_b, g_ln_ffn_g, g_ln_ffn_b,
     gq_all, gkv_all, cw_all) = small_sum
    (loss_all,) = _all_gather([_pack([jnp.reshape(0.5 * jnp.sum(lp) / D, (1,))])], name="ag_loss")
    loss = jnp.reshape(_sum8(loss_all, name="loss_sum")[0, 0], ())
    qn_l = mla_q_norm_g.shape[1]
    g_mla_q_norm_g = lax.dynamic_slice_in_dim(gq_all, dev * qn_l, qn_l, axis=1)
    g_mla_kv_norm_g = lax.dynamic_slice_in_dim(gkv_all, dev * qn_l, qn_l, axis=1)
    g_ffn_conv_w = lax.dynamic_slice_in_dim(cw_all, dev * cw_l, cw_l, axis=2)

    def adam_big(w, g, m, v, tag):
        sh = w.shape
        two = lambda a: a.reshape((-1, sh[-1]))
        d, mn, vn = _adamw(two(w), two(g), two(m), two(v), name=f"adamw_{tag}")
        return d.reshape(sh), mn.reshape(sh), vn.reshape(sh)

    big = [("gm_w_in", gm_w_in, g_gm_w_in, m_gm_w_in, v_gm_w_in), ("gm_w_out", gm_w_out, g_gm_w_out, m_gm_w_out, v_gm_w_out),
           ("mla_w_in", mla_w_in, g_mla_w_in, m_mla_w_in, v_mla_w_in), ("mla_w_q_b", mla_w_q_b, g_mla_w_q_b, m_mla_w_q_b, v_mla_w_q_b),
           ("mla_w_kv_b", mla_w_kv_b, g_mla_w_kv_b, m_mla_w_kv_b, v_mla_w_kv_b), ("mla_w_out", mla_w_out, g_mla_w_out, m_mla_w_out, v_mla_w_out),
           ("ffn_w_up", ffn_w_up, g_ffn_w_up, m_ffn_w_up, v_ffn_w_up), ("ffn_w_down", ffn_w_down, g_ffn_w_down, m_ffn_w_down, v_ffn_w_down)]
    res = {}
    for tag, w, g, m, v in big:
        res[tag] = (g,) + adam_big(w, g, m, v, tag)

    small = [("gm_ln_g", gm_ln_g, g_gm_ln_g, m_gm_ln_g, v_gm_ln_g), ("gm_ln_b", gm_ln_b, g_gm_ln_b, m_gm_ln_b, v_gm_ln_b),
             ("gm_w_s", gm_w_s, g_gm_w_s, m_gm_w_s, v_gm_w_s), ("gm_b_s", gm_b_s, g_gm_b_s, m_gm_b_s, v_gm_b_s),
             ("mla_q_norm_g", mla_q_norm_g, g_mla_q_norm_g, m_mla_q_norm_g, v_mla_q_norm_g),
             ("mla_kv_norm_g", mla_kv_norm_g, g_mla_kv_norm_g, m_mla_kv_norm_g, v_mla_kv_norm_g),
             ("ffn_conv_w", ffn_conv_w, g_ffn_conv_w, m_ffn_conv_w, v_ffn_conv_w), ("ffn_conv_b", ffn_conv_b, g_ffn_conv_b, m_ffn_conv_b, v_ffn_conv_b),
             ("ln_mix_g", ln_mix_g, g_ln_mix_g, m_ln_mix_g, v_ln_mix_g), ("ln_mix_b", ln_mix_b, g_ln_mix_b, m_ln_mix_b, v_ln_mix_b),
             ("ln_ffn_g", ln_ffn_g, g_ln_ffn_g, m_ln_ffn_g, v_ln_ffn_g), ("ln_ffn_b", ln_ffn_b, g_ln_ffn_b, m_ln_ffn_b, v_ln_ffn_b)]
    shapes = [t[1].shape for t in small]
    d_s, m_s, v_s = _adamw(_pack([t[1] for t in small]), _pack([t[2] for t in small]), _pack([t[3] for t in small]),
                           _pack([t[4] for t in small]), name="adamw_small")
    d_l, m_l, v_l = _unpack(d_s, shapes), _unpack(m_s, shapes), _unpack(v_s, shapes)
    for (tag, _, g, _, _), d, mn, vn in zip(small, d_l, m_l, v_l):
        res[tag] = (g, d, mn, vn)

    order = ["gm_w_in", "gm_ln_g", "gm_ln_b", "gm_w_s", "gm_b_s", "gm_w_out", "mla_w_in", "mla_q_norm_g", "mla_kv_norm_g",
             "mla_w_q_b", "mla_w_kv_b", "mla_w_out", "ffn_w_up", "ffn_conv_w", "ffn_conv_b", "ffn_w_down",
             "ln_mix_g", "ln_mix_b", "ln_ffn_g", "ln_ffn_b"]
    out = [loss, grad_x[None]]
    for q in range(4):
        out += [res[k][q] for k in order]
    return tuple(out)
```

```python
import jax
import jax.numpy as jnp
from jax import lax
from jax.experimental import pallas as pl
from jax.experimental.pallas import tpu as pltpu
from jax.experimental.pallas import tpu_sc as plsc

F32, BF16 = jnp.float32, jnp.bfloat16

DEPTH = 4
CHUNK = 64
GM_BLOCK = 128
GM_GROUPS = 8
HEADS = 16
NOPE, ROPE, VDIM = 128, 64, 128
QRANK, KVRANK = 512, 512
ROPE_THETA = 10000.0
SM_SCALE = (NOPE + ROPE) ** -0.5
ALPHA = (2 * DEPTH) ** 0.25
LN_EPS = 1e-5
RMS_EPS = 1e-6
ADAM_LR, ADAM_B1, ADAM_B2, ADAM_EPS, ADAM_WD, ADAM_STEP = 0.001, 0.9, 0.999, 1e-08, 0.01, 10

N_DEV = 8
LANES = 128
SUBLANES = 8
VMEM_LIMIT = 56 * 1024 * 1024
MESH = pl.DeviceIdType.MESH
HBM_SPEC = pl.BlockSpec(memory_space=pltpu.HBM)


class _Schedule:
    def __init__(self):
        self.last = None
        self.reducer = None
        self.big = {}
        self.clock = 0.0

    def tick(self, us):
        self.clock += us
        if self.reducer is not None:
            self.reducer.advance()


_SCHED = _Schedule()


def _ticked(out, us):
    _SCHED.tick(us)
    return out


def _call(body, *, name, out_shape, in_specs, out_specs, grid=None, scratch=(), dims=None, n_prefetch=0):
    kw = dict(vmem_limit_bytes=VMEM_LIMIT)
    if dims is not None:
        kw["dimension_semantics"] = dims
    cp = pltpu.CompilerParams(**kw)
    in_specs = list(in_specs)
    token = _SCHED.last
    if token is not None:
        pos = n_prefetch + len(in_specs)
        in_specs.append(pl.BlockSpec(memory_space=pl.ANY))
        inner = body

        def body(*refs):
            return inner(*refs[:pos], *refs[pos + 1:])

    if n_prefetch:
        gs = pltpu.PrefetchScalarGridSpec(num_scalar_prefetch=n_prefetch, grid=grid, in_specs=in_specs, out_specs=out_specs,
                                          scratch_shapes=list(scratch))
        fn = pl.pallas_call(body, name=name, grid_spec=gs, out_shape=out_shape, compiler_params=cp, interpret=False)
    else:
        extra = {} if grid is None else {"grid": grid}
        fn = pl.pallas_call(body, name=name, in_specs=in_specs, out_specs=out_specs, out_shape=out_shape,
                            scratch_shapes=list(scratch), compiler_params=cp, interpret=False, **extra)

    def run(*args):
        out = fn(*args, token) if token is not None else fn(*args)
        _SCHED.last = out[0] if isinstance(out, (list, tuple)) else out
        return out

    return run


def _tile(n, pref, mult=LANES):
    if n <= pref:
        return n
    t = (pref // mult) * mult
    while t >= mult:
        if n % t == 0:
            return t
        t -= mult
    return n


def _sds(shape, dtype):
    return jax.ShapeDtypeStruct(tuple(shape), dtype)


FFN_TILE = 1408
FFN_CHUNK = 64
FFN_SHARD_MULT = 64
MXU_FLOPS_PER_US = 8e8
TOKENS_K = 4096
_DN = {"nn": (((1,), (0,)), ((), ())), "nt": (((1,), (1,)), ((), ())), "tn": (((0,), (0,)), ((), ()))}


def _mm(a, b, mode, *, name, out_dtype, tm=1024, tn=1024, tk=2048, res=None, res_scale=1.0, a_parts=1, b_resident=False):
    if a_parts > 1:
        S_, F_ = a.shape[0] // a_parts, a.shape[1]
        a_shape = (S_, a_parts * F_)
    else:
        a_shape = a.shape
    if mode == "nn":
        (M, K), (K2, N) = a_shape, b.shape
    elif mode == "nt":
        (M, K), (N, K2) = a_shape, b.shape
    else:
        (K, M), (K2, N) = a_shape, b.shape
    assert K == K2, (name, a.shape, b.shape)
    tm, tn, tk = _tile(M, tm), _tile(N, tn), _tile(K, tk)
    nk = K // tk
    if a_parts > 1 and mode == "nn":
        per = F_ // tk
        assert F_ % tk == 0
        a_spec = pl.BlockSpec((tm, tk), lambda i, j, k: ((k // per) * (S_ // tm) + i, k % per))
    elif a_parts > 1:
        per = F_ // tm
        assert mode == "tn" and F_ % tm == 0
        a_spec = pl.BlockSpec((tk, tm), lambda i, j, k: ((i // per) * (S_ // tk) + k, i % per))
    elif mode == "tn":
        a_spec = pl.BlockSpec((tk, tm), lambda i, j, k: (k, i))
    else:
        a_spec = pl.BlockSpec((tm, tk), lambda i, j, k: (i, k))
    b_kw = {"pipeline_mode": pl.Buffered(1)} if b_resident else {}
    b_spec = (pl.BlockSpec((tn, tk), lambda i, j, k: (j, k), **b_kw) if mode == "nt"
              else pl.BlockSpec((tk, tn), lambda i, j, k: (k, j), **b_kw))
    in_specs = [a_spec, b_spec]
    args = [a, b]
    if res is not None:
        in_specs.append(pl.BlockSpec((tm, tn), lambda i, j, k: (i, j)))
        args.append(res)
    dn = _DN[mode]
    has_res = res is not None

    def body(*refs):
        a_ref, b_ref = refs[0], refs[1]
        r_ref = refs[2] if has_res else None
        o_ref = refs[2 + has_res]
        part = lax.dot_general(a_ref[...], b_ref[...], dn, preferred_element_type=F32)

        def finish(acc):
            if has_res:
                acc = acc + res_scale * r_ref[...]
            o_ref[...] = acc.astype(o_ref.dtype)

        if nk == 1:
            finish(part)
        else:
            acc_ref = refs[3 + has_res]
            k = pl.program_id(2)

            @pl.when(k == 0)
            def _():
                acc_ref[...] = part

            @pl.when(k > 0)
            def _():
                acc_ref[...] += part

            @pl.when(k == nk - 1)
            def _():
                finish(acc_ref[...])

    scratch = [pltpu.VMEM((tm, tn), F32)] if nk > 1 else []
    out = _call(body, name=name, grid=(M // tm, N // tn, nk), in_specs=in_specs,
                out_specs=pl.BlockSpec((tm, tn), lambda i, j, k: (i, j)), out_shape=_sds((M, N), out_dtype),
                scratch=scratch, dims=("parallel", "parallel", "arbitrary"))(*args)
    _SCHED.tick(2.0 * M * N * K / MXU_FLOPS_PER_US)
    return out


def _ln_fwd(x, m, g, b, *, name):
    S, D = x.shape
    tr = _tile(S, 256, 8)

    def body(x_ref, m_ref, g_ref, b_ref, y_ref, yb_ref, xh_ref, rs_ref):
        r = ALPHA * x_ref[...] + m_ref[...]
        mu = jnp.mean(r, axis=-1, keepdims=True)
        d = r - mu
        var = jnp.mean(d * d, axis=-1, keepdims=True)
        rstd = lax.rsqrt(var + LN_EPS)
        xh = d * rstd
        y = xh * g_ref[...] + b_ref[...]
        y_ref[...] = y
        yb_ref[...] = y.astype(BF16)
        xh_ref[...] = xh
        rs_ref[...] = rstd

    row = pl.BlockSpec((tr, D), lambda i: (i, 0))
    vec = pl.BlockSpec((1, D), lambda i: (0, 0))
    return _call(body, name=name, grid=(S // tr,), in_specs=[row, row, vec, vec],
                 out_specs=[row, row, row, pl.BlockSpec((tr, 1), lambda i: (i, 0))],
                 out_shape=[_sds((S, D), F32), _sds((S, D), BF16), _sds((S, D), F32), _sds((S, 1), F32)],
                 dims=("parallel",))(x, m, g, b)


def _ln_bwd(dy, xh, rstd, g, *, name):
    S, D = dy.shape
    tr = _tile(S, 256, 8)

    def body(dy_ref, xh_ref, rs_ref, g_ref, dr_ref, drb_ref, dg_ref, db_ref):
        @pl.when(pl.program_id(0) == 0)
        def _():
            dg_ref[...] = jnp.zeros_like(dg_ref)
            db_ref[...] = jnp.zeros_like(db_ref)

        dyv = dy_ref[...]
        xhv = xh_ref[...]
        dxh = dyv * g_ref[...]
        m1 = jnp.mean(dxh, axis=-1, keepdims=True)
        m2 = jnp.mean(dxh * xhv, axis=-1, keepdims=True)
        dr = rs_ref[...] * (dxh - m1 - xhv * m2)
        dr_ref[...] = dr
        drb_ref[...] = dr.astype(BF16)
        dg_ref[...] += jnp.sum(dyv * xhv, axis=0, keepdims=True)
        db_ref[...] += jnp.sum(dyv, axis=0, keepdims=True)

    row = pl.BlockSpec((tr, D), lambda i: (i, 0))
    vec = pl.BlockSpec((1, D), lambda i: (0, 0))
    return _ticked(_call(body, name=name, grid=(S // tr,), in_specs=[row, row, pl.BlockSpec((tr, 1), lambda i: (i, 0)), vec],
                         out_specs=[row, row, vec, vec],
                         out_shape=[_sds((S, D), F32), _sds((S, D), BF16), _sds((1, D), F32), _sds((1, D), F32)],
                         dims=("arbitrary",))(dy, xh, rstd, g), 45.0 * S / 4096)


def _loss_kernel(y, t, *, name):
    S, D = y.shape
    tr = _tile(S, 256, 8)

    def body(y_ref, t_ref, lp_ref, dy_ref):
        @pl.when(pl.program_id(0) == 0)
        def _():
            lp_ref[...] = jnp.zeros_like(lp_ref)

        e = y_ref[...] - t_ref[...]
        dy_ref[...] = e / D
        lp_ref[...] += jnp.sum(e * e, axis=0, keepdims=True)

    row = pl.BlockSpec((tr, D), lambda i: (i, 0))
    vec = pl.BlockSpec((1, D), lambda i: (0, 0))
    return _call(body, name=name, grid=(S // tr,), in_specs=[row, row], out_specs=[vec, row],
                 out_shape=[_sds((1, D), F32), _sds((S, D), F32)], dims=("arbitrary",))(y, t)


_GELU_C = 0.7978845608028654
_GELU_A = 0.044715


def _gelu(x):
    return 0.5 * x * (1.0 + jnp.tanh(_GELU_C * (x + _GELU_A * x * x * x)))


def _gelu_grad(x):
    x2 = x * x
    t = jnp.tanh(_GELU_C * (x + _GELU_A * x * x2))
    return 0.5 * (1.0 + t) + 0.5 * x * (1.0 - t * t) * (_GELU_C * (1.0 + 3.0 * _GELU_A * x2))


def _gelu_both(x):
    x2 = x * x
    t = jnp.tanh(_GELU_C * (x + _GELU_A * x * x2))
    half = 0.5 * (1.0 + t)
    return x * half, half + 0.5 * x * (1.0 - t * t) * (_GELU_C * (1.0 + 3.0 * _GELU_A * x2))


def _masked_ws(w):
    i = lax.broadcasted_iota(jnp.int32, w.shape, 0) // CHUNK
    j = lax.broadcasted_iota(jnp.int32, w.shape, 1) // CHUNK
    return jnp.where(j <= i, w, 0.0)


def _gm_mid_fwd(zp, ln_g, ln_b, w_s, b_st, *, name):
    S, H2 = zp.shape
    H = H2 // 2
    gd = H // GM_GROUPS
    nb = S // GM_BLOCK

    def body(zu_ref, zv_ref, g_ref, b_ref, w_ref, bs_ref, p_ref, s_ref, vh_ref, rs_ref):
        v = _gelu(zv_ref[...].astype(F32))
        mu = jnp.mean(v, axis=-1, keepdims=True)
        d = v - mu
        var = jnp.mean(d * d, axis=-1, keepdims=True)
        rstd = lax.rsqrt(var + LN_EPS)
        vh = d * rstd
        vh_ref[...] = vh.astype(BF16)
        rs_ref[...] = rstd
        vn = (vh * g_ref[...] + b_ref[...]).astype(BF16)
        bs = bs_ref[...]
        for gi in range(GM_GROUPS):
            cs = slice(gi * gd, (gi + 1) * gd)
            wm = _masked_ws(w_ref[gi]).astype(BF16)
            s = jnp.dot(wm, vn[:, cs], preferred_element_type=F32) + bs[:, gi:gi + 1]
            u = _gelu(zu_ref[:, cs].astype(F32))
            s_ref[:, cs] = s.astype(BF16)
            p_ref[:, cs] = (u * s).astype(BF16)

    blk = lambda c: pl.BlockSpec((GM_BLOCK, H), lambda n, c=c: (n, c))
    vec = pl.BlockSpec((1, H), lambda n: (0, 0))
    return _call(body, name=name, grid=(nb,),
                 in_specs=[blk(0), blk(1), vec, vec, pl.BlockSpec((GM_GROUPS, GM_BLOCK, GM_BLOCK), lambda n: (0, 0, 0)),
                           pl.BlockSpec((GM_BLOCK, GM_GROUPS), lambda n: (0, 0))],
                 out_specs=[blk(0), blk(0), blk(0), pl.BlockSpec((GM_BLOCK, 1), lambda n: (n, 0))],
                 out_shape=[_sds((S, H), BF16), _sds((S, H), BF16), _sds((S, H), BF16), _sds((S, 1), F32)],
                 dims=("parallel",))(zp, zp, ln_g, ln_b, w_s, b_st)


def _gm_mid_bwd(dp, zp, s, vhat, rstd, ln_g, ln_b, w_s, *, name):
    S, H2 = zp.shape
    H = H2 // 2
    gd = H // GM_GROUPS
    nb = S // GM_BLOCK

    def body(dp_ref, zu_ref, zv_ref, s_ref, vh_ref, rs_ref, g_ref, b_ref, w_ref,
             dz_ref, dw_ref, dbs_ref, dg_ref, db_ref, dvh_ref):
        @pl.when(pl.program_id(0) == 0)
        def _():
            dw_ref[...] = jnp.zeros_like(dw_ref)
            dbs_ref[...] = jnp.zeros_like(dbs_ref)
            dg_ref[...] = jnp.zeros_like(dg_ref)
            db_ref[...] = jnp.zeros_like(db_ref)

        m1 = jnp.zeros((GM_BLOCK, 1), F32)
        m2 = jnp.zeros((GM_BLOCK, 1), F32)
        for gi in range(GM_GROUPS):
            cs = slice(gi * gd, (gi + 1) * gd)
            dpg = dp_ref[:, cs].astype(F32)
            zu = zu_ref[:, cs].astype(F32)
            u, du_dz = _gelu_both(zu)
            ds = dpg * u
            du = dpg * s_ref[:, cs].astype(F32)
            dz_ref[:, cs] = (du * du_dz).astype(BF16)
            dsb = ds.astype(BF16)
            vh = vh_ref[:, cs].astype(F32)
            lg = g_ref[:, cs]
            vn = (vh * lg + b_ref[:, cs]).astype(BF16)
            wm = _masked_ws(w_ref[gi]).astype(BF16)
            dvn = lax.dot_general(wm, dsb, _DN["tn"], preferred_element_type=F32)
            dw_ref[gi] += lax.dot_general(dsb, vn, _DN["nt"], preferred_element_type=F32)
            dbs_ref[gi] += jnp.sum(ds, axis=1, keepdims=True)
            dg_ref[:, cs] += jnp.sum(dvn * vh, axis=0, keepdims=True)
            db_ref[:, cs] += jnp.sum(dvn, axis=0, keepdims=True)
            dvh = dvn * lg
            dvh_ref[:, cs] = dvh
            m1 = m1 + jnp.sum(dvh, axis=1, keepdims=True)
            m2 = m2 + jnp.sum(dvh * vh, axis=1, keepdims=True)
        m1 = m1 / H
        m2 = m2 / H
        rs = rs_ref[...]
        for gi in range(GM_GROUPS):
            cs = slice(gi * gd, (gi + 1) * gd)
            vh = vh_ref[:, cs].astype(F32)
            dv = rs * (dvh_ref[:, cs] - m1 - vh * m2)
            zv = zv_ref[:, cs].astype(F32)
            dz_ref[:, H + gi * gd:H + (gi + 1) * gd] = (dv * _gelu_grad(zv)).astype(BF16)

        @pl.when(pl.program_id(0) == nb - 1)
        def _():
            for gi in range(GM_GROUPS):
                dw_ref[gi] = _masked_ws(dw_ref[gi])

    blk = lambda c: pl.BlockSpec((GM_BLOCK, H), lambda n, c=c: (n, c))
    vec = pl.BlockSpec((1, H), lambda n: (0, 0))
    wspec = pl.BlockSpec((GM_GROUPS, GM_BLOCK, GM_BLOCK), lambda n: (0, 0, 0))
    return _ticked(_call(body, name=name, grid=(nb,),
                 in_specs=[blk(0), blk(0), blk(1), blk(0), blk(0), pl.BlockSpec((GM_BLOCK, 1), lambda n: (n, 0)), vec, vec, wspec],
                 out_specs=[pl.BlockSpec((GM_BLOCK, H2), lambda n: (n, 0)), wspec,
                            pl.BlockSpec((GM_GROUPS, GM_BLOCK, 1), lambda n: (0, 0, 0)), vec, vec],
                 out_shape=[_sds((S, H2), BF16), _sds((GM_GROUPS, GM_BLOCK, GM_BLOCK), F32),
                            _sds((GM_GROUPS, GM_BLOCK, 1), F32), _sds((1, H), F32), _sds((1, H), F32)],
                 scratch=[pltpu.VMEM((GM_BLOCK, H), F32)], dims=("arbitrary",))(dp, zp, zp, s, vhat, rstd, ln_g, ln_b, w_s),
                   280.0 * S / 4096)


def _shift_down(x, k):
    r = pltpu.roll(x, k, 0)
    rows = lax.broadcasted_iota(jnp.int32, (SUBLANES, x.shape[1]), 0)
    return jnp.concatenate([jnp.where(rows >= k, r[:SUBLANES], 0.0), r[SUBLANES:]], axis=0)


def _shift_up(x, k):
    n = x.shape[0]
    r = pltpu.roll(x, n - k, 0)
    rows = lax.broadcasted_iota(jnp.int32, (SUBLANES, x.shape[1]), 0)
    return jnp.concatenate([r[:n - SUBLANES], jnp.where(rows < SUBLANES - k, r[n - SUBLANES:], 0.0)], axis=0)


def _conv(h, w, b):
    return w[0:1, :] * _shift_down(h, 2) + w[1:2, :] * _shift_down(h, 1) + w[2:3, :] * h + b


def _ffn_mid_fwd(h, cw, cb, *, name):
    S, F2 = h.shape
    F = F2 // 2
    nf = F // LANES

    def body(ha_ref, hg_ref, wa_ref, wg_ref, ba_ref, bg_ref, act_ref):
        a = _conv(ha_ref[...].astype(F32), wa_ref[...], ba_ref[...])
        g = _conv(hg_ref[...].astype(F32), wg_ref[...], bg_ref[...])
        act_ref[...] = (g * jax.nn.sigmoid(g) * a).astype(BF16)

    sl = lambda off, r: pl.BlockSpec((r, LANES), lambda j, off=off: (0, j + off))
    return _call(body, name=name, grid=(nf,),
                 in_specs=[sl(0, S), sl(nf, S), sl(0, 3), sl(nf, 3), sl(0, 1), sl(nf, 1)],
                 out_specs=sl(0, S), out_shape=_sds((S, F), BF16), dims=("parallel",))(h, h, cw, cw, cb, cb)


def _ffn_mid_bwd(dact, h, cw, cb, *, name):
    S, F2 = h.shape
    F = F2 // 2
    nf = F // LANES

    R, HB, HF = FFN_CHUNK, 16, SUBLANES
    nc = S // R

    def body(da_ref, ha_ref, hg_ref, wa_ref, wg_ref, ba_ref, bg_ref, dh_ref, dw_ref, db_ref, a_s, g_s, d_s):
        wa, wg, ba, bg = wa_ref[...], wg_ref[...], ba_ref[...], bg_ref[...]
        zero_tail = jnp.zeros((HF, LANES), F32)
        a_s[S:S + HF, :] = zero_tail
        g_s[S:S + HF, :] = zero_tail
        d_s[S:S + HF, :] = zero_tail

        def conv_chunk(win, w, b):
            return (w[0:1, :] * pltpu.roll(win, 2, 0)[HB:] + w[1:2, :] * pltpu.roll(win, 1, 0)[HB:] + w[2:3, :] * win[HB:] + b)

        def pass1(r0, win_a, win_g):
            rows = pl.ds(r0, R)
            a_s[rows, :] = conv_chunk(win_a, wa, ba)
            g_s[rows, :] = conv_chunk(win_g, wg, bg)
            d_s[rows, :] = da_ref[rows, :].astype(F32)

        lead = jnp.zeros((HB, LANES), F32)
        pass1(0, jnp.concatenate([lead, ha_ref[0:R, :].astype(F32)], axis=0),
              jnp.concatenate([lead, hg_ref[0:R, :].astype(F32)], axis=0))

        @pl.loop(1, nc)
        def _(c):
            r0 = pl.multiple_of(c * R, R)
            win = pl.ds(pl.multiple_of(c * R - HB, HB), R + HB)
            pass1(r0, ha_ref[win, :].astype(F32), hg_ref[win, :].astype(F32))

        def fold(x):
            return jnp.sum(x.reshape(R // SUBLANES, SUBLANES, LANES), axis=0)

        def pass2(c, acc):
            r0 = pl.multiple_of(c * R, R)
            rows, win = pl.ds(r0, R), pl.ds(r0, R + HF)
            a, g, d = a_s[win, :], g_s[win, :], d_s[win, :]
            sg = jax.nn.sigmoid(g)
            out = []
            for p, dc, w, h_ref in ((0, d * (g * sg), wa, ha_ref), (1, d * a * (sg * (1.0 + g * (1.0 - sg))), wg, hg_ref)):
                up1 = pltpu.roll(dc, R + HF - 1, 0)[0:R]
                up2 = pltpu.roll(dc, R + HF - 2, 0)[0:R]
                dc = dc[0:R]
                dh_ref[p, rows, :] = (w[2:3, :] * dc + w[1:2, :] * up1 + w[0:1, :] * up2).astype(BF16)
                hin = h_ref[rows, :].astype(F32)
                out.append((acc[p][0] + fold(up2 * hin), acc[p][1] + fold(up1 * hin), acc[p][2] + fold(dc * hin), acc[p][3] + fold(dc)))
            return tuple(out)

        z = jnp.zeros((SUBLANES, LANES), F32)
        acc = lax.fori_loop(0, nc, pass2, ((z, z, z, z), (z, z, z, z)))
        for p in range(2):
            for k in range(3):
                dw_ref[p, k:k + 1, :] = jnp.sum(acc[p][k], axis=0, keepdims=True)
            db_ref[p] = jnp.sum(acc[p][3], axis=0, keepdims=True)

    sl = lambda off, r: pl.BlockSpec((r, LANES), lambda j, off=off: (0, j + off))
    out = lambda r: pl.BlockSpec((2, r, LANES), lambda j: (0, 0, j))
    pad = pltpu.VMEM((S + HF, LANES), F32)
    return _ticked(_call(body, name=name, grid=(nf,),
                 in_specs=[sl(0, S), sl(0, S), sl(nf, S), sl(0, 3), sl(nf, 3), sl(0, 1), sl(nf, 1)],
                 out_specs=[out(S), out(3), out(1)],
                 out_shape=[_sds((2, S, F), BF16), _sds((2, 3, F), F32), _sds((2, 1, F), F32)],
                 scratch=[pad, pad, pad], dims=("parallel",))(dact, h, h, cw, cw, cb, cb), 320.0 * S / 4096)


def _swap_half(x):
    lane = lax.broadcasted_iota(jnp.int32, x.shape, 1)
    return jnp.where((lane % ROPE) < ROPE // 2, pltpu.roll(x, LANES - ROPE // 2, 1), pltpu.roll(x, ROPE // 2, 1))


def _rope(x, cos, sin_s):
    return x * cos + _swap_half(x) * sin_s


def _mla_prep(h, gq, gkv, cos, sin_s, *, name):
    S, W = h.shape
    tr = _tile(S, 512, 8)

    def body(h_ref, gq_ref, gkv_ref, cos_ref, sin_ref, cq_ref, ckv_ref, kr_ref, rq_ref, rkv_ref):
        cq = h_ref[:, 0:QRANK]
        rq = lax.rsqrt(jnp.mean(cq * cq, axis=-1, keepdims=True) + RMS_EPS)
        cq_ref[...] = (cq * rq * gq_ref[...]).astype(BF16)
        rq_ref[...] = rq
        ckv = h_ref[:, QRANK:QRANK + KVRANK]
        rkv = lax.rsqrt(jnp.mean(ckv * ckv, axis=-1, keepdims=True) + RMS_EPS)
        ckv_ref[...] = (ckv * rkv * gkv_ref[...]).astype(BF16)
        rkv_ref[...] = rkv
        kr = _rope(h_ref[:, QRANK + KVRANK:W], cos_ref[...], sin_ref[...])
        lane = lax.broadcasted_iota(jnp.int32, kr.shape, 1)
        kr = jnp.where(lane < ROPE, kr, 0.0)
        kr_ref[...] = (kr + pltpu.roll(kr, ROPE, 1)).astype(BF16)

    row = lambda w: pl.BlockSpec((tr, w), lambda i: (i, 0))
    vec = lambda w: pl.BlockSpec((1, w), lambda i: (0, 0))
    return _call(body, name=name, grid=(S // tr,),
                 in_specs=[row(W), vec(QRANK), vec(KVRANK), row(LANES), row(LANES)],
                 out_specs=[row(QRANK), row(KVRANK), row(LANES), row(1), row(1)],
                 out_shape=[_sds((S, QRANK), BF16), _sds((S, KVRANK), BF16), _sds((S, LANES), BF16), _sds((S, 1), F32), _sds((S, 1), F32)],
                 dims=("parallel",))(h, gq, gkv, cos, sin_s)


def _mla_prep_bwd(dcqn, dckvn, dkr2, h, rq, rkv, gq, gkv, cos, sin_neg, *, name):
    S, W = h.shape
    tr = _tile(S, 512, 8)

    def rms_bwd(dy, c, r, g):
        n = c * r
        dn = dy * g
        return r * (dn - n * jnp.mean(dn * n, axis=-1, keepdims=True)), jnp.sum(dy * n, axis=0, keepdims=True)

    def body(dq_ref, dkv_ref, dkr_ref, h_ref, rq_ref, rkv_ref, gq_ref, gkv_ref, cos_ref, sin_ref, dh_ref, dgq_ref, dgkv_ref):
        @pl.when(pl.program_id(0) == 0)
        def _():
            dgq_ref[...] = jnp.zeros_like(dgq_ref)
            dgkv_ref[...] = jnp.zeros_like(dgkv_ref)

        dcq, dg = rms_bwd(dq_ref[...], h_ref[:, 0:QRANK], rq_ref[...], gq_ref[...])
        dgq_ref[...] += dg
        dh_ref[:, 0:QRANK] = dcq.astype(BF16)
        dckv, dg = rms_bwd(dkv_ref[...], h_ref[:, QRANK:QRANK + KVRANK], rkv_ref[...], gkv_ref[...])
        dgkv_ref[...] += dg
        dh_ref[:, QRANK:QRANK + KVRANK] = dckv.astype(BF16)
        dk = dkr_ref[...]
        dk = dk + pltpu.roll(dk, ROPE, 1)
        dk = _rope(dk, cos_ref[...], sin_ref[...])
        lane = lax.broadcasted_iota(jnp.int32, dk.shape, 1)
        dh_ref[:, QRANK + KVRANK:W] = jnp.where(lane < ROPE, dk, 0.0).astype(BF16)

    row = lambda w: pl.BlockSpec((tr, w), lambda i: (i, 0))
    vec = lambda w: pl.BlockSpec((1, w), lambda i: (0, 0))
    return _call(body, name=name, grid=(S // tr,),
                 in_specs=[row(QRANK), row(KVRANK), row(LANES), row(W), row(1), row(1), vec(QRANK), vec(KVRANK), row(LANES), row(LANES)],
                 out_specs=[row(W), vec(QRANK), vec(KVRANK)],
                 out_shape=[_sds((S, W), BF16), _sds((1, QRANK), F32), _sds((1, KVRANK), F32)],
                 dims=("arbitrary",))(dcqn, dckvn, dkr2, h, rq, rkv, gq, gkv, cos, sin_neg)


QPAIR = 2 * NOPE + 2 * ROPE
KVPAIR = 2 * (NOPE + VDIM)


def _rope_q(q, cos, sin_s, *, name):
    S, Wq = q.shape
    tr = _tile(S, 512, 8)

    def body(q_ref, cos_ref, sin_ref, o_ref):
        o_ref[:, 0:2 * NOPE] = q_ref[:, 0:2 * NOPE].astype(BF16)
        o_ref[:, 2 * NOPE:QPAIR] = _rope(q_ref[:, 2 * NOPE:QPAIR], cos_ref[...], sin_ref[...]).astype(BF16)

    blk = pl.BlockSpec((tr, QPAIR), lambda i, p: (i, p))
    tab = pl.BlockSpec((tr, LANES), lambda i, p: (i, 0))
    return _call(body, name=name, grid=(S // tr, Wq // QPAIR), in_specs=[blk, tab, tab], out_specs=blk,
                 out_shape=_sds((S, Wq), BF16), dims=("parallel", "parallel"))(q, cos, sin_s)


ATT_T = 512
LOG2_E = 1.4426950408889634


def _att_scores(kc, qc, allowed):
    s = lax.dot_general(kc, qc, _DN["nt"], preferred_element_type=F32) * (SM_SCALE * LOG2_E)
    return s if allowed is None else jnp.where(allowed, s, -jnp.inf)


def _att_allowed(k0, q0, t):
    kpos = k0 + lax.broadcasted_iota(jnp.int32, (t, t), 0)
    qpos = q0 + lax.broadcasted_iota(jnp.int32, (t, t), 1)
    return (kpos // CHUNK) <= (qpos // CHUNK)


def _head_mask(r, hh):
    lane = lax.broadcasted_iota(jnp.int32, r.shape, 1)
    keep = (lane < ROPE) if hh == 0 else (lane >= ROPE)
    return jnp.where(keep, r, jnp.zeros_like(r))


def _att_q(q_ref):
    r = q_ref[:, 2 * NOPE:QPAIR]
    return [jnp.concatenate([q_ref[:, hh * NOPE:(hh + 1) * NOPE], _head_mask(r, hh)], axis=1) for hh in range(2)]


def _attn_fwd(qb, kv, kr2, *, name):
    S = qb.shape[0]
    t = _tile(S, ATT_T, 8)
    nq = S // t
    npair = HEADS // 2

    def body(q_ref, kv_ref, kr_ref, o_ref, lse_ref):
        i = pl.program_id(1)
        q0 = i * t
        qc = _att_q(q_ref)

        def step(j, carry, masked):
            k0 = pl.multiple_of(j * t, t)
            rows = pl.ds(k0, t)
            kr = kr_ref[rows, :]
            allowed = _att_allowed(k0, q0, t) if masked else None
            out = []
            for hh in range(2):
                m, l, acc = carry[hh]
                c0 = hh * (NOPE + VDIM)
                kc = jnp.concatenate([kv_ref[rows, c0:c0 + NOPE], kr], axis=1)
                v = kv_ref[rows, c0 + NOPE:c0 + NOPE + VDIM]
                s = _att_scores(kc, qc[hh], allowed)
                m_new = jnp.maximum(m, jnp.max(s, axis=0, keepdims=True))
                p = jnp.exp2(s - m_new)
                a = jnp.exp2(m - m_new)
                l = a * l + jnp.sum(p, axis=0, keepdims=True)
                acc = a * acc + lax.dot_general(v, p.astype(BF16), _DN["tn"], preferred_element_type=F32)
                out.append((m_new, l, acc))
            return tuple(out)

        one = (jnp.full((1, t), -jnp.inf, F32), jnp.zeros((1, t), F32), jnp.zeros((VDIM, t), F32))
        carry = lax.fori_loop(0, i, lambda j, c: step(j, c, False), (one, one))
        carry = step(i, carry, True)
        for hh in range(2):
            m, l, acc = carry[hh]
            o_ref[:, hh * VDIM:(hh + 1) * VDIM] = jnp.transpose(acc / l).astype(BF16)
            lse_ref[hh:hh + 1, :] = m + jnp.log2(l)

    return _call(body, name=name, grid=(npair, nq),
                 in_specs=[pl.BlockSpec((t, QPAIR), lambda p, i: (i, p)), pl.BlockSpec((S, KVPAIR), lambda p, i: (0, p)),
                           pl.BlockSpec((S, LANES), lambda p, i: (0, 0))],
                 out_specs=[pl.BlockSpec((t, 2 * VDIM), lambda p, i: (i, p)), pl.BlockSpec((None, 2, t), lambda p, i: (p, 0, i))],
                 out_shape=[_sds((S, HEADS * VDIM), BF16), _sds((npair, 2, S), F32)],
                 dims=("parallel", "arbitrary"))(qb, kv, kr2)


def _attn_bwd(qb, kv, kr2, o, do, lse, *, name):
    S = qb.shape[0]
    t = _tile(S, ATT_T, 8)
    nq = S // t
    npair = HEADS // 2

    def body(q_ref, kv_ref, kr_ref, o_ref, do_ref, lse_ref, dq_ref, dkv_ref, dkr_ref, acc_ref):
        pid = pl.program_id(0)
        i = pl.program_id(1)
        q0 = i * t

        @pl.when(i == 0)
        def _():
            acc_ref[...] = jnp.zeros_like(acc_ref)

        @pl.when((i == 0) & (pid == 0))
        def _():
            dkr_ref[...] = jnp.zeros_like(dkr_ref)

        qc = _att_q(q_ref)
        dov = [do_ref[:, hh * VDIM:(hh + 1) * VDIM] for hh in range(2)]
        dsum = [jnp.sum(jnp.transpose(dov[hh].astype(F32) * o_ref[:, hh * VDIM:(hh + 1) * VDIM].astype(F32)), axis=0, keepdims=True)
                for hh in range(2)]
        lse_v = [lse_ref[hh:hh + 1, :] for hh in range(2)]

        def step(j, carry, masked):
            k0 = pl.multiple_of(j * t, t)
            rows = pl.ds(k0, t)
            kr = kr_ref[rows, :]
            allowed = _att_allowed(k0, q0, t) if masked else None
            out = []
            dkr = jnp.zeros((t, LANES), F32)
            for hh in range(2):
                c0 = hh * (NOPE + VDIM)
                kc = jnp.concatenate([kv_ref[rows, c0:c0 + NOPE], kr], axis=1)
                v = kv_ref[rows, c0 + NOPE:c0 + NOPE + VDIM]
                p = jnp.exp2(_att_scores(kc, qc[hh], allowed) - lse_v[hh])
                dp = lax.dot_general(v, dov[hh], _DN["nt"], preferred_element_type=F32)
                ds = (p * (dp - dsum[hh]) * SM_SCALE).astype(BF16)
                acc_ref[rows, c0 + NOPE:c0 + NOPE + VDIM] += jnp.dot(p.astype(BF16), dov[hh], preferred_element_type=F32)
                dkc = jnp.dot(ds, qc[hh], preferred_element_type=F32)
                acc_ref[rows, c0:c0 + NOPE] += dkc[:, 0:NOPE]
                dkr = dkr + dkc[:, NOPE:]
                out.append(carry[hh] + lax.dot_general(ds, kc, _DN["tn"], preferred_element_type=F32))
            dkr_ref[rows, :] += dkr
            return tuple(out)

        zero = jnp.zeros((t, NOPE + LANES), F32)
        carry = lax.fori_loop(0, i, lambda j, c: step(j, c, False), (zero, zero))
        dqc = step(i, carry, True)
        for hh in range(2):
            dq_ref[:, hh * NOPE:(hh + 1) * NOPE] = dqc[hh][:, 0:NOPE]
        dq_ref[:, 2 * NOPE:QPAIR] = _head_mask(dqc[0][:, NOPE:], 0) + _head_mask(dqc[1][:, NOPE:], 1)

        @pl.when(i == nq - 1)
        def _():
            dkv_ref[...] = acc_ref[...].astype(BF16)

    qspec = pl.BlockSpec((t, QPAIR), lambda p, i: (i, p))
    hspec = pl.BlockSpec((t, 2 * VDIM), lambda p, i: (i, p))
    kvspec = pl.BlockSpec((S, KVPAIR), lambda p, i: (0, p))
    krspec = pl.BlockSpec((S, LANES), lambda p, i: (0, 0))
    return _ticked(_call(body, name=name, grid=(npair, nq),
                 in_specs=[qspec, kvspec, krspec, hspec, hspec, pl.BlockSpec((None, 2, t), lambda p, i: (p, 0, i))],
                 out_specs=[qspec, kvspec, krspec],
                 out_shape=[_sds(qb.shape, F32), _sds(kv.shape, BF16), _sds((S, LANES), F32)],
                 scratch=[pltpu.VMEM((S, KVPAIR), F32)], dims=("arbitrary", "arbitrary"))(qb, kv, kr2, o, do, lse),
                   640.0 * (S / 4096) ** 2)


def _coords():
    return lax.axis_index("x"), lax.axis_index("y"), lax.axis_index("c")


def _all_gather_body(n, shake):
    def body(*refs):
        ins, outs = refs[:n], refs[n:2 * n]
        send_sems, recv_sems, local_sems = refs[2 * n:]
        x, y, c = _coords()
        me, sib = (x, y, c), (x, y, 1 - c)
        chips = [(1 - x, y), (x, 1 - y), (1 - x, 1 - y)]
        if shake:
            _handshake([sib] + [(*chip, c) for chip in chips])

        def copy(a, k, block, to, src=None):
            dst = outs[a].at[4 * block[0] + 2 * block[1] + block[2]]
            return pltpu.make_async_remote_copy(src_ref=dst if src is None else src, dst_ref=dst,
                                                send_sem=send_sems.at[a, k], recv_sem=recv_sems.at[a, k],
                                                device_id=to, device_id_type=MESH)

        mine = [pltpu.make_async_copy(ins[a], outs[a].at[4 * x + 2 * y + c], local_sems.at[a]) for a in range(n)]
        for cp in mine:
            cp.start()
        sends = []
        for a in range(n):
            sends.append(copy(a, 0, me, sib, src=ins[a]))
            sends += [copy(a, 1 + j, me, (*chip, c), src=ins[a]) for j, chip in enumerate(chips)]
        for cp in sends:
            cp.start()
        for j, chip in enumerate(chips):
            for a in range(n):
                copy(a, 1 + j, (*chip, c), me).wait_recv()
                fwd = copy(a, 4 + j, (*chip, c), sib)
                fwd.start()
                sends.append(fwd)
        for a in range(n):
            copy(a, 0, sib, me).wait_recv()
            for j, chip in enumerate(chips):
                copy(a, 4 + j, (*chip, 1 - c), me).wait_recv()
        for cp in sends:
            cp.wait_send()
        for cp in mine:
            cp.wait()

    return body


def _all_gather(arrs, *, name):
    n = len(arrs)
    outs = _call(_all_gather_body(n, False), name=name, in_specs=[HBM_SPEC] * n, out_specs=[HBM_SPEC] * n,
                 out_shape=[_sds((N_DEV,) + a.shape, a.dtype) for a in arrs],
                 scratch=[pltpu.SemaphoreType.DMA((n, 7)), pltpu.SemaphoreType.DMA((n, 7)), pltpu.SemaphoreType.DMA((n,))])(*arrs)
    return list(outs)


def _sequencer_call(body, *, name, out_type, n_sems, collective_id):
    return pl.kernel(body, out_type=out_type, mesh=plsc.ScalarSubcoreMesh(axis_name="sq", num_cores=1), name=name,
                     scratch_types=[pltpu.SemaphoreType.DMA(n_sems), pltpu.SemaphoreType.DMA(n_sems), pltpu.SemaphoreType.DMA((n_sems[0],))],
                     compiler_params=pltpu.CompilerParams(collective_id=collective_id))


def _handshake(peers):
    barrier = pltpu.get_barrier_semaphore()
    for p in peers:
        pl.semaphore_signal(barrier, inc=1, device_id=p, device_id_type=MESH)
    pl.semaphore_wait(barrier, len(peers))


AG_ID, PAIR_ID, CHIP_ID = 1, 2, 3


def _all_gather_sc(arrs, *, name):
    n = len(arrs)
    outs = _sequencer_call(_all_gather_body(n, True), name=name, out_type=[_sds((N_DEV,) + a.shape, a.dtype) for a in arrs],
                           n_sems=(n, 7), collective_id=AG_ID)(*arrs)
    return list(outs)


def _pair_exchange_sc(gs, *, name):
    n = len(gs)

    def body(*refs):
        ins, outs = refs[:n], refs[n:2 * n]
        send_sems, recv_sems, _ = refs[2 * n:]
        x, y, c = _coords()
        sib = (x, y, 1 - c)
        _handshake([sib])
        cps = []
        for a in range(n):
            for j in range(4):
                cps.append(pltpu.make_async_remote_copy(src_ref=ins[a].at[2 * j + (1 - c)], dst_ref=outs[a].at[j],
                                                        send_sem=send_sems.at[a, j], recv_sem=recv_sems.at[a, j],
                                                        device_id=sib, device_id_type=MESH))
        for cp in cps:
            cp.start()
        for cp in cps:
            cp.wait()

    outs = _sequencer_call(body, name=name, out_type=[_sds((4,) + g.shape[1:], g.dtype) for g in gs],
                           n_sems=(n, 4), collective_id=PAIR_ID)(*gs)
    return list(outs)


def _chip_exchange_sc(ps, *, name):
    n = len(ps)

    def body(*refs):
        ins, outs = refs[:n], refs[n:2 * n]
        send_sems, recv_sems, _ = refs[2 * n:]
        x, y, c = _coords()
        chips = [(1 - x, y), (x, 1 - y), (1 - x, 1 - y)]
        _handshake([(*chip, c) for chip in chips])
        cps = []
        for a in range(n):
            for r, (px, py) in enumerate(chips):
                cps.append(pltpu.make_async_remote_copy(src_ref=ins[a].at[2 * px + py], dst_ref=outs[a].at[r],
                                                        send_sem=send_sems.at[a, r], recv_sem=recv_sems.at[a, r],
                                                        device_id=(px, py, c), device_id_type=MESH))
        for cp in cps:
            cp.start()
        for cp in cps:
            cp.wait()

    outs = _sequencer_call(body, name=name, out_type=[_sds((3,) + p.shape[1:], p.dtype) for p in ps],
                           n_sems=(n, 3), collective_id=CHIP_ID)(*ps)
    return list(outs)


def _row_tile(R, C, budget=1 << 20, mult=16):
    want = max(mult, budget // (4 * C))
    if R <= want:
        return R
    t = (want // mult) * mult
    while t >= mult:
        if R % t == 0:
            return t
        t -= mult
    return R


RS_TILE_BYTES = 6 << 20


def _pair_sum(g, l1, c_idx, *, name):
    _, R, C = g.shape
    tr = _row_tile(R, C, budget=RS_TILE_BYTES)
    g4 = g.reshape(4, 2, R, C)

    def body(c_ref, g_ref, l_ref, o_ref):
        o_ref[...] = (g_ref[...].astype(F32) + l_ref[...].astype(F32)).astype(o_ref.dtype)

    return _call(body, name=name, n_prefetch=1, grid=(4, R // tr),
                 in_specs=[pl.BlockSpec((None, None, tr, C), lambda j, r, c_ref: (j, c_ref[0], r, 0)),
                           pl.BlockSpec((None, tr, C), lambda j, r, c_ref: (j, r, 0))],
                 out_specs=pl.BlockSpec((None, tr, C), lambda j, r, c_ref: (j, r, 0)),
                 out_shape=_sds((4, R, C), g.dtype))(c_idx, g4, l1)


def _chip_sum(p, l2, chip_idx, *, name):
    _, R, C = p.shape
    tr = _row_tile(R, C, budget=RS_TILE_BYTES)

    def body(j_ref, p_ref, l_ref, o_ref):
        o_ref[...] = ((p_ref[...].astype(F32) + l_ref[0].astype(F32)) + l_ref[1].astype(F32)) + l_ref[2].astype(F32)

    return _call(body, name=name, n_prefetch=1, grid=(R // tr,),
                 in_specs=[pl.BlockSpec((None, tr, C), lambda r, j_ref: (j_ref[0], r, 0)),
                           pl.BlockSpec((3, tr, C), lambda r, j_ref: (0, r, 0))],
                 out_specs=pl.BlockSpec((tr, C), lambda r, j_ref: (r, 0)), out_shape=_sds((R, C), F32))(chip_idx, p, l2)


class _Reducer:
    PAIR_US = (12.0, 7.0)
    CHIP_US = (15.0, 43.0)

    def __init__(self, c_idx, chip_idx):
        self.c_idx, self.chip_idx = c_idx, chip_idx
        self.units, self.out, self.sc_free = [], {}, 0.0

    def _sequencer_done(self, cost):
        self.sc_free = max(self.sc_free, _SCHED.clock) + cost[0] + cost[1] * self._mb
        return self.sc_free

    def submit(self, name, g):
        R = g.shape[0] // N_DEV
        g8 = g.reshape((N_DEV, R, g.shape[1]))
        self._mb = R * g.shape[1] * g.dtype.itemsize / 1e6
        (l1,) = _pair_exchange_sc([g8], name=f"rs_pair_{name}")
        self.units.append(dict(name=name, g8=g8, l1=l1, mb=self._mb, stage=1, ready=self._sequencer_done(self.PAIR_US)))

    def advance(self, force=False):
        for u in self.units:
            if not (force or _SCHED.clock >= u["ready"]):
                continue
            if u["stage"] == 1:
                p = _pair_sum(u.pop("g8"), u.pop("l1"), self.c_idx, name=f"rs_psum_{u['name']}")
                (l2,) = _chip_exchange_sc([p], name=f"rs_chip_{u['name']}")
                self._mb = u["mb"]
                u.update(p=p, l2=l2, stage=2, ready=self._sequencer_done(self.CHIP_US))
            elif u["stage"] == 2:
                self.out[u["name"]] = _chip_sum(u.pop("p"), u.pop("l2"), self.chip_idx, name=f"rs_csum_{u['name']}")
                u["stage"] = 3

    def finish(self):
        self.advance(force=True)
        self.advance(force=True)
        return self.out


def _grad_ready(name, g):
    _SCHED.big[name] = g
    if _SCHED.reducer is not None:
        _SCHED.reducer.submit(name, g)
    return g


def _sum8(a, *, name):
    _, R, C = a.shape
    tr = _row_tile(R, C, budget=1 << 18, mult=8)

    def body(a_ref, o_ref):
        acc = a_ref[0]
        for k in range(1, N_DEV):
            acc = acc + a_ref[k]
        o_ref[...] = acc

    return _call(body, name=name, grid=(R // tr,), in_specs=[pl.BlockSpec((N_DEV, tr, C), lambda r: (0, r, 0))],
                 out_specs=pl.BlockSpec((tr, C), lambda r: (r, 0)), out_shape=_sds((R, C), F32), dims=("parallel",))(a)


def _adamw(w, g, m, v, *, name):
    R, C = w.shape
    tr = _row_tile(R, C, budget=1 << 20, mult=8)
    c1 = 1.0 - ADAM_B1 ** ADAM_STEP
    c2 = 1.0 - ADAM_B2 ** ADAM_STEP

    def body(w_ref, g_ref, m_ref, v_ref, d_ref, mo_ref, vo_ref):
        gv = g_ref[...]
        mn = ADAM_B1 * m_ref[...] + (1.0 - ADAM_B1) * gv
        vn = ADAM_B2 * v_ref[...] + (1.0 - ADAM_B2) * (gv * gv)
        mo_ref[...] = mn
        vo_ref[...] = vn
        d_ref[...] = -ADAM_LR * ((mn / c1) / (jnp.sqrt(vn / c2) + ADAM_EPS) + ADAM_WD * w_ref[...])

    blk = pl.BlockSpec((tr, C), lambda r: (r, 0))
    return _call(body, name=name, grid=(R // tr,), in_specs=[blk] * 4, out_specs=[blk] * 3,
                 out_shape=[_sds((R, C), F32)] * 3, dims=("parallel",))(w, g, m, v)


def _gmlp_fwd(xb, W, sp, tag):
    zp = _mm(xb, W["w_in_t"], "nt", name=f"gm_zp_{tag}", out_dtype=BF16)
    p, s, vhat, rstd = _gm_mid_fwd(zp, sp["ln_g"], sp["ln_b"], sp["w_s"], sp["b_st"], name=f"gm_mid_{tag}")
    m = _mm(p, W["w_out"], "nn", name=f"gm_out_{tag}", out_dtype=F32, tm=512, tk=p.shape[1])
    return m, dict(xb=xb, zp=zp, p=p, s=s, vhat=vhat, rstd=rstd)


def _gmlp_bwd(drb, dr, W, sp, sv, tag, on_small=None):
    _grad_ready(f"{tag}_w_out", _mm(sv["p"], drb, "tn", name=f"gm_dwout_{tag}", out_dtype=BF16, tk=TOKENS_K))
    dp = _mm(drb, W["w_out"], "nt", name=f"gm_dp_{tag}", out_dtype=BF16)
    dzp, dws, dbs, dlg, dlb = _gm_mid_bwd(dp, sv["zp"], sv["s"], sv["vhat"], sv["rstd"], sp["ln_g"], sp["ln_b"], sp["w_s"],
                                          name=f"gm_midb_{tag}")
    small = dict(w_s=dws, b_s=dbs, ln_g=dlg, ln_b=dlb)
    if on_small is not None:
        on_small(small)
    _grad_ready(f"{tag}_w_in_t", _mm(dzp, sv["xb"], "tn", name=f"gm_dwin_{tag}", out_dtype=BF16, tk=TOKENS_K))
    dx = _mm(dzp, W["w_in_t"], "nn", name=f"gm_dx_{tag}", out_dtype=F32, res=dr, res_scale=ALPHA)
    return dx, small


def _mla_fwd(xb, W, sp, rope, tag):
    cos, sin_s, _ = rope
    h = _mm(xb, W["w_in"], "nn", name=f"mla_h_{tag}", out_dtype=F32, tn=W["w_in"].shape[1])
    cqn, ckvn, kr2, rq, rkv = _mla_prep(h, sp["gq"], sp["gkv"], cos, sin_s, name=f"mla_prep_{tag}")
    q = _mm(cqn, W["w_qb_t"], "nt", name=f"mla_q_{tag}", out_dtype=F32)
    qb = _rope_q(q, cos, sin_s, name=f"mla_ropeq_{tag}")
    kv = _mm(ckvn, W["w_kvb_t"], "nt", name=f"mla_kv_{tag}", out_dtype=BF16)
    o, lse = _attn_fwd(qb, kv, kr2, name=f"mla_attn_{tag}")
    m = _mm(o, W["w_out"], "nn", name=f"mla_out_{tag}", out_dtype=F32)
    return m, dict(xb=xb, h=h, cqn=cqn, ckvn=ckvn, kr2=kr2, rq=rq, rkv=rkv, qb=qb, kv=kv, o=o, lse=lse)


def _mla_bwd(drb, dr, W, sp, rope, sv, tag):
    cos, _, sin_neg = rope
    _grad_ready(f"{tag}_w_out", _mm(sv["o"], drb, "tn", name=f"mla_dwout_{tag}", out_dtype=BF16, tk=TOKENS_K))
    do = _mm(drb, W["w_out"], "nt", name=f"mla_do_{tag}", out_dtype=BF16)
    dq, dkv, dkr2 = _attn_bwd(sv["qb"], sv["kv"], sv["kr2"], sv["o"], do, sv["lse"], name=f"mla_attnb_{tag}")
    dqb = _rope_q(dq, cos, sin_neg, name=f"mla_ropedq_{tag}")
    _grad_ready(f"{tag}_w_qb_t", _mm(dqb, sv["cqn"], "tn", name=f"mla_dwqb_{tag}", out_dtype=BF16, tk=TOKENS_K))
    _grad_ready(f"{tag}_w_kvb_t", _mm(dkv, sv["ckvn"], "tn", name=f"mla_dwkvb_{tag}", out_dtype=BF16, tk=TOKENS_K))
    dcqn = _mm(dqb, W["w_qb_t"], "nn", name=f"mla_dcq_{tag}", out_dtype=F32, tk=dqb.shape[1])
    dckvn = _mm(dkv, W["w_kvb_t"], "nn", name=f"mla_dckv_{tag}", out_dtype=F32, tk=dkv.shape[1])
    dh, dgq, dgkv = _mla_prep_bwd(dcqn, dckvn, dkr2, sv["h"], sv["rq"], sv["rkv"], sp["gq"], sp["gkv"], cos, sin_neg,
                                  name=f"mla_prepb_{tag}")
    _grad_ready(f"{tag}_w_in", _mm(sv["xb"], dh, "tn", name=f"mla_dwin_{tag}", out_dtype=BF16, tn=dh.shape[1], tk=TOKENS_K))
    dx = _mm(dh, W["w_in"], "nt", name=f"mla_dx_{tag}", out_dtype=F32, tk=dh.shape[1], res=dr, res_scale=ALPHA)
    return dx, dict(gq=dgq, gkv=dgkv)


def _ffn_fwd(xb, W, sp, tag):
    S = xb.shape[0]
    F = W["w_down"].shape[0]
    h = _mm(xb, W["w_up_t"], "nt", name=f"ffn_h_{tag}", out_dtype=BF16)
    act = _ffn_mid_fwd(h, sp["cw"], sp["cb"], name=f"ffn_mid_{tag}")
    f = _mm(act, W["w_down"], "nn", name=f"ffn_out_{tag}", out_dtype=F32, tk=F // 2)
    return f, dict(xb=xb, h=h, act=act)


def _ffn_bwd(drb, dr, W, sp, sv, tag):
    F = W["w_down"].shape[0]
    S, D = drb.shape
    tf = _tile(F, FFN_TILE)
    _grad_ready(f"{tag}_w_down", _mm(sv["act"], drb, "tn", name=f"ffn_dwdown_{tag}", out_dtype=BF16, tm=tf, tn=512, tk=TOKENS_K))
    dact = _mm(drb, W["w_down"], "nt", name=f"ffn_dact_{tag}", out_dtype=BF16, tm=512, tn=F, tk=D, b_resident=True)
    dh, dcw, dcb = _ffn_mid_bwd(dact, sv["h"], sp["cw"], sp["cb"], name=f"ffn_midb_{tag}")
    dh = dh.reshape(2 * S, F)
    _grad_ready(f"{tag}_w_up_t", _mm(dh, sv["xb"], "tn", name=f"ffn_dwup_{tag}", out_dtype=BF16, tm=tf, tn=512, tk=TOKENS_K, a_parts=2))
    dx = _mm(dh, W["w_up_t"], "nn", name=f"ffn_dx_{tag}", out_dtype=F32, res=dr, res_scale=ALPHA, a_parts=2, tk=F // 2)
    dcw = jnp.concatenate([dcw[0], dcw[1]], axis=1)
    dcb = jnp.concatenate([dcb[0], dcb[1]], axis=1)
    return dx, dict(cw=dcw, cb=dcb)


def _rope_tables(S):
    half = ROPE // 2
    inv_freq = ROPE_THETA ** (-jnp.arange(half, dtype=F32) / half)
    ang = jnp.arange(S, dtype=F32)[:, None] * inv_freq[None, :]
    cos, sin = jnp.cos(ang), jnp.sin(ang)
    cos128 = jnp.concatenate([cos] * 4, axis=1)
    sin128 = jnp.concatenate([-sin, sin, -sin, sin], axis=1)
    return cos128, sin128, -sin128


def _fwd_bwd(x, tgt, Wm, Wf, spm, spf, ln, on_small=None):
    S, D = x.shape
    _SCHED.last, _SCHED.clock = None, 0.0
    rope = _rope_tables(S)
    xf, xb = x, x.astype(BF16)
    saved = []
    for i in range(DEPTH):
        if i % 2 == 0:
            m, svm = _gmlp_fwd(xb, Wm[i], spm[i], f"l{i}")
        else:
            m, svm = _mla_fwd(xb, Wm[i], spm[i], rope, f"l{i}")
        y, yb, xh1, rs1 = _ln_fwd(xf, m, ln["mix_g"][i], ln["mix_b"][i], name=f"ln_mix_{i}")
        f, svf = _ffn_fwd(yb, Wf[i], spf[i], f"l{i}")
        z, zb, xh2, rs2 = _ln_fwd(y, f, ln["ffn_g"][i], ln["ffn_b"][i], name=f"ln_ffn_{i}")
        saved.append((svm, xh1, rs1, svf, xh2, rs2))
        xf, xb = z, zb
    lp, dy = _loss_kernel(xf, tgt, name="loss")
    gm_small, gf_small, gln = [None] * DEPTH, [None] * DEPTH, [None] * DEPTH
    _SCHED.big = {}

    for i in reversed(range(DEPTH)):
        svm, xh1, rs1, svf, xh2, rs2 = saved[i]
        dr, drb, dg2, db2 = _ln_bwd(dy, xh2, rs2, ln["ffn_g"][i], name=f"lnb_ffn_{i}")
        dy, gf_small[i] = _ffn_bwd(drb, dr, Wf[i], spf[i], svf, f"l{i}")
        dr, drb, dg1, db1 = _ln_bwd(dy, xh1, rs1, ln["mix_g"][i], name=f"lnb_mix_{i}")
        gln[i] = dict(mix_g=dg1, mix_b=db1, ffn_g=dg2, ffn_b=db2)
        if i % 2 == 0:
            hook = None if (i > 0 or on_small is None) else (lambda ms: on_small(lp, [ms] + gm_small[1:], gf_small, gln))
            dy, gm_small[i] = _gmlp_bwd(drb, dr, Wm[i], spm[i], svm, f"l{i}", hook)
        else:
            dy, gm_small[i] = _mla_bwd(drb, dr, Wm[i], spm[i], rope, svm, f"l{i}")
    return lp, dy, _SCHED.big, gm_small, gf_small, gln


def _perm_q_rows(wt):
    hd = NOPE + ROPE
    return jnp.concatenate([wt[0:NOPE], wt[hd:hd + NOPE], wt[NOPE:hd], wt[hd + NOPE:2 * hd]], axis=0)


def _unperm_q_rows(wt):
    return jnp.concatenate([wt[0:NOPE], wt[2 * NOPE:2 * NOPE + ROPE], wt[NOPE:2 * NOPE], wt[2 * NOPE + ROPE:]], axis=0)


def _pad_cols(w, to):
    return jnp.pad(w, ((0, 0), (0, to - w.shape[1])))


def _pad_pieces(a, axis, piece, piece_pad):
    n = a.shape[axis] // piece
    sh = a.shape[:axis] + (n, piece) + a.shape[axis + 1:]
    pad = [(0, 0)] * (len(sh))
    pad[axis + 1] = (0, piece_pad - piece)
    return jnp.pad(a.reshape(sh), pad).reshape(a.shape[:axis] + (n * piece_pad,) + a.shape[axis + 1:])


def _unpad_pieces(a, axis, piece, piece_pad):
    n = a.shape[axis] // piece_pad
    sh = a.shape[:axis] + (n, piece_pad) + a.shape[axis + 1:]
    return lax.slice_in_dim(a.reshape(sh), 0, piece, axis=axis + 1).reshape(a.shape[:axis] + (n * piece,) + a.shape[axis + 1:])


def _pack(parts, rows_mult=8):
    flat = jnp.concatenate([p.reshape(-1).astype(F32) for p in parts])
    n = flat.shape[0]
    per = LANES * rows_mult
    tot = ((n + per - 1) // per) * per
    return jnp.pad(flat, (0, tot - n)).reshape(tot // LANES, LANES)


def _unpack(buf, shapes):
    flat = buf.reshape(-1)
    out, off = [], 0
    for s in shapes:
        n = 1
        for d in s:
            n *= d
        out.append(flat[off:off + n].reshape(s))
        off += n
    return out


def kernel(x, gm_w_in, gm_ln_g, gm_ln_b, gm_w_s, gm_b_s, gm_w_out, mla_w_in, mla_q_norm_g, mla_kv_norm_g, mla_w_q_b, mla_w_kv_b, mla_w_out, ffn_w_up, ffn_conv_w, ffn_conv_b, ffn_w_down, ln_mix_g, ln_mix_b, ln_ffn_g, ln_ffn_b, loss_target, m_gm_w_in, m_gm_ln_g, m_gm_ln_b, m_gm_w_s, m_gm_b_s, m_gm_w_out, m_mla_w_in, m_mla_q_norm_g, m_mla_kv_norm_g, m_mla_w_q_b, m_mla_w_kv_b, m_mla_w_out, m_ffn_w_up, m_ffn_conv_w, m_ffn_conv_b, m_ffn_w_down, m_ln_mix_g, m_ln_mix_b, m_ln_ffn_g, m_ln_ffn_b, v_gm_w_in, v_gm_ln_g, v_gm_ln_b, v_gm_w_s, v_gm_b_s, v_gm_w_out, v_mla_w_in, v_mla_q_norm_g, v_mla_kv_norm_g, v_mla_w_q_b, v_mla_w_kv_b, v_mla_w_out, v_ffn_w_up, v_ffn_conv_w, v_ffn_conv_b, v_ffn_w_down, v_ln_mix_g, v_ln_mix_b, v_ln_ffn_g, v_ln_ffn_b):
    S, D = x.shape[1], x.shape[2]
    xi, yi, ci = _coords()
    dev = 4 * xi + 2 * yi + ci
    c_idx = jnp.reshape(ci, (1,)).astype(jnp.int32)
    chip_idx = jnp.reshape(2 * xi + yi, (1,)).astype(jnp.int32)
    w_in_cols = mla_w_in.shape[2]
    w_in_pad = ((w_in_cols + LANES - 1) // LANES) * LANES
    n_gm, n_mla = gm_w_in.shape[0], mla_w_in.shape[0]

    cw_l = ffn_conv_w.shape[2]
    small_in = _pack([mla_q_norm_g, mla_kv_norm_g, ffn_conv_w])
    r_down = ffn_w_down.shape[1]
    r_pad = ((r_down + FFN_SHARD_MULT - 1) // FFN_SHARD_MULT) * FFN_SHARD_MULT

    def gather(shards, name):
        full = _all_gather_sc(list(shards.values()), name=name)
        return {k: f.reshape((-1, f.shape[2])) for k, f in zip(shards.keys(), full)}

    Wm, Wf = [None] * DEPTH, [None] * DEPTH
    for i in range(DEPTH):
        s = i // 2
        if i == 0:
            Wm[i] = gather(dict(w_in_t=gm_w_in[s].T.astype(BF16)), "ag_first")
            Wm[i].update(gather(dict(w_out=gm_w_out[s].astype(BF16), small=small_in), f"ag_gm_{i}_rest"))
            small_all = Wm[i].pop("small").reshape((N_DEV,) + small_in.shape)
        elif i % 2 == 0:
            Wm[i] = gather(dict(w_in_t=gm_w_in[s].T.astype(BF16), w_out=gm_w_out[s].astype(BF16)), f"ag_gm_{i}")
        else:
            Wm[i] = gather(dict(w_in=_pad_cols(mla_w_in[s], w_in_pad).astype(BF16),
                                w_qb_t=_perm_q_rows(mla_w_q_b[s].T).astype(BF16),
                                w_kvb_t=mla_w_kv_b[s].T.astype(BF16), w_out=mla_w_out[s].astype(BF16)), f"ag_mla_{i}")
        Wf[i] = gather(dict(w_up_t=_pad_pieces(ffn_w_up[i].T, 0, r_down, r_pad).astype(BF16),
                            w_down=_pad_pieces(ffn_w_down[i], 0, r_down, r_pad).astype(BF16)), f"ag_ffn_{i}")

    gq_parts, gkv_parts, cw_parts = [], [], []
    for k in range(N_DEV):
        a, b, c_ = _unpack(small_all[k], [mla_q_norm_g.shape, mla_kv_norm_g.shape, ffn_conv_w.shape])
        gq_parts.append(a)
        gkv_parts.append(b)
        cw_parts.append(c_)
    gq_full = jnp.concatenate(gq_parts, axis=1)
    gkv_full = jnp.concatenate(gkv_parts, axis=1)
    cw_full = jnp.concatenate(cw_parts, axis=2)

    spm, spf = [None] * DEPTH, [None] * DEPTH
    for i in range(DEPTH):
        s = i // 2
        if i % 2 == 0:
            spm[i] = dict(ln_g=gm_ln_g[s][None], ln_b=gm_ln_b[s][None], w_s=gm_w_s[s], b_st=gm_b_s[s].T)
        else:
            spm[i] = dict(gq=gq_full[s][None], gkv=gkv_full[s][None])
        spf[i] = dict(cw=_pad_pieces(cw_full[i], 1, r_down, r_pad), cb=_pad_pieces(ffn_conv_b[i][None], 1, r_down, r_pad))
    ln = dict(mix_g=ln_mix_g[:, None], mix_b=ln_mix_b[:, None], ffn_g=ln_ffn_g[:, None], ffn_b=ln_ffn_b[:, None])

    small_box = {}

    def on_small(lp, gm_small, gf_small, gln):
        small_g = [
            jnp.stack([gm_small[2 * s]["ln_g"][0] for s in range(n_gm)]),
            jnp.stack([gm_small[2 * s]["ln_b"][0] for s in range(n_gm)]),
            jnp.stack([gm_small[2 * s]["w_s"] for s in range(n_gm)]),
            jnp.stack([gm_small[2 * s]["b_s"][:, :, 0] for s in range(n_gm)]),
            jnp.stack([_unpad_pieces(gf_small[i]["cb"], 1, r_down, r_pad)[0] for i in range(DEPTH)]),
            jnp.stack([gln[i]["mix_g"][0] for i in range(DEPTH)]),
            jnp.stack([gln[i]["mix_b"][0] for i in range(DEPTH)]),
            jnp.stack([gln[i]["ffn_g"][0] for i in range(DEPTH)]),
            jnp.stack([gln[i]["ffn_b"][0] for i in range(DEPTH)]),
            jnp.stack([gm_small[2 * s + 1]["gq"][0] for s in range(n_mla)]),
            jnp.stack([gm_small[2 * s + 1]["gkv"][0] for s in range(n_mla)]),
            jnp.stack([_unpad_pieces(gf_small[i]["cw"], 1, r_down, r_pad) for i in range(DEPTH)]),
        ]
        small_box["shapes"] = [g.shape for g in small_g]
        (small_box["all"],) = _all_gather_sc([_pack(small_g)], name="ag_small_grads")

    _SCHED.reducer = _Reducer(c_idx, chip_idx)
    lp, grad_x, _, _, _, _ = _fwd_bwd(x[0], loss_target[0], Wm, Wf, spm, spf, ln, on_small)
    red = _SCHED.reducer.finish()
    _SCHED.reducer = None
    rm = [{k: red[f"l{i}_{k}"] for k in Wm[i]} for i in range(DEPTH)]
    rf = [{k: red[f"l{i}_{k}"] for k in Wf[i]} for i in range(DEPTH)]

    g_gm_w_in = jnp.stack([rm[2 * s]["w_in_t"].T for s in range(n_gm)])
    g_gm_w_out = jnp.stack([rm[2 * s]["w_out"] for s in range(n_gm)])
    g_mla_w_in = jnp.stack([rm[2 * s + 1]["w_in"][:, :w_in_cols] for s in range(n_mla)])
    g_mla_w_q_b = jnp.stack([_unperm_q_rows(rm[2 * s + 1]["w_qb_t"]).T for s in range(n_mla)])
    g_mla_w_kv_b = jnp.stack([rm[2 * s + 1]["w_kvb_t"].T for s in range(n_mla)])
    g_mla_w_out = jnp.stack([rm[2 * s + 1]["w_out"] for s in range(n_mla)])
    g_ffn_w_up = jnp.stack([_unpad_pieces(rf[i]["w_up_t"], 0, r_down, r_pad).T for i in range(DEPTH)])
    g_ffn_w_down = jnp.stack([_unpad_pieces(rf[i]["w_down"], 0, r_down, r_pad) for i in range(DEPTH)])

    small_sum = _unpack(_sum8(small_box["all"], name="small_grad_sum"), small_box["shapes"])
    (g_gm_ln_g, g_gm_ln_b, g_gm_w_s, g_gm_b_s, g_ffn_conv_b, g_ln_mix_g, g_ln_mix_b, g_ln_ffn_g, g_ln_ffn_b,
     gq_all, gkv_all, cw_all) = small_sum
    (loss_all,) = _all_gather([_pack([jnp.reshape(0.5 * jnp.sum(lp) / D, (1,))])], name="ag_loss")
    loss = jnp.reshape(_sum8(loss_all, name="loss_sum")[0, 0], ())
    qn_l = mla_q_norm_g.shape[1]
    g_mla_q_norm_g = lax.dynamic_slice_in_dim(gq_all, dev * qn_l, qn_l, axis=1)
    g_mla_kv_norm_g = lax.dynamic_slice_in_dim(gkv_all, dev * qn_l, qn_l, axis=1)
    g_ffn_conv_w = lax.dynamic_slice_in_dim(cw_all, dev * cw_l, cw_l, axis=2)

    def adam_big(w, g, m, v, tag):
        sh = w.shape
        two = lambda a: a.reshape((-1, sh[-1]))
        d, mn, vn = _adamw(two(w), two(g), two(m), two(v), name=f"adamw_{tag}")
        return d.reshape(sh), mn.reshape(sh), vn.reshape(sh)

    big = [("gm_w_in", gm_w_in, g_gm_w_in, m_gm_w_in, v_gm_w_in), ("gm_w_out", gm_w_out, g_gm_w_out, m_gm_w_out, v_gm_w_out),
           ("mla_w_in", mla_w_in, g_mla_w_in, m_mla_w_in, v_mla_w_in), ("mla_w_q_b", mla_w_q_b, g_mla_w_q_b, m_mla_w_q_b, v_mla_w_q_b),
           ("mla_w_kv_b", mla_w_kv_b, g_mla_w_kv_b, m_mla_w_kv_b, v_mla_w_kv_b), ("mla_w_out", mla_w_out, g_mla_w_out, m_mla_w_out, v_mla_w_out),
           ("ffn_w_up", ffn_w_up, g_ffn_w_up, m_ffn_w_up, v_ffn_w_up), ("ffn_w_down", ffn_w_down, g_ffn_w_down, m_ffn_w_down, v_ffn_w_down)]
    res = {}
    for tag, w, g, m, v in big:
        res[tag] = (g,) + adam_big(w, g, m, v, tag)

    small = [("gm_ln_g", gm_ln_g, g_gm_ln_g, m_gm_ln_g, v_gm_ln_g), ("gm_ln_b", gm_ln_b, g_gm_ln_b, m_gm_ln_b, v_gm_ln_b),
             ("gm_w_s", gm_w_s, g_gm_w_s, m_gm_w_s, v_gm_w_s), ("gm_b_s", gm_b_s, g_gm_b_s, m_gm_b_s, v_gm_b_s),
             ("mla_q_norm_g", mla_q_norm_g, g_mla_q_norm_g, m_mla_q_norm_g, v_mla_q_norm_g),
             ("mla_kv_norm_g", mla_kv_norm_g, g_mla_kv_norm_g, m_mla_kv_norm_g, v_mla_kv_norm_g),
             ("ffn_conv_w", ffn_conv_w, g_ffn_conv_w, m_ffn_conv_w, v_ffn_conv_w), ("ffn_conv_b", ffn_conv_b, g_ffn_conv_b, m_ffn_conv_b, v_ffn_conv_b),
             ("ln_mix_g", ln_mix_g, g_ln_mix_g, m_ln_mix_g, v_ln_mix_g), ("ln_mix_b", ln_mix_b, g_ln_mix_b, m_ln_mix_b, v_ln_mix_b),
             ("ln_ffn_g", ln_ffn_g, g_ln_ffn_g, m_ln_ffn_g, v_ln_ffn_g), ("ln_ffn_b", ln_ffn_b, g_ln_ffn_b, m_ln_ffn_b, v_ln_ffn_b)]
    shapes = [t[1].shape for t in small]
    d_s, m_s, v_s = _adamw(_pack([t[1] for t in small]), _pack([t[2] for t in small]), _pack([t[3] for t in small]),
                           _pack([t[4] for t in small]), name="adamw_small")
    d_l, m_l, v_l = _unpack(d_s, shapes), _unpack(m_s, shapes), _unpack(v_s, shapes)
    for (tag, _, g, _, _), d, mn, vn in zip(small, d_l, m_l, v_l):
        res[tag] = (g, d, mn, vn)

    order = ["gm_w_in", "gm_ln_g", "gm_ln_b", "gm_w_s", "gm_b_s", "gm_w_out", "mla_w_in", "mla_q_norm_g", "mla_kv_norm_g",
             "mla_w_q_b", "mla_w_kv_b", "mla_w_out", "ffn_w_up", "ffn_conv_w", "ffn_conv_b", "ffn_w_down",
             "ln_mix_g", "ln_mix_b", "ln_ffn_g", "ln_ffn_b"]
    out = [loss, grad_x[None]]
    for q in range(4):
        out += [res[k][q] for k in order]
    return tuple(out)
```

```python
import jax
import jax.numpy as jnp
from jax import lax
from jax.experimental import pallas as pl
from jax.experimental.pallas import tpu as pltpu
from jax.experimental.pallas import tpu_sc as plsc

F32, BF16 = jnp.float32, jnp.bfloat16

DEPTH = 4
CHUNK = 64
GM_BLOCK = 128
GM_GROUPS = 8
HEADS = 16
NOPE, ROPE, VDIM = 128, 64, 128
QRANK, KVRANK = 512, 512
ROPE_THETA = 10000.0
SM_SCALE = (NOPE + ROPE) ** -0.5
ALPHA = (2 * DEPTH) ** 0.25
LN_EPS = 1e-5
RMS_EPS = 1e-6
ADAM_LR, ADAM_B1, ADAM_B2, ADAM_EPS, ADAM_WD, ADAM_STEP = 0.001, 0.9, 0.999, 1e-08, 0.01, 10

N_DEV = 8
LANES = 128
SUBLANES = 8
VMEM_LIMIT = 56 * 1024 * 1024
MESH = pl.DeviceIdType.MESH
HBM_SPEC = pl.BlockSpec(memory_space=pltpu.HBM)


class _Schedule:
    def __init__(self):
        self.last = None
        self.reducer = None
        self.big = {}
        self.clock = 0.0

    def tick(self, us):
        self.clock += us
        if self.reducer is not None:
            self.reducer.advance()


_SCHED = _Schedule()


def _ticked(out, us):
    _SCHED.tick(us)
    return out


def _call(body, *, name, out_shape, in_specs, out_specs, grid=None, scratch=(), dims=None, n_prefetch=0):
    kw = dict(vmem_limit_bytes=VMEM_LIMIT)
    if dims is not None:
        kw["dimension_semantics"] = dims
    cp = pltpu.CompilerParams(**kw)
    in_specs = list(in_specs)
    token = _SCHED.last
    if token is not None:
        pos = n_prefetch + len(in_specs)
        in_specs.append(pl.BlockSpec(memory_space=pl.ANY))
        inner = body

        def body(*refs):
            return inner(*refs[:pos], *refs[pos + 1:])

    if n_prefetch:
        gs = pltpu.PrefetchScalarGridSpec(num_scalar_prefetch=n_prefetch, grid=grid, in_specs=in_specs, out_specs=out_specs,
                                          scratch_shapes=list(scratch))
        fn = pl.pallas_call(body, name=name, grid_spec=gs, out_shape=out_shape, compiler_params=cp, interpret=False)
    else:
        extra = {} if grid is None else {"grid": grid}
        fn = pl.pallas_call(body, name=name, in_specs=in_specs, out_specs=out_specs, out_shape=out_shape,
                            scratch_shapes=list(scratch), compiler_params=cp, interpret=False, **extra)

    def run(*args):
        out = fn(*args, token) if token is not None else fn(*args)
        _SCHED.last = out[0] if isinstance(out, (list, tuple)) else out
        return out

    return run


def _tile(n, pref, mult=LANES):
    if n <= pref:
        return n
    t = (pref // mult) * mult
    while t >= mult:
        if n % t == 0:
            return t
        t -= mult
    return n


def _sds(shape, dtype):
    return jax.ShapeDtypeStruct(tuple(shape), dtype)


FFN_TILE = 1408
FFN_CHUNK = 64
FFN_SHARD_MULT = 64
MXU_FLOPS_PER_US = 8e8
TOKENS_K = 4096
_DN = {"nn": (((1,), (0,)), ((), ())), "nt": (((1,), (1,)), ((), ())), "tn": (((0,), (0,)), ((), ()))}


def _mm(a, b, mode, *, name, out_dtype, tm=1024, tn=1024, tk=2048, res=None, res_scale=1.0, a_parts=1, b_resident=False):
    if a_parts > 1:
        S_, F_ = a.shape[0] // a_parts, a.shape[1]
        a_shape = (S_, a_parts * F_)
    else:
        a_shape = a.shape
    if mode == "nn":
        (M, K), (K2, N) = a_shape, b.shape
    elif mode == "nt":
        (M, K), (N, K2) = a_shape, b.shape
    else:
        (K, M), (K2, N) = a_shape, b.shape
    assert K == K2, (name, a.shape, b.shape)
    tm, tn, tk = _tile(M, tm), _tile(N, tn), _tile(K, tk)
    nk = K // tk
    if a_parts > 1 and mode == "nn":
        per = F_ // tk
        assert F_ % tk == 0
        a_spec = pl.BlockSpec((tm, tk), lambda i, j, k: ((k // per) * (S_ // tm) + i, k % per))
    elif a_parts > 1:
        per = F_ // tm
        assert mode == "tn" and F_ % tm == 0
        a_spec = pl.BlockSpec((tk, tm), lambda i, j, k: ((i // per) * (S_ // tk) + k, i % per))
    elif mode == "tn":
        a_spec = pl.BlockSpec((tk, tm), lambda i, j, k: (k, i))
    else:
        a_spec = pl.BlockSpec((tm, tk), lambda i, j, k: (i, k))
    b_kw = {"pipeline_mode": pl.Buffered(1)} if b_resident else {}
    b_spec = (pl.BlockSpec((tn, tk), lambda i, j, k: (j, k), **b_kw) if mode == "nt"
              else pl.BlockSpec((tk, tn), lambda i, j, k: (k, j), **b_kw))
    in_specs = [a_spec, b_spec]
    args = [a, b]
    if res is not None:
        in_specs.append(pl.BlockSpec((tm, tn), lambda i, j, k: (i, j)))
        args.append(res)
    dn = _DN[mode]
    has_res = res is not None

    def body(*refs):
        a_ref, b_ref = refs[0], refs[1]
        r_ref = refs[2] if has_res else None
        o_ref = refs[2 + has_res]
        part = lax.dot_general(a_ref[...], b_ref[...], dn, preferred_element_type=F32)

        def finish(acc):
            if has_res:
                acc = acc + res_scale * r_ref[...]
            o_ref[...] = acc.astype(o_ref.dtype)

        if nk == 1:
            finish(part)
        else:
            acc_ref = refs[3 + has_res]
            k = pl.program_id(2)

            @pl.when(k == 0)
            def _():
                acc_ref[...] = part

            @pl.when(k > 0)
            def _():
                acc_ref[...] += part

            @pl.when(k == nk - 1)
            def _():
                finish(acc_ref[...])

    scratch = [pltpu.VMEM((tm, tn), F32)] if nk > 1 else []
    out = _call(body, name=name, grid=(M // tm, N // tn, nk), in_specs=in_specs,
                out_specs=pl.BlockSpec((tm, tn), lambda i, j, k: (i, j)), out_shape=_sds((M, N), out_dtype),
                scratch=scratch, dims=("parallel", "parallel", "arbitrary"))(*args)
    _SCHED.tick(2.0 * M * N * K / MXU_FLOPS_PER_US)
    return out


def _ln_fwd(x, m, g, b, *, name):
    S, D = x.shape
    tr = _tile(S, 256, 8)

    def body(x_ref, m_ref, g_ref, b_ref, y_ref, yb_ref, xh_ref, rs_ref):
        r = ALPHA * x_ref[...] + m_ref[...]
        mu = jnp.mean(r, axis=-1, keepdims=True)
        d = r - mu
        var = jnp.mean(d * d, axis=-1, keepdims=True)
        rstd = lax.rsqrt(var + LN_EPS)
        xh = d * rstd
        y = xh * g_ref[...] + b_ref[...]
        y_ref[...] = y
        yb_ref[...] = y.astype(BF16)
        xh_ref[...] = xh
        rs_ref[...] = rstd

    row = pl.BlockSpec((tr, D), lambda i: (i, 0))
    vec = pl.BlockSpec((1, D), lambda i: (0, 0))
    return _call(body, name=name, grid=(S // tr,), in_specs=[row, row, vec, vec],
                 out_specs=[row, row, row, pl.BlockSpec((tr, 1), lambda i: (i, 0))],
                 out_shape=[_sds((S, D), F32), _sds((S, D), BF16), _sds((S, D), F32), _sds((S, 1), F32)],
                 dims=("parallel",))(x, m, g, b)


def _ln_bwd(dy, xh, rstd, g, *, name):
    S, D = dy.shape
    tr = _tile(S, 256, 8)

    def body(dy_ref, xh_ref, rs_ref, g_ref, dr_ref, drb_ref, dg_ref, db_ref):
        @pl.when(pl.program_id(0) == 0)
        def _():
            dg_ref[...] = jnp.zeros_like(dg_ref)
            db_ref[...] = jnp.zeros_like(db_ref)

        dyv = dy_ref[...]
        xhv = xh_ref[...]
        dxh = dyv * g_ref[...]
        m1 = jnp.mean(dxh, axis=-1, keepdims=True)
        m2 = jnp.mean(dxh * xhv, axis=-1, keepdims=True)
        dr = rs_ref[...] * (dxh - m1 - xhv * m2)
        dr_ref[...] = dr
        drb_ref[...] = dr.astype(BF16)
        dg_ref[...] += jnp.sum(dyv * xhv, axis=0, keepdims=True)
        db_ref[...] += jnp.sum(dyv, axis=0, keepdims=True)

    row = pl.BlockSpec((tr, D), lambda i: (i, 0))
    vec = pl.BlockSpec((1, D), lambda i: (0, 0))
    return _ticked(_call(body, name=name, grid=(S // tr,), in_specs=[row, row, pl.BlockSpec((tr, 1), lambda i: (i, 0)), vec],
                         out_specs=[row, row, vec, vec],
                         out_shape=[_sds((S, D), F32), _sds((S, D), BF16), _sds((1, D), F32), _sds((1, D), F32)],
                         dims=("arbitrary",))(dy, xh, rstd, g), 45.0 * S / 4096)


def _loss_kernel(y, t, *, name):
    S, D = y.shape
    tr = _tile(S, 256, 8)

    def body(y_ref, t_ref, lp_ref, dy_ref):
        @pl.when(pl.program_id(0) == 0)
        def _():
            lp_ref[...] = jnp.zeros_like(lp_ref)

        e = y_ref[...] - t_ref[...]
        dy_ref[...] = e / D
        lp_ref[...] += jnp.sum(e * e, axis=0, keepdims=True)

    row = pl.BlockSpec((tr, D), lambda i: (i, 0))
    vec = pl.BlockSpec((1, D), lambda i: (0, 0))
    return _call(body, name=name, grid=(S // tr,), in_specs=[row, row], out_specs=[vec, row],
                 out_shape=[_sds((1, D), F32), _sds((S, D), F32)], dims=("arbitrary",))(y, t)


_GELU_C = 0.7978845608028654
_GELU_A = 0.044715


def _gelu(x):
    return 0.5 * x * (1.0 + jnp.tanh(_GELU_C * (x + _GELU_A * x * x * x)))


def _gelu_grad(x):
    x2 = x * x
    t = jnp.tanh(_GELU_C * (x + _GELU_A * x * x2))
    return 0.5 * (1.0 + t) + 0.5 * x * (1.0 - t * t) * (_GELU_C * (1.0 + 3.0 * _GELU_A * x2))


def _gelu_both(x):
    x2 = x * x
    t = jnp.tanh(_GELU_C * (x + _GELU_A * x * x2))
    half = 0.5 * (1.0 + t)
    return x * half, half + 0.5 * x * (1.0 - t * t) * (_GELU_C * (1.0 + 3.0 * _GELU_A * x2))


def _masked_ws(w):
    i = lax.broadcasted_iota(jnp.int32, w.shape, 0) // CHUNK
    j = lax.broadcasted_iota(jnp.int32, w.shape, 1) // CHUNK
    return jnp.where(j <= i, w, 0.0)


def _gm_mid_fwd(zp, ln_g, ln_b, w_s, b_st, *, name):
    S, H2 = zp.shape
    H = H2 // 2
    gd = H // GM_GROUPS
    nb = S // GM_BLOCK

    def body(zu_ref, zv_ref, g_ref, b_ref, w_ref, bs_ref, p_ref, s_ref, vh_ref, rs_ref):
        v = _gelu(zv_ref[...].astype(F32))
        mu = jnp.mean(v, axis=-1, keepdims=True)
        d = v - mu
        var = jnp.mean(d * d, axis=-1, keepdims=True)
        rstd = lax.rsqrt(var + LN_EPS)
        vh = d * rstd
        vh_ref[...] = vh.astype(BF16)
        rs_ref[...] = rstd
        vn = (vh * g_ref[...] + b_ref[...]).astype(BF16)
        bs = bs_ref[...]
        for gi in range(GM_GROUPS):
            cs = slice(gi * gd, (gi + 1) * gd)
            wm = _masked_ws(w_ref[gi]).astype(BF16)
            s = jnp.dot(wm, vn[:, cs], preferred_element_type=F32) + bs[:, gi:gi + 1]
            u = _gelu(zu_ref[:, cs].astype(F32))
            s_ref[:, cs] = s.astype(BF16)
            p_ref[:, cs] = (u * s).astype(BF16)

    blk = lambda c: pl.BlockSpec((GM_BLOCK, H), lambda n, c=c: (n, c))
    vec = pl.BlockSpec((1, H), lambda n: (0, 0))
    return _call(body, name=name, grid=(nb,),
                 in_specs=[blk(0), blk(1), vec, vec, pl.BlockSpec((GM_GROUPS, GM_BLOCK, GM_BLOCK), lambda n: (0, 0, 0)),
                           pl.BlockSpec((GM_BLOCK, GM_GROUPS), lambda n: (0, 0))],
                 out_specs=[blk(0), blk(0), blk(0), pl.BlockSpec((GM_BLOCK, 1), lambda n: (n, 0))],
                 out_shape=[_sds((S, H), BF16), _sds((S, H), BF16), _sds((S, H), BF16), _sds((S, 1), F32)],
                 dims=("parallel",))(zp, zp, ln_g, ln_b, w_s, b_st)


def _gm_mid_bwd(dp, zp, s, vhat, rstd, ln_g, ln_b, w_s, *, name):
    S, H2 = zp.shape
    H = H2 // 2
    gd = H // GM_GROUPS
    nb = S // GM_BLOCK

    def body(dp_ref, zu_ref, zv_ref, s_ref, vh_ref, rs_ref, g_ref, b_ref, w_ref,
             dz_ref, dw_ref, dbs_ref, dg_ref, db_ref, dvh_ref):
        @pl.when(pl.program_id(0) == 0)
        def _():
            dw_ref[...] = jnp.zeros_like(dw_ref)
            dbs_ref[...] = jnp.zeros_like(dbs_ref)
            dg_ref[...] = jnp.zeros_like(dg_ref)
            db_ref[...] = jnp.zeros_like(db_ref)

        m1 = jnp.zeros((GM_BLOCK, 1), F32)
        m2 = jnp.zeros((GM_BLOCK, 1), F32)
        for gi in range(GM_GROUPS):
            cs = slice(gi * gd, (gi + 1) * gd)
            dpg = dp_ref[:, cs].astype(F32)
            zu = zu_ref[:, cs].astype(F32)
            u, du_dz = _gelu_both(zu)
            ds = dpg * u
            du = dpg * s_ref[:, cs].astype(F32)
            dz_ref[:, cs] = (du * du_dz).astype(BF16)
            dsb = ds.astype(BF16)
            vh = vh_ref[:, cs].astype(F32)
            lg = g_ref[:, cs]
            vn = (vh * lg + b_ref[:, cs]).astype(BF16)
            wm = _masked_ws(w_ref[gi]).astype(BF16)
            dvn = lax.dot_general(wm, dsb, _DN["tn"], preferred_element_type=F32)
            dw_ref[gi] += lax.dot_general(dsb, vn, _DN["nt"], preferred_element_type=F32)
            dbs_ref[gi] += jnp.sum(ds, axis=1, keepdims=True)
            dg_ref[:, cs] += jnp.sum(dvn * vh, axis=0, keepdims=True)
            db_ref[:, cs] += jnp.sum(dvn, axis=0, keepdims=True)
            dvh = dvn * lg
            dvh_ref[:, cs] = dvh
            m1 = m1 + jnp.sum(dvh, axis=1, keepdims=True)
            m2 = m2 + jnp.sum(dvh * vh, axis=1, keepdims=True)
        m1 = m1 / H
        m2 = m2 / H
        rs = rs_ref[...]
        for gi in range(GM_GROUPS):
            cs = slice(gi * gd, (gi + 1) * gd)
            vh = vh_ref[:, cs].astype(F32)
            dv = rs * (dvh_ref[:, cs] - m1 - vh * m2)
            zv = zv_ref[:, cs].astype(F32)
            dz_ref[:, H + gi * gd:H + (gi + 1) * gd] = (dv * _gelu_grad(zv)).astype(BF16)

        @pl.when(pl.program_id(0) == nb - 1)
        def _():
            for gi in range(GM_GROUPS):
                dw_ref[gi] = _masked_ws(dw_ref[gi])

    blk = lambda c: pl.BlockSpec((GM_BLOCK, H), lambda n, c=c: (n, c))
    vec = pl.BlockSpec((1, H), lambda n: (0, 0))
    wspec = pl.BlockSpec((GM_GROUPS, GM_BLOCK, GM_BLOCK), lambda n: (0, 0, 0))
    return _ticked(_call(body, name=name, grid=(nb,),
                 in_specs=[blk(0), blk(0), blk(1), blk(0), blk(0), pl.BlockSpec((GM_BLOCK, 1), lambda n: (n, 0)), vec, vec, wspec],
                 out_specs=[pl.BlockSpec((GM_BLOCK, H2), lambda n: (n, 0)), wspec,
                            pl.BlockSpec((GM_GROUPS, GM_BLOCK, 1), lambda n: (0, 0, 0)), vec, vec],
                 out_shape=[_sds((S, H2), BF16), _sds((GM_GROUPS, GM_BLOCK, GM_BLOCK), F32),
                            _sds((GM_GROUPS, GM_BLOCK, 1), F32), _sds((1, H), F32), _sds((1, H), F32)],
                 scratch=[pltpu.VMEM((GM_BLOCK, H), F32)], dims=("arbitrary",))(dp, zp, zp, s, vhat, rstd, ln_g, ln_b, w_s),
                   280.0 * S / 4096)


def _shift_down(x, k):
    r = pltpu.roll(x, k, 0)
    rows = lax.broadcasted_iota(jnp.int32, (SUBLANES, x.shape[1]), 0)
    return jnp.concatenate([jnp.where(rows >= k, r[:SUBLANES], 0.0), r[SUBLANES:]], axis=0)


def _shift_up(x, k):
    n = x.shape[0]
    r = pltpu.roll(x, n - k, 0)
    rows = lax.broadcasted_iota(jnp.int32, (SUBLANES, x.shape[1]), 0)
    return jnp.concatenate([r[:n - SUBLANES], jnp.where(rows < SUBLANES - k, r[n - SUBLANES:], 0.0)], axis=0)


def _conv(h, w, b):
    return w[0:1, :] * _shift_down(h, 2) + w[1:2, :] * _shift_down(h, 1) + w[2:3, :] * h + b


def _ffn_mid_fwd(h, cw, cb, *, name):
    S, F2 = h.shape
    F = F2 // 2
    nf = F // LANES

    def body(ha_ref, hg_ref, wa_ref, wg_ref, ba_ref, bg_ref, act_ref):
        a = _conv(ha_ref[...].astype(F32), wa_ref[...], ba_ref[...])
        g = _conv(hg_ref[...].astype(F32), wg_ref[...], bg_ref[...])
        act_ref[...] = (g * jax.nn.sigmoid(g) * a).astype(BF16)

    sl = lambda off, r: pl.BlockSpec((r, LANES), lambda j, off=off: (0, j + off))
    return _call(body, name=name, grid=(nf,),
                 in_specs=[sl(0, S), sl(nf, S), sl(0, 3), sl(nf, 3), sl(0, 1), sl(nf, 1)],
                 out_specs=sl(0, S), out_shape=_sds((S, F), BF16), dims=("parallel",))(h, h, cw, cw, cb, cb)


def _ffn_mid_bwd(dact, h, cw, cb, *, name):
    S, F2 = h.shape
    F = F2 // 2
    nf = F // LANES

    R, HB, HF = FFN_CHUNK, 16, SUBLANES
    nc = S // R

    def body(da_ref, ha_ref, hg_ref, wa_ref, wg_ref, ba_ref, bg_ref, dh_ref, dw_ref, db_ref, a_s, g_s, d_s):
        wa, wg, ba, bg = wa_ref[...], wg_ref[...], ba_ref[...], bg_ref[...]
        zero_tail = jnp.zeros((HF, LANES), F32)
        a_s[S:S + HF, :] = zero_tail
        g_s[S:S + HF, :] = zero_tail
        d_s[S:S + HF, :] = zero_tail

        def conv_chunk(win, w, b):
            return (w[0:1, :] * pltpu.roll(win, 2, 0)[HB:] + w[1:2, :] * pltpu.roll(win, 1, 0)[HB:] + w[2:3, :] * win[HB:] + b)

        def pass1(r0, win_a, win_g):
            rows = pl.ds(r0, R)
            a_s[rows, :] = conv_chunk(win_a, wa, ba)
            g_s[rows, :] = conv_chunk(win_g, wg, bg)
            d_s[rows, :] = da_ref[rows, :].astype(F32)

        lead = jnp.zeros((HB, LANES), F32)
        pass1(0, jnp.concatenate([lead, ha_ref[0:R, :].astype(F32)], axis=0),
              jnp.concatenate([lead, hg_ref[0:R, :].astype(F32)], axis=0))

        @pl.loop(1, nc)
        def _(c):
            r0 = pl.multiple_of(c * R, R)
            win = pl.ds(pl.multiple_of(c * R - HB, HB), R + HB)
            pass1(r0, ha_ref[win, :].astype(F32), hg_ref[win, :].astype(F32))

        def fold(x):
            return jnp.sum(x.reshape(R // SUBLANES, SUBLANES, LANES), axis=0)

        def pass2(c, acc):
            r0 = pl.multiple_of(c * R, R)
            rows, win = pl.ds(r0, R), pl.ds(r0, R + HF)
            a, g, d = a_s[win, :], g_s[win, :], d_s[win, :]
            sg = jax.nn.sigmoid(g)
            out = []
            for p, dc, w, h_ref in ((0, d * (g * sg), wa, ha_ref), (1, d * a * (sg * (1.0 + g * (1.0 - sg))), wg, hg_ref)):
                up1 = pltpu.roll(dc, R + HF - 1, 0)[0:R]
                up2 = pltpu.roll(dc, R + HF - 2, 0)[0:R]
                dc = dc[0:R]
                dh_ref[p, rows, :] = (w[2:3, :] * dc + w[1:2, :] * up1 + w[0:1, :] * up2).astype(BF16)
                hin = h_ref[rows, :].astype(F32)
                out.append((acc[p][0] + fold(up2 * hin), acc[p][1] + fold(up1 * hin), acc[p][2] + fold(dc * hin), acc[p][3] + fold(dc)))
            return tuple(out)

        z = jnp.zeros((SUBLANES, LANES), F32)
        acc = lax.fori_loop(0, nc, pass2, ((z, z, z, z), (z, z, z, z)))
        for p in range(2):
            for k in range(3):
                dw_ref[p, k:k + 1, :] = jnp.sum(acc[p][k], axis=0, keepdims=True)
            db_ref[p] = jnp.sum(acc[p][3], axis=0, keepdims=True)

    sl = lambda off, r: pl.BlockSpec((r, LANES), lambda j, off=off: (0, j + off))
    out = lambda r: pl.BlockSpec((2, r, LANES), lambda j: (0, 0, j))
    pad = pltpu.VMEM((S + HF, LANES), F32)
    return _ticked(_call(body, name=name, grid=(nf,),
                 in_specs=[sl(0, S), sl(0, S), sl(nf, S), sl(0, 3), sl(nf, 3), sl(0, 1), sl(nf, 1)],
                 out_specs=[out(S), out(3), out(1)],
                 out_shape=[_sds((2, S, F), BF16), _sds((2, 3, F), F32), _sds((2, 1, F), F32)],
                 scratch=[pad, pad, pad], dims=("parallel",))(dact, h, h, cw, cw, cb, cb), 320.0 * S / 4096)


def _swap_half(x):
    lane = lax.broadcasted_iota(jnp.int32, x.shape, 1)
    return jnp.where((lane % ROPE) < ROPE // 2, pltpu.roll(x, LANES - ROPE // 2, 1), pltpu.roll(x, ROPE // 2, 1))


def _rope(x, cos, sin_s):
    return x * cos + _swap_half(x) * sin_s


def _mla_prep(h, gq, gkv, cos, sin_s, *, name):
    S, W = h.shape
    tr = _tile(S, 512, 8)

    def body(h_ref, gq_ref, gkv_ref, cos_ref, sin_ref, cq_ref, ckv_ref, kr_ref, rq_ref, rkv_ref):
        cq = h_ref[:, 0:QRANK]
        rq = lax.rsqrt(jnp.mean(cq * cq, axis=-1, keepdims=True) + RMS_EPS)
        cq_ref[...] = (cq * rq * gq_ref[...]).astype(BF16)
        rq_ref[...] = rq
        ckv = h_ref[:, QRANK:QRANK + KVRANK]
        rkv = lax.rsqrt(jnp.mean(ckv * ckv, axis=-1, keepdims=True) + RMS_EPS)
        ckv_ref[...] = (ckv * rkv * gkv_ref[...]).astype(BF16)
        rkv_ref[...] = rkv
        kr = _rope(h_ref[:, QRANK + KVRANK:W], cos_ref[...], sin_ref[...])
        lane = lax.broadcasted_iota(jnp.int32, kr.shape, 1)
        kr = jnp.where(lane < ROPE, kr, 0.0)
        kr_ref[...] = (kr + pltpu.roll(kr, ROPE, 1)).astype(BF16)

    row = lambda w: pl.BlockSpec((tr, w), lambda i: (i, 0))
    vec = lambda w: pl.BlockSpec((1, w), lambda i: (0, 0))
    return _call(body, name=name, grid=(S // tr,),
                 in_specs=[row(W), vec(QRANK), vec(KVRANK), row(LANES), row(LANES)],
                 out_specs=[row(QRANK), row(KVRANK), row(LANES), row(1), row(1)],
                 out_shape=[_sds((S, QRANK), BF16), _sds((S, KVRANK), BF16), _sds((S, LANES), BF16), _sds((S, 1), F32), _sds((S, 1), F32)],
                 dims=("parallel",))(h, gq, gkv, cos, sin_s)


def _mla_prep_bwd(dcqn, dckvn, dkr2, h, rq, rkv, gq, gkv, cos, sin_neg, *, name):
    S, W = h.shape
    tr = _tile(S, 512, 8)

    def rms_bwd(dy, c, r, g):
        n = c * r
        dn = dy * g
        return r * (dn - n * jnp.mean(dn * n, axis=-1, keepdims=True)), jnp.sum(dy * n, axis=0, keepdims=True)

    def body(dq_ref, dkv_ref, dkr_ref, h_ref, rq_ref, rkv_ref, gq_ref, gkv_ref, cos_ref, sin_ref, dh_ref, dgq_ref, dgkv_ref):
        @pl.when(pl.program_id(0) == 0)
        def _():
            dgq_ref[...] = jnp.zeros_like(dgq_ref)
            dgkv_ref[...] = jnp.zeros_like(dgkv_ref)

        dcq, dg = rms_bwd(dq_ref[...], h_ref[:, 0:QRANK], rq_ref[...], gq_ref[...])
        dgq_ref[...] += dg
        dh_ref[:, 0:QRANK] = dcq.astype(BF16)
        dckv, dg = rms_bwd(dkv_ref[...], h_ref[:, QRANK:QRANK + KVRANK], rkv_ref[...], gkv_ref[...])
        dgkv_ref[...] += dg
        dh_ref[:, QRANK:QRANK + KVRANK] = dckv.astype(BF16)
        dk = dkr_ref[...]
        dk = dk + pltpu.roll(dk, ROPE, 1)
        dk = _rope(dk, cos_ref[...], sin_ref[...])
        lane = lax.broadcasted_iota(jnp.int32, dk.shape, 1)
        dh_ref[:, QRANK + KVRANK:W] = jnp.where(lane < ROPE, dk, 0.0).astype(BF16)

    row = lambda w: pl.BlockSpec((tr, w), lambda i: (i, 0))
    vec = lambda w: pl.BlockSpec((1, w), lambda i: (0, 0))
    return _call(body, name=name, grid=(S // tr,),
                 in_specs=[row(QRANK), row(KVRANK), row(LANES), row(W), row(1), row(1), vec(QRANK), vec(KVRANK), row(LANES), row(LANES)],
                 out_specs=[row(W), vec(QRANK), vec(KVRANK)],
                 out_shape=[_sds((S, W), BF16), _sds((1, QRANK), F32), _sds((1, KVRANK), F32)],
                 dims=("arbitrary",))(dcqn, dckvn, dkr2, h, rq, rkv, gq, gkv, cos, sin_neg)


QPAIR = 2 * NOPE + 2 * ROPE
KVPAIR = 2 * (NOPE + VDIM)


def _rope_q(q, cos, sin_s, *, name):
    S, Wq = q.shape
    tr = _tile(S, 512, 8)

    def body(q_ref, cos_ref, sin_ref, o_ref):
        o_ref[:, 0:2 * NOPE] = q_ref[:, 0:2 * NOPE].astype(BF16)
        o_ref[:, 2 * NOPE:QPAIR] = _rope(q_ref[:, 2 * NOPE:QPAIR], cos_ref[...], sin_ref[...]).astype(BF16)

    blk = pl.BlockSpec((tr, QPAIR), lambda i, p: (i, p))
    tab = pl.BlockSpec((tr, LANES), lambda i, p: (i, 0))
    return _call(body, name=name, grid=(S // tr, Wq // QPAIR), in_specs=[blk, tab, tab], out_specs=blk,
                 out_shape=_sds((S, Wq), BF16), dims=("parallel", "parallel"))(q, cos, sin_s)


ATT_T = 512
LOG2_E = 1.4426950408889634


def _att_scores(kc, qc, allowed):
    s = lax.dot_general(kc, qc, _DN["nt"], preferred_element_type=F32) * (SM_SCALE * LOG2_E)
    return s if allowed is None else jnp.where(allowed, s, -jnp.inf)


def _att_allowed(k0, q0, t):
    kpos = k0 + lax.broadcasted_iota(jnp.int32, (t, t), 0)
    qpos = q0 + lax.broadcasted_iota(jnp.int32, (t, t), 1)
    return (kpos // CHUNK) <= (qpos // CHUNK)


def _head_mask(r, hh):
    lane = lax.broadcasted_iota(jnp.int32, r.shape, 1)
    keep = (lane < ROPE) if hh == 0 else (lane >= ROPE)
    return jnp.where(keep, r, jnp.zeros_like(r))


def _att_q(q_ref):
    r = q_ref[:, 2 * NOPE:QPAIR]
    return [jnp.concatenate([q_ref[:, hh * NOPE:(hh + 1) * NOPE], _head_mask(r, hh)], axis=1) for hh in range(2)]


def _attn_fwd(qb, kv, kr2, *, name):
    S = qb.shape[0]
    t = _tile(S, ATT_T, 8)
    nq = S // t
    npair = HEADS // 2

    def body(q_ref, kv_ref, kr_ref, o_ref, lse_ref):
        i = pl.program_id(1)
        q0 = i * t
        qc = _att_q(q_ref)

        def step(j, carry, masked):
            k0 = pl.multiple_of(j * t, t)
            rows = pl.ds(k0, t)
            kr = kr_ref[rows, :]
            allowed = _att_allowed(k0, q0, t) if masked else None
            out = []
            for hh in range(2):
                m, l, acc = carry[hh]
                c0 = hh * (NOPE + VDIM)
                kc = jnp.concatenate([kv_ref[rows, c0:c0 + NOPE], kr], axis=1)
                v = kv_ref[rows, c0 + NOPE:c0 + NOPE + VDIM]
                s = _att_scores(kc, qc[hh], allowed)
                m_new = jnp.maximum(m, jnp.max(s, axis=0, keepdims=True))
                p = jnp.exp2(s - m_new)
                a = jnp.exp2(m - m_new)
                l = a * l + jnp.sum(p, axis=0, keepdims=True)
                acc = a * acc + lax.dot_general(v, p.astype(BF16), _DN["tn"], preferred_element_type=F32)
                out.append((m_new, l, acc))
            return tuple(out)

        one = (jnp.full((1, t), -jnp.inf, F32), jnp.zeros((1, t), F32), jnp.zeros((VDIM, t), F32))
        carry = lax.fori_loop(0, i, lambda j, c: step(j, c, False), (one, one))
        carry = step(i, carry, True)
        for hh in range(2):
            m, l, acc = carry[hh]
            o_ref[:, hh * VDIM:(hh + 1) * VDIM] = jnp.transpose(acc / l).astype(BF16)
            lse_ref[hh:hh + 1, :] = m + jnp.log2(l)

    return _call(body, name=name, grid=(npair, nq),
                 in_specs=[pl.BlockSpec((t, QPAIR), lambda p, i: (i, p)), pl.BlockSpec((S, KVPAIR), lambda p, i: (0, p)),
                           pl.BlockSpec((S, LANES), lambda p, i: (0, 0))],
                 out_specs=[pl.BlockSpec((t, 2 * VDIM), lambda p, i: (i, p)), pl.BlockSpec((None, 2, t), lambda p, i: (p, 0, i))],
                 out_shape=[_sds((S, HEADS * VDIM), BF16), _sds((npair, 2, S), F32)],
                 dims=("parallel", "arbitrary"))(qb, kv, kr2)


def _attn_bwd(qb, kv, kr2, o, do, lse, *, name):
    S = qb.shape[0]
    t = _tile(S, ATT_T, 8)
    nq = S // t
    npair = HEADS // 2

    def body(q_ref, kv_ref, kr_ref, o_ref, do_ref, lse_ref, dq_ref, dkv_ref, dkr_ref, acc_ref):
        pid = pl.program_id(0)
        i = pl.program_id(1)
        q0 = i * t

        @pl.when(i == 0)
        def _():
            acc_ref[...] = jnp.zeros_like(acc_ref)

        @pl.when((i == 0) & (pid == 0))
        def _():
            dkr_ref[...] = jnp.zeros_like(dkr_ref)

        qc = _att_q(q_ref)
        dov = [do_ref[:, hh * VDIM:(hh + 1) * VDIM] for hh in range(2)]
        dsum = [jnp.sum(jnp.transpose(dov[hh].astype(F32) * o_ref[:, hh * VDIM:(hh + 1) * VDIM].astype(F32)), axis=0, keepdims=True)
                for hh in range(2)]
        lse_v = [lse_ref[hh:hh + 1, :] for hh in range(2)]

        def step(j, carry, masked):
            k0 = pl.multiple_of(j * t, t)
            rows = pl.ds(k0, t)
            kr = kr_ref[rows, :]
            allowed = _att_allowed(k0, q0, t) if masked else None
            out = []
            dkr = jnp.zeros((t, LANES), F32)
            for hh in range(2):
                c0 = hh * (NOPE + VDIM)
                kc = jnp.concatenate([kv_ref[rows, c0:c0 + NOPE], kr], axis=1)
                v = kv_ref[rows, c0 + NOPE:c0 + NOPE + VDIM]
                p = jnp.exp2(_att_scores(kc, qc[hh], allowed) - lse_v[hh])
                dp = lax.dot_general(v, dov[hh], _DN["nt"], preferred_element_type=F32)
                ds = (p * (dp - dsum[hh]) * SM_SCALE).astype(BF16)
                acc_ref[rows, c0 + NOPE:c0 + NOPE + VDIM] += jnp.dot(p.astype(BF16), dov[hh], preferred_element_type=F32)
                dkc = jnp.dot(ds, qc[hh], preferred_element_type=F32)
                acc_ref[rows, c0:c0 + NOPE] += dkc[:, 0:NOPE]
                dkr = dkr + dkc[:, NOPE:]
                out.append(carry[hh] + lax.dot_general(ds, kc, _DN["tn"], preferred_element_type=F32))
            dkr_ref[rows, :] += dkr
            return tuple(out)

        zero = jnp.zeros((t, NOPE + LANES), F32)
        carry = lax.fori_loop(0, i, lambda j, c: step(j, c, False), (zero, zero))
        dqc = step(i, carry, True)
        for hh in range(2):
            dq_ref[:, hh * NOPE:(hh + 1) * NOPE] = dqc[hh][:, 0:NOPE]
        dq_ref[:, 2 * NOPE:QPAIR] = _head_mask(dqc[0][:, NOPE:], 0) + _head_mask(dqc[1][:, NOPE:], 1)

        @pl.when(i == nq - 1)
        def _():
            dkv_ref[...] = acc_ref[...].astype(BF16)

    qspec = pl.BlockSpec((t, QPAIR), lambda p, i: (i, p))
    hspec = pl.BlockSpec((t, 2 * VDIM), lambda p, i: (i, p))
    kvspec = pl.BlockSpec((S, KVPAIR), lambda p, i: (0, p))
    krspec = pl.BlockSpec((S, LANES), lambda p, i: (0, 0))
    return _ticked(_call(body, name=name, grid=(npair, nq),
                 in_specs=[qspec, kvspec, krspec, hspec, hspec, pl.BlockSpec((None, 2, t), lambda p, i: (p, 0, i))],
                 out_specs=[qspec, kvspec, krspec],
                 out_shape=[_sds(qb.shape, F32), _sds(kv.shape, BF16), _sds((S, LANES), F32)],
                 scratch=[pltpu.VMEM((S, KVPAIR), F32)], dims=("arbitrary", "arbitrary"))(qb, kv, kr2, o, do, lse),
                   640.0 * (S / 4096) ** 2)


def _coords():
    return lax.axis_index("x"), lax.axis_index("y"), lax.axis_index("c")


def _all_gather_body(n, shake):
    def body(*refs):
        ins, outs = refs[:n], refs[n:2 * n]
        send_sems, recv_sems, local_sems = refs[2 * n:]
        x, y, c = _coords()
        me, sib = (x, y, c), (x, y, 1 - c)
        chips = [(1 - x, y), (x, 1 - y), (1 - x, 1 - y)]
        if shake:
            _handshake([sib] + [(*chip, c) for chip in chips])

        def copy(a, k, block, to, src=None):
            dst = outs[a].at[4 * block[0] + 2 * block[1] + block[2]]
            return pltpu.make_async_remote_copy(src_ref=dst if src is None else src, dst_ref=dst,
                                                send_sem=send_sems.at[a, k], recv_sem=recv_sems.at[a, k],
                                                device_id=to, device_id_type=MESH)

        mine = [pltpu.make_async_copy(ins[a], outs[a].at[4 * x + 2 * y + c], local_sems.at[a]) for a in range(n)]
        for cp in mine:
            cp.start()
        sends = []
        for a in range(n):
            sends.append(copy(a, 0, me, sib, src=ins[a]))
            sends += [copy(a, 1 + j, me, (*chip, c), src=ins[a]) for j, chip in enumerate(chips)]
        for cp in sends:
            cp.start()
        for j, chip in enumerate(chips):
            for a in range(n):
                copy(a, 1 + j, (*chip, c), me).wait_recv()
                fwd = copy(a, 4 + j, (*chip, c), sib)
                fwd.start()
                sends.append(fwd)
        for a in range(n):
            copy(a, 0, sib, me).wait_recv()
            for j, chip in enumerate(chips):
                copy(a, 4 + j, (*chip, 1 - c), me).wait_recv()
        for cp in sends:
            cp.wait_send()
        for cp in mine:
            cp.wait()

    return body


def _all_gather(arrs, *, name):
    n = len(arrs)
    outs = _call(_all_gather_body(n, False), name=name, in_specs=[HBM_SPEC] * n, out_specs=[HBM_SPEC] * n,
                 out_shape=[_sds((N_DEV,) + a.shape, a.dtype) for a in arrs],
                 scratch=[pltpu.SemaphoreType.DMA((n, 7)), pltpu.SemaphoreType.DMA((n, 7)), pltpu.SemaphoreType.DMA((n,))])(*arrs)
    return list(outs)


def _sequencer_call(body, *, name, out_type, n_sems, collective_id):
    return pl.kernel(body, out_type=out_type, mesh=plsc.ScalarSubcoreMesh(axis_name="sq", num_cores=1), name=name,
                     scratch_types=[pltpu.SemaphoreType.DMA(n_sems), pltpu.SemaphoreType.DMA(n_sems), pltpu.SemaphoreType.DMA((n_sems[0],))],
                     compiler_params=pltpu.CompilerParams(collective_id=collective_id))


def _handshake(peers):
    barrier = pltpu.get_barrier_semaphore()
    for p in peers:
        pl.semaphore_signal(barrier, inc=1, device_id=p, device_id_type=MESH)
    pl.semaphore_wait(barrier, len(peers))


AG_ID, PAIR_ID, CHIP_ID = 1, 2, 3


def _all_gather_sc(arrs, *, name):
    n = len(arrs)
    outs = _sequencer_call(_all_gather_body(n, True), name=name, out_type=[_sds((N_DEV,) + a.shape, a.dtype) for a in arrs],
                           n_sems=(n, 7), collective_id=AG_ID)(*arrs)
    return list(outs)


def _pair_exchange_sc(gs, *, name):
    n = len(gs)

    def body(*refs):
        ins, outs = refs[:n], refs[n:2 * n]
        send_sems, recv_sems, _ = refs[2 * n:]
        x, y, c = _coords()
        sib = (x, y, 1 - c)
        _handshake([sib])
        cps = []
        for a in range(n):
            for j in range(4):
                cps.append(pltpu.make_async_remote_copy(src_ref=ins[a].at[2 * j + (1 - c)], dst_ref=outs[a].at[j],
                                                        send_sem=send_sems.at[a, j], recv_sem=recv_sems.at[a, j],
                                                        device_id=sib, device_id_type=MESH))
        for cp in cps:
            cp.start()
        for cp in cps:
            cp.wait()

    outs = _sequencer_call(body, name=name, out_type=[_sds((4,) + g.shape[1:], g.dtype) for g in gs],
                           n_sems=(n, 4), collective_id=PAIR_ID)(*gs)
    return list(outs)


def _chip_exchange_sc(ps, *, name):
    n = len(ps)

    def body(*refs):
        ins, outs = refs[:n], refs[n:2 * n]
        send_sems, recv_sems, _ = refs[2 * n:]
        x, y, c = _coords()
        chips = [(1 - x, y), (x, 1 - y), (1 - x, 1 - y)]
        _handshake([(*chip, c) for chip in chips])
        cps = []
        for a in range(n):
            for r, (px, py) in enumerate(chips):
                cps.append(pltpu.make_async_remote_copy(src_ref=ins[a].at[2 * px + py], dst_ref=outs[a].at[r],
                                                        send_sem=send_sems.at[a, r], recv_sem=recv_sems.at[a, r],
                                                        device_id=(px, py, c), device_id_type=MESH))
        for cp in cps:
            cp.start()
        for cp in cps:
            cp.wait()

    outs = _sequencer_call(body, name=name, out_type=[_sds((3,) + p.shape[1:], p.dtype) for p in ps],
                           n_sems=(n, 3), collective_id=CHIP_ID)(*ps)
    return list(outs)


def _row_tile(R, C, budget=1 << 20, mult=16):
    want = max(mult, budget // (4 * C))
    if R <= want:
        return R
    t = (want // mult) * mult
    while t >= mult:
        if R % t == 0:
            return t
        t -= mult
    return R


RS_TILE_BYTES = 6 << 20


def _pair_sum(g, l1, c_idx, *, name):
    _, R, C = g.shape
    tr = _row_tile(R, C, budget=RS_TILE_BYTES)
    g4 = g.reshape(4, 2, R, C)

    def body(c_ref, g_ref, l_ref, o_ref):
        o_ref[...] = (g_ref[...].astype(F32) + l_ref[...].astype(F32)).astype(o_ref.dtype)

    return _call(body, name=name, n_prefetch=1, grid=(4, R // tr),
                 in_specs=[pl.BlockSpec((None, None, tr, C), lambda j, r, c_ref: (j, c_ref[0], r, 0)),
                           pl.BlockSpec((None, tr, C), lambda j, r, c_ref: (j, r, 0))],
                 out_specs=pl.BlockSpec((None, tr, C), lambda j, r, c_ref: (j, r, 0)),
                 out_shape=_sds((4, R, C), g.dtype))(c_idx, g4, l1)


def _chip_sum(p, l2, chip_idx, *, name):
    _, R, C = p.shape
    tr = _row_tile(R, C, budget=RS_TILE_BYTES)

    def body(j_ref, p_ref, l_ref, o_ref):
        o_ref[...] = ((p_ref[...].astype(F32) + l_ref[0].astype(F32)) + l_ref[1].astype(F32)) + l_ref[2].astype(F32)

    return _call(body, name=name, n_prefetch=1, grid=(R // tr,),
                 in_specs=[pl.BlockSpec((None, tr, C), lambda r, j_ref: (j_ref[0], r, 0)),
                           pl.BlockSpec((3, tr, C), lambda r, j_ref: (0, r, 0))],
                 out_specs=pl.BlockSpec((tr, C), lambda r, j_ref: (r, 0)), out_shape=_sds((R, C), F32))(chip_idx, p, l2)


class _Reducer:
    PAIR_US = (12.0, 7.0)
    CHIP_US = (15.0, 43.0)

    def __init__(self, c_idx, chip_idx):
        self.c_idx, self.chip_idx = c_idx, chip_idx
        self.units, self.out, self.sc_free = [], {}, 0.0

    def _sequencer_done(self, cost):
        self.sc_free = max(self.sc_free, _SCHED.clock) + cost[0] + cost[1] * self._mb
        return self.sc_free

    def submit(self, name, g):
        R = g.shape[0] // N_DEV
        g8 = g.reshape((N_DEV, R, g.shape[1]))
        self._mb = R * g.shape[1] * g.dtype.itemsize / 1e6
        (l1,) = _pair_exchange_sc([g8], name=f"rs_pair_{name}")
        self.units.append(dict(name=name, g8=g8, l1=l1, mb=self._mb, stage=1, ready=self._sequencer_done(self.PAIR_US)))

    def advance(self, force=False):
        for u in self.units:
            if not (force or _SCHED.clock >= u["ready"]):
                continue
            if u["stage"] == 1:
                p = _pair_sum(u.pop("g8"), u.pop("l1"), self.c_idx, name=f"rs_psum_{u['name']}")
                (l2,) = _chip_exchange_sc([p], name=f"rs_chip_{u['name']}")
                self._mb = u["mb"]
                u.update(p=p, l2=l2, stage=2, ready=self._sequencer_done(self.CHIP_US))
            elif u["stage"] == 2:
                self.out[u["name"]] = _chip_sum(u.pop("p"), u.pop("l2"), self.chip_idx, name=f"rs_csum_{u['name']}")
                u["stage"] = 3

    def finish(self):
        self.advance(force=True)
        self.advance(force=True)
        return self.out


def _grad_ready(name, g):
    _SCHED.big[name] = g
    if _SCHED.reducer is not None:
        _SCHED.reducer.submit(name, g)
    return g


def _sum8(a, *, name):
    _, R, C = a.shape
    tr = _row_tile(R, C, budget=1 << 18, mult=8)

    def body(a_ref, o_ref):
        acc = a_ref[0]
        for k in range(1, N_DEV):
            acc = acc + a_ref[k]
        o_ref[...] = acc

    return _call(body, name=name, grid=(R // tr,), in_specs=[pl.BlockSpec((N_DEV, tr, C), lambda r: (0, r, 0))],
                 out_specs=pl.BlockSpec((tr, C), lambda r: (r, 0)), out_shape=_sds((R, C), F32), dims=("parallel",))(a)


def _adamw(w, g, m, v, *, name):
    R, C = w.shape
    tr = _row_tile(R, C, budget=1 << 20, mult=8)
    c1 = 1.0 - ADAM_B1 ** ADAM_STEP
    c2 = 1.0 - ADAM_B2 ** ADAM_STEP

    def body(w_ref, g_ref, m_ref, v_ref, d_ref, mo_ref, vo_ref):
        gv = g_ref[...]
        mn = ADAM_B1 * m_ref[...] + (1.0 - ADAM_B1) * gv
        vn = ADAM_B2 * v_ref[...] + (1.0 - ADAM_B2) * (gv * gv)
        mo_ref[...] = mn
        vo_ref[...] = vn
        d_ref[...] = -ADAM_LR * ((mn / c1) / (jnp.sqrt(vn / c2) + ADAM_EPS) + ADAM_WD * w_ref[...])

    blk = pl.BlockSpec((tr, C), lambda r: (r, 0))
    return _call(body, name=name, grid=(R // tr,), in_specs=[blk] * 4, out_specs=[blk] * 3,
                 out_shape=[_sds((R, C), F32)] * 3, dims=("parallel",))(w, g, m, v)


def _gmlp_fwd(xb, W, sp, tag):
    zp = _mm(xb, W["w_in_t"], "nt", name=f"gm_zp_{tag}", out_dtype=BF16)
    p, s, vhat, rstd = _gm_mid_fwd(zp, sp["ln_g"], sp["ln_b"], sp["w_s"], sp["b_st"], name=f"gm_mid_{tag}")
    m = _mm(p, W["w_out"], "nn", name=f"gm_out_{tag}", out_dtype=F32, tm=512, tk=p.shape[1])
    return m, dict(xb=xb, zp=zp, p=p, s=s, vhat=vhat, rstd=rstd)


def _gmlp_bwd(drb, dr, W, sp, sv, tag, on_small=None):
    _grad_ready(f"{tag}_w_out", _mm(sv["p"], drb, "tn", name=f"gm_dwout_{tag}", out_dtype=BF16, tk=TOKENS_K))
    dp = _mm(drb, W["w_out"], "nt", name=f"gm_dp_{tag}", out_dtype=BF16)
    dzp, dws, dbs, dlg, dlb = _gm_mid_bwd(dp, sv["zp"], sv["s"], sv["vhat"], sv["rstd"], sp["ln_g"], sp["ln_b"], sp["w_s"],
                                          name=f"gm_midb_{tag}")
    small = dict(w_s=dws, b_s=dbs, ln_g=dlg, ln_b=dlb)
    if on_small is not None:
        on_small(small)
    _grad_ready(f"{tag}_w_in_t", _mm(dzp, sv["xb"], "tn", name=f"gm_dwin_{tag}", out_dtype=BF16, tk=TOKENS_K))
    dx = _mm(dzp, W["w_in_t"], "nn", name=f"gm_dx_{tag}", out_dtype=F32, res=dr, res_scale=ALPHA)
    return dx, small


def _mla_fwd(xb, W, sp, rope, tag):
    cos, sin_s, _ = rope
    h = _mm(xb, W["w_in"], "nn", name=f"mla_h_{tag}", out_dtype=F32, tn=W["w_in"].shape[1])
    cqn, ckvn, kr2, rq, rkv = _mla_prep(h, sp["gq"], sp["gkv"], cos, sin_s, name=f"mla_prep_{tag}")
    q = _mm(cqn, W["w_qb_t"], "nt", name=f"mla_q_{tag}", out_dtype=F32)
    qb = _rope_q(q, cos, sin_s, name=f"mla_ropeq_{tag}")
    kv = _mm(ckvn, W["w_kvb_t"], "nt", name=f"mla_kv_{tag}", out_dtype=BF16)
    o, lse = _attn_fwd(qb, kv, kr2, name=f"mla_attn_{tag}")
    m = _mm(o, W["w_out"], "nn", name=f"mla_out_{tag}", out_dtype=F32)
    return m, dict(xb=xb, h=h, cqn=cqn, ckvn=ckvn, kr2=kr2, rq=rq, rkv=rkv, qb=qb, kv=kv, o=o, lse=lse)


def _mla_bwd(drb, dr, W, sp, rope, sv, tag):
    cos, _, sin_neg = rope
    _grad_ready(f"{tag}_w_out", _mm(sv["o"], drb, "tn", name=f"mla_dwout_{tag}", out_dtype=BF16, tk=TOKENS_K))
    do = _mm(drb, W["w_out"], "nt", name=f"mla_do_{tag}", out_dtype=BF16)
    dq, dkv, dkr2 = _attn_bwd(sv["qb"], sv["kv"], sv["kr2"], sv["o"], do, sv["lse"], name=f"mla_attnb_{tag}")
    dqb = _rope_q(dq, cos, sin_neg, name=f"mla_ropedq_{tag}")
    _grad_ready(f"{tag}_w_qb_t", _mm(dqb, sv["cqn"], "tn", name=f"mla_dwqb_{tag}", out_dtype=BF16, tk=TOKENS_K))
    _grad_ready(f"{tag}_w_kvb_t", _mm(dkv, sv["ckvn"], "tn", name=f"mla_dwkvb_{tag}", out_dtype=BF16, tk=TOKENS_K))
    dcqn = _mm(dqb, W["w_qb_t"], "nn", name=f"mla_dcq_{tag}", out_dtype=F32, tk=dqb.shape[1])
    dckvn = _mm(dkv, W["w_kvb_t"], "nn", name=f"mla_dckv_{tag}", out_dtype=F32, tk=dkv.shape[1])
    dh, dgq, dgkv = _mla_prep_bwd(dcqn, dckvn, dkr2, sv["h"], sv["rq"], sv["rkv"], sp["gq"], sp["gkv"], cos, sin_neg,
                                  name=f"mla_prepb_{tag}")
    _grad_ready(f"{tag}_w_in", _mm(sv["xb"], dh, "tn", name=f"mla_dwin_{tag}", out_dtype=BF16, tn=dh.shape[1], tk=TOKENS_K))
    dx = _mm(dh, W["w_in"], "nt", name=f"mla_dx_{tag}", out_dtype=F32, tk=dh.shape[1], res=dr, res_scale=ALPHA)
    return dx, dict(gq=dgq, gkv=dgkv)


def _ffn_fwd(xb, W, sp, tag):
    S = xb.shape[0]
    F = W["w_down"].shape[0]
    h = _mm(xb, W["w_up_t"], "nt", name=f"ffn_h_{tag}", out_dtype=BF16)
    act = _ffn_mid_fwd(h, sp["cw"], sp["cb"], name=f"ffn_mid_{tag}")
    f = _mm(act, W["w_down"], "nn", name=f"ffn_out_{tag}", out_dtype=F32, tk=F // 2)
    return f, dict(xb=xb, h=h, act=act)


def _ffn_bwd(drb, dr, W, sp, sv, tag):
    F = W["w_down"].shape[0]
    S, D = drb.shape
    tf = _tile(F, FFN_TILE)
    _grad_ready(f"{tag}_w_down", _mm(sv["act"], drb, "tn", name=f"ffn_dwdown_{tag}", out_dtype=BF16, tm=tf, tn=512, tk=TOKENS_K))
    dact = _mm(drb, W["w_down"], "nt", name=f"ffn_dact_{tag}", out_dtype=BF16, tm=512, tn=F, tk=D, b_resident=True)
    dh, dcw, dcb = _ffn_mid_bwd(dact, sv["h"], sp["cw"], sp["cb"], name=f"ffn_midb_{tag}")
    dh = dh.reshape(2 * S, F)
    _grad_ready(f"{tag}_w_up_t", _mm(dh, sv["xb"], "tn", name=f"ffn_dwup_{tag}", out_dtype=BF16, tm=tf, tn=512, tk=TOKENS_K, a_parts=2))
    dx = _mm(dh, W["w_up_t"], "nn", name=f"ffn_dx_{tag}", out_dtype=F32, res=dr, res_scale=ALPHA, a_parts=2, tk=F // 2)
    dcw = jnp.concatenate([dcw[0], dcw[1]], axis=1)
    dcb = jnp.concatenate([dcb[0], dcb[1]], axis=1)
    return dx, dict(cw=dcw, cb=dcb)


def _rope_tables(S):
    half = ROPE // 2
    inv_freq = ROPE_THETA ** (-jnp.arange(half, dtype=F32) / half)
    ang = jnp.arange(S, dtype=F32)[:, None] * inv_freq[None, :]
    cos, sin = jnp.cos(ang), jnp.sin(ang)
    cos128 = jnp.concatenate([cos] * 4, axis=1)
    sin128 = jnp.concatenate([-sin, sin, -sin, sin], axis=1)
    return cos128, sin128, -sin128


def _fwd_bwd(x, tgt, Wm, Wf, spm, spf, ln, on_small=None, late_params=None):
    S, D = x.shape
    _SCHED.last, _SCHED.clock = None, 0.0
    rope = _rope_tables(S)
    xf, xb = x, x.astype(BF16)
    saved = []
    for i in range(DEPTH):
        if i % 2 == 0:
            m, svm = _gmlp_fwd(xb, Wm[i], spm[i], f"l{i}")
        else:
            m, svm = _mla_fwd(xb, Wm[i], spm[i], rope, f"l{i}")
        if i == 0 and late_params is not None:
            spm, spf = late_params(m)
        y, yb, xh1, rs1 = _ln_fwd(xf, m, ln["mix_g"][i], ln["mix_b"][i], name=f"ln_mix_{i}")
        f, svf = _ffn_fwd(yb, Wf[i], spf[i], f"l{i}")
        z, zb, xh2, rs2 = _ln_fwd(y, f, ln["ffn_g"][i], ln["ffn_b"][i], name=f"ln_ffn_{i}")
        saved.append((svm, xh1, rs1, svf, xh2, rs2))
        xf, xb = z, zb
    lp, dy = _loss_kernel(xf, tgt, name="loss")
    gm_small, gf_small, gln = [None] * DEPTH, [None] * DEPTH, [None] * DEPTH
    _SCHED.big = {}

    for i in reversed(range(DEPTH)):
        svm, xh1, rs1, svf, xh2, rs2 = saved[i]
        dr, drb, dg2, db2 = _ln_bwd(dy, xh2, rs2, ln["ffn_g"][i], name=f"lnb_ffn_{i}")
        dy, gf_small[i] = _ffn_bwd(drb, dr, Wf[i], spf[i], svf, f"l{i}")
        dr, drb, dg1, db1 = _ln_bwd(dy, xh1, rs1, ln["mix_g"][i], name=f"lnb_mix_{i}")
        gln[i] = dict(mix_g=dg1, mix_b=db1, ffn_g=dg2, ffn_b=db2)
        if i % 2 == 0:
            hook = None if (i > 0 or on_small is None) else (lambda ms: on_small(lp, [ms] + gm_small[1:], gf_small, gln))
            dy, gm_small[i] = _gmlp_bwd(drb, dr, Wm[i], spm[i], svm, f"l{i}", hook)
        else:
            dy, gm_small[i] = _mla_bwd(drb, dr, Wm[i], spm[i], rope, svm, f"l{i}")
    return lp, dy, _SCHED.big, gm_small, gf_small, gln


def _perm_q_rows(wt):
    hd = NOPE + ROPE
    return jnp.concatenate([wt[0:NOPE], wt[hd:hd + NOPE], wt[NOPE:hd], wt[hd + NOPE:2 * hd]], axis=0)


def _unperm_q_rows(wt):
    return jnp.concatenate([wt[0:NOPE], wt[2 * NOPE:2 * NOPE + ROPE], wt[NOPE:2 * NOPE], wt[2 * NOPE + ROPE:]], axis=0)


def _pad_cols(w, to):
    return jnp.pad(w, ((0, 0), (0, to - w.shape[1])))


def _pad_pieces(a, axis, piece, piece_pad):
    n = a.shape[axis] // piece
    sh = a.shape[:axis] + (n, piece) + a.shape[axis + 1:]
    pad = [(0, 0)] * (len(sh))
    pad[axis + 1] = (0, piece_pad - piece)
    return jnp.pad(a.reshape(sh), pad).reshape(a.shape[:axis] + (n * piece_pad,) + a.shape[axis + 1:])


def _unpad_pieces(a, axis, piece, piece_pad):
    n = a.shape[axis] // piece_pad
    sh = a.shape[:axis] + (n, piece_pad) + a.shape[axis + 1:]
    return lax.slice_in_dim(a.reshape(sh), 0, piece, axis=axis + 1).reshape(a.shape[:axis] + (n * piece,) + a.shape[axis + 1:])


def _pack(parts, rows_mult=8):
    flat = jnp.concatenate([p.reshape(-1).astype(F32) for p in parts])
    n = flat.shape[0]
    per = LANES * rows_mult
    tot = ((n + per - 1) // per) * per
    return jnp.pad(flat, (0, tot - n)).reshape(tot // LANES, LANES)


def _unpack(buf, shapes):
    flat = buf.reshape(-1)
    out, off = [], 0
    for s in shapes:
        n = 1
        for d in s:
            n *= d
        out.append(flat[off:off + n].reshape(s))
        off += n
    return out


def kernel(x, gm_w_in, gm_ln_g, gm_ln_b, gm_w_s, gm_b_s, gm_w_out, mla_w_in, mla_q_norm_g, mla_kv_norm_g, mla_w_q_b, mla_w_kv_b, mla_w_out, ffn_w_up, ffn_conv_w, ffn_conv_b, ffn_w_down, ln_mix_g, ln_mix_b, ln_ffn_g, ln_ffn_b, loss_target, m_gm_w_in, m_gm_ln_g, m_gm_ln_b, m_gm_w_s, m_gm_b_s, m_gm_w_out, m_mla_w_in, m_mla_q_norm_g, m_mla_kv_norm_g, m_mla_w_q_b, m_mla_w_kv_b, m_mla_w_out, m_ffn_w_up, m_ffn_conv_w, m_ffn_conv_b, m_ffn_w_down, m_ln_mix_g, m_ln_mix_b, m_ln_ffn_g, m_ln_ffn_b, v_gm_w_in, v_gm_ln_g, v_gm_ln_b, v_gm_w_s, v_gm_b_s, v_gm_w_out, v_mla_w_in, v_mla_q_norm_g, v_mla_kv_norm_g, v_mla_w_q_b, v_mla_w_kv_b, v_mla_w_out, v_ffn_w_up, v_ffn_conv_w, v_ffn_conv_b, v_ffn_w_down, v_ln_mix_g, v_ln_mix_b, v_ln_ffn_g, v_ln_ffn_b):
    S, D = x.shape[1], x.shape[2]
    xi, yi, ci = _coords()
    dev = 4 * xi + 2 * yi + ci
    c_idx = jnp.reshape(ci, (1,)).astype(jnp.int32)
    chip_idx = jnp.reshape(2 * xi + yi, (1,)).astype(jnp.int32)
    w_in_cols = mla_w_in.shape[2]
    w_in_pad = ((w_in_cols + LANES - 1) // LANES) * LANES
    n_gm, n_mla = gm_w_in.shape[0], mla_w_in.shape[0]

    cw_l = ffn_conv_w.shape[2]
    small_in = _pack([mla_q_norm_g, mla_kv_norm_g, ffn_conv_w])
    r_down = ffn_w_down.shape[1]
    r_pad = ((r_down + FFN_SHARD_MULT - 1) // FFN_SHARD_MULT) * FFN_SHARD_MULT

    def gather(shards, name):
        full = _all_gather_sc(list(shards.values()), name=name)
        return {k: f.reshape((-1, f.shape[2])) for k, f in zip(shards.keys(), full)}

    Wm, Wf = [None] * DEPTH, [None] * DEPTH
    for i in range(DEPTH):
        s = i // 2
        if i == 0:
            Wm[i] = gather(dict(w_in_t=gm_w_in[s].T.astype(BF16)), "ag_first")
            Wm[i].update(gather(dict(w_out=gm_w_out[s].astype(BF16), small=small_in), f"ag_gm_{i}_rest"))
            small_all = Wm[i].pop("small").reshape((N_DEV,) + small_in.shape)
        elif i % 2 == 0:
            Wm[i] = gather(dict(w_in_t=gm_w_in[s].T.astype(BF16), w_out=gm_w_out[s].astype(BF16)), f"ag_gm_{i}")
        else:
            Wm[i] = gather(dict(w_in=_pad_cols(mla_w_in[s], w_in_pad).astype(BF16),
                                w_qb_t=_perm_q_rows(mla_w_q_b[s].T).astype(BF16),
                                w_kvb_t=mla_w_kv_b[s].T.astype(BF16), w_out=mla_w_out[s].astype(BF16)), f"ag_mla_{i}")
        Wf[i] = gather(dict(w_up_t=_pad_pieces(ffn_w_up[i].T, 0, r_down, r_pad).astype(BF16),
                            w_down=_pad_pieces(ffn_w_down[i], 0, r_down, r_pad).astype(BF16)), f"ag_ffn_{i}")

    def small_params(anchor):
        spm, spf = [None] * DEPTH, [None] * DEPTH
        if anchor is not None:
            pack, _ = lax.optimization_barrier((small_all, anchor))
            parts = [_unpack(pack[k], [mla_q_norm_g.shape, mla_kv_norm_g.shape, ffn_conv_w.shape]) for k in range(N_DEV)]
            gq_full = jnp.concatenate([p[0] for p in parts], axis=1)
            gkv_full = jnp.concatenate([p[1] for p in parts], axis=1)
            cw_full = jnp.concatenate([p[2] for p in parts], axis=2)
        for i in range(DEPTH):
            s = i // 2
            if i % 2 == 0:
                spm[i] = dict(ln_g=gm_ln_g[s][None], ln_b=gm_ln_b[s][None], w_s=gm_w_s[s], b_st=gm_b_s[s].T)
            elif anchor is not None:
                spm[i] = dict(gq=gq_full[s][None], gkv=gkv_full[s][None])
            if anchor is not None:
                spf[i] = dict(cw=_pad_pieces(cw_full[i], 1, r_down, r_pad), cb=_pad_pieces(ffn_conv_b[i][None], 1, r_down, r_pad))
        return spm, spf

    spm, spf = small_params(None)
    ln = dict(mix_g=ln_mix_g[:, None], mix_b=ln_mix_b[:, None], ffn_g=ln_ffn_g[:, None], ffn_b=ln_ffn_b[:, None])

    small_box = {}

    def on_small(lp, gm_small, gf_small, gln):
        small_g = [
            jnp.stack([gm_small[2 * s]["ln_g"][0] for s in range(n_gm)]),
            jnp.stack([gm_small[2 * s]["ln_b"][0] for s in range(n_gm)]),
            jnp.stack([gm_small[2 * s]["w_s"] for s in range(n_gm)]),
            jnp.stack([gm_small[2 * s]["b_s"][:, :, 0] for s in range(n_gm)]),
            jnp.stack([_unpad_pieces(gf_small[i]["cb"], 1, r_down, r_pad)[0] for i in range(DEPTH)]),
            jnp.stack([gln[i]["mix_g"][0] for i in range(DEPTH)]),
            jnp.stack([gln[i]["mix_b"][0] for i in range(DEPTH)]),
            jnp.stack([gln[i]["ffn_g"][0] for i in range(DEPTH)]),
            jnp.stack([gln[i]["ffn_b"][0] for i in range(DEPTH)]),
            jnp.stack([gm_small[2 * s + 1]["gq"][0] for s in range(n_mla)]),
            jnp.stack([gm_small[2 * s + 1]["gkv"][0] for s in range(n_mla)]),
            jnp.stack([_unpad_pieces(gf_small[i]["cw"], 1, r_down, r_pad) for i in range(DEPTH)]),
        ]
        small_box["shapes"] = [g.shape for g in small_g]
        (small_box["all"],) = _all_gather_sc([_pack(small_g)], name="ag_small_grads")

    _SCHED.reducer = _Reducer(c_idx, chip_idx)
    lp, grad_x, _, _, _, _ = _fwd_bwd(x[0], loss_target[0], Wm, Wf, spm, spf, ln, on_small, small_params)
    red = _SCHED.reducer.finish()
    _SCHED.reducer = None
    rm = [{k: red[f"l{i}_{k}"] for k in Wm[i]} for i in range(DEPTH)]
    rf = [{k: red[f"l{i}_{k}"] for k in Wf[i]} for i in range(DEPTH)]

    g_gm_w_in = jnp.stack([rm[2 * s]["w_in_t"].T for s in range(n_gm)])
    g_gm_w_out = jnp.stack([rm[2 * s]["w_out"] for s in range(n_gm)])
    g_mla_w_in = jnp.stack([rm[2 * s + 1]["w_in"][:, :w_in_cols] for s in range(n_mla)])
    g_mla_w_q_b = jnp.stack([_unperm_q_rows(rm[2 * s + 1]["w_qb_t"]).T for s in range(n_mla)])
    g_mla_w_kv_b = jnp.stack([rm[2 * s + 1]["w_kvb_t"].T for s in range(n_mla)])
    g_mla_w_out = jnp.stack([rm[2 * s + 1]["w_out"] for s in range(n_mla)])
    g_ffn_w_up = jnp.stack([_unpad_pieces(rf[i]["w_up_t"], 0, r_down, r_pad).T for i in range(DEPTH)])
    g_ffn_w_down = jnp.stack([_unpad_pieces(rf[i]["w_down"], 0, r_down, r_pad) for i in range(DEPTH)])

    small_sum = _unpack(_sum8(small_box["all"], name="small_grad_sum"), small_box["shapes"])
    (g_gm_ln_g, g_gm_ln_b, g_gm_w_s, g_gm_b_s, g_ffn_conv_b, g_ln_mix_g, g_ln_mix_b, g_ln_ffn_g, g_ln_ffn_b,
     gq_all, gkv_all, cw_all) = small_sum
    (loss_all,) = _all_gather([_pack([jnp.reshape(0.5 * jnp.sum(lp) / D, (1,))])], name="ag_loss")
    loss = jnp.reshape(_sum8(loss_all, name="loss_sum")[0, 0], ())
    qn_l = mla_q_norm_g.shape[1]
    g_mla_q_norm_g = lax.dynamic_slice_in_dim(gq_all, dev * qn_l, qn_l, axis=1)
    g_mla_kv_norm_g = lax.dynamic_slice_in_dim(gkv_all, dev * qn_l, qn_l, axis=1)
    g_ffn_conv_w = lax.dynamic_slice_in_dim(cw_all, dev * cw_l, cw_l, axis=2)

    def adam_big(w, g, m, v, tag):
        sh = w.shape
        two = lambda a: a.reshape((-1, sh[-1]))
        d, mn, vn = _adamw(two(w), two(g), two(m), two(v), name=f"adamw_{tag}")
        return d.reshape(sh), mn.reshape(sh), vn.reshape(sh)

    big = [("gm_w_in", gm_w_in, g_gm_w_in, m_gm_w_in, v_gm_w_in), ("gm_w_out", gm_w_out, g_gm_w_out, m_gm_w_out, v_gm_w_out),
           ("mla_w_in", mla_w_in, g_mla_w_in, m_mla_w_in, v_mla_w_in), ("mla_w_q_b", mla_w_q_b, g_mla_w_q_b, m_mla_w_q_b, v_mla_w_q_b),
           ("mla_w_kv_b", mla_w_kv_b, g_mla_w_kv_b, m_mla_w_kv_b, v_mla_w_kv_b), ("mla_w_out", mla_w_out, g_mla_w_out, m_mla_w_out, v_mla_w_out),
           ("ffn_w_up", ffn_w_up, g_ffn_w_up, m_ffn_w_up, v_ffn_w_up), ("ffn_w_down", ffn_w_down, g_ffn_w_down, m_ffn_w_down, v_ffn_w_down)]
    res = {}
    for tag, w, g, m, v in big:
        res[tag] = (g,) + adam_big(w, g, m, v, tag)

    small = [("gm_ln_g", gm_ln_g, g_gm_ln_g, m_gm_ln_g, v_gm_ln_g), ("gm_ln_b", gm_ln_b, g_gm_ln_b, m_gm_ln_b, v_gm_ln_b),
             ("gm_w_s", gm_w_s, g_gm_w_s, m_gm_w_s, v_gm_w_s), ("gm_b_s", gm_b_s, g_gm_b_s, m_gm_b_s, v_gm_b_s),
             ("mla_q_norm_g", mla_q_norm_g, g_mla_q_norm_g, m_mla_q_norm_g, v_mla_q_norm_g),
             ("mla_kv_norm_g", mla_kv_norm_g, g_mla_kv_norm_g, m_mla_kv_norm_g, v_mla_kv_norm_g),
             ("ffn_conv_w", ffn_conv_w, g_ffn_conv_w, m_ffn_conv_w, v_ffn_conv_w), ("ffn_conv_b", ffn_conv_b, g_ffn_conv_b, m_ffn_conv_b, v_ffn_conv_b),
             ("ln_mix_g", ln_mix_g, g_ln_mix_g, m_ln_mix_g, v_ln_mix_g), ("ln_mix_b", ln_mix_b, g_ln_mix_b, m_ln_mix_b, v_ln_mix_b),
             ("ln_ffn_g", ln_ffn_g, g_ln_ffn_g, m_ln_ffn_g, v_ln_ffn_g), ("ln_ffn_b", ln_ffn_b, g_ln_ffn_b, m_ln_ffn_b, v_ln_ffn_b)]
    shapes = [t[1].shape for t in small]
    d_s, m_s, v_s = _adamw(_pack([t[1] for t in small]), _pack([t[2] for t in small]), _pack([t[3] for t in small]),
                           _pack([t[4] for t in small]), name="adamw_small")
    d_l, m_l, v_l = _unpack(d_s, shapes), _unpack(m_s, shapes), _unpack(v_s, shapes)
    for (tag, _, g, _, _), d, mn, vn in zip(small, d_l, m_l, v_l):
        res[tag] = (g, d, mn, vn)

    order = ["gm_w_in", "gm_ln_g", "gm_ln_b", "gm_w_s", "gm_b_s", "gm_w_out", "mla_w_in", "mla_q_norm_g", "mla_kv_norm_g",
             "mla_w_q_b", "mla_w_kv_b", "mla_w_out", "ffn_w_up", "ffn_conv_w", "ffn_conv_b", "ffn_w_down",
             "ln_mix_g", "ln_mix_b", "ln_ffn_g", "ln_ffn_b"]
    out = [loss, grad_x[None]]
    for q in range(4):
        out += [res[k][q] for k in order]
    return tuple(out)
```

```python
import jax
import jax.numpy as jnp
from jax import lax
from jax.experimental import pallas as pl
from jax.experimental.pallas import tpu as pltpu
from jax.experimental.pallas import tpu_sc as plsc

F32, BF16 = jnp.float32, jnp.bfloat16

DEPTH = 4
CHUNK = 64
GM_BLOCK = 128
GM_GROUPS = 8
HEADS = 16
NOPE, ROPE, VDIM = 128, 64, 128
QRANK, KVRANK = 512, 512
ROPE_THETA = 10000.0
SM_SCALE = (NOPE + ROPE) ** -0.5
ALPHA = (2 * DEPTH) ** 0.25
LN_EPS = 1e-5
RMS_EPS = 1e-6
ADAM_LR, ADAM_B1, ADAM_B2, ADAM_EPS, ADAM_WD, ADAM_STEP = 0.001, 0.9, 0.999, 1e-08, 0.01, 10

N_DEV = 8
LANES = 128
SUBLANES = 8
VMEM_LIMIT = 56 * 1024 * 1024
MESH = pl.DeviceIdType.MESH
HBM_SPEC = pl.BlockSpec(memory_space=pltpu.HBM)


class _Schedule:
    def __init__(self):
        self.last = None
        self.reducer = None
        self.big = {}
        self.clock = 0.0

    def tick(self, us):
        self.clock += us
        if self.reducer is not None:
            self.reducer.advance()


_SCHED = _Schedule()


def _ticked(out, us):
    _SCHED.tick(us)
    return out


def _call(body, *, name, out_shape, in_specs, out_specs, grid=None, scratch=(), dims=None, n_prefetch=0):
    kw = dict(vmem_limit_bytes=VMEM_LIMIT)
    if dims is not None:
        kw["dimension_semantics"] = dims
    cp = pltpu.CompilerParams(**kw)
    in_specs = list(in_specs)
    token = _SCHED.last
    if token is not None:
        pos = n_prefetch + len(in_specs)
        in_specs.append(pl.BlockSpec(memory_space=pl.ANY))
        inner = body

        def body(*refs):
            return inner(*refs[:pos], *refs[pos + 1:])

    if n_prefetch:
        gs = pltpu.PrefetchScalarGridSpec(num_scalar_prefetch=n_prefetch, grid=grid, in_specs=in_specs, out_specs=out_specs,
                                          scratch_shapes=list(scratch))
        fn = pl.pallas_call(body, name=name, grid_spec=gs, out_shape=out_shape, compiler_params=cp, interpret=False)
    else:
        extra = {} if grid is None else {"grid": grid}
        fn = pl.pallas_call(body, name=name, in_specs=in_specs, out_specs=out_specs, out_shape=out_shape,
                            scratch_shapes=list(scratch), compiler_params=cp, interpret=False, **extra)

    def run(*args):
        out = fn(*args, token) if token is not None else fn(*args)
        _SCHED.last = out[0] if isinstance(out, (list, tuple)) else out
        return out

    return run


def _tile(n, pref, mult=LANES):
    if n <= pref:
        return n
    t = (pref // mult) * mult
    while t >= mult:
        if n % t == 0:
            return t
        t -= mult
    return n


def _sds(shape, dtype):
    return jax.ShapeDtypeStruct(tuple(shape), dtype)


FFN_TILE = 1408
FFN_CHUNK = 64
FFN_SHARD_MULT = 64
MXU_FLOPS_PER_US = 8e8
TOKENS_K = 4096
_DN = {"nn": (((1,), (0,)), ((), ())), "nt": (((1,), (1,)), ((), ())), "tn": (((0,), (0,)), ((), ()))}


def _mm(a, b, mode, *, name, out_dtype, tm=1024, tn=1024, tk=2048, res=None, res_scale=1.0, a_parts=1, b_resident=False):
    if a_parts > 1:
        S_, F_ = a.shape[0] // a_parts, a.shape[1]
        a_shape = (S_, a_parts * F_)
    else:
        a_shape = a.shape
    if mode == "nn":
        (M, K), (K2, N) = a_shape, b.shape
    elif mode == "nt":
        (M, K), (N, K2) = a_shape, b.shape
    else:
        (K, M), (K2, N) = a_shape, b.shape
    assert K == K2, (name, a.shape, b.shape)
    tm, tn, tk = _tile(M, tm), _tile(N, tn), _tile(K, tk)
    nk = K // tk
    if a_parts > 1 and mode == "nn":
        per = F_ // tk
        assert F_ % tk == 0
        a_spec = pl.BlockSpec((tm, tk), lambda i, j, k: ((k // per) * (S_ // tm) + i, k % per))
    elif a_parts > 1:
        per = F_ // tm
        assert mode == "tn" and F_ % tm == 0
        a_spec = pl.BlockSpec((tk, tm), lambda i, j, k: ((i // per) * (S_ // tk) + k, i % per))
    elif mode == "tn":
        a_spec = pl.BlockSpec((tk, tm), lambda i, j, k: (k, i))
    else:
        a_spec = pl.BlockSpec((tm, tk), lambda i, j, k: (i, k))
    b_kw = {"pipeline_mode": pl.Buffered(1)} if b_resident else {}
    b_spec = (pl.BlockSpec((tn, tk), lambda i, j, k: (j, k), **b_kw) if mode == "nt"
              else pl.BlockSpec((tk, tn), lambda i, j, k: (k, j), **b_kw))
    in_specs = [a_spec, b_spec]
    args = [a, b]
    if res is not None:
        in_specs.append(pl.BlockSpec((tm, tn), lambda i, j, k: (i, j)))
        args.append(res)
    dn = _DN[mode]
    has_res = res is not None

    def body(*refs):
        a_ref, b_ref = refs[0], refs[1]
        r_ref = refs[2] if has_res else None
        o_ref = refs[2 + has_res]
        part = lax.dot_general(a_ref[...], b_ref[...], dn, preferred_element_type=F32)

        def finish(acc):
            if has_res:
                acc = acc + res_scale * r_ref[...]
            o_ref[...] = acc.astype(o_ref.dtype)

        if nk == 1:
            finish(part)
        else:
            acc_ref = refs[3 + has_res]
            k = pl.program_id(2)

            @pl.when(k == 0)
            def _():
                acc_ref[...] = part

            @pl.when(k > 0)
            def _():
                acc_ref[...] += part

            @pl.when(k == nk - 1)
            def _():
                finish(acc_ref[...])

    scratch = [pltpu.VMEM((tm, tn), F32)] if nk > 1 else []
    out = _call(body, name=name, grid=(M // tm, N // tn, nk), in_specs=in_specs,
                out_specs=pl.BlockSpec((tm, tn), lambda i, j, k: (i, j)), out_shape=_sds((M, N), out_dtype),
                scratch=scratch, dims=("parallel", "parallel", "arbitrary"))(*args)
    _SCHED.tick(2.0 * M * N * K / MXU_FLOPS_PER_US)
    return out


def _ln_fwd(x, m, g, b, *, name):
    S, D = x.shape
    tr = _tile(S, 256, 8)

    def body(x_ref, m_ref, g_ref, b_ref, y_ref, yb_ref, xh_ref, rs_ref):
        r = ALPHA * x_ref[...] + m_ref[...]
        mu = jnp.mean(r, axis=-1, keepdims=True)
        d = r - mu
        var = jnp.mean(d * d, axis=-1, keepdims=True)
        rstd = lax.rsqrt(var + LN_EPS)
        xh = d * rstd
        y = xh * g_ref[...] + b_ref[...]
        y_ref[...] = y
        yb_ref[...] = y.astype(BF16)
        xh_ref[...] = xh
        rs_ref[...] = rstd

    row = pl.BlockSpec((tr, D), lambda i: (i, 0))
    vec = pl.BlockSpec((1, D), lambda i: (0, 0))
    return _call(body, name=name, grid=(S // tr,), in_specs=[row, row, vec, vec],
                 out_specs=[row, row, row, pl.BlockSpec((tr, 1), lambda i: (i, 0))],
                 out_shape=[_sds((S, D), F32), _sds((S, D), BF16), _sds((S, D), F32), _sds((S, 1), F32)],
                 dims=("parallel",))(x, m, g, b)


def _ln_bwd(dy, xh, rstd, g, *, name):
    S, D = dy.shape
    tr = _tile(S, 256, 8)

    def body(dy_ref, xh_ref, rs_ref, g_ref, dr_ref, drb_ref, dg_ref, db_ref):
        @pl.when(pl.program_id(0) == 0)
        def _():
            dg_ref[...] = jnp.zeros_like(dg_ref)
            db_ref[...] = jnp.zeros_like(db_ref)

        dyv = dy_ref[...]
        xhv = xh_ref[...]
        dxh = dyv * g_ref[...]
        m1 = jnp.mean(dxh, axis=-1, keepdims=True)
        m2 = jnp.mean(dxh * xhv, axis=-1, keepdims=True)
        dr = rs_ref[...] * (dxh - m1 - xhv * m2)
        dr_ref[...] = dr
        drb_ref[...] = dr.astype(BF16)
        dg_ref[...] += jnp.sum(dyv * xhv, axis=0, keepdims=True)
        db_ref[...] += jnp.sum(dyv, axis=0, keepdims=True)

    row = pl.BlockSpec((tr, D), lambda i: (i, 0))
    vec = pl.BlockSpec((1, D), lambda i: (0, 0))
    return _ticked(_call(body, name=name, grid=(S // tr,), in_specs=[row, row, pl.BlockSpec((tr, 1), lambda i: (i, 0)), vec],
                         out_specs=[row, row, vec, vec],
                         out_shape=[_sds((S, D), F32), _sds((S, D), BF16), _sds((1, D), F32), _sds((1, D), F32)],
                         dims=("arbitrary",))(dy, xh, rstd, g), 45.0 * S / 4096)


def _loss_kernel(y, t, *, name):
    S, D = y.shape
    tr = _tile(S, 256, 8)

    def body(y_ref, t_ref, lp_ref, dy_ref):
        @pl.when(pl.program_id(0) == 0)
        def _():
            lp_ref[...] = jnp.zeros_like(lp_ref)

        e = y_ref[...] - t_ref[...]
        dy_ref[...] = e / D
        lp_ref[...] += jnp.sum(e * e, axis=0, keepdims=True)

    row = pl.BlockSpec((tr, D), lambda i: (i, 0))
    vec = pl.BlockSpec((1, D), lambda i: (0, 0))
    return _call(body, name=name, grid=(S // tr,), in_specs=[row, row], out_specs=[vec, row],
                 out_shape=[_sds((1, D), F32), _sds((S, D), F32)], dims=("arbitrary",))(y, t)


_GELU_C = 0.7978845608028654
_GELU_A = 0.044715


def _gelu(x):
    return 0.5 * x * (1.0 + jnp.tanh(_GELU_C * (x + _GELU_A * x * x * x)))


def _gelu_grad(x):
    x2 = x * x
    t = jnp.tanh(_GELU_C * (x + _GELU_A * x * x2))
    return 0.5 * (1.0 + t) + 0.5 * x * (1.0 - t * t) * (_GELU_C * (1.0 + 3.0 * _GELU_A * x2))


def _gelu_both(x):
    x2 = x * x
    t = jnp.tanh(_GELU_C * (x + _GELU_A * x * x2))
    half = 0.5 * (1.0 + t)
    return x * half, half + 0.5 * x * (1.0 - t * t) * (_GELU_C * (1.0 + 3.0 * _GELU_A * x2))


def _masked_ws(w):
    i = lax.broadcasted_iota(jnp.int32, w.shape, 0) // CHUNK
    j = lax.broadcasted_iota(jnp.int32, w.shape, 1) // CHUNK
    return jnp.where(j <= i, w, 0.0)


def _gm_mid_fwd(zp, ln_g, ln_b, w_s, b_st, *, name):
    S, H2 = zp.shape
    H = H2 // 2
    gd = H // GM_GROUPS
    nb = S // GM_BLOCK

    def body(zu_ref, zv_ref, g_ref, b_ref, w_ref, bs_ref, p_ref, s_ref, vh_ref, rs_ref):
        v = _gelu(zv_ref[...].astype(F32))
        mu = jnp.mean(v, axis=-1, keepdims=True)
        d = v - mu
        var = jnp.mean(d * d, axis=-1, keepdims=True)
        rstd = lax.rsqrt(var + LN_EPS)
        vh = d * rstd
        vh_ref[...] = vh.astype(BF16)
        rs_ref[...] = rstd
        vn = (vh * g_ref[...] + b_ref[...]).astype(BF16)
        bs = bs_ref[...]
        for gi in range(GM_GROUPS):
            cs = slice(gi * gd, (gi + 1) * gd)
            wm = _masked_ws(w_ref[gi]).astype(BF16)
            s = jnp.dot(wm, vn[:, cs], preferred_element_type=F32) + bs[:, gi:gi + 1]
            u = _gelu(zu_ref[:, cs].astype(F32))
            s_ref[:, cs] = s.astype(BF16)
            p_ref[:, cs] = (u * s).astype(BF16)

    blk = lambda c: pl.BlockSpec((GM_BLOCK, H), lambda n, c=c: (n, c))
    vec = pl.BlockSpec((1, H), lambda n: (0, 0))
    return _call(body, name=name, grid=(nb,),
                 in_specs=[blk(0), blk(1), vec, vec, pl.BlockSpec((GM_GROUPS, GM_BLOCK, GM_BLOCK), lambda n: (0, 0, 0)),
                           pl.BlockSpec((GM_BLOCK, GM_GROUPS), lambda n: (0, 0))],
                 out_specs=[blk(0), blk(0), blk(0), pl.BlockSpec((GM_BLOCK, 1), lambda n: (n, 0))],
                 out_shape=[_sds((S, H), BF16), _sds((S, H), BF16), _sds((S, H), BF16), _sds((S, 1), F32)],
                 dims=("parallel",))(zp, zp, ln_g, ln_b, w_s, b_st)


def _gm_mid_bwd(dp, zp, s, vhat, rstd, ln_g, ln_b, w_s, *, name):
    S, H2 = zp.shape
    H = H2 // 2
    gd = H // GM_GROUPS
    nb = S // GM_BLOCK

    def body(dp_ref, zu_ref, zv_ref, s_ref, vh_ref, rs_ref, g_ref, b_ref, w_ref,
             dz_ref, dw_ref, dbs_ref, dg_ref, db_ref, dvh_ref):
        @pl.when(pl.program_id(0) == 0)
        def _():
            dw_ref[...] = jnp.zeros_like(dw_ref)
            dbs_ref[...] = jnp.zeros_like(dbs_ref)
            dg_ref[...] = jnp.zeros_like(dg_ref)
            db_ref[...] = jnp.zeros_like(db_ref)

        m1 = jnp.zeros((GM_BLOCK, 1), F32)
        m2 = jnp.zeros((GM_BLOCK, 1), F32)
        for gi in range(GM_GROUPS):
            cs = slice(gi * gd, (gi + 1) * gd)
            dpg = dp_ref[:, cs].astype(F32)
            zu = zu_ref[:, cs].astype(F32)
            u, du_dz = _gelu_both(zu)
            ds = dpg * u
            du = dpg * s_ref[:, cs].astype(F32)
            dz_ref[:, cs] = (du * du_dz).astype(BF16)
            dsb = ds.astype(BF16)
            vh = vh_ref[:, cs].astype(F32)
            lg = g_ref[:, cs]
            vn = (vh * lg + b_ref[:, cs]).astype(BF16)
            wm = _masked_ws(w_ref[gi]).astype(BF16)
            dvn = lax.dot_general(wm, dsb, _DN["tn"], preferred_element_type=F32)
            dw_ref[gi] += lax.dot_general(dsb, vn, _DN["nt"], preferred_element_type=F32)
            dbs_ref[gi] += jnp.sum(ds, axis=1, keepdims=True)
            dg_ref[:, cs] += jnp.sum(dvn * vh, axis=0, keepdims=True)
            db_ref[:, cs] += jnp.sum(dvn, axis=0, keepdims=True)
            dvh = dvn * lg
            dvh_ref[:, cs] = dvh
            m1 = m1 + jnp.sum(dvh, axis=1, keepdims=True)
            m2 = m2 + jnp.sum(dvh * vh, axis=1, keepdims=True)
        m1 = m1 / H
        m2 = m2 / H
        rs = rs_ref[...]
        for gi in range(GM_GROUPS):
            cs = slice(gi * gd, (gi + 1) * gd)
            vh = vh_ref[:, cs].astype(F32)
            dv = rs * (dvh_ref[:, cs] - m1 - vh * m2)
            zv = zv_ref[:, cs].astype(F32)
            dz_ref[:, H + gi * gd:H + (gi + 1) * gd] = (dv * _gelu_grad(zv)).astype(BF16)

        @pl.when(pl.program_id(0) == nb - 1)
        def _():
            for gi in range(GM_GROUPS):
                dw_ref[gi] = _masked_ws(dw_ref[gi])

    blk = lambda c: pl.BlockSpec((GM_BLOCK, H), lambda n, c=c: (n, c))
    vec = pl.BlockSpec((1, H), lambda n: (0, 0))
    wspec = pl.BlockSpec((GM_GROUPS, GM_BLOCK, GM_BLOCK), lambda n: (0, 0, 0))
    return _ticked(_call(body, name=name, grid=(nb,),
                 in_specs=[blk(0), blk(0), blk(1), blk(0), blk(0), pl.BlockSpec((GM_BLOCK, 1), lambda n: (n, 0)), vec, vec, wspec],
                 out_specs=[pl.BlockSpec((GM_BLOCK, H2), lambda n: (n, 0)), wspec,
                            pl.BlockSpec((GM_GROUPS, GM_BLOCK, 1), lambda n: (0, 0, 0)), vec, vec],
                 out_shape=[_sds((S, H2), BF16), _sds((GM_GROUPS, GM_BLOCK, GM_BLOCK), F32),
                            _sds((GM_GROUPS, GM_BLOCK, 1), F32), _sds((1, H), F32), _sds((1, H), F32)],
                 scratch=[pltpu.VMEM((GM_BLOCK, H), F32)], dims=("arbitrary",))(dp, zp, zp, s, vhat, rstd, ln_g, ln_b, w_s),
                   280.0 * S / 4096)


def _shift_down(x, k):
    r = pltpu.roll(x, k, 0)
    rows = lax.broadcasted_iota(jnp.int32, (SUBLANES, x.shape[1]), 0)
    return jnp.concatenate([jnp.where(rows >= k, r[:SUBLANES], 0.0), r[SUBLANES:]], axis=0)


def _shift_up(x, k):
    n = x.shape[0]
    r = pltpu.roll(x, n - k, 0)
    rows = lax.broadcasted_iota(jnp.int32, (SUBLANES, x.shape[1]), 0)
    return jnp.concatenate([r[:n - SUBLANES], jnp.where(rows < SUBLANES - k, r[n - SUBLANES:], 0.0)], axis=0)


def _conv(h, w, b):
    return w[0:1, :] * _shift_down(h, 2) + w[1:2, :] * _shift_down(h, 1) + w[2:3, :] * h + b


def _ffn_mid_fwd(h, cw, cb, *, name):
    S, F2 = h.shape
    F = F2 // 2
    nf = F // LANES

    def body(ha_ref, hg_ref, wa_ref, wg_ref, ba_ref, bg_ref, act_ref):
        a = _conv(ha_ref[...].astype(F32), wa_ref[...], ba_ref[...])
        g = _conv(hg_ref[...].astype(F32), wg_ref[...], bg_ref[...])
        act_ref[...] = (g * jax.nn.sigmoid(g) * a).astype(BF16)

    sl = lambda off, r: pl.BlockSpec((r, LANES), lambda j, off=off: (0, j + off))
    return _call(body, name=name, grid=(nf,),
                 in_specs=[sl(0, S), sl(nf, S), sl(0, 3), sl(nf, 3), sl(0, 1), sl(nf, 1)],
                 out_specs=sl(0, S), out_shape=_sds((S, F), BF16), dims=("parallel",))(h, h, cw, cw, cb, cb)


def _ffn_mid_bwd(dact, h, cw, cb, *, name):
    S, F2 = h.shape
    F = F2 // 2
    nf = F // LANES

    R, HB, HF = FFN_CHUNK, 16, SUBLANES
    nc = S // R

    def body(da_ref, ha_ref, hg_ref, wa_ref, wg_ref, ba_ref, bg_ref, dh_ref, dw_ref, db_ref, a_s, g_s, d_s):
        wa, wg, ba, bg = wa_ref[...], wg_ref[...], ba_ref[...], bg_ref[...]
        zero_tail = jnp.zeros((HF, LANES), F32)
        a_s[S:S + HF, :] = zero_tail
        g_s[S:S + HF, :] = zero_tail
        d_s[S:S + HF, :] = zero_tail

        def conv_chunk(win, w, b):
            return (w[0:1, :] * pltpu.roll(win, 2, 0)[HB:] + w[1:2, :] * pltpu.roll(win, 1, 0)[HB:] + w[2:3, :] * win[HB:] + b)

        def pass1(r0, win_a, win_g):
            rows = pl.ds(r0, R)
            a_s[rows, :] = conv_chunk(win_a, wa, ba)
            g_s[rows, :] = conv_chunk(win_g, wg, bg)
            d_s[rows, :] = da_ref[rows, :].astype(F32)

        lead = jnp.zeros((HB, LANES), F32)
        pass1(0, jnp.concatenate([lead, ha_ref[0:R, :].astype(F32)], axis=0),
              jnp.concatenate([lead, hg_ref[0:R, :].astype(F32)], axis=0))

        @pl.loop(1, nc)
        def _(c):
            r0 = pl.multiple_of(c * R, R)
            win = pl.ds(pl.multiple_of(c * R - HB, HB), R + HB)
            pass1(r0, ha_ref[win, :].astype(F32), hg_ref[win, :].astype(F32))

        def fold(x):
            return jnp.sum(x.reshape(R // SUBLANES, SUBLANES, LANES), axis=0)

        def pass2(c, acc):
            r0 = pl.multiple_of(c * R, R)
            rows, win = pl.ds(r0, R), pl.ds(r0, R + HF)
            a, g, d = a_s[win, :], g_s[win, :], d_s[win, :]
            sg = jax.nn.sigmoid(g)
            out = []
            for p, dc, w, h_ref in ((0, d * (g * sg), wa, ha_ref), (1, d * a * (sg * (1.0 + g * (1.0 - sg))), wg, hg_ref)):
                up1 = pltpu.roll(dc, R + HF - 1, 0)[0:R]
                up2 = pltpu.roll(dc, R + HF - 2, 0)[0:R]
                dc = dc[0:R]
                dh_ref[p, rows, :] = (w[2:3, :] * dc + w[1:2, :] * up1 + w[0:1, :] * up2).astype(BF16)
                hin = h_ref[rows, :].astype(F32)
                out.append((acc[p][0] + fold(up2 * hin), acc[p][1] + fold(up1 * hin), acc[p][2] + fold(dc * hin), acc[p][3] + fold(dc)))
            return tuple(out)

        z = jnp.zeros((SUBLANES, LANES), F32)
        acc = lax.fori_loop(0, nc, pass2, ((z, z, z, z), (z, z, z, z)))
        for p in range(2):
            for k in range(3):
                dw_ref[p, k:k + 1, :] = jnp.sum(acc[p][k], axis=0, keepdims=True)
            db_ref[p] = jnp.sum(acc[p][3], axis=0, keepdims=True)

    sl = lambda off, r: pl.BlockSpec((r, LANES), lambda j, off=off: (0, j + off))
    out = lambda r: pl.BlockSpec((2, r, LANES), lambda j: (0, 0, j))
    pad = pltpu.VMEM((S + HF, LANES), F32)
    return _ticked(_call(body, name=name, grid=(nf,),
                 in_specs=[sl(0, S), sl(0, S), sl(nf, S), sl(0, 3), sl(nf, 3), sl(0, 1), sl(nf, 1)],
                 out_specs=[out(S), out(3), out(1)],
                 out_shape=[_sds((2, S, F), BF16), _sds((2, 3, F), F32), _sds((2, 1, F), F32)],
                 scratch=[pad, pad, pad], dims=("parallel",))(dact, h, h, cw, cw, cb, cb), 320.0 * S / 4096)


def _swap_half(x):
    lane = lax.broadcasted_iota(jnp.int32, x.shape, 1)
    return jnp.where((lane % ROPE) < ROPE // 2, pltpu.roll(x, LANES - ROPE // 2, 1), pltpu.roll(x, ROPE // 2, 1))


def _rope(x, cos, sin_s):
    return x * cos + _swap_half(x) * sin_s


def _mla_prep(h, gq, gkv, cos, sin_s, *, name):
    S, W = h.shape
    tr = _tile(S, 512, 8)

    def body(h_ref, gq_ref, gkv_ref, cos_ref, sin_ref, cq_ref, ckv_ref, kr_ref, rq_ref, rkv_ref):
        cq = h_ref[:, 0:QRANK]
        rq = lax.rsqrt(jnp.mean(cq * cq, axis=-1, keepdims=True) + RMS_EPS)
        cq_ref[...] = (cq * rq * gq_ref[...]).astype(BF16)
        rq_ref[...] = rq
        ckv = h_ref[:, QRANK:QRANK + KVRANK]
        rkv = lax.rsqrt(jnp.mean(ckv * ckv, axis=-1, keepdims=True) + RMS_EPS)
        ckv_ref[...] = (ckv * rkv * gkv_ref[...]).astype(BF16)
        rkv_ref[...] = rkv
        kr = _rope(h_ref[:, QRANK + KVRANK:W], cos_ref[...], sin_ref[...])
        lane = lax.broadcasted_iota(jnp.int32, kr.shape, 1)
        kr = jnp.where(lane < ROPE, kr, 0.0)
        kr_ref[...] = (kr + pltpu.roll(kr, ROPE, 1)).astype(BF16)

    row = lambda w: pl.BlockSpec((tr, w), lambda i: (i, 0))
    vec = lambda w: pl.BlockSpec((1, w), lambda i: (0, 0))
    return _call(body, name=name, grid=(S // tr,),
                 in_specs=[row(W), vec(QRANK), vec(KVRANK), row(LANES), row(LANES)],
                 out_specs=[row(QRANK), row(KVRANK), row(LANES), row(1), row(1)],
                 out_shape=[_sds((S, QRANK), BF16), _sds((S, KVRANK), BF16), _sds((S, LANES), BF16), _sds((S, 1), F32), _sds((S, 1), F32)],
                 dims=("parallel",))(h, gq, gkv, cos, sin_s)


def _mla_prep_bwd(dcqn, dckvn, dkr2, h, rq, rkv, gq, gkv, cos, sin_neg, *, name):
    S, W = h.shape
    tr = _tile(S, 512, 8)

    def rms_bwd(dy, c, r, g):
        n = c * r
        dn = dy * g
        return r * (dn - n * jnp.mean(dn * n, axis=-1, keepdims=True)), jnp.sum(dy * n, axis=0, keepdims=True)

    def body(dq_ref, dkv_ref, dkr_ref, h_ref, rq_ref, rkv_ref, gq_ref, gkv_ref, cos_ref, sin_ref, dh_ref, dgq_ref, dgkv_ref):
        @pl.when(pl.program_id(0) == 0)
        def _():
            dgq_ref[...] = jnp.zeros_like(dgq_ref)
            dgkv_ref[...] = jnp.zeros_like(dgkv_ref)

        dcq, dg = rms_bwd(dq_ref[...], h_ref[:, 0:QRANK], rq_ref[...], gq_ref[...])
        dgq_ref[...] += dg
        dh_ref[:, 0:QRANK] = dcq.astype(BF16)
        dckv, dg = rms_bwd(dkv_ref[...], h_ref[:, QRANK:QRANK + KVRANK], rkv_ref[...], gkv_ref[...])
        dgkv_ref[...] += dg
        dh_ref[:, QRANK:QRANK + KVRANK] = dckv.astype(BF16)
        dk = dkr_ref[...]
        dk = dk + pltpu.roll(dk, ROPE, 1)
        dk = _rope(dk, cos_ref[...], sin_ref[...])
        lane = lax.broadcasted_iota(jnp.int32, dk.shape, 1)
        dh_ref[:, QRANK + KVRANK:W] = jnp.where(lane < ROPE, dk, 0.0).astype(BF16)

    row = lambda w: pl.BlockSpec((tr, w), lambda i: (i, 0))
    vec = lambda w: pl.BlockSpec((1, w), lambda i: (0, 0))
    return _call(body, name=name, grid=(S // tr,),
                 in_specs=[row(QRANK), row(KVRANK), row(LANES), row(W), row(1), row(1), vec(QRANK), vec(KVRANK), row(LANES), row(LANES)],
                 out_specs=[row(W), vec(QRANK), vec(KVRANK)],
                 out_shape=[_sds((S, W), BF16), _sds((1, QRANK), F32), _sds((1, KVRANK), F32)],
                 dims=("arbitrary",))(dcqn, dckvn, dkr2, h, rq, rkv, gq, gkv, cos, sin_neg)


QPAIR = 2 * NOPE + 2 * ROPE
KVPAIR = 2 * (NOPE + VDIM)


ATT_T = 512
LOG2_E = 1.4426950408889634


def _att_scores(kc, qc, allowed):
    s = lax.dot_general(kc, qc, _DN["nt"], preferred_element_type=F32) * (SM_SCALE * LOG2_E)
    return s if allowed is None else jnp.where(allowed, s, -jnp.inf)


def _att_allowed(k0, q0, t):
    kpos = k0 + lax.broadcasted_iota(jnp.int32, (t, t), 0)
    qpos = q0 + lax.broadcasted_iota(jnp.int32, (t, t), 1)
    return (kpos // CHUNK) <= (qpos // CHUNK)


def _head_mask(r, hh):
    lane = lax.broadcasted_iota(jnp.int32, r.shape, 1)
    keep = (lane < ROPE) if hh == 0 else (lane >= ROPE)
    return jnp.where(keep, r, jnp.zeros_like(r))


def _att_q(q_ref):
    r = q_ref[:, 2 * NOPE:QPAIR]
    return [jnp.concatenate([q_ref[:, hh * NOPE:(hh + 1) * NOPE], _head_mask(r, hh)], axis=1) for hh in range(2)]


def _attn_fwd(q, kv, kr2, cos, sin_s, *, name):
    S = q.shape[0]
    t = _tile(S, ATT_T, 8)
    nq = S // t
    npair = HEADS // 2

    def body(qf_ref, kv_ref, kr_ref, cos_ref, sin_ref, o_ref, lse_ref, q_ref):
        i = pl.program_id(1)
        q0 = i * t
        q_ref[:, 0:2 * NOPE] = qf_ref[:, 0:2 * NOPE].astype(BF16)
        q_ref[:, 2 * NOPE:QPAIR] = _rope(qf_ref[:, 2 * NOPE:QPAIR], cos_ref[...], sin_ref[...]).astype(BF16)
        qc = _att_q(q_ref)

        def step(j, carry, masked):
            k0 = pl.multiple_of(j * t, t)
            rows = pl.ds(k0, t)
            kr = kr_ref[rows, :]
            allowed = _att_allowed(k0, q0, t) if masked else None
            out = []
            for hh in range(2):
                m, l, acc = carry[hh]
                c0 = hh * (NOPE + VDIM)
                kc = jnp.concatenate([kv_ref[rows, c0:c0 + NOPE], kr], axis=1)
                v = kv_ref[rows, c0 + NOPE:c0 + NOPE + VDIM]
                s = _att_scores(kc, qc[hh], allowed)
                m_new = jnp.maximum(m, jnp.max(s, axis=0, keepdims=True))
                p = jnp.exp2(s - m_new)
                a = jnp.exp2(m - m_new)
                l = a * l + jnp.sum(p, axis=0, keepdims=True)
                acc = a * acc + lax.dot_general(v, p.astype(BF16), _DN["tn"], preferred_element_type=F32)
                out.append((m_new, l, acc))
            return tuple(out)

        one = (jnp.full((1, t), -jnp.inf, F32), jnp.zeros((1, t), F32), jnp.zeros((VDIM, t), F32))
        carry = lax.fori_loop(0, i, lambda j, c: step(j, c, False), (one, one))
        carry = step(i, carry, True)
        for hh in range(2):
            m, l, acc = carry[hh]
            o_ref[:, hh * VDIM:(hh + 1) * VDIM] = jnp.transpose(acc / l).astype(BF16)
            lse_ref[hh:hh + 1, :] = m + jnp.log2(l)

    qspec = pl.BlockSpec((t, QPAIR), lambda p, i: (i, p))
    tab = pl.BlockSpec((t, LANES), lambda p, i: (i, 0))
    return _call(body, name=name, grid=(npair, nq),
                 in_specs=[qspec, pl.BlockSpec((S, KVPAIR), lambda p, i: (0, p)), pl.BlockSpec((S, LANES), lambda p, i: (0, 0)), tab, tab],
                 out_specs=[pl.BlockSpec((t, 2 * VDIM), lambda p, i: (i, p)), pl.BlockSpec((None, 2, t), lambda p, i: (p, 0, i)), qspec],
                 out_shape=[_sds((S, HEADS * VDIM), BF16), _sds((npair, 2, S), F32), _sds(q.shape, BF16)],
                 dims=("parallel", "arbitrary"))(q, kv, kr2, cos, sin_s)


def _attn_bwd(qb, kv, kr2, o, do, lse, cos, sin_neg, *, name):
    S = qb.shape[0]
    t = _tile(S, ATT_T, 8)
    nq = S // t
    npair = HEADS // 2

    def body(q_ref, kv_ref, kr_ref, o_ref, do_ref, lse_ref, cos_ref, sin_ref, dq_ref, dkv_ref, dkr_ref, acc_ref):
        pid = pl.program_id(0)
        i = pl.program_id(1)
        q0 = i * t

        @pl.when(i == 0)
        def _():
            acc_ref[...] = jnp.zeros_like(acc_ref)

        @pl.when((i == 0) & (pid == 0))
        def _():
            dkr_ref[...] = jnp.zeros_like(dkr_ref)

        qc = _att_q(q_ref)
        dov = [do_ref[:, hh * VDIM:(hh + 1) * VDIM] for hh in range(2)]
        dsum = [jnp.sum(jnp.transpose(dov[hh].astype(F32) * o_ref[:, hh * VDIM:(hh + 1) * VDIM].astype(F32)), axis=0, keepdims=True)
                for hh in range(2)]
        lse_v = [lse_ref[hh:hh + 1, :] for hh in range(2)]

        def step(j, carry, masked):
            k0 = pl.multiple_of(j * t, t)
            rows = pl.ds(k0, t)
            kr = kr_ref[rows, :]
            allowed = _att_allowed(k0, q0, t) if masked else None
            out = []
            dkr = jnp.zeros((t, LANES), F32)
            for hh in range(2):
                c0 = hh * (NOPE + VDIM)
                kc = jnp.concatenate([kv_ref[rows, c0:c0 + NOPE], kr], axis=1)
                v = kv_ref[rows, c0 + NOPE:c0 + NOPE + VDIM]
                p = jnp.exp2(_att_scores(kc, qc[hh], allowed) - lse_v[hh])
                dp = lax.dot_general(v, dov[hh], _DN["nt"], preferred_element_type=F32)
                ds = (p * (dp - dsum[hh]) * SM_SCALE).astype(BF16)
                acc_ref[rows, c0 + NOPE:c0 + NOPE + VDIM] += jnp.dot(p.astype(BF16), dov[hh], preferred_element_type=F32)
                dkc = jnp.dot(ds, qc[hh], preferred_element_type=F32)
                acc_ref[rows, c0:c0 + NOPE] += dkc[:, 0:NOPE]
                dkr = dkr + dkc[:, NOPE:]
                out.append(carry[hh] + lax.dot_general(ds, kc, _DN["tn"], preferred_element_type=F32))
            dkr_ref[rows, :] += dkr
            return tuple(out)

        zero = jnp.zeros((t, NOPE + LANES), F32)
        carry = lax.fori_loop(0, i, lambda j, c: step(j, c, False), (zero, zero))
        dqc = step(i, carry, True)
        for hh in range(2):
            dq_ref[:, hh * NOPE:(hh + 1) * NOPE] = dqc[hh][:, 0:NOPE].astype(BF16)
        dr = _head_mask(dqc[0][:, NOPE:], 0) + _head_mask(dqc[1][:, NOPE:], 1)
        dq_ref[:, 2 * NOPE:QPAIR] = _rope(dr, cos_ref[...], sin_ref[...]).astype(BF16)

        @pl.when(i == nq - 1)
        def _():
            dkv_ref[...] = acc_ref[...].astype(BF16)

    qspec = pl.BlockSpec((t, QPAIR), lambda p, i: (i, p))
    hspec = pl.BlockSpec((t, 2 * VDIM), lambda p, i: (i, p))
    kvspec = pl.BlockSpec((S, KVPAIR), lambda p, i: (0, p))
    krspec = pl.BlockSpec((S, LANES), lambda p, i: (0, 0))
    tab = pl.BlockSpec((t, LANES), lambda p, i: (i, 0))
    return _ticked(_call(body, name=name, grid=(npair, nq),
                 in_specs=[qspec, kvspec, krspec, hspec, hspec, pl.BlockSpec((None, 2, t), lambda p, i: (p, 0, i)), tab, tab],
                 out_specs=[qspec, kvspec, krspec],
                 out_shape=[_sds(qb.shape, BF16), _sds(kv.shape, BF16), _sds((S, LANES), F32)],
                 scratch=[pltpu.VMEM((S, KVPAIR), F32)], dims=("arbitrary", "arbitrary"))(qb, kv, kr2, o, do, lse, cos, sin_neg),
                   640.0 * (S / 4096) ** 2)


def _coords():
    return lax.axis_index("x"), lax.axis_index("y"), lax.axis_index("c")


def _all_gather_body(n, shake):
    def body(*refs):
        ins, outs = refs[:n], refs[n:2 * n]
        send_sems, recv_sems, local_sems = refs[2 * n:]
        x, y, c = _coords()
        me, sib = (x, y, c), (x, y, 1 - c)
        chips = [(1 - x, y), (x, 1 - y), (1 - x, 1 - y)]
        if shake:
            _handshake([sib] + [(*chip, c) for chip in chips])

        def copy(a, k, block, to, src=None):
            dst = outs[a].at[4 * block[0] + 2 * block[1] + block[2]]
            return pltpu.make_async_remote_copy(src_ref=dst if src is None else src, dst_ref=dst,
                                                send_sem=send_sems.at[a, k], recv_sem=recv_sems.at[a, k],
                                                device_id=to, device_id_type=MESH)

        mine = [pltpu.make_async_copy(ins[a], outs[a].at[4 * x + 2 * y + c], local_sems.at[a]) for a in range(n)]
        for cp in mine:
            cp.start()
        sends = []
        for a in range(n):
            sends.append(copy(a, 0, me, sib, src=ins[a]))
            sends += [copy(a, 1 + j, me, (*chip, c), src=ins[a]) for j, chip in enumerate(chips)]
        for cp in sends:
            cp.start()
        for j, chip in enumerate(chips):
            for a in range(n):
                copy(a, 1 + j, (*chip, c), me).wait_recv()
                fwd = copy(a, 4 + j, (*chip, c), sib)
                fwd.start()
                sends.append(fwd)
        for a in range(n):
            copy(a, 0, sib, me).wait_recv()
            for j, chip in enumerate(chips):
                copy(a, 4 + j, (*chip, 1 - c), me).wait_recv()
        for cp in sends:
            cp.wait_send()
        for cp in mine:
            cp.wait()

    return body


def _all_gather(arrs, *, name):
    n = len(arrs)
    outs = _call(_all_gather_body(n, False), name=name, in_specs=[HBM_SPEC] * n, out_specs=[HBM_SPEC] * n,
                 out_shape=[_sds((N_DEV,) + a.shape, a.dtype) for a in arrs],
                 scratch=[pltpu.SemaphoreType.DMA((n, 7)), pltpu.SemaphoreType.DMA((n, 7)), pltpu.SemaphoreType.DMA((n,))])(*arrs)
    return list(outs)


def _sequencer_call(body, *, name, out_type, n_sems, collective_id):
    return pl.kernel(body, out_type=out_type, mesh=plsc.ScalarSubcoreMesh(axis_name="sq", num_cores=1), name=name,
                     scratch_types=[pltpu.SemaphoreType.DMA(n_sems), pltpu.SemaphoreType.DMA(n_sems), pltpu.SemaphoreType.DMA((n_sems[0],))],
                     compiler_params=pltpu.CompilerParams(collective_id=collective_id))


def _handshake(peers):
    barrier = pltpu.get_barrier_semaphore()
    for p in peers:
        pl.semaphore_signal(barrier, inc=1, device_id=p, device_id_type=MESH)
    pl.semaphore_wait(barrier, len(peers))


AG_ID, PAIR_ID, CHIP_ID = 1, 2, 3


def _all_gather_sc(arrs, *, name):
    n = len(arrs)
    outs = _sequencer_call(_all_gather_body(n, True), name=name, out_type=[_sds((N_DEV,) + a.shape, a.dtype) for a in arrs],
                           n_sems=(n, 7), collective_id=AG_ID)(*arrs)
    return list(outs)


def _pair_exchange_sc(gs, *, name):
    n = len(gs)

    def body(*refs):
        ins, outs = refs[:n], refs[n:2 * n]
        send_sems, recv_sems, _ = refs[2 * n:]
        x, y, c = _coords()
        sib = (x, y, 1 - c)
        _handshake([sib])
        cps = []
        for a in range(n):
            for j in range(4):
                cps.append(pltpu.make_async_remote_copy(src_ref=ins[a].at[2 * j + (1 - c)], dst_ref=outs[a].at[j],
                                                        send_sem=send_sems.at[a, j], recv_sem=recv_sems.at[a, j],
                                                        device_id=sib, device_id_type=MESH))
        for cp in cps:
            cp.start()
        for cp in cps:
            cp.wait()

    outs = _sequencer_call(body, name=name, out_type=[_sds((4,) + g.shape[1:], g.dtype) for g in gs],
                           n_sems=(n, 4), collective_id=PAIR_ID)(*gs)
    return list(outs)


def _chip_exchange_sc(ps, *, name):
    n = len(ps)

    def body(*refs):
        ins, outs = refs[:n], refs[n:2 * n]
        send_sems, recv_sems, _ = refs[2 * n:]
        x, y, c = _coords()
        chips = [(1 - x, y), (x, 1 - y), (1 - x, 1 - y)]
        _handshake([(*chip, c) for chip in chips])
        cps = []
        for a in range(n):
            for r, (px, py) in enumerate(chips):
                cps.append(pltpu.make_async_remote_copy(src_ref=ins[a].at[2 * px + py], dst_ref=outs[a].at[r],
                                                        send_sem=send_sems.at[a, r], recv_sem=recv_sems.at[a, r],
                                                        device_id=(px, py, c), device_id_type=MESH))
        for cp in cps:
            cp.start()
        for cp in cps:
            cp.wait()

    outs = _sequencer_call(body, name=name, out_type=[_sds((3,) + p.shape[1:], p.dtype) for p in ps],
                           n_sems=(n, 3), collective_id=CHIP_ID)(*ps)
    return list(outs)


def _row_tile(R, C, budget=1 << 20, mult=16):
    want = max(mult, budget // (4 * C))
    if R <= want:
        return R
    t = (want // mult) * mult
    while t >= mult:
        if R % t == 0:
            return t
        t -= mult
    return R


RS_TILE_BYTES = 6 << 20


def _pair_sum(g, l1, c_idx, *, name):
    _, R, C = g.shape
    tr = _row_tile(R, C, budget=RS_TILE_BYTES)
    g4 = g.reshape(4, 2, R, C)

    def body(c_ref, g_ref, l_ref, o_ref):
        o_ref[...] = (g_ref[...].astype(F32) + l_ref[...].astype(F32)).astype(o_ref.dtype)

    return _call(body, name=name, n_prefetch=1, grid=(4, R // tr),
                 in_specs=[pl.BlockSpec((None, None, tr, C), lambda j, r, c_ref: (j, c_ref[0], r, 0)),
                           pl.BlockSpec((None, tr, C), lambda j, r, c_ref: (j, r, 0))],
                 out_specs=pl.BlockSpec((None, tr, C), lambda j, r, c_ref: (j, r, 0)),
                 out_shape=_sds((4, R, C), g.dtype))(c_idx, g4, l1)


def _chip_sum(p, l2, chip_idx, *, name):
    _, R, C = p.shape
    tr = _row_tile(R, C, budget=RS_TILE_BYTES)

    def body(j_ref, p_ref, l_ref, o_ref):
        o_ref[...] = ((p_ref[...].astype(F32) + l_ref[0].astype(F32)) + l_ref[1].astype(F32)) + l_ref[2].astype(F32)

    return _call(body, name=name, n_prefetch=1, grid=(R // tr,),
                 in_specs=[pl.BlockSpec((None, tr, C), lambda r, j_ref: (j_ref[0], r, 0)),
                           pl.BlockSpec((3, tr, C), lambda r, j_ref: (0, r, 0))],
                 out_specs=pl.BlockSpec((tr, C), lambda r, j_ref: (r, 0)), out_shape=_sds((R, C), F32))(chip_idx, p, l2)


class _Reducer:
    PAIR_US = (12.0, 7.0)
    CHIP_US = (15.0, 43.0)

    def __init__(self, c_idx, chip_idx):
        self.c_idx, self.chip_idx = c_idx, chip_idx
        self.units, self.out, self.sc_free = [], {}, 0.0

    def _sequencer_done(self, cost):
        self.sc_free = max(self.sc_free, _SCHED.clock) + cost[0] + cost[1] * self._mb
        return self.sc_free

    def submit(self, name, g):
        R = g.shape[0] // N_DEV
        g8 = g.reshape((N_DEV, R, g.shape[1]))
        self._mb = R * g.shape[1] * g.dtype.itemsize / 1e6
        (l1,) = _pair_exchange_sc([g8], name=f"rs_pair_{name}")
        self.units.append(dict(name=name, g8=g8, l1=l1, mb=self._mb, stage=1, ready=self._sequencer_done(self.PAIR_US)))

    def advance(self, force=False):
        for u in self.units:
            if not (force or _SCHED.clock >= u["ready"]):
                continue
            if u["stage"] == 1:
                p = _pair_sum(u.pop("g8"), u.pop("l1"), self.c_idx, name=f"rs_psum_{u['name']}")
                (l2,) = _chip_exchange_sc([p], name=f"rs_chip_{u['name']}")
                self._mb = u["mb"]
                u.update(p=p, l2=l2, stage=2, ready=self._sequencer_done(self.CHIP_US))
            elif u["stage"] == 2:
                self.out[u["name"]] = _chip_sum(u.pop("p"), u.pop("l2"), self.chip_idx, name=f"rs_csum_{u['name']}")
                u["stage"] = 3

    def finish(self):
        self.advance(force=True)
        self.advance(force=True)
        return self.out


def _grad_ready(name, g):
    _SCHED.big[name] = g
    if _SCHED.reducer is not None:
        _SCHED.reducer.submit(name, g)
    return g


def _sum8(a, *, name):
    _, R, C = a.shape
    tr = _row_tile(R, C, budget=1 << 18, mult=8)

    def body(a_ref, o_ref):
        acc = a_ref[0]
        for k in range(1, N_DEV):
            acc = acc + a_ref[k]
        o_ref[...] = acc

    return _call(body, name=name, grid=(R // tr,), in_specs=[pl.BlockSpec((N_DEV, tr, C), lambda r: (0, r, 0))],
                 out_specs=pl.BlockSpec((tr, C), lambda r: (r, 0)), out_shape=_sds((R, C), F32), dims=("parallel",))(a)


def _adamw(w, g, m, v, *, name):
    R, C = w.shape
    tr = _row_tile(R, C, budget=2 << 20, mult=8)
    c1 = 1.0 - ADAM_B1 ** ADAM_STEP
    c2 = 1.0 - ADAM_B2 ** ADAM_STEP

    def body(w_ref, g_ref, m_ref, v_ref, d_ref, mo_ref, vo_ref):
        gv = g_ref[...]
        mn = ADAM_B1 * m_ref[...] + (1.0 - ADAM_B1) * gv
        vn = ADAM_B2 * v_ref[...] + (1.0 - ADAM_B2) * (gv * gv)
        mo_ref[...] = mn
        vo_ref[...] = vn
        d_ref[...] = -ADAM_LR * ((mn / c1) / (jnp.sqrt(vn / c2) + ADAM_EPS) + ADAM_WD * w_ref[...])

    blk = pl.BlockSpec((tr, C), lambda r: (r, 0))
    return _call(body, name=name, grid=(R // tr,), in_specs=[blk] * 4, out_specs=[blk] * 3,
                 out_shape=[_sds((R, C), F32)] * 3, dims=("parallel",))(w, g, m, v)


def _gmlp_fwd(xb, W, sp, tag):
    zp = _mm(xb, W["w_in_t"], "nt", name=f"gm_zp_{tag}", out_dtype=BF16)
    p, s, vhat, rstd = _gm_mid_fwd(zp, sp["ln_g"], sp["ln_b"], sp["w_s"], sp["b_st"], name=f"gm_mid_{tag}")
    m = _mm(p, W["w_out"], "nn", name=f"gm_out_{tag}", out_dtype=F32, tm=512, tk=p.shape[1])
    return m, dict(xb=xb, zp=zp, p=p, s=s, vhat=vhat, rstd=rstd)


def _gmlp_bwd(drb, dr, W, sp, sv, tag, on_small=None):
    _grad_ready(f"{tag}_w_out", _mm(sv["p"], drb, "tn", name=f"gm_dwout_{tag}", out_dtype=BF16, tk=TOKENS_K))
    dp = _mm(drb, W["w_out"], "nt", name=f"gm_dp_{tag}", out_dtype=BF16)
    dzp, dws, dbs, dlg, dlb = _gm_mid_bwd(dp, sv["zp"], sv["s"], sv["vhat"], sv["rstd"], sp["ln_g"], sp["ln_b"], sp["w_s"],
                                          name=f"gm_midb_{tag}")
    small = dict(w_s=dws, b_s=dbs, ln_g=dlg, ln_b=dlb)
    if on_small is not None:
        on_small(small)
    _grad_ready(f"{tag}_w_in_t", _mm(dzp, sv["xb"], "tn", name=f"gm_dwin_{tag}", out_dtype=BF16, tk=TOKENS_K))
    dx = _mm(dzp, W["w_in_t"], "nn", name=f"gm_dx_{tag}", out_dtype=F32, res=dr, res_scale=ALPHA)
    return dx, small


def _mla_fwd(xb, W, sp, rope, tag):
    cos, sin_s, _ = rope
    h = _mm(xb, W["w_in"], "nn", name=f"mla_h_{tag}", out_dtype=F32, tn=W["w_in"].shape[1])
    cqn, ckvn, kr2, rq, rkv = _mla_prep(h, sp["gq"], sp["gkv"], cos, sin_s, name=f"mla_prep_{tag}")
    q = _mm(cqn, W["w_qb_t"], "nt", name=f"mla_q_{tag}", out_dtype=F32)
    kv = _mm(ckvn, W["w_kvb_t"], "nt", name=f"mla_kv_{tag}", out_dtype=BF16)
    o, lse, qb = _attn_fwd(q, kv, kr2, cos, sin_s, name=f"mla_attn_{tag}")
    m = _mm(o, W["w_out"], "nn", name=f"mla_out_{tag}", out_dtype=F32)
    return m, dict(xb=xb, h=h, cqn=cqn, ckvn=ckvn, kr2=kr2, rq=rq, rkv=rkv, qb=qb, kv=kv, o=o, lse=lse)


def _mla_bwd(drb, dr, W, sp, rope, sv, tag):
    cos, _, sin_neg = rope
    _grad_ready(f"{tag}_w_out", _mm(sv["o"], drb, "tn", name=f"mla_dwout_{tag}", out_dtype=BF16, tk=TOKENS_K))
    do = _mm(drb, W["w_out"], "nt", name=f"mla_do_{tag}", out_dtype=BF16)
    dqb, dkv, dkr2 = _attn_bwd(sv["qb"], sv["kv"], sv["kr2"], sv["o"], do, sv["lse"], cos, sin_neg, name=f"mla_attnb_{tag}")
    _grad_ready(f"{tag}_w_qb_t", _mm(dqb, sv["cqn"], "tn", name=f"mla_dwqb_{tag}", out_dtype=BF16, tk=TOKENS_K))
    _grad_ready(f"{tag}_w_kvb_t", _mm(dkv, sv["ckvn"], "tn", name=f"mla_dwkvb_{tag}", out_dtype=BF16, tk=TOKENS_K))
    dcqn = _mm(dqb, W["w_qb_t"], "nn", name=f"mla_dcq_{tag}", out_dtype=F32, tk=dqb.shape[1])
    dckvn = _mm(dkv, W["w_kvb_t"], "nn", name=f"mla_dckv_{tag}", out_dtype=F32, tk=dkv.shape[1])
    dh, dgq, dgkv = _mla_prep_bwd(dcqn, dckvn, dkr2, sv["h"], sv["rq"], sv["rkv"], sp["gq"], sp["gkv"], cos, sin_neg,
                                  name=f"mla_prepb_{tag}")
    _grad_ready(f"{tag}_w_in", _mm(sv["xb"], dh, "tn", name=f"mla_dwin_{tag}", out_dtype=BF16, tn=dh.shape[1], tk=TOKENS_K))
    dx = _mm(dh, W["w_in"], "nt", name=f"mla_dx_{tag}", out_dtype=F32, tk=dh.shape[1], res=dr, res_scale=ALPHA)
    return dx, dict(gq=dgq, gkv=dgkv)


def _ffn_fwd(xb, W, sp, tag):
    S = xb.shape[0]
    F = W["w_down"].shape[0]
    h = _mm(xb, W["w_up_t"], "nt", name=f"ffn_h_{tag}", out_dtype=BF16)
    act = _ffn_mid_fwd(h, sp["cw"], sp["cb"], name=f"ffn_mid_{tag}")
    f = _mm(act, W["w_down"], "nn", name=f"ffn_out_{tag}", out_dtype=F32, tk=F // 2)
    return f, dict(xb=xb, h=h, act=act)


def _ffn_bwd(drb, dr, W, sp, sv, tag):
    F = W["w_down"].shape[0]
    S, D = drb.shape
    tf = _tile(F, FFN_TILE)
    _grad_ready(f"{tag}_w_down", _mm(sv["act"], drb, "tn", name=f"ffn_dwdown_{tag}", out_dtype=BF16, tm=tf, tn=512, tk=TOKENS_K))
    dact = _mm(drb, W["w_down"], "nt", name=f"ffn_dact_{tag}", out_dtype=BF16, tm=512, tn=F, tk=D, b_resident=True)
    dh, dcw, dcb = _ffn_mid_bwd(dact, sv["h"], sp["cw"], sp["cb"], name=f"ffn_midb_{tag}")
    dh = dh.reshape(2 * S, F)
    _grad_ready(f"{tag}_w_up_t", _mm(dh, sv["xb"], "tn", name=f"ffn_dwup_{tag}", out_dtype=BF16, tm=tf, tn=512, tk=TOKENS_K, a_parts=2))
    dx = _mm(dh, W["w_up_t"], "nn", name=f"ffn_dx_{tag}", out_dtype=F32, res=dr, res_scale=ALPHA, a_parts=2, tk=F // 2)
    dcw = jnp.concatenate([dcw[0], dcw[1]], axis=1)
    dcb = jnp.concatenate([dcb[0], dcb[1]], axis=1)
    return dx, dict(cw=dcw, cb=dcb)


def _rope_tables(S):
    half = ROPE // 2
    inv_freq = ROPE_THETA ** (-jnp.arange(half, dtype=F32) / half)
    ang = jnp.arange(S, dtype=F32)[:, None] * inv_freq[None, :]
    cos, sin = jnp.cos(ang), jnp.sin(ang)
    cos128 = jnp.concatenate([cos] * 4, axis=1)
    sin128 = jnp.concatenate([-sin, sin, -sin, sin], axis=1)
    return cos128, sin128, -sin128


def _fwd_bwd(x, tgt, Wm, Wf, spm, spf, ln, on_small=None, late_params=None):
    S, D = x.shape
    _SCHED.last, _SCHED.clock = None, 0.0
    rope = _rope_tables(S)
    xf, xb = x, x.astype(BF16)
    saved = []
    for i in range(DEPTH):
        if i % 2 == 0:
            m, svm = _gmlp_fwd(xb, Wm[i], spm[i], f"l{i}")
        else:
            m, svm = _mla_fwd(xb, Wm[i], spm[i], rope, f"l{i}")
        if i == 0 and late_params is not None:
            spm, spf = late_params(m)
        y, yb, xh1, rs1 = _ln_fwd(xf, m, ln["mix_g"][i], ln["mix_b"][i], name=f"ln_mix_{i}")
        f, svf = _ffn_fwd(yb, Wf[i], spf[i], f"l{i}")
        z, zb, xh2, rs2 = _ln_fwd(y, f, ln["ffn_g"][i], ln["ffn_b"][i], name=f"ln_ffn_{i}")
        saved.append((svm, xh1, rs1, svf, xh2, rs2))
        xf, xb = z, zb
    lp, dy = _loss_kernel(xf, tgt, name="loss")
    gm_small, gf_small, gln = [None] * DEPTH, [None] * DEPTH, [None] * DEPTH
    _SCHED.big = {}

    for i in reversed(range(DEPTH)):
        svm, xh1, rs1, svf, xh2, rs2 = saved[i]
        dr, drb, dg2, db2 = _ln_bwd(dy, xh2, rs2, ln["ffn_g"][i], name=f"lnb_ffn_{i}")
        dy, gf_small[i] = _ffn_bwd(drb, dr, Wf[i], spf[i], svf, f"l{i}")
        dr, drb, dg1, db1 = _ln_bwd(dy, xh1, rs1, ln["mix_g"][i], name=f"lnb_mix_{i}")
        gln[i] = dict(mix_g=dg1, mix_b=db1, ffn_g=dg2, ffn_b=db2)
        if i % 2 == 0:
            hook = None if (i > 0 or on_small is None) else (lambda ms: on_small(lp, [ms] + gm_small[1:], gf_small, gln))
            dy, gm_small[i] = _gmlp_bwd(drb, dr, Wm[i], spm[i], svm, f"l{i}", hook)
        else:
            dy, gm_small[i] = _mla_bwd(drb, dr, Wm[i], spm[i], rope, svm, f"l{i}")
    return lp, dy, _SCHED.big, gm_small, gf_small, gln


def _perm_q_rows(wt):
    hd = NOPE + ROPE
    return jnp.concatenate([wt[0:NOPE], wt[hd:hd + NOPE], wt[NOPE:hd], wt[hd + NOPE:2 * hd]], axis=0)


def _unperm_q_rows(wt):
    return jnp.concatenate([wt[0:NOPE], wt[2 * NOPE:2 * NOPE + ROPE], wt[NOPE:2 * NOPE], wt[2 * NOPE + ROPE:]], axis=0)


def _pad_cols(w, to):
    return jnp.pad(w, ((0, 0), (0, to - w.shape[1])))


def _pad_pieces(a, axis, piece, piece_pad):
    n = a.shape[axis] // piece
    sh = a.shape[:axis] + (n, piece) + a.shape[axis + 1:]
    pad = [(0, 0)] * (len(sh))
    pad[axis + 1] = (0, piece_pad - piece)
    return jnp.pad(a.reshape(sh), pad).reshape(a.shape[:axis] + (n * piece_pad,) + a.shape[axis + 1:])


def _unpad_pieces(a, axis, piece, piece_pad):
    n = a.shape[axis] // piece_pad
    sh = a.shape[:axis] + (n, piece_pad) + a.shape[axis + 1:]
    return lax.slice_in_dim(a.reshape(sh), 0, piece, axis=axis + 1).reshape(a.shape[:axis] + (n * piece,) + a.shape[axis + 1:])


def _pack(parts, rows_mult=8):
    flat = jnp.concatenate([p.reshape(-1).astype(F32) for p in parts])
    n = flat.shape[0]
    per = LANES * rows_mult
    tot = ((n + per - 1) // per) * per
    return jnp.pad(flat, (0, tot - n)).reshape(tot // LANES, LANES)


def _unpack(buf, shapes):
    flat = buf.reshape(-1)
    out, off = [], 0
    for s in shapes:
        n = 1
        for d in s:
            n *= d
        out.append(flat[off:off + n].reshape(s))
        off += n
    return out


def kernel(x, gm_w_in, gm_ln_g, gm_ln_b, gm_w_s, gm_b_s, gm_w_out, mla_w_in, mla_q_norm_g, mla_kv_norm_g, mla_w_q_b, mla_w_kv_b, mla_w_out, ffn_w_up, ffn_conv_w, ffn_conv_b, ffn_w_down, ln_mix_g, ln_mix_b, ln_ffn_g, ln_ffn_b, loss_target, m_gm_w_in, m_gm_ln_g, m_gm_ln_b, m_gm_w_s, m_gm_b_s, m_gm_w_out, m_mla_w_in, m_mla_q_norm_g, m_mla_kv_norm_g, m_mla_w_q_b, m_mla_w_kv_b, m_mla_w_out, m_ffn_w_up, m_ffn_conv_w, m_ffn_conv_b, m_ffn_w_down, m_ln_mix_g, m_ln_mix_b, m_ln_ffn_g, m_ln_ffn_b, v_gm_w_in, v_gm_ln_g, v_gm_ln_b, v_gm_w_s, v_gm_b_s, v_gm_w_out, v_mla_w_in, v_mla_q_norm_g, v_mla_kv_norm_g, v_mla_w_q_b, v_mla_w_kv_b, v_mla_w_out, v_ffn_w_up, v_ffn_conv_w, v_ffn_conv_b, v_ffn_w_down, v_ln_mix_g, v_ln_mix_b, v_ln_ffn_g, v_ln_ffn_b):
    S, D = x.shape[1], x.shape[2]
    xi, yi, ci = _coords()
    dev = 4 * xi + 2 * yi + ci
    c_idx = jnp.reshape(ci, (1,)).astype(jnp.int32)
    chip_idx = jnp.reshape(2 * xi + yi, (1,)).astype(jnp.int32)
    w_in_cols = mla_w_in.shape[2]
    w_in_pad = ((w_in_cols + LANES - 1) // LANES) * LANES
    n_gm, n_mla = gm_w_in.shape[0], mla_w_in.shape[0]

    cw_l = ffn_conv_w.shape[2]
    small_in = _pack([mla_q_norm_g, mla_kv_norm_g, ffn_conv_w])
    r_down = ffn_w_down.shape[1]
    r_pad = ((r_down + FFN_SHARD_MULT - 1) // FFN_SHARD_MULT) * FFN_SHARD_MULT

    def gather(shards, name):
        full = _all_gather_sc(list(shards.values()), name=name)
        return {k: f.reshape((-1, f.shape[2])) for k, f in zip(shards.keys(), full)}

    Wm, Wf = [None] * DEPTH, [None] * DEPTH
    for i in range(DEPTH):
        s = i // 2
        if i == 0:
            Wm[i] = gather(dict(w_in_t=gm_w_in[s].T.astype(BF16)), "ag_first")
            Wm[i].update(gather(dict(w_out=gm_w_out[s].astype(BF16), small=small_in), f"ag_gm_{i}_rest"))
            small_all = Wm[i].pop("small").reshape((N_DEV,) + small_in.shape)
        elif i % 2 == 0:
            Wm[i] = gather(dict(w_in_t=gm_w_in[s].T.astype(BF16), w_out=gm_w_out[s].astype(BF16)), f"ag_gm_{i}")
        else:
            Wm[i] = gather(dict(w_in=_pad_cols(mla_w_in[s], w_in_pad).astype(BF16),
                                w_qb_t=_perm_q_rows(mla_w_q_b[s].T).astype(BF16),
                                w_kvb_t=mla_w_kv_b[s].T.astype(BF16), w_out=mla_w_out[s].astype(BF16)), f"ag_mla_{i}")
        Wf[i] = gather(dict(w_up_t=_pad_pieces(ffn_w_up[i].T, 0, r_down, r_pad).astype(BF16),
                            w_down=_pad_pieces(ffn_w_down[i], 0, r_down, r_pad).astype(BF16)), f"ag_ffn_{i}")

    def small_params(anchor):
        spm, spf = [None] * DEPTH, [None] * DEPTH
        if anchor is not None:
            pack, _ = lax.optimization_barrier((small_all, anchor))
            parts = [_unpack(pack[k], [mla_q_norm_g.shape, mla_kv_norm_g.shape, ffn_conv_w.shape]) for k in range(N_DEV)]
            gq_full = jnp.concatenate([p[0] for p in parts], axis=1)
            gkv_full = jnp.concatenate([p[1] for p in parts], axis=1)
            cw_full = jnp.concatenate([p[2] for p in parts], axis=2)
        for i in range(DEPTH):
            s = i // 2
            if i % 2 == 0:
                spm[i] = dict(ln_g=gm_ln_g[s][None], ln_b=gm_ln_b[s][None], w_s=gm_w_s[s], b_st=gm_b_s[s].T)
            elif anchor is not None:
                spm[i] = dict(gq=gq_full[s][None], gkv=gkv_full[s][None])
            if anchor is not None:
                spf[i] = dict(cw=_pad_pieces(cw_full[i], 1, r_down, r_pad), cb=_pad_pieces(ffn_conv_b[i][None], 1, r_down, r_pad))
        return spm, spf

    spm, spf = small_params(None)
    ln = dict(mix_g=ln_mix_g[:, None], mix_b=ln_mix_b[:, None], ffn_g=ln_ffn_g[:, None], ffn_b=ln_ffn_b[:, None])

    small_box = {}

    def on_small(lp, gm_small, gf_small, gln):
        small_g = [
            jnp.stack([gm_small[2 * s]["ln_g"][0] for s in range(n_gm)]),
            jnp.stack([gm_small[2 * s]["ln_b"][0] for s in range(n_gm)]),
            jnp.stack([gm_small[2 * s]["w_s"] for s in range(n_gm)]),
            jnp.stack([gm_small[2 * s]["b_s"][:, :, 0] for s in range(n_gm)]),
            jnp.stack([_unpad_pieces(gf_small[i]["cb"], 1, r_down, r_pad)[0] for i in range(DEPTH)]),
            jnp.stack([gln[i]["mix_g"][0] for i in range(DEPTH)]),
            jnp.stack([gln[i]["mix_b"][0] for i in range(DEPTH)]),
            jnp.stack([gln[i]["ffn_g"][0] for i in range(DEPTH)]),
            jnp.stack([gln[i]["ffn_b"][0] for i in range(DEPTH)]),
            jnp.stack([gm_small[2 * s + 1]["gq"][0] for s in range(n_mla)]),
            jnp.stack([gm_small[2 * s + 1]["gkv"][0] for s in range(n_mla)]),
            jnp.stack([_unpad_pieces(gf_small[i]["cw"], 1, r_down, r_pad) for i in range(DEPTH)]),
        ]
        small_box["shapes"] = [g.shape for g in small_g]
        (small_box["all"],) = _all_gather_sc([_pack(small_g)], name="ag_small_grads")

    _SCHED.reducer = _Reducer(c_idx, chip_idx)
    lp, grad_x, _, _, _, _ = _fwd_bwd(x[0], loss_target[0], Wm, Wf, spm, spf, ln, on_small, small_params)
    red = _SCHED.reducer.finish()
    _SCHED.reducer = None
    rm = [{k: red[f"l{i}_{k}"] for k in Wm[i]} for i in range(DEPTH)]
    rf = [{k: red[f"l{i}_{k}"] for k in Wf[i]} for i in range(DEPTH)]

    g_gm_w_in = jnp.stack([rm[2 * s]["w_in_t"].T for s in range(n_gm)])
    g_gm_w_out = jnp.stack([rm[2 * s]["w_out"] for s in range(n_gm)])
    g_mla_w_in = jnp.stack([rm[2 * s + 1]["w_in"][:, :w_in_cols] for s in range(n_mla)])
    g_mla_w_q_b = jnp.stack([_unperm_q_rows(rm[2 * s + 1]["w_qb_t"]).T for s in range(n_mla)])
    g_mla_w_kv_b = jnp.stack([rm[2 * s + 1]["w_kvb_t"].T for s in range(n_mla)])
    g_mla_w_out = jnp.stack([rm[2 * s + 1]["w_out"] for s in range(n_mla)])
    g_ffn_w_up = jnp.stack([_unpad_pieces(rf[i]["w_up_t"], 0, r_down, r_pad).T for i in range(DEPTH)])
    g_ffn_w_down = jnp.stack([_unpad_pieces(rf[i]["w_down"], 0, r_down, r_pad) for i in range(DEPTH)])

    small_sum = _unpack(_sum8(small_box["all"], name="small_grad_sum"), small_box["shapes"])
    (g_gm_ln_g, g_gm_ln_b, g_gm_w_s, g_gm_b_s, g_ffn_conv_b, g_ln_mix_g, g_ln_mix_b, g_ln_ffn_g, g_ln_ffn_b,
     gq_all, gkv_all, cw_all) = small_sum
    (loss_all,) = _all_gather([_pack([jnp.reshape(0.5 * jnp.sum(lp) / D, (1,))])], name="ag_loss")
    loss = jnp.reshape(_sum8(loss_all, name="loss_sum")[0, 0], ())
    qn_l = mla_q_norm_g.shape[1]
    g_mla_q_norm_g = lax.dynamic_slice_in_dim(gq_all, dev * qn_l, qn_l, axis=1)
    g_mla_kv_norm_g = lax.dynamic_slice_in_dim(gkv_all, dev * qn_l, qn_l, axis=1)
    g_ffn_conv_w = lax.dynamic_slice_in_dim(cw_all, dev * cw_l, cw_l, axis=2)

    def adam_big(w, g, m, v, tag):
        sh = w.shape
        two = lambda a: a.reshape((-1, sh[-1]))
        d, mn, vn = _adamw(two(w), two(g), two(m), two(v), name=f"adamw_{tag}")
        return d.reshape(sh), mn.reshape(sh), vn.reshape(sh)

    big = [("gm_w_in", gm_w_in, g_gm_w_in, m_gm_w_in, v_gm_w_in), ("gm_w_out", gm_w_out, g_gm_w_out, m_gm_w_out, v_gm_w_out),
           ("mla_w_in", mla_w_in, g_mla_w_in, m_mla_w_in, v_mla_w_in), ("mla_w_q_b", mla_w_q_b, g_mla_w_q_b, m_mla_w_q_b, v_mla_w_q_b),
           ("mla_w_kv_b", mla_w_kv_b, g_mla_w_kv_b, m_mla_w_kv_b, v_mla_w_kv_b), ("mla_w_out", mla_w_out, g_mla_w_out, m_mla_w_out, v_mla_w_out),
           ("ffn_w_up", ffn_w_up, g_ffn_w_up, m_ffn_w_up, v_ffn_w_up), ("ffn_w_down", ffn_w_down, g_ffn_w_down, m_ffn_w_down, v_ffn_w_down)]
    res = {}
    for tag, w, g, m, v in big:
        res[tag] = (g,) + adam_big(w, g, m, v, tag)

    small = [("gm_ln_g", gm_ln_g, g_gm_ln_g, m_gm_ln_g, v_gm_ln_g), ("gm_ln_b", gm_ln_b, g_gm_ln_b, m_gm_ln_b, v_gm_ln_b),
             ("gm_w_s", gm_w_s, g_gm_w_s, m_gm_w_s, v_gm_w_s), ("gm_b_s", gm_b_s, g_gm_b_s, m_gm_b_s, v_gm_b_s),
             ("mla_q_norm_g", mla_q_norm_g, g_mla_q_norm_g, m_mla_q_norm_g, v_mla_q_norm_g),
             ("mla_kv_norm_g", mla_kv_norm_g, g_mla_kv_norm_g, m_mla_kv_norm_g, v_mla_kv_norm_g),
             ("ffn_conv_w", ffn_conv_w, g_ffn_conv_w, m_ffn_conv_w, v_ffn_conv_w), ("ffn_conv_b", ffn_conv_b, g_ffn_conv_b, m_ffn_conv_b, v_ffn_conv_b),
             ("ln_mix_g", ln_mix_g, g_ln_mix_g, m_ln_mix_g, v_ln_mix_g), ("ln_mix_b", ln_mix_b, g_ln_mix_b, m_ln_mix_b, v_ln_mix_b),
             ("ln_ffn_g", ln_ffn_g, g_ln_ffn_g, m_ln_ffn_g, v_ln_ffn_g), ("ln_ffn_b", ln_ffn_b, g_ln_ffn_b, m_ln_ffn_b, v_ln_ffn_b)]
    shapes = [t[1].shape for t in small]
    d_s, m_s, v_s = _adamw(_pack([t[1] for t in small]), _pack([t[2] for t in small]), _pack([t[3] for t in small]),
                           _pack([t[4] for t in small]), name="adamw_small")
    d_l, m_l, v_l = _unpack(d_s, shapes), _unpack(m_s, shapes), _unpack(v_s, shapes)
    for (tag, _, g, _, _), d, mn, vn in zip(small, d_l, m_l, v_l):
        res[tag] = (g, d, mn, vn)

    order = ["gm_w_in", "gm_ln_g", "gm_ln_b", "gm_w_s", "gm_b_s", "gm_w_out", "mla_w_in", "mla_q_norm_g", "mla_kv_norm_g",
             "mla_w_q_b", "mla_w_kv_b", "mla_w_out", "ffn_w_up", "ffn_conv_w", "ffn_conv_b", "ffn_w_down",
             "ln_mix_g", "ln_mix_b", "ln_ffn_g", "ln_ffn_b"]
    out = [loss, grad_x[None]]
    for q in range(4):
        out += [res[k][q] for k in order]
    return tuple(out)
```

```python
import jax
import jax.numpy as jnp
from jax import lax
from jax.experimental import pallas as pl
from jax.experimental.pallas import tpu as pltpu
from jax.experimental.pallas import tpu_sc as plsc

F32, BF16 = jnp.float32, jnp.bfloat16

DEPTH = 4
CHUNK = 64
GM_BLOCK = 128
GM_GROUPS = 8
HEADS = 16
NOPE, ROPE, VDIM = 128, 64, 128
QRANK, KVRANK = 512, 512
ROPE_THETA = 10000.0
SM_SCALE = (NOPE + ROPE) ** -0.5
ALPHA = (2 * DEPTH) ** 0.25
LN_EPS = 1e-5
RMS_EPS = 1e-6
ADAM_LR, ADAM_B1, ADAM_B2, ADAM_EPS, ADAM_WD, ADAM_STEP = 0.001, 0.9, 0.999, 1e-08, 0.01, 10

N_DEV = 8
LANES = 128
SUBLANES = 8
VMEM_LIMIT = 56 * 1024 * 1024
MESH = pl.DeviceIdType.MESH
HBM_SPEC = pl.BlockSpec(memory_space=pltpu.HBM)


class _Schedule:
    def __init__(self):
        self.last = None
        self.reducer = None
        self.big = {}
        self.clock = 0.0

    def tick(self, us):
        self.clock += us
        if self.reducer is not None:
            self.reducer.advance()


_SCHED = _Schedule()


def _ticked(out, us):
    _SCHED.tick(us)
    return out


def _call(body, *, name, out_shape, in_specs, out_specs, grid=None, scratch=(), dims=None, n_prefetch=0):
    kw = dict(vmem_limit_bytes=VMEM_LIMIT)
    if dims is not None:
        kw["dimension_semantics"] = dims
    cp = pltpu.CompilerParams(**kw)
    in_specs = list(in_specs)
    token = _SCHED.last
    if token is not None:
        pos = n_prefetch + len(in_specs)
        in_specs.append(pl.BlockSpec(memory_space=pl.ANY))
        inner = body

        def body(*refs):
            return inner(*refs[:pos], *refs[pos + 1:])

    if n_prefetch:
        gs = pltpu.PrefetchScalarGridSpec(num_scalar_prefetch=n_prefetch, grid=grid, in_specs=in_specs, out_specs=out_specs,
                                          scratch_shapes=list(scratch))
        fn = pl.pallas_call(body, name=name, grid_spec=gs, out_shape=out_shape, compiler_params=cp, interpret=False)
    else:
        extra = {} if grid is None else {"grid": grid}
        fn = pl.pallas_call(body, name=name, in_specs=in_specs, out_specs=out_specs, out_shape=out_shape,
                            scratch_shapes=list(scratch), compiler_params=cp, interpret=False, **extra)

    def run(*args):
        out = fn(*args, token) if token is not None else fn(*args)
        _SCHED.last = out[0] if isinstance(out, (list, tuple)) else out
        return out

    return run


def _tile(n, pref, mult=LANES):
    if n <= pref:
        return n
    t = (pref // mult) * mult
    while t >= mult:
        if n % t == 0:
            return t
        t -= mult
    return n


def _sds(shape, dtype):
    return jax.ShapeDtypeStruct(tuple(shape), dtype)


FFN_TILE = 1408
FFN_CHUNK = 64
FFN_SHARD_MULT = 64
MXU_FLOPS_PER_US = 8e8
TOKENS_K = 4096
_DN = {"nn": (((1,), (0,)), ((), ())), "nt": (((1,), (1,)), ((), ())), "tn": (((0,), (0,)), ((), ()))}


def _mm(a, b, mode, *, name, out_dtype, tm=1024, tn=1024, tk=2048, res=None, res_scale=1.0, a_parts=1, b_resident=False):
    if a_parts > 1:
        S_, F_ = a.shape[0] // a_parts, a.shape[1]
        a_shape = (S_, a_parts * F_)
    else:
        a_shape = a.shape
    if mode == "nn":
        (M, K), (K2, N) = a_shape, b.shape
    elif mode == "nt":
        (M, K), (N, K2) = a_shape, b.shape
    else:
        (K, M), (K2, N) = a_shape, b.shape
    assert K == K2, (name, a.shape, b.shape)
    tm, tn, tk = _tile(M, tm), _tile(N, tn), _tile(K, tk)
    nk = K // tk
    if a_parts > 1 and mode == "nn":
        per = F_ // tk
        assert F_ % tk == 0
        a_spec = pl.BlockSpec((tm, tk), lambda i, j, k: ((k // per) * (S_ // tm) + i, k % per))
    elif a_parts > 1:
        per = F_ // tm
        assert mode == "tn" and F_ % tm == 0
        a_spec = pl.BlockSpec((tk, tm), lambda i, j, k: ((i // per) * (S_ // tk) + k, i % per))
    elif mode == "tn":
        a_spec = pl.BlockSpec((tk, tm), lambda i, j, k: (k, i))
    else:
        a_spec = pl.BlockSpec((tm, tk), lambda i, j, k: (i, k))
    b_kw = {"pipeline_mode": pl.Buffered(1)} if b_resident else {}
    b_spec = (pl.BlockSpec((tn, tk), lambda i, j, k: (j, k), **b_kw) if mode == "nt"
              else pl.BlockSpec((tk, tn), lambda i, j, k: (k, j), **b_kw))
    in_specs = [a_spec, b_spec]
    args = [a, b]
    if res is not None:
        in_specs.append(pl.BlockSpec((tm, tn), lambda i, j, k: (i, j)))
        args.append(res)
    dn = _DN[mode]
    has_res = res is not None

    def body(*refs):
        a_ref, b_ref = refs[0], refs[1]
        r_ref = refs[2] if has_res else None
        o_ref = refs[2 + has_res]
        part = lax.dot_general(a_ref[...], b_ref[...], dn, preferred_element_type=F32)

        def finish(acc):
            if has_res:
                acc = acc + res_scale * r_ref[...]
            o_ref[...] = acc.astype(o_ref.dtype)

        if nk == 1:
            finish(part)
        else:
            acc_ref = refs[3 + has_res]
            k = pl.program_id(2)

            @pl.when(k == 0)
            def _():
                acc_ref[...] = part

            @pl.when(k > 0)
            def _():
                acc_ref[...] += part

            @pl.when(k == nk - 1)
            def _():
                finish(acc_ref[...])

    scratch = [pltpu.VMEM((tm, tn), F32)] if nk > 1 else []
    out = _call(body, name=name, grid=(M // tm, N // tn, nk), in_specs=in_specs,
                out_specs=pl.BlockSpec((tm, tn), lambda i, j, k: (i, j)), out_shape=_sds((M, N), out_dtype),
                scratch=scratch, dims=("parallel", "parallel", "arbitrary"))(*args)
    _SCHED.tick(2.0 * M * N * K / MXU_FLOPS_PER_US)
    return out


def _ln_fwd(x, m, g, b, *, name):
    S, D = x.shape
    tr = _tile(S, 256, 8)

    def body(x_ref, m_ref, g_ref, b_ref, y_ref, yb_ref, xh_ref, rs_ref):
        r = ALPHA * x_ref[...] + m_ref[...]
        mu = jnp.mean(r, axis=-1, keepdims=True)
        d = r - mu
        var = jnp.mean(d * d, axis=-1, keepdims=True)
        rstd = lax.rsqrt(var + LN_EPS)
        xh = d * rstd
        y = xh * g_ref[...] + b_ref[...]
        y_ref[...] = y
        yb_ref[...] = y.astype(BF16)
        xh_ref[...] = xh
        rs_ref[...] = rstd

    row = pl.BlockSpec((tr, D), lambda i: (i, 0))
    vec = pl.BlockSpec((1, D), lambda i: (0, 0))
    return _call(body, name=name, grid=(S // tr,), in_specs=[row, row, vec, vec],
                 out_specs=[row, row, row, pl.BlockSpec((tr, 1), lambda i: (i, 0))],
                 out_shape=[_sds((S, D), F32), _sds((S, D), BF16), _sds((S, D), F32), _sds((S, 1), F32)],
                 dims=("parallel",))(x, m, g, b)


def _ln_bwd(dy, xh, rstd, g, *, name):
    S, D = dy.shape
    tr = _tile(S, 256, 8)

    def body(dy_ref, xh_ref, rs_ref, g_ref, dr_ref, drb_ref, dg_ref, db_ref):
        @pl.when(pl.program_id(0) == 0)
        def _():
            dg_ref[...] = jnp.zeros_like(dg_ref)
            db_ref[...] = jnp.zeros_like(db_ref)

        dyv = dy_ref[...]
        xhv = xh_ref[...]
        dxh = dyv * g_ref[...]
        m1 = jnp.mean(dxh, axis=-1, keepdims=True)
        m2 = jnp.mean(dxh * xhv, axis=-1, keepdims=True)
        dr = rs_ref[...] * (dxh - m1 - xhv * m2)
        dr_ref[...] = dr
        drb_ref[...] = dr.astype(BF16)
        dg_ref[...] += jnp.sum(dyv * xhv, axis=0, keepdims=True)
        db_ref[...] += jnp.sum(dyv, axis=0, keepdims=True)

    row = pl.BlockSpec((tr, D), lambda i: (i, 0))
    vec = pl.BlockSpec((1, D), lambda i: (0, 0))
    return _ticked(_call(body, name=name, grid=(S // tr,), in_specs=[row, row, pl.BlockSpec((tr, 1), lambda i: (i, 0)), vec],
                         out_specs=[row, row, vec, vec],
                         out_shape=[_sds((S, D), F32), _sds((S, D), BF16), _sds((1, D), F32), _sds((1, D), F32)],
                         dims=("arbitrary",))(dy, xh, rstd, g), 45.0 * S / 4096)


def _loss_kernel(y, t, *, name):
    S, D = y.shape
    tr = _tile(S, 256, 8)

    def body(y_ref, t_ref, lp_ref, dy_ref):
        @pl.when(pl.program_id(0) == 0)
        def _():
            lp_ref[...] = jnp.zeros_like(lp_ref)

        e = y_ref[...] - t_ref[...]
        dy_ref[...] = e / D
        lp_ref[...] += jnp.sum(e * e, axis=0, keepdims=True)

    row = pl.BlockSpec((tr, D), lambda i: (i, 0))
    vec = pl.BlockSpec((1, D), lambda i: (0, 0))
    return _call(body, name=name, grid=(S // tr,), in_specs=[row, row], out_specs=[vec, row],
                 out_shape=[_sds((1, D), F32), _sds((S, D), F32)], dims=("arbitrary",))(y, t)


_GELU_C = 0.7978845608028654
_GELU_A = 0.044715


def _gelu(x):
    return 0.5 * x * (1.0 + jnp.tanh(_GELU_C * (x + _GELU_A * x * x * x)))


def _gelu_grad(x):
    x2 = x * x
    t = jnp.tanh(_GELU_C * (x + _GELU_A * x * x2))
    return 0.5 * (1.0 + t) + 0.5 * x * (1.0 - t * t) * (_GELU_C * (1.0 + 3.0 * _GELU_A * x2))


def _gelu_both(x):
    x2 = x * x
    t = jnp.tanh(_GELU_C * (x + _GELU_A * x * x2))
    half = 0.5 * (1.0 + t)
    return x * half, half + 0.5 * x * (1.0 - t * t) * (_GELU_C * (1.0 + 3.0 * _GELU_A * x2))


def _masked_ws(w):
    i = lax.broadcasted_iota(jnp.int32, w.shape, 0) // CHUNK
    j = lax.broadcasted_iota(jnp.int32, w.shape, 1) // CHUNK
    return jnp.where(j <= i, w, 0.0)


def _gm_mid_fwd(zp, ln_g, ln_b, w_s, b_st, *, name):
    S, H2 = zp.shape
    H = H2 // 2
    gd = H // GM_GROUPS
    nb = S // GM_BLOCK

    def body(zu_ref, zv_ref, g_ref, b_ref, w_ref, bs_ref, p_ref, s_ref, vh_ref, rs_ref):
        v = _gelu(zv_ref[...].astype(F32))
        mu = jnp.mean(v, axis=-1, keepdims=True)
        d = v - mu
        var = jnp.mean(d * d, axis=-1, keepdims=True)
        rstd = lax.rsqrt(var + LN_EPS)
        vh = d * rstd
        vh_ref[...] = vh.astype(BF16)
        rs_ref[...] = rstd
        vn = (vh * g_ref[...] + b_ref[...]).astype(BF16)
        bs = bs_ref[...]
        for gi in range(GM_GROUPS):
            cs = slice(gi * gd, (gi + 1) * gd)
            wm = _masked_ws(w_ref[gi]).astype(BF16)
            s = jnp.dot(wm, vn[:, cs], preferred_element_type=F32) + bs[:, gi:gi + 1]
            u = _gelu(zu_ref[:, cs].astype(F32))
            s_ref[:, cs] = s.astype(BF16)
            p_ref[:, cs] = (u * s).astype(BF16)

    blk = lambda c: pl.BlockSpec((GM_BLOCK, H), lambda n, c=c: (n, c))
    vec = pl.BlockSpec((1, H), lambda n: (0, 0))
    return _call(body, name=name, grid=(nb,),
                 in_specs=[blk(0), blk(1), vec, vec, pl.BlockSpec((GM_GROUPS, GM_BLOCK, GM_BLOCK), lambda n: (0, 0, 0)),
                           pl.BlockSpec((GM_BLOCK, GM_GROUPS), lambda n: (0, 0))],
                 out_specs=[blk(0), blk(0), blk(0), pl.BlockSpec((GM_BLOCK, 1), lambda n: (n, 0))],
                 out_shape=[_sds((S, H), BF16), _sds((S, H), BF16), _sds((S, H), BF16), _sds((S, 1), F32)],
                 dims=("parallel",))(zp, zp, ln_g, ln_b, w_s, b_st)


def _gm_mid_bwd(dp, zp, s, vhat, rstd, ln_g, ln_b, w_s, *, name):
    S, H2 = zp.shape
    H = H2 // 2
    gd = H // GM_GROUPS
    nb = S // GM_BLOCK

    def body(dp_ref, zu_ref, zv_ref, s_ref, vh_ref, rs_ref, g_ref, b_ref, w_ref,
             dz_ref, dw_ref, dbs_ref, dg_ref, db_ref, dvh_ref):
        @pl.when(pl.program_id(0) == 0)
        def _():
            dw_ref[...] = jnp.zeros_like(dw_ref)
            dbs_ref[...] = jnp.zeros_like(dbs_ref)
            dg_ref[...] = jnp.zeros_like(dg_ref)
            db_ref[...] = jnp.zeros_like(db_ref)

        m1 = jnp.zeros((GM_BLOCK, 1), F32)
        m2 = jnp.zeros((GM_BLOCK, 1), F32)
        for gi in range(GM_GROUPS):
            cs = slice(gi * gd, (gi + 1) * gd)
            dpg = dp_ref[:, cs].astype(F32)
            zu = zu_ref[:, cs].astype(F32)
            u, du_dz = _gelu_both(zu)
            ds = dpg * u
            du = dpg * s_ref[:, cs].astype(F32)
            dz_ref[:, cs] = (du * du_dz).astype(BF16)
            dsb = ds.astype(BF16)
            vh = vh_ref[:, cs].astype(F32)
            lg = g_ref[:, cs]
            vn = (vh * lg + b_ref[:, cs]).astype(BF16)
            wm = _masked_ws(w_ref[gi]).astype(BF16)
            dvn = lax.dot_general(wm, dsb, _DN["tn"], preferred_element_type=F32)
            dw_ref[gi] += lax.dot_general(dsb, vn, _DN["nt"], preferred_element_type=F32)
            dbs_ref[gi] += jnp.sum(ds, axis=1, keepdims=True)
            dg_ref[:, cs] += jnp.sum(dvn * vh, axis=0, keepdims=True)
            db_ref[:, cs] += jnp.sum(dvn, axis=0, keepdims=True)
            dvh = dvn * lg
            dvh_ref[:, cs] = dvh
            m1 = m1 + jnp.sum(dvh, axis=1, keepdims=True)
            m2 = m2 + jnp.sum(dvh * vh, axis=1, keepdims=True)
        m1 = m1 / H
        m2 = m2 / H
        rs = rs_ref[...]
        for gi in range(GM_GROUPS):
            cs = slice(gi * gd, (gi + 1) * gd)
            vh = vh_ref[:, cs].astype(F32)
            dv = rs * (dvh_ref[:, cs] - m1 - vh * m2)
            zv = zv_ref[:, cs].astype(F32)
            dz_ref[:, H + gi * gd:H + (gi + 1) * gd] = (dv * _gelu_grad(zv)).astype(BF16)

        @pl.when(pl.program_id(0) == nb - 1)
        def _():
            for gi in range(GM_GROUPS):
                dw_ref[gi] = _masked_ws(dw_ref[gi])

    blk = lambda c: pl.BlockSpec((GM_BLOCK, H), lambda n, c=c: (n, c))
    vec = pl.BlockSpec((1, H), lambda n: (0, 0))
    wspec = pl.BlockSpec((GM_GROUPS, GM_BLOCK, GM_BLOCK), lambda n: (0, 0, 0))
    return _ticked(_call(body, name=name, grid=(nb,),
                 in_specs=[blk(0), blk(0), blk(1), blk(0), blk(0), pl.BlockSpec((GM_BLOCK, 1), lambda n: (n, 0)), vec, vec, wspec],
                 out_specs=[pl.BlockSpec((GM_BLOCK, H2), lambda n: (n, 0)), wspec,
                            pl.BlockSpec((GM_GROUPS, GM_BLOCK, 1), lambda n: (0, 0, 0)), vec, vec],
                 out_shape=[_sds((S, H2), BF16), _sds((GM_GROUPS, GM_BLOCK, GM_BLOCK), F32),
                            _sds((GM_GROUPS, GM_BLOCK, 1), F32), _sds((1, H), F32), _sds((1, H), F32)],
                 scratch=[pltpu.VMEM((GM_BLOCK, H), F32)], dims=("arbitrary",))(dp, zp, zp, s, vhat, rstd, ln_g, ln_b, w_s),
                   280.0 * S / 4096)


def _shift_down(x, k):
    r = pltpu.roll(x, k, 0)
    rows = lax.broadcasted_iota(jnp.int32, (SUBLANES, x.shape[1]), 0)
    return jnp.concatenate([jnp.where(rows >= k, r[:SUBLANES], 0.0), r[SUBLANES:]], axis=0)


def _shift_up(x, k):
    n = x.shape[0]
    r = pltpu.roll(x, n - k, 0)
    rows = lax.broadcasted_iota(jnp.int32, (SUBLANES, x.shape[1]), 0)
    return jnp.concatenate([r[:n - SUBLANES], jnp.where(rows < SUBLANES - k, r[n - SUBLANES:], 0.0)], axis=0)


def _sigmoid(x):
    return 0.5 * (1.0 + jnp.tanh(0.5 * x))


def _conv(h, w, b):
    return w[0:1, :] * _shift_down(h, 2) + w[1:2, :] * _shift_down(h, 1) + w[2:3, :] * h + b


def _ffn_mid_fwd(h, cw, cb, *, name):
    S, F2 = h.shape
    F = F2 // 2
    nf = F // LANES

    def body(ha_ref, hg_ref, wa_ref, wg_ref, ba_ref, bg_ref, act_ref):
        a = _conv(ha_ref[...].astype(F32), wa_ref[...], ba_ref[...])
        g = _conv(hg_ref[...].astype(F32), wg_ref[...], bg_ref[...])
        act_ref[...] = (g * _sigmoid(g) * a).astype(BF16)

    sl = lambda off, r: pl.BlockSpec((r, LANES), lambda j, off=off: (0, j + off))
    return _call(body, name=name, grid=(nf,),
                 in_specs=[sl(0, S), sl(nf, S), sl(0, 3), sl(nf, 3), sl(0, 1), sl(nf, 1)],
                 out_specs=sl(0, S), out_shape=_sds((S, F), BF16), dims=("parallel",))(h, h, cw, cw, cb, cb)


def _ffn_mid_bwd(dact, h, cw, cb, *, name):
    S, F2 = h.shape
    F = F2 // 2
    nf = F // LANES

    R, HB, HF = FFN_CHUNK, 16, SUBLANES
    nc = S // R

    def body(da_ref, ha_ref, hg_ref, wa_ref, wg_ref, ba_ref, bg_ref, dh_ref, dw_ref, db_ref, a_s, g_s, d_s):
        wa, wg, ba, bg = wa_ref[...], wg_ref[...], ba_ref[...], bg_ref[...]
        zero_tail = jnp.zeros((HF, LANES), F32)
        a_s[S:S + HF, :] = zero_tail
        g_s[S:S + HF, :] = zero_tail
        d_s[S:S + HF, :] = zero_tail

        def conv_chunk(win, w, b):
            return (w[0:1, :] * pltpu.roll(win, 2, 0)[HB:] + w[1:2, :] * pltpu.roll(win, 1, 0)[HB:] + w[2:3, :] * win[HB:] + b)

        def pass1(r0, win_a, win_g):
            rows = pl.ds(r0, R)
            a_s[rows, :] = conv_chunk(win_a, wa, ba)
            g_s[rows, :] = conv_chunk(win_g, wg, bg)
            d_s[rows, :] = da_ref[rows, :].astype(F32)

        lead = jnp.zeros((HB, LANES), F32)
        pass1(0, jnp.concatenate([lead, ha_ref[0:R, :].astype(F32)], axis=0),
              jnp.concatenate([lead, hg_ref[0:R, :].astype(F32)], axis=0))

        @pl.loop(1, nc)
        def _(c):
            r0 = pl.multiple_of(c * R, R)
            win = pl.ds(pl.multiple_of(c * R - HB, HB), R + HB)
            pass1(r0, ha_ref[win, :].astype(F32), hg_ref[win, :].astype(F32))

        def fold(x):
            return jnp.sum(x.reshape(R // SUBLANES, SUBLANES, LANES), axis=0)

        def pass2(c, acc):
            r0 = pl.multiple_of(c * R, R)
            rows, win = pl.ds(r0, R), pl.ds(r0, R + HF)
            a, g, d = a_s[win, :], g_s[win, :], d_s[win, :]
            sg = _sigmoid(g)
            out = []
            for p, dc, w, h_ref in ((0, d * (g * sg), wa, ha_ref), (1, d * a * (sg * (1.0 + g * (1.0 - sg))), wg, hg_ref)):
                up1 = pltpu.roll(dc, R + HF - 1, 0)[0:R]
                up2 = pltpu.roll(dc, R + HF - 2, 0)[0:R]
                dc = dc[0:R]
                dh_ref[p, rows, :] = (w[2:3, :] * dc + w[1:2, :] * up1 + w[0:1, :] * up2).astype(BF16)
                hin = h_ref[rows, :].astype(F32)
                out.append((acc[p][0] + fold(up2 * hin), acc[p][1] + fold(up1 * hin), acc[p][2] + fold(dc * hin), acc[p][3] + fold(dc)))
            return tuple(out)

        z = jnp.zeros((SUBLANES, LANES), F32)
        acc = lax.fori_loop(0, nc, pass2, ((z, z, z, z), (z, z, z, z)))
        for p in range(2):
            for k in range(3):
                dw_ref[p, k:k + 1, :] = jnp.sum(acc[p][k], axis=0, keepdims=True)
            db_ref[p] = jnp.sum(acc[p][3], axis=0, keepdims=True)

    sl = lambda off, r: pl.BlockSpec((r, LANES), lambda j, off=off: (0, j + off))
    out = lambda r: pl.BlockSpec((2, r, LANES), lambda j: (0, 0, j))
    pad = pltpu.VMEM((S + HF, LANES), F32)
    return _ticked(_call(body, name=name, grid=(nf,),
                 in_specs=[sl(0, S), sl(0, S), sl(nf, S), sl(0, 3), sl(nf, 3), sl(0, 1), sl(nf, 1)],
                 out_specs=[out(S), out(3), out(1)],
                 out_shape=[_sds((2, S, F), BF16), _sds((2, 3, F), F32), _sds((2, 1, F), F32)],
                 scratch=[pad, pad, pad], dims=("parallel",))(dact, h, h, cw, cw, cb, cb), 320.0 * S / 4096)


def _swap_half(x):
    lane = lax.broadcasted_iota(jnp.int32, x.shape, 1)
    return jnp.where((lane % ROPE) < ROPE // 2, pltpu.roll(x, LANES - ROPE // 2, 1), pltpu.roll(x, ROPE // 2, 1))


def _rope(x, cos, sin_s):
    return x * cos + _swap_half(x) * sin_s


def _mla_prep(h, gq, gkv, cos, sin_s, *, name):
    S, W = h.shape
    tr = _tile(S, 512, 8)

    def body(h_ref, gq_ref, gkv_ref, cos_ref, sin_ref, cq_ref, ckv_ref, kr_ref, rq_ref, rkv_ref):
        cq = h_ref[:, 0:QRANK]
        rq = lax.rsqrt(jnp.mean(cq * cq, axis=-1, keepdims=True) + RMS_EPS)
        cq_ref[...] = (cq * rq * gq_ref[...]).astype(BF16)
        rq_ref[...] = rq
        ckv = h_ref[:, QRANK:QRANK + KVRANK]
        rkv = lax.rsqrt(jnp.mean(ckv * ckv, axis=-1, keepdims=True) + RMS_EPS)
        ckv_ref[...] = (ckv * rkv * gkv_ref[...]).astype(BF16)
        rkv_ref[...] = rkv
        kr = _rope(h_ref[:, QRANK + KVRANK:W], cos_ref[...], sin_ref[...])
        lane = lax.broadcasted_iota(jnp.int32, kr.shape, 1)
        kr = jnp.where(lane < ROPE, kr, 0.0)
        kr_ref[...] = (kr + pltpu.roll(kr, ROPE, 1)).astype(BF16)

    row = lambda w: pl.BlockSpec((tr, w), lambda i: (i, 0))
    vec = lambda w: pl.BlockSpec((1, w), lambda i: (0, 0))
    return _call(body, name=name, grid=(S // tr,),
                 in_specs=[row(W), vec(QRANK), vec(KVRANK), row(LANES), row(LANES)],
                 out_specs=[row(QRANK), row(KVRANK), row(LANES), row(1), row(1)],
                 out_shape=[_sds((S, QRANK), BF16), _sds((S, KVRANK), BF16), _sds((S, LANES), BF16), _sds((S, 1), F32), _sds((S, 1), F32)],
                 dims=("parallel",))(h, gq, gkv, cos, sin_s)


def _mla_prep_bwd(dcqn, dckvn, dkr2, h, rq, rkv, gq, gkv, cos, sin_neg, *, name):
    S, W = h.shape
    tr = _tile(S, 512, 8)

    def rms_bwd(dy, c, r, g):
        n = c * r
        dn = dy * g
        return r * (dn - n * jnp.mean(dn * n, axis=-1, keepdims=True)), jnp.sum(dy * n, axis=0, keepdims=True)

    def body(dq_ref, dkv_ref, dkr_ref, h_ref, rq_ref, rkv_ref, gq_ref, gkv_ref, cos_ref, sin_ref, dh_ref, dgq_ref, dgkv_ref):
        @pl.when(pl.program_id(0) == 0)
        def _():
            dgq_ref[...] = jnp.zeros_like(dgq_ref)
            dgkv_ref[...] = jnp.zeros_like(dgkv_ref)

        dcq, dg = rms_bwd(dq_ref[...], h_ref[:, 0:QRANK], rq_ref[...], gq_ref[...])
        dgq_ref[...] += dg
        dh_ref[:, 0:QRANK] = dcq.astype(BF16)
        dckv, dg = rms_bwd(dkv_ref[...], h_ref[:, QRANK:QRANK + KVRANK], rkv_ref[...], gkv_ref[...])
        dgkv_ref[...] += dg
        dh_ref[:, QRANK:QRANK + KVRANK] = dckv.astype(BF16)
        dk = dkr_ref[...]
        dk = dk + pltpu.roll(dk, ROPE, 1)
        dk = _rope(dk, cos_ref[...], sin_ref[...])
        lane = lax.broadcasted_iota(jnp.int32, dk.shape, 1)
        dh_ref[:, QRANK + KVRANK:W] = jnp.where(lane < ROPE, dk, 0.0).astype(BF16)

    row = lambda w: pl.BlockSpec((tr, w), lambda i: (i, 0))
    vec = lambda w: pl.BlockSpec((1, w), lambda i: (0, 0))
    return _call(body, name=name, grid=(S // tr,),
                 in_specs=[row(QRANK), row(KVRANK), row(LANES), row(W), row(1), row(1), vec(QRANK), vec(KVRANK), row(LANES), row(LANES)],
                 out_specs=[row(W), vec(QRANK), vec(KVRANK)],
                 out_shape=[_sds((S, W), BF16), _sds((1, QRANK), F32), _sds((1, KVRANK), F32)],
                 dims=("arbitrary",))(dcqn, dckvn, dkr2, h, rq, rkv, gq, gkv, cos, sin_neg)


QPAIR = 2 * NOPE + 2 * ROPE
KVPAIR = 2 * (NOPE + VDIM)


ATT_T = 512
LOG2_E = 1.4426950408889634


def _att_scores(kc, qc, allowed):
    s = lax.dot_general(kc, qc, _DN["nt"], preferred_element_type=F32) * (SM_SCALE * LOG2_E)
    return s if allowed is None else jnp.where(allowed, s, -jnp.inf)


def _att_allowed(k0, q0, t):
    kpos = k0 + lax.broadcasted_iota(jnp.int32, (t, t), 0)
    qpos = q0 + lax.broadcasted_iota(jnp.int32, (t, t), 1)
    return (kpos // CHUNK) <= (qpos // CHUNK)


def _head_mask(r, hh):
    lane = lax.broadcasted_iota(jnp.int32, r.shape, 1)
    keep = (lane < ROPE) if hh == 0 else (lane >= ROPE)
    return jnp.where(keep, r, jnp.zeros_like(r))


def _att_q(q_ref):
    r = q_ref[:, 2 * NOPE:QPAIR]
    return [jnp.concatenate([q_ref[:, hh * NOPE:(hh + 1) * NOPE], _head_mask(r, hh)], axis=1) for hh in range(2)]


def _attn_fwd(q, kv, kr2, cos, sin_s, *, name):
    S = q.shape[0]
    t = _tile(S, ATT_T, 8)
    nq = S // t
    npair = HEADS // 2

    def body(qf_ref, kv_ref, kr_ref, cos_ref, sin_ref, o_ref, lse_ref, q_ref):
        i = pl.program_id(1)
        q0 = i * t
        q_ref[:, 0:2 * NOPE] = qf_ref[:, 0:2 * NOPE].astype(BF16)
        q_ref[:, 2 * NOPE:QPAIR] = _rope(qf_ref[:, 2 * NOPE:QPAIR], cos_ref[...], sin_ref[...]).astype(BF16)
        qc = _att_q(q_ref)

        def step(j, carry, masked):
            k0 = pl.multiple_of(j * t, t)
            rows = pl.ds(k0, t)
            kr = kr_ref[rows, :]
            allowed = _att_allowed(k0, q0, t) if masked else None
            out = []
            for hh in range(2):
                m, l, acc = carry[hh]
                c0 = hh * (NOPE + VDIM)
                kc = jnp.concatenate([kv_ref[rows, c0:c0 + NOPE], kr], axis=1)
                v = kv_ref[rows, c0 + NOPE:c0 + NOPE + VDIM]
                s = _att_scores(kc, qc[hh], allowed)
                m_new = jnp.maximum(m, jnp.max(s, axis=0, keepdims=True))
                p = jnp.exp2(s - m_new)
                a = jnp.exp2(m - m_new)
                l = a * l + jnp.sum(p, axis=0, keepdims=True)
                acc = a * acc + lax.dot_general(v, p.astype(BF16), _DN["tn"], preferred_element_type=F32)
                out.append((m_new, l, acc))
            return tuple(out)

        one = (jnp.full((1, t), -jnp.inf, F32), jnp.zeros((1, t), F32), jnp.zeros((VDIM, t), F32))
        carry = lax.fori_loop(0, i, lambda j, c: step(j, c, False), (one, one))
        carry = step(i, carry, True)
        for hh in range(2):
            m, l, acc = carry[hh]
            o_ref[:, hh * VDIM:(hh + 1) * VDIM] = jnp.transpose(acc / l).astype(BF16)
            lse_ref[hh:hh + 1, :] = m + jnp.log2(l)

    qspec = pl.BlockSpec((t, QPAIR), lambda p, i: (i, p))
    tab = pl.BlockSpec((t, LANES), lambda p, i: (i, 0))
    return _call(body, name=name, grid=(npair, nq),
                 in_specs=[qspec, pl.BlockSpec((S, KVPAIR), lambda p, i: (0, p)), pl.BlockSpec((S, LANES), lambda p, i: (0, 0)), tab, tab],
                 out_specs=[pl.BlockSpec((t, 2 * VDIM), lambda p, i: (i, p)), pl.BlockSpec((None, 2, t), lambda p, i: (p, 0, i)), qspec],
                 out_shape=[_sds((S, HEADS * VDIM), BF16), _sds((npair, 2, S), F32), _sds(q.shape, BF16)],
                 dims=("parallel", "arbitrary"))(q, kv, kr2, cos, sin_s)


def _attn_bwd(qb, kv, kr2, o, do, lse, cos, sin_neg, *, name):
    S = qb.shape[0]
    t = _tile(S, ATT_T, 8)
    nq = S // t
    npair = HEADS // 2

    def body(q_ref, kv_ref, kr_ref, o_ref, do_ref, lse_ref, cos_ref, sin_ref, dq_ref, dkv_ref, dkr_ref, acc_ref):
        pid = pl.program_id(0)
        i = pl.program_id(1)
        q0 = i * t

        @pl.when(i == 0)
        def _():
            acc_ref[...] = jnp.zeros_like(acc_ref)

        @pl.when((i == 0) & (pid == 0))
        def _():
            dkr_ref[...] = jnp.zeros_like(dkr_ref)

        qc = _att_q(q_ref)
        dov = [do_ref[:, hh * VDIM:(hh + 1) * VDIM] for hh in range(2)]
        dsum = [jnp.sum(jnp.transpose(dov[hh].astype(F32) * o_ref[:, hh * VDIM:(hh + 1) * VDIM].astype(F32)), axis=0, keepdims=True)
                for hh in range(2)]
        lse_v = [lse_ref[hh:hh + 1, :] for hh in range(2)]

        def step(j, carry, masked):
            k0 = pl.multiple_of(j * t, t)
            rows = pl.ds(k0, t)
            kr = kr_ref[rows, :]
            allowed = _att_allowed(k0, q0, t) if masked else None
            out = []
            dkr = jnp.zeros((t, LANES), F32)
            for hh in range(2):
                c0 = hh * (NOPE + VDIM)
                kc = jnp.concatenate([kv_ref[rows, c0:c0 + NOPE], kr], axis=1)
                v = kv_ref[rows, c0 + NOPE:c0 + NOPE + VDIM]
                p = jnp.exp2(_att_scores(kc, qc[hh], allowed) - lse_v[hh])
                dp = lax.dot_general(v, dov[hh], _DN["nt"], preferred_element_type=F32)
                ds = (p * (dp - dsum[hh]) * SM_SCALE).astype(BF16)
                acc_ref[rows, c0 + NOPE:c0 + NOPE + VDIM] += jnp.dot(p.astype(BF16), dov[hh], preferred_element_type=F32)
                dkc = jnp.dot(ds, qc[hh], preferred_element_type=F32)
                acc_ref[rows, c0:c0 + NOPE] += dkc[:, 0:NOPE]
                dkr = dkr + dkc[:, NOPE:]
                out.append(carry[hh] + lax.dot_general(ds, kc, _DN["tn"], preferred_element_type=F32))
            dkr_ref[rows, :] += dkr
            return tuple(out)

        zero = jnp.zeros((t, NOPE + LANES), F32)
        carry = lax.fori_loop(0, i, lambda j, c: step(j, c, False), (zero, zero))
        dqc = step(i, carry, True)
        for hh in range(2):
            dq_ref[:, hh * NOPE:(hh + 1) * NOPE] = dqc[hh][:, 0:NOPE].astype(BF16)
        dr = _head_mask(dqc[0][:, NOPE:], 0) + _head_mask(dqc[1][:, NOPE:], 1)
        dq_ref[:, 2 * NOPE:QPAIR] = _rope(dr, cos_ref[...], sin_ref[...]).astype(BF16)

        @pl.when(i == nq - 1)
        def _():
            dkv_ref[...] = acc_ref[...].astype(BF16)

    qspec = pl.BlockSpec((t, QPAIR), lambda p, i: (i, p))
    hspec = pl.BlockSpec((t, 2 * VDIM), lambda p, i: (i, p))
    kvspec = pl.BlockSpec((S, KVPAIR), lambda p, i: (0, p))
    krspec = pl.BlockSpec((S, LANES), lambda p, i: (0, 0))
    tab = pl.BlockSpec((t, LANES), lambda p, i: (i, 0))
    return _ticked(_call(body, name=name, grid=(npair, nq),
                 in_specs=[qspec, kvspec, krspec, hspec, hspec, pl.BlockSpec((None, 2, t), lambda p, i: (p, 0, i)), tab, tab],
                 out_specs=[qspec, kvspec, krspec],
                 out_shape=[_sds(qb.shape, BF16), _sds(kv.shape, BF16), _sds((S, LANES), F32)],
                 scratch=[pltpu.VMEM((S, KVPAIR), F32)], dims=("arbitrary", "arbitrary"))(qb, kv, kr2, o, do, lse, cos, sin_neg),
                   640.0 * (S / 4096) ** 2)


def _coords():
    return lax.axis_index("x"), lax.axis_index("y"), lax.axis_index("c")


def _all_gather_body(n, shake):
    def body(*refs):
        ins, outs = refs[:n], refs[n:2 * n]
        send_sems, recv_sems, local_sems = refs[2 * n:]
        x, y, c = _coords()
        me, sib = (x, y, c), (x, y, 1 - c)
        chips = [(1 - x, y), (x, 1 - y), (1 - x, 1 - y)]
        if shake:
            _handshake([sib] + [(*chip, c) for chip in chips])

        def copy(a, k, block, to, src=None):
            dst = outs[a].at[4 * block[0] + 2 * block[1] + block[2]]
            return pltpu.make_async_remote_copy(src_ref=dst if src is None else src, dst_ref=dst,
                                                send_sem=send_sems.at[a, k], recv_sem=recv_sems.at[a, k],
                                                device_id=to, device_id_type=MESH)

        mine = [pltpu.make_async_copy(ins[a], outs[a].at[4 * x + 2 * y + c], local_sems.at[a]) for a in range(n)]
        for cp in mine:
            cp.start()
        sends = []
        for a in range(n):
            sends.append(copy(a, 0, me, sib, src=ins[a]))
            sends += [copy(a, 1 + j, me, (*chip, c), src=ins[a]) for j, chip in enumerate(chips)]
        for cp in sends:
            cp.start()
        for j, chip in enumerate(chips):
            for a in range(n):
                copy(a, 1 + j, (*chip, c), me).wait_recv()
                fwd = copy(a, 4 + j, (*chip, c), sib)
                fwd.start()
                sends.append(fwd)
        for a in range(n):
            copy(a, 0, sib, me).wait_recv()
            for j, chip in enumerate(chips):
                copy(a, 4 + j, (*chip, 1 - c), me).wait_recv()
        for cp in sends:
            cp.wait_send()
        for cp in mine:
            cp.wait()

    return body


def _all_gather(arrs, *, name):
    n = len(arrs)
    outs = _call(_all_gather_body(n, False), name=name, in_specs=[HBM_SPEC] * n, out_specs=[HBM_SPEC] * n,
                 out_shape=[_sds((N_DEV,) + a.shape, a.dtype) for a in arrs],
                 scratch=[pltpu.SemaphoreType.DMA((n, 7)), pltpu.SemaphoreType.DMA((n, 7)), pltpu.SemaphoreType.DMA((n,))])(*arrs)
    return list(outs)


def _sequencer_call(body, *, name, out_type, n_sems, collective_id):
    return pl.kernel(body, out_type=out_type, mesh=plsc.ScalarSubcoreMesh(axis_name="sq", num_cores=1), name=name,
                     scratch_types=[pltpu.SemaphoreType.DMA(n_sems), pltpu.SemaphoreType.DMA(n_sems), pltpu.SemaphoreType.DMA((n_sems[0],))],
                     compiler_params=pltpu.CompilerParams(collective_id=collective_id))


def _handshake(peers):
    barrier = pltpu.get_barrier_semaphore()
    for p in peers:
        pl.semaphore_signal(barrier, inc=1, device_id=p, device_id_type=MESH)
    pl.semaphore_wait(barrier, len(peers))


AG_ID, PAIR_ID, CHIP_ID = 1, 2, 3


def _all_gather_sc(arrs, *, name):
    n = len(arrs)
    outs = _sequencer_call(_all_gather_body(n, True), name=name, out_type=[_sds((N_DEV,) + a.shape, a.dtype) for a in arrs],
                           n_sems=(n, 7), collective_id=AG_ID)(*arrs)
    return list(outs)


def _pair_exchange_sc(gs, *, name):
    n = len(gs)

    def body(*refs):
        ins, outs = refs[:n], refs[n:2 * n]
        send_sems, recv_sems, _ = refs[2 * n:]
        x, y, c = _coords()
        sib = (x, y, 1 - c)
        _handshake([sib])
        cps = []
        for a in range(n):
            for j in range(4):
                cps.append(pltpu.make_async_remote_copy(src_ref=ins[a].at[2 * j + (1 - c)], dst_ref=outs[a].at[j],
                                                        send_sem=send_sems.at[a, j], recv_sem=recv_sems.at[a, j],
                                                        device_id=sib, device_id_type=MESH))
        for cp in cps:
            cp.start()
        for cp in cps:
            cp.wait()

    outs = _sequencer_call(body, name=name, out_type=[_sds((4,) + g.shape[1:], g.dtype) for g in gs],
                           n_sems=(n, 4), collective_id=PAIR_ID)(*gs)
    return list(outs)


def _chip_exchange_sc(ps, *, name):
    n = len(ps)

    def body(*refs):
        ins, outs = refs[:n], refs[n:2 * n]
        send_sems, recv_sems, _ = refs[2 * n:]
        x, y, c = _coords()
        chips = [(1 - x, y), (x, 1 - y), (1 - x, 1 - y)]
        _handshake([(*chip, c) for chip in chips])
        cps = []
        for a in range(n):
            for r, (px, py) in enumerate(chips):
                cps.append(pltpu.make_async_remote_copy(src_ref=ins[a].at[2 * px + py], dst_ref=outs[a].at[r],
                                                        send_sem=send_sems.at[a, r], recv_sem=recv_sems.at[a, r],
                                                        device_id=(px, py, c), device_id_type=MESH))
        for cp in cps:
            cp.start()
        for cp in cps:
            cp.wait()

    outs = _sequencer_call(body, name=name, out_type=[_sds((3,) + p.shape[1:], p.dtype) for p in ps],
                           n_sems=(n, 3), collective_id=CHIP_ID)(*ps)
    return list(outs)


def _row_tile(R, C, budget=1 << 20, mult=16):
    want = max(mult, budget // (4 * C))
    if R <= want:
        return R
    t = (want // mult) * mult
    while t >= mult:
        if R % t == 0:
            return t
        t -= mult
    return R


RS_TILE_BYTES = 6 << 20


def _pair_sum(g, l1, c_idx, *, name):
    _, R, C = g.shape
    tr = _row_tile(R, C, budget=RS_TILE_BYTES)
    g4 = g.reshape(4, 2, R, C)

    def body(c_ref, g_ref, l_ref, o_ref):
        o_ref[...] = (g_ref[...].astype(F32) + l_ref[...].astype(F32)).astype(o_ref.dtype)

    return _call(body, name=name, n_prefetch=1, grid=(4, R // tr),
                 in_specs=[pl.BlockSpec((None, None, tr, C), lambda j, r, c_ref: (j, c_ref[0], r, 0)),
                           pl.BlockSpec((None, tr, C), lambda j, r, c_ref: (j, r, 0))],
                 out_specs=pl.BlockSpec((None, tr, C), lambda j, r, c_ref: (j, r, 0)),
                 out_shape=_sds((4, R, C), g.dtype))(c_idx, g4, l1)


def _chip_sum(p, l2, chip_idx, *, name):
    _, R, C = p.shape
    tr = _row_tile(R, C, budget=RS_TILE_BYTES)

    def body(j_ref, p_ref, l_ref, o_ref):
        o_ref[...] = ((p_ref[...].astype(F32) + l_ref[0].astype(F32)) + l_ref[1].astype(F32)) + l_ref[2].astype(F32)

    return _call(body, name=name, n_prefetch=1, grid=(R // tr,),
                 in_specs=[pl.BlockSpec((None, tr, C), lambda r, j_ref: (j_ref[0], r, 0)),
                           pl.BlockSpec((3, tr, C), lambda r, j_ref: (0, r, 0))],
                 out_specs=pl.BlockSpec((tr, C), lambda r, j_ref: (r, 0)), out_shape=_sds((R, C), F32))(chip_idx, p, l2)


class _Reducer:
    PAIR_US = (12.0, 7.0)
    CHIP_US = (15.0, 43.0)

    def __init__(self, c_idx, chip_idx):
        self.c_idx, self.chip_idx = c_idx, chip_idx
        self.units, self.out, self.sc_free = [], {}, 0.0

    def _sequencer_done(self, cost):
        self.sc_free = max(self.sc_free, _SCHED.clock) + cost[0] + cost[1] * self._mb
        return self.sc_free

    def submit(self, name, g):
        R = g.shape[0] // N_DEV
        g8 = g.reshape((N_DEV, R, g.shape[1]))
        self._mb = R * g.shape[1] * g.dtype.itemsize / 1e6
        (l1,) = _pair_exchange_sc([g8], name=f"rs_pair_{name}")
        self.units.append(dict(name=name, g8=g8, l1=l1, mb=self._mb, stage=1, ready=self._sequencer_done(self.PAIR_US)))

    def advance(self, force=False):
        for u in self.units:
            if not (force or _SCHED.clock >= u["ready"]):
                continue
            if u["stage"] == 1:
                p = _pair_sum(u.pop("g8"), u.pop("l1"), self.c_idx, name=f"rs_psum_{u['name']}")
                (l2,) = _chip_exchange_sc([p], name=f"rs_chip_{u['name']}")
                self._mb = u["mb"]
                u.update(p=p, l2=l2, stage=2, ready=self._sequencer_done(self.CHIP_US))
            elif u["stage"] == 2:
                self.out[u["name"]] = _chip_sum(u.pop("p"), u.pop("l2"), self.chip_idx, name=f"rs_csum_{u['name']}")
                u["stage"] = 3

    def finish(self):
        self.advance(force=True)
        self.advance(force=True)
        return self.out


def _grad_ready(name, g):
    _SCHED.big[name] = g
    if _SCHED.reducer is not None:
        _SCHED.reducer.submit(name, g)
    return g


def _sum8(a, *, name):
    _, R, C = a.shape
    tr = _row_tile(R, C, budget=1 << 18, mult=8)

    def body(a_ref, o_ref):
        acc = a_ref[0]
        for k in range(1, N_DEV):
            acc = acc + a_ref[k]
        o_ref[...] = acc

    return _call(body, name=name, grid=(R // tr,), in_specs=[pl.BlockSpec((N_DEV, tr, C), lambda r: (0, r, 0))],
                 out_specs=pl.BlockSpec((tr, C), lambda r: (r, 0)), out_shape=_sds((R, C), F32), dims=("parallel",))(a)


def _adamw(w, g, m, v, *, name):
    R, C = w.shape
    tr = _row_tile(R, C, budget=2 << 20, mult=8)
    c1 = 1.0 - ADAM_B1 ** ADAM_STEP
    c2 = 1.0 - ADAM_B2 ** ADAM_STEP

    def body(w_ref, g_ref, m_ref, v_ref, d_ref, mo_ref, vo_ref):
        gv = g_ref[...]
        mn = ADAM_B1 * m_ref[...] + (1.0 - ADAM_B1) * gv
        vn = ADAM_B2 * v_ref[...] + (1.0 - ADAM_B2) * (gv * gv)
        mo_ref[...] = mn
        vo_ref[...] = vn
        d_ref[...] = -ADAM_LR * ((mn / c1) / (jnp.sqrt(vn / c2) + ADAM_EPS) + ADAM_WD * w_ref[...])

    blk = pl.BlockSpec((tr, C), lambda r: (r, 0))
    return _call(body, name=name, grid=(R // tr,), in_specs=[blk] * 4, out_specs=[blk] * 3,
                 out_shape=[_sds((R, C), F32)] * 3, dims=("parallel",))(w, g, m, v)


def _gmlp_fwd(xb, W, sp, tag):
    zp = _mm(xb, W["w_in_t"], "nt", name=f"gm_zp_{tag}", out_dtype=BF16)
    p, s, vhat, rstd = _gm_mid_fwd(zp, sp["ln_g"], sp["ln_b"], sp["w_s"], sp["b_st"], name=f"gm_mid_{tag}")
    m = _mm(p, W["w_out"], "nn", name=f"gm_out_{tag}", out_dtype=F32, tm=512, tk=p.shape[1])
    return m, dict(xb=xb, zp=zp, p=p, s=s, vhat=vhat, rstd=rstd)


def _gmlp_bwd(drb, dr, W, sp, sv, tag, on_small=None):
    _grad_ready(f"{tag}_w_out", _mm(sv["p"], drb, "tn", name=f"gm_dwout_{tag}", out_dtype=BF16, tk=TOKENS_K))
    dp = _mm(drb, W["w_out"], "nt", name=f"gm_dp_{tag}", out_dtype=BF16)
    dzp, dws, dbs, dlg, dlb = _gm_mid_bwd(dp, sv["zp"], sv["s"], sv["vhat"], sv["rstd"], sp["ln_g"], sp["ln_b"], sp["w_s"],
                                          name=f"gm_midb_{tag}")
    small = dict(w_s=dws, b_s=dbs, ln_g=dlg, ln_b=dlb)
    if on_small is not None:
        on_small(small)
    _grad_ready(f"{tag}_w_in_t", _mm(dzp, sv["xb"], "tn", name=f"gm_dwin_{tag}", out_dtype=BF16, tk=TOKENS_K))
    dx = _mm(dzp, W["w_in_t"], "nn", name=f"gm_dx_{tag}", out_dtype=F32, res=dr, res_scale=ALPHA)
    return dx, small


def _mla_fwd(xb, W, sp, rope, tag):
    cos, sin_s, _ = rope
    h = _mm(xb, W["w_in"], "nn", name=f"mla_h_{tag}", out_dtype=F32, tn=W["w_in"].shape[1])
    cqn, ckvn, kr2, rq, rkv = _mla_prep(h, sp["gq"], sp["gkv"], cos, sin_s, name=f"mla_prep_{tag}")
    q = _mm(cqn, W["w_qb_t"], "nt", name=f"mla_q_{tag}", out_dtype=F32)
    kv = _mm(ckvn, W["w_kvb_t"], "nt", name=f"mla_kv_{tag}", out_dtype=BF16)
    o, lse, qb = _attn_fwd(q, kv, kr2, cos, sin_s, name=f"mla_attn_{tag}")
    m = _mm(o, W["w_out"], "nn", name=f"mla_out_{tag}", out_dtype=F32)
    return m, dict(xb=xb, h=h, cqn=cqn, ckvn=ckvn, kr2=kr2, rq=rq, rkv=rkv, qb=qb, kv=kv, o=o, lse=lse)


def _mla_bwd(drb, dr, W, sp, rope, sv, tag):
    cos, _, sin_neg = rope
    _grad_ready(f"{tag}_w_out", _mm(sv["o"], drb, "tn", name=f"mla_dwout_{tag}", out_dtype=BF16, tk=TOKENS_K))
    do = _mm(drb, W["w_out"], "nt", name=f"mla_do_{tag}", out_dtype=BF16)
    dqb, dkv, dkr2 = _attn_bwd(sv["qb"], sv["kv"], sv["kr2"], sv["o"], do, sv["lse"], cos, sin_neg, name=f"mla_attnb_{tag}")
    _grad_ready(f"{tag}_w_qb_t", _mm(dqb, sv["cqn"], "tn", name=f"mla_dwqb_{tag}", out_dtype=BF16, tk=TOKENS_K))
    _grad_ready(f"{tag}_w_kvb_t", _mm(dkv, sv["ckvn"], "tn", name=f"mla_dwkvb_{tag}", out_dtype=BF16, tk=TOKENS_K))
    dcqn = _mm(dqb, W["w_qb_t"], "nn", name=f"mla_dcq_{tag}", out_dtype=F32, tk=dqb.shape[1])
    dckvn = _mm(dkv, W["w_kvb_t"], "nn", name=f"mla_dckv_{tag}", out_dtype=F32, tk=dkv.shape[1])
    dh, dgq, dgkv = _mla_prep_bwd(dcqn, dckvn, dkr2, sv["h"], sv["rq"], sv["rkv"], sp["gq"], sp["gkv"], cos, sin_neg,
                                  name=f"mla_prepb_{tag}")
    _grad_ready(f"{tag}_w_in", _mm(sv["xb"], dh, "tn", name=f"mla_dwin_{tag}", out_dtype=BF16, tn=dh.shape[1], tk=TOKENS_K))
    dx = _mm(dh, W["w_in"], "nt", name=f"mla_dx_{tag}", out_dtype=F32, tk=dh.shape[1], res=dr, res_scale=ALPHA)
    return dx, dict(gq=dgq, gkv=dgkv)


def _ffn_fwd(xb, W, sp, tag):
    S = xb.shape[0]
    F = W["w_down"].shape[0]
    h = _mm(xb, W["w_up_t"], "nt", name=f"ffn_h_{tag}", out_dtype=BF16)
    act = _ffn_mid_fwd(h, sp["cw"], sp["cb"], name=f"ffn_mid_{tag}")
    f = _mm(act, W["w_down"], "nn", name=f"ffn_out_{tag}", out_dtype=F32, tk=F // 2)
    return f, dict(xb=xb, h=h, act=act)


def _ffn_bwd(drb, dr, W, sp, sv, tag):
    F = W["w_down"].shape[0]
    S, D = drb.shape
    tf = _tile(F, FFN_TILE)
    _grad_ready(f"{tag}_w_down", _mm(sv["act"], drb, "tn", name=f"ffn_dwdown_{tag}", out_dtype=BF16, tm=tf, tn=512, tk=TOKENS_K))
    dact = _mm(drb, W["w_down"], "nt", name=f"ffn_dact_{tag}", out_dtype=BF16, tm=512, tn=F, tk=D, b_resident=True)
    dh, dcw, dcb = _ffn_mid_bwd(dact, sv["h"], sp["cw"], sp["cb"], name=f"ffn_midb_{tag}")
    dh = dh.reshape(2 * S, F)
    _grad_ready(f"{tag}_w_up_t", _mm(dh, sv["xb"], "tn", name=f"ffn_dwup_{tag}", out_dtype=BF16, tm=tf, tn=512, tk=TOKENS_K, a_parts=2))
    dx = _mm(dh, W["w_up_t"], "nn", name=f"ffn_dx_{tag}", out_dtype=F32, res=dr, res_scale=ALPHA, a_parts=2, tk=F // 2)
    dcw = jnp.concatenate([dcw[0], dcw[1]], axis=1)
    dcb = jnp.concatenate([dcb[0], dcb[1]], axis=1)
    return dx, dict(cw=dcw, cb=dcb)


def _rope_tables(S):
    half = ROPE // 2
    inv_freq = ROPE_THETA ** (-jnp.arange(half, dtype=F32) / half)
    ang = jnp.arange(S, dtype=F32)[:, None] * inv_freq[None, :]
    cos, sin = jnp.cos(ang), jnp.sin(ang)
    cos128 = jnp.concatenate([cos] * 4, axis=1)
    sin128 = jnp.concatenate([-sin, sin, -sin, sin], axis=1)
    return cos128, sin128, -sin128


def _fwd_bwd(x, tgt, Wm, Wf, spm, spf, ln, on_small=None, late_params=None):
    S, D = x.shape
    _SCHED.last, _SCHED.clock = None, 0.0
    rope = _rope_tables(S)
    xf, xb = x, x.astype(BF16)
    saved = []
    for i in range(DEPTH):
        if i % 2 == 0:
            m, svm = _gmlp_fwd(xb, Wm[i], spm[i], f"l{i}")
        else:
            m, svm = _mla_fwd(xb, Wm[i], spm[i], rope, f"l{i}")
        if i == 0 and late_params is not None:
            spm, spf = late_params(m)
        y, yb, xh1, rs1 = _ln_fwd(xf, m, ln["mix_g"][i], ln["mix_b"][i], name=f"ln_mix_{i}")
        f, svf = _ffn_fwd(yb, Wf[i], spf[i], f"l{i}")
        z, zb, xh2, rs2 = _ln_fwd(y, f, ln["ffn_g"][i], ln["ffn_b"][i], name=f"ln_ffn_{i}")
        saved.append((svm, xh1, rs1, svf, xh2, rs2))
        xf, xb = z, zb
    lp, dy = _loss_kernel(xf, tgt, name="loss")
    gm_small, gf_small, gln = [None] * DEPTH, [None] * DEPTH, [None] * DEPTH
    _SCHED.big = {}

    for i in reversed(range(DEPTH)):
        svm, xh1, rs1, svf, xh2, rs2 = saved[i]
        dr, drb, dg2, db2 = _ln_bwd(dy, xh2, rs2, ln["ffn_g"][i], name=f"lnb_ffn_{i}")
        dy, gf_small[i] = _ffn_bwd(drb, dr, Wf[i], spf[i], svf, f"l{i}")
        dr, drb, dg1, db1 = _ln_bwd(dy, xh1, rs1, ln["mix_g"][i], name=f"lnb_mix_{i}")
        gln[i] = dict(mix_g=dg1, mix_b=db1, ffn_g=dg2, ffn_b=db2)
        if i % 2 == 0:
            hook = None if (i > 0 or on_small is None) else (lambda ms: on_small(lp, [ms] + gm_small[1:], gf_small, gln))
            dy, gm_small[i] = _gmlp_bwd(drb, dr, Wm[i], spm[i], svm, f"l{i}", hook)
        else:
            dy, gm_small[i] = _mla_bwd(drb, dr, Wm[i], spm[i], rope, svm, f"l{i}")
    return lp, dy, _SCHED.big, gm_small, gf_small, gln


def _perm_q_rows(wt):
    hd = NOPE + ROPE
    return jnp.concatenate([wt[0:NOPE], wt[hd:hd + NOPE], wt[NOPE:hd], wt[hd + NOPE:2 * hd]], axis=0)


def _unperm_q_rows(wt):
    return jnp.concatenate([wt[0:NOPE], wt[2 * NOPE:2 * NOPE + ROPE], wt[NOPE:2 * NOPE], wt[2 * NOPE + ROPE:]], axis=0)


def _pad_cols(w, to):
    return jnp.pad(w, ((0, 0), (0, to - w.shape[1])))


def _pad_pieces(a, axis, piece, piece_pad):
    n = a.shape[axis] // piece
    sh = a.shape[:axis] + (n, piece) + a.shape[axis + 1:]
    pad = [(0, 0)] * (len(sh))
    pad[axis + 1] = (0, piece_pad - piece)
    return jnp.pad(a.reshape(sh), pad).reshape(a.shape[:axis] + (n * piece_pad,) + a.shape[axis + 1:])


def _unpad_pieces(a, axis, piece, piece_pad):
    n = a.shape[axis] // piece_pad
    sh = a.shape[:axis] + (n, piece_pad) + a.shape[axis + 1:]
    return lax.slice_in_dim(a.reshape(sh), 0, piece, axis=axis + 1).reshape(a.shape[:axis] + (n * piece,) + a.shape[axis + 1:])


def _pack(parts, rows_mult=8):
    flat = jnp.concatenate([p.reshape(-1).astype(F32) for p in parts])
    n = flat.shape[0]
    per = LANES * rows_mult
    tot = ((n + per - 1) // per) * per
    return jnp.pad(flat, (0, tot - n)).reshape(tot // LANES, LANES)


def _unpack(buf, shapes):
    flat = buf.reshape(-1)
    out, off = [], 0
    for s in shapes:
        n = 1
        for d in s:
            n *= d
        out.append(flat[off:off + n].reshape(s))
        off += n
    return out


def kernel(x, gm_w_in, gm_ln_g, gm_ln_b, gm_w_s, gm_b_s, gm_w_out, mla_w_in, mla_q_norm_g, mla_kv_norm_g, mla_w_q_b, mla_w_kv_b, mla_w_out, ffn_w_up, ffn_conv_w, ffn_conv_b, ffn_w_down, ln_mix_g, ln_mix_b, ln_ffn_g, ln_ffn_b, loss_target, m_gm_w_in, m_gm_ln_g, m_gm_ln_b, m_gm_w_s, m_gm_b_s, m_gm_w_out, m_mla_w_in, m_mla_q_norm_g, m_mla_kv_norm_g, m_mla_w_q_b, m_mla_w_kv_b, m_mla_w_out, m_ffn_w_up, m_ffn_conv_w, m_ffn_conv_b, m_ffn_w_down, m_ln_mix_g, m_ln_mix_b, m_ln_ffn_g, m_ln_ffn_b, v_gm_w_in, v_gm_ln_g, v_gm_ln_b, v_gm_w_s, v_gm_b_s, v_gm_w_out, v_mla_w_in, v_mla_q_norm_g, v_mla_kv_norm_g, v_mla_w_q_b, v_mla_w_kv_b, v_mla_w_out, v_ffn_w_up, v_ffn_conv_w, v_ffn_conv_b, v_ffn_w_down, v_ln_mix_g, v_ln_mix_b, v_ln_ffn_g, v_ln_ffn_b):
    S, D = x.shape[1], x.shape[2]
    xi, yi, ci = _coords()
    dev = 4 * xi + 2 * yi + ci
    c_idx = jnp.reshape(ci, (1,)).astype(jnp.int32)
    chip_idx = jnp.reshape(2 * xi + yi, (1,)).astype(jnp.int32)
    w_in_cols = mla_w_in.shape[2]
    w_in_pad = ((w_in_cols + LANES - 1) // LANES) * LANES
    n_gm, n_mla = gm_w_in.shape[0], mla_w_in.shape[0]

    cw_l = ffn_conv_w.shape[2]
    small_in = _pack([mla_q_norm_g, mla_kv_norm_g, ffn_conv_w])
    r_down = ffn_w_down.shape[1]
    r_pad = ((r_down + FFN_SHARD_MULT - 1) // FFN_SHARD_MULT) * FFN_SHARD_MULT

    def gather(shards, name):
        full = _all_gather_sc(list(shards.values()), name=name)
        return {k: f.reshape((-1, f.shape[2])) for k, f in zip(shards.keys(), full)}

    Wm, Wf = [None] * DEPTH, [None] * DEPTH
    for i in range(DEPTH):
        s = i // 2
        if i == 0:
            Wm[i] = gather(dict(w_in_t=gm_w_in[s].T.astype(BF16)), "ag_first")
            Wm[i].update(gather(dict(w_out=gm_w_out[s].astype(BF16), small=small_in), f"ag_gm_{i}_rest"))
            small_all = Wm[i].pop("small").reshape((N_DEV,) + small_in.shape)
        elif i % 2 == 0:
            Wm[i] = gather(dict(w_in_t=gm_w_in[s].T.astype(BF16), w_out=gm_w_out[s].astype(BF16)), f"ag_gm_{i}")
        else:
            Wm[i] = gather(dict(w_in=_pad_cols(mla_w_in[s], w_in_pad).astype(BF16),
                                w_qb_t=_perm_q_rows(mla_w_q_b[s].T).astype(BF16),
                                w_kvb_t=mla_w_kv_b[s].T.astype(BF16), w_out=mla_w_out[s].astype(BF16)), f"ag_mla_{i}")
        Wf[i] = gather(dict(w_up_t=_pad_pieces(ffn_w_up[i].T, 0, r_down, r_pad).astype(BF16),
                            w_down=_pad_pieces(ffn_w_down[i], 0, r_down, r_pad).astype(BF16)), f"ag_ffn_{i}")

    def small_params(anchor):
        spm, spf = [None] * DEPTH, [None] * DEPTH
        if anchor is not None:
            pack, _ = lax.optimization_barrier((small_all, anchor))
            parts = [_unpack(pack[k], [mla_q_norm_g.shape, mla_kv_norm_g.shape, ffn_conv_w.shape]) for k in range(N_DEV)]
            gq_full = jnp.concatenate([p[0] for p in parts], axis=1)
            gkv_full = jnp.concatenate([p[1] for p in parts], axis=1)
            cw_full = jnp.concatenate([p[2] for p in parts], axis=2)
        for i in range(DEPTH):
            s = i // 2
            if i % 2 == 0:
                spm[i] = dict(ln_g=gm_ln_g[s][None], ln_b=gm_ln_b[s][None], w_s=gm_w_s[s], b_st=gm_b_s[s].T)
            elif anchor is not None:
                spm[i] = dict(gq=gq_full[s][None], gkv=gkv_full[s][None])
            if anchor is not None:
                spf[i] = dict(cw=_pad_pieces(cw_full[i], 1, r_down, r_pad), cb=_pad_pieces(ffn_conv_b[i][None], 1, r_down, r_pad))
        return spm, spf

    spm, spf = small_params(None)
    ln = dict(mix_g=ln_mix_g[:, None], mix_b=ln_mix_b[:, None], ffn_g=ln_ffn_g[:, None], ffn_b=ln_ffn_b[:, None])

    small_box = {}

    def on_small(lp, gm_small, gf_small, gln):
        small_g = [
            jnp.stack([gm_small[2 * s]["ln_g"][0] for s in range(n_gm)]),
            jnp.stack([gm_small[2 * s]["ln_b"][0] for s in range(n_gm)]),
            jnp.stack([gm_small[2 * s]["w_s"] for s in range(n_gm)]),
            jnp.stack([gm_small[2 * s]["b_s"][:, :, 0] for s in range(n_gm)]),
            jnp.stack([_unpad_pieces(gf_small[i]["cb"], 1, r_down, r_pad)[0] for i in range(DEPTH)]),
            jnp.stack([gln[i]["mix_g"][0] for i in range(DEPTH)]),
            jnp.stack([gln[i]["mix_b"][0] for i in range(DEPTH)]),
            jnp.stack([gln[i]["ffn_g"][0] for i in range(DEPTH)]),
            jnp.stack([gln[i]["ffn_b"][0] for i in range(DEPTH)]),
            jnp.stack([gm_small[2 * s + 1]["gq"][0] for s in range(n_mla)]),
            jnp.stack([gm_small[2 * s + 1]["gkv"][0] for s in range(n_mla)]),
            jnp.stack([_unpad_pieces(gf_small[i]["cw"], 1, r_down, r_pad) for i in range(DEPTH)]),
        ]
        small_box["shapes"] = [g.shape for g in small_g]
        (small_box["all"],) = _all_gather_sc([_pack(small_g)], name="ag_small_grads")

    _SCHED.reducer = _Reducer(c_idx, chip_idx)
    lp, grad_x, _, _, _, _ = _fwd_bwd(x[0], loss_target[0], Wm, Wf, spm, spf, ln, on_small, small_params)
    red = _SCHED.reducer.finish()
    _SCHED.reducer = None
    rm = [{k: red[f"l{i}_{k}"] for k in Wm[i]} for i in range(DEPTH)]
    rf = [{k: red[f"l{i}_{k}"] for k in Wf[i]} for i in range(DEPTH)]

    g_gm_w_in = jnp.stack([rm[2 * s]["w_in_t"].T for s in range(n_gm)])
    g_gm_w_out = jnp.stack([rm[2 * s]["w_out"] for s in range(n_gm)])
    g_mla_w_in = jnp.stack([rm[2 * s + 1]["w_in"][:, :w_in_cols] for s in range(n_mla)])
    g_mla_w_q_b = jnp.stack([_unperm_q_rows(rm[2 * s + 1]["w_qb_t"]).T for s in range(n_mla)])
    g_mla_w_kv_b = jnp.stack([rm[2 * s + 1]["w_kvb_t"].T for s in range(n_mla)])
    g_mla_w_out = jnp.stack([rm[2 * s + 1]["w_out"] for s in range(n_mla)])
    g_ffn_w_up = jnp.stack([_unpad_pieces(rf[i]["w_up_t"], 0, r_down, r_pad).T for i in range(DEPTH)])
    g_ffn_w_down = jnp.stack([_unpad_pieces(rf[i]["w_down"], 0, r_down, r_pad) for i in range(DEPTH)])

    small_sum = _unpack(_sum8(small_box["all"], name="small_grad_sum"), small_box["shapes"])
    (g_gm_ln_g, g_gm_ln_b, g_gm_w_s, g_gm_b_s, g_ffn_conv_b, g_ln_mix_g, g_ln_mix_b, g_ln_ffn_g, g_ln_ffn_b,
     gq_all, gkv_all, cw_all) = small_sum
    (loss_all,) = _all_gather([_pack([jnp.reshape(0.5 * jnp.sum(lp) / D, (1,))])], name="ag_loss")
    loss = jnp.reshape(_sum8(loss_all, name="loss_sum")[0, 0], ())
    qn_l = mla_q_norm_g.shape[1]
    g_mla_q_norm_g = lax.dynamic_slice_in_dim(gq_all, dev * qn_l, qn_l, axis=1)
    g_mla_kv_norm_g = lax.dynamic_slice_in_dim(gkv_all, dev * qn_l, qn_l, axis=1)
    g_ffn_conv_w = lax.dynamic_slice_in_dim(cw_all, dev * cw_l, cw_l, axis=2)

    def adam_big(w, g, m, v, tag):
        sh = w.shape
        two = lambda a: a.reshape((-1, sh[-1]))
        d, mn, vn = _adamw(two(w), two(g), two(m), two(v), name=f"adamw_{tag}")
        return d.reshape(sh), mn.reshape(sh), vn.reshape(sh)

    big = [("gm_w_in", gm_w_in, g_gm_w_in, m_gm_w_in, v_gm_w_in), ("gm_w_out", gm_w_out, g_gm_w_out, m_gm_w_out, v_gm_w_out),
           ("mla_w_in", mla_w_in, g_mla_w_in, m_mla_w_in, v_mla_w_in), ("mla_w_q_b", mla_w_q_b, g_mla_w_q_b, m_mla_w_q_b, v_mla_w_q_b),
           ("mla_w_kv_b", mla_w_kv_b, g_mla_w_kv_b, m_mla_w_kv_b, v_mla_w_kv_b), ("mla_w_out", mla_w_out, g_mla_w_out, m_mla_w_out, v_mla_w_out),
           ("ffn_w_up", ffn_w_up, g_ffn_w_up, m_ffn_w_up, v_ffn_w_up), ("ffn_w_down", ffn_w_down, g_ffn_w_down, m_ffn_w_down, v_ffn_w_down)]
    res = {}
    for tag, w, g, m, v in big:
        res[tag] = (g,) + adam_big(w, g, m, v, tag)

    small = [("gm_ln_g", gm_ln_g, g_gm_ln_g, m_gm_ln_g, v_gm_ln_g), ("gm_ln_b", gm_ln_b, g_gm_ln_b, m_gm_ln_b, v_gm_ln_b),
             ("gm_w_s", gm_w_s, g_gm_w_s, m_gm_w_s, v_gm_w_s), ("gm_b_s", gm_b_s, g_gm_b_s, m_gm_b_s, v_gm_b_s),
             ("mla_q_norm_g", mla_q_norm_g, g_mla_q_norm_g, m_mla_q_norm_g, v_mla_q_norm_g),
             ("mla_kv_norm_g", mla_kv_norm_g, g_mla_kv_norm_g, m_mla_kv_norm_g, v_mla_kv_norm_g),
             ("ffn_conv_w", ffn_conv_w, g_ffn_conv_w, m_ffn_conv_w, v_ffn_conv_w), ("ffn_conv_b", ffn_conv_b, g_ffn_conv_b, m_ffn_conv_b, v_ffn_conv_b),
             ("ln_mix_g", ln_mix_g, g_ln_mix_g, m_ln_mix_g, v_ln_mix_g), ("ln_mix_b", ln_mix_b, g_ln_mix_b, m_ln_mix_b, v_ln_mix_b),
             ("ln_ffn_g", ln_ffn_g, g_ln_ffn_g, m_ln_ffn_g, v_ln_ffn_g), ("ln_ffn_b", ln_ffn_b, g_ln_ffn_b, m_ln_ffn_b, v_ln_ffn_b)]
    shapes = [t[1].shape for t in small]
    d_s, m_s, v_s = _adamw(_pack([t[1] for t in small]), _pack([t[2] for t in small]), _pack([t[3] for t in small]),
                           _pack([t[4] for t in small]), name="adamw_small")
    d_l, m_l, v_l = _unpack(d_s, shapes), _unpack(m_s, shapes), _unpack(v_s, shapes)
    for (tag, _, g, _, _), d, mn, vn in zip(small, d_l, m_l, v_l):
        res[tag] = (g, d, mn, vn)

    order = ["gm_w_in", "gm_ln_g", "gm_ln_b", "gm_w_s", "gm_b_s", "gm_w_out", "mla_w_in", "mla_q_norm_g", "mla_kv_norm_g",
             "mla_w_q_b", "mla_w_kv_b", "mla_w_out", "ffn_w_up", "ffn_conv_w", "ffn_conv_b", "ffn_w_down",
             "ln_mix_g", "ln_mix_b", "ln_ffn_g", "ln_ffn_b"]
    out = [loss, grad_x[None]]
    for q in range(4):
        out += [res[k][q] for k in order]
    return tuple(out)
```

```python
import jax
import jax.numpy as jnp
from jax import lax
from jax.experimental import pallas as pl
from jax.experimental.pallas import tpu as pltpu
from jax.experimental.pallas import tpu_sc as plsc

F32, BF16 = jnp.float32, jnp.bfloat16

DEPTH = 4
CHUNK = 64
GM_BLOCK = 128
GM_GROUPS = 8
HEADS = 16
NOPE, ROPE, VDIM = 128, 64, 128
QRANK, KVRANK = 512, 512
ROPE_THETA = 10000.0
SM_SCALE = (NOPE + ROPE) ** -0.5
ALPHA = (2 * DEPTH) ** 0.25
LN_EPS = 1e-5
RMS_EPS = 1e-6
ADAM_LR, ADAM_B1, ADAM_B2, ADAM_EPS, ADAM_WD, ADAM_STEP = 0.001, 0.9, 0.999, 1e-08, 0.01, 10

N_DEV = 8
LANES = 128
SUBLANES = 8
VMEM_LIMIT = 56 * 1024 * 1024
MESH = pl.DeviceIdType.MESH
HBM_SPEC = pl.BlockSpec(memory_space=pltpu.HBM)


class _Schedule:
    def __init__(self):
        self.last = None
        self.reducer = None
        self.big = {}
        self.clock = 0.0

    def tick(self, us):
        self.clock += us
        if self.reducer is not None:
            self.reducer.advance()


_SCHED = _Schedule()


def _ticked(out, us):
    _SCHED.tick(us)
    return out


def _call(body, *, name, out_shape, in_specs, out_specs, grid=None, scratch=(), dims=None, n_prefetch=0):
    kw = dict(vmem_limit_bytes=VMEM_LIMIT)
    if dims is not None:
        kw["dimension_semantics"] = dims
    cp = pltpu.CompilerParams(**kw)
    in_specs = list(in_specs)
    token = _SCHED.last
    if token is not None:
        pos = n_prefetch + len(in_specs)
        in_specs.append(pl.BlockSpec(memory_space=pl.ANY))
        inner = body

        def body(*refs):
            return inner(*refs[:pos], *refs[pos + 1:])

    if n_prefetch:
        gs = pltpu.PrefetchScalarGridSpec(num_scalar_prefetch=n_prefetch, grid=grid, in_specs=in_specs, out_specs=out_specs,
                                          scratch_shapes=list(scratch))
        fn = pl.pallas_call(body, name=name, grid_spec=gs, out_shape=out_shape, compiler_params=cp, interpret=False)
    else:
        extra = {} if grid is None else {"grid": grid}
        fn = pl.pallas_call(body, name=name, in_specs=in_specs, out_specs=out_specs, out_shape=out_shape,
                            scratch_shapes=list(scratch), compiler_params=cp, interpret=False, **extra)

    def run(*args):
        out = fn(*args, token) if token is not None else fn(*args)
        _SCHED.last = out[0] if isinstance(out, (list, tuple)) else out
        return out

    return run


def _tile(n, pref, mult=LANES):
    if n <= pref:
        return n
    t = (pref // mult) * mult
    while t >= mult:
        if n % t == 0:
            return t
        t -= mult
    return n


def _sds(shape, dtype):
    return jax.ShapeDtypeStruct(tuple(shape), dtype)


FFN_TILE = 1408
FFN_CHUNK = 128
FFN_SHARD_MULT = 64
MXU_FLOPS_PER_US = 8e8
TOKENS_K = 4096
_DN = {"nn": (((1,), (0,)), ((), ())), "nt": (((1,), (1,)), ((), ())), "tn": (((0,), (0,)), ((), ()))}


def _mm(a, b, mode, *, name, out_dtype, tm=1024, tn=1024, tk=2048, res=None, res_scale=1.0, a_parts=1, b_resident=False):
    if a_parts > 1:
        S_, F_ = a.shape[0] // a_parts, a.shape[1]
        a_shape = (S_, a_parts * F_)
    else:
        a_shape = a.shape
    if mode == "nn":
        (M, K), (K2, N) = a_shape, b.shape
    elif mode == "nt":
        (M, K), (N, K2) = a_shape, b.shape
    else:
        (K, M), (K2, N) = a_shape, b.shape
    assert K == K2, (name, a.shape, b.shape)
    tm, tn, tk = _tile(M, tm), _tile(N, tn), _tile(K, tk)
    nk = K // tk
    if a_parts > 1 and mode == "nn":
        per = F_ // tk
        assert F_ % tk == 0
        a_spec = pl.BlockSpec((tm, tk), lambda i, j, k: ((k // per) * (S_ // tm) + i, k % per))
    elif a_parts > 1:
        per = F_ // tm
        assert mode == "tn" and F_ % tm == 0
        a_spec = pl.BlockSpec((tk, tm), lambda i, j, k: ((i // per) * (S_ // tk) + k, i % per))
    elif mode == "tn":
        a_spec = pl.BlockSpec((tk, tm), lambda i, j, k: (k, i))
    else:
        a_spec = pl.BlockSpec((tm, tk), lambda i, j, k: (i, k))
    b_kw = {"pipeline_mode": pl.Buffered(1)} if b_resident else {}
    b_spec = (pl.BlockSpec((tn, tk), lambda i, j, k: (j, k), **b_kw) if mode == "nt"
              else pl.BlockSpec((tk, tn), lambda i, j, k: (k, j), **b_kw))
    in_specs = [a_spec, b_spec]
    args = [a, b]
    if res is not None:
        in_specs.append(pl.BlockSpec((tm, tn), lambda i, j, k: (i, j)))
        args.append(res)
    dn = _DN[mode]
    has_res = res is not None

    def body(*refs):
        a_ref, b_ref = refs[0], refs[1]
        r_ref = refs[2] if has_res else None
        o_ref = refs[2 + has_res]
        part = lax.dot_general(a_ref[...], b_ref[...], dn, preferred_element_type=F32)

        def finish(acc):
            if has_res:
                acc = acc + res_scale * r_ref[...]
            o_ref[...] = acc.astype(o_ref.dtype)

        if nk == 1:
            finish(part)
        else:
            acc_ref = refs[3 + has_res]
            k = pl.program_id(2)

            @pl.when(k == 0)
            def _():
                acc_ref[...] = part

            @pl.when(k > 0)
            def _():
                acc_ref[...] += part

            @pl.when(k == nk - 1)
            def _():
                finish(acc_ref[...])

    scratch = [pltpu.VMEM((tm, tn), F32)] if nk > 1 else []
    out = _call(body, name=name, grid=(M // tm, N // tn, nk), in_specs=in_specs,
                out_specs=pl.BlockSpec((tm, tn), lambda i, j, k: (i, j)), out_shape=_sds((M, N), out_dtype),
                scratch=scratch, dims=("parallel", "parallel", "arbitrary"))(*args)
    _SCHED.tick(2.0 * M * N * K / MXU_FLOPS_PER_US)
    return out


def _ln_fwd(x, m, g, b, *, name):
    S, D = x.shape
    tr = _tile(S, 256, 8)

    def body(x_ref, m_ref, g_ref, b_ref, y_ref, yb_ref, xh_ref, rs_ref):
        r = ALPHA * x_ref[...] + m_ref[...]
        mu = jnp.mean(r, axis=-1, keepdims=True)
        d = r - mu
        var = jnp.mean(d * d, axis=-1, keepdims=True)
        rstd = lax.rsqrt(var + LN_EPS)
        xh = d * rstd
        y = xh * g_ref[...] + b_ref[...]
        y_ref[...] = y
        yb_ref[...] = y.astype(BF16)
        xh_ref[...] = xh
        rs_ref[...] = rstd

    row = pl.BlockSpec((tr, D), lambda i: (i, 0))
    vec = pl.BlockSpec((1, D), lambda i: (0, 0))
    return _call(body, name=name, grid=(S // tr,), in_specs=[row, row, vec, vec],
                 out_specs=[row, row, row, pl.BlockSpec((tr, 1), lambda i: (i, 0))],
                 out_shape=[_sds((S, D), F32), _sds((S, D), BF16), _sds((S, D), F32), _sds((S, 1), F32)],
                 dims=("parallel",))(x, m, g, b)


def _ln_bwd(dy, xh, rstd, g, *, name):
    S, D = dy.shape
    tr = _tile(S, 256, 8)

    def body(dy_ref, xh_ref, rs_ref, g_ref, dr_ref, drb_ref, dg_ref, db_ref):
        @pl.when(pl.program_id(0) == 0)
        def _():
            dg_ref[...] = jnp.zeros_like(dg_ref)
            db_ref[...] = jnp.zeros_like(db_ref)

        dyv = dy_ref[...]
        xhv = xh_ref[...]
        dxh = dyv * g_ref[...]
        m1 = jnp.mean(dxh, axis=-1, keepdims=True)
        m2 = jnp.mean(dxh * xhv, axis=-1, keepdims=True)
        dr = rs_ref[...] * (dxh - m1 - xhv * m2)
        dr_ref[...] = dr
        drb_ref[...] = dr.astype(BF16)
        dg_ref[...] += jnp.sum(dyv * xhv, axis=0, keepdims=True)
        db_ref[...] += jnp.sum(dyv, axis=0, keepdims=True)

    row = pl.BlockSpec((tr, D), lambda i: (i, 0))
    vec = pl.BlockSpec((1, D), lambda i: (0, 0))
    return _ticked(_call(body, name=name, grid=(S // tr,), in_specs=[row, row, pl.BlockSpec((tr, 1), lambda i: (i, 0)), vec],
                         out_specs=[row, row, vec, vec],
                         out_shape=[_sds((S, D), F32), _sds((S, D), BF16), _sds((1, D), F32), _sds((1, D), F32)],
                         dims=("arbitrary",))(dy, xh, rstd, g), 45.0 * S / 4096)


def _loss_kernel(y, t, *, name):
    S, D = y.shape
    tr = _tile(S, 256, 8)

    def body(y_ref, t_ref, lp_ref, dy_ref):
        @pl.when(pl.program_id(0) == 0)
        def _():
            lp_ref[...] = jnp.zeros_like(lp_ref)

        e = y_ref[...] - t_ref[...]
        dy_ref[...] = e / D
        lp_ref[...] += jnp.sum(e * e, axis=0, keepdims=True)

    row = pl.BlockSpec((tr, D), lambda i: (i, 0))
    vec = pl.BlockSpec((1, D), lambda i: (0, 0))
    return _call(body, name=name, grid=(S // tr,), in_specs=[row, row], out_specs=[vec, row],
                 out_shape=[_sds((1, D), F32), _sds((S, D), F32)], dims=("arbitrary",))(y, t)


_GELU_C = 0.7978845608028654
_GELU_A = 0.044715


def _gelu(x):
    return 0.5 * x * (1.0 + jnp.tanh(_GELU_C * (x + _GELU_A * x * x * x)))


def _gelu_grad(x):
    x2 = x * x
    t = jnp.tanh(_GELU_C * (x + _GELU_A * x * x2))
    return 0.5 * (1.0 + t) + 0.5 * x * (1.0 - t * t) * (_GELU_C * (1.0 + 3.0 * _GELU_A * x2))


def _gelu_both(x):
    x2 = x * x
    t = jnp.tanh(_GELU_C * (x + _GELU_A * x * x2))
    half = 0.5 * (1.0 + t)
    return x * half, half + 0.5 * x * (1.0 - t * t) * (_GELU_C * (1.0 + 3.0 * _GELU_A * x2))


def _masked_ws(w):
    i = lax.broadcasted_iota(jnp.int32, w.shape, 0) // CHUNK
    j = lax.broadcasted_iota(jnp.int32, w.shape, 1) // CHUNK
    return jnp.where(j <= i, w, 0.0)


def _gm_mid_fwd(zp, ln_g, ln_b, w_s, b_st, *, name):
    S, H2 = zp.shape
    H = H2 // 2
    gd = H // GM_GROUPS
    nb = S // GM_BLOCK

    def body(zu_ref, zv_ref, g_ref, b_ref, w_ref, bs_ref, p_ref, s_ref, vh_ref, rs_ref):
        v = _gelu(zv_ref[...].astype(F32))
        mu = jnp.mean(v, axis=-1, keepdims=True)
        d = v - mu
        var = jnp.mean(d * d, axis=-1, keepdims=True)
        rstd = lax.rsqrt(var + LN_EPS)
        vh = d * rstd
        vh_ref[...] = vh.astype(BF16)
        rs_ref[...] = rstd
        vn = (vh * g_ref[...] + b_ref[...]).astype(BF16)
        bs = bs_ref[...]
        for gi in range(GM_GROUPS):
            cs = slice(gi * gd, (gi + 1) * gd)
            wm = _masked_ws(w_ref[gi]).astype(BF16)
            s = jnp.dot(wm, vn[:, cs], preferred_element_type=F32) + bs[:, gi:gi + 1]
            u = _gelu(zu_ref[:, cs].astype(F32))
            s_ref[:, cs] = s.astype(BF16)
            p_ref[:, cs] = (u * s).astype(BF16)

    blk = lambda c: pl.BlockSpec((GM_BLOCK, H), lambda n, c=c: (n, c))
    vec = pl.BlockSpec((1, H), lambda n: (0, 0))
    return _call(body, name=name, grid=(nb,),
                 in_specs=[blk(0), blk(1), vec, vec, pl.BlockSpec((GM_GROUPS, GM_BLOCK, GM_BLOCK), lambda n: (0, 0, 0)),
                           pl.BlockSpec((GM_BLOCK, GM_GROUPS), lambda n: (0, 0))],
                 out_specs=[blk(0), blk(0), blk(0), pl.BlockSpec((GM_BLOCK, 1), lambda n: (n, 0))],
                 out_shape=[_sds((S, H), BF16), _sds((S, H), BF16), _sds((S, H), BF16), _sds((S, 1), F32)],
                 dims=("parallel",))(zp, zp, ln_g, ln_b, w_s, b_st)


def _gm_mid_bwd(dp, zp, s, vhat, rstd, ln_g, ln_b, w_s, *, name):
    S, H2 = zp.shape
    H = H2 // 2
    gd = H // GM_GROUPS
    nb = S // GM_BLOCK

    def body(dp_ref, zu_ref, zv_ref, s_ref, vh_ref, rs_ref, g_ref, b_ref, w_ref,
             dz_ref, dw_ref, dbs_ref, dg_ref, db_ref, dvh_ref):
        @pl.when(pl.program_id(0) == 0)
        def _():
            dw_ref[...] = jnp.zeros_like(dw_ref)
            dbs_ref[...] = jnp.zeros_like(dbs_ref)
            dg_ref[...] = jnp.zeros_like(dg_ref)
            db_ref[...] = jnp.zeros_like(db_ref)

        m1 = jnp.zeros((GM_BLOCK, 1), F32)
        m2 = jnp.zeros((GM_BLOCK, 1), F32)
        for gi in range(GM_GROUPS):
            cs = slice(gi * gd, (gi + 1) * gd)
            dpg = dp_ref[:, cs].astype(F32)
            zu = zu_ref[:, cs].astype(F32)
            u, du_dz = _gelu_both(zu)
            ds = dpg * u
            du = dpg * s_ref[:, cs].astype(F32)
            dz_ref[:, cs] = (du * du_dz).astype(BF16)
            dsb = ds.astype(BF16)
            vh = vh_ref[:, cs].astype(F32)
            lg = g_ref[:, cs]
            vn = (vh * lg + b_ref[:, cs]).astype(BF16)
            wm = _masked_ws(w_ref[gi]).astype(BF16)
            dvn = lax.dot_general(wm, dsb, _DN["tn"], preferred_element_type=F32)
            dw_ref[gi] += lax.dot_general(dsb, vn, _DN["nt"], preferred_element_type=F32)
            dbs_ref[gi] += jnp.sum(ds, axis=1, keepdims=True)
            dg_ref[:, cs] += jnp.sum(dvn * vh, axis=0, keepdims=True)
            db_ref[:, cs] += jnp.sum(dvn, axis=0, keepdims=True)
            dvh = dvn * lg
            dvh_ref[:, cs] = dvh
            m1 = m1 + jnp.sum(dvh, axis=1, keepdims=True)
            m2 = m2 + jnp.sum(dvh * vh, axis=1, keepdims=True)
        m1 = m1 / H
        m2 = m2 / H
        rs = rs_ref[...]
        for gi in range(GM_GROUPS):
            cs = slice(gi * gd, (gi + 1) * gd)
            vh = vh_ref[:, cs].astype(F32)
            dv = rs * (dvh_ref[:, cs] - m1 - vh * m2)
            zv = zv_ref[:, cs].astype(F32)
            dz_ref[:, H + gi * gd:H + (gi + 1) * gd] = (dv * _gelu_grad(zv)).astype(BF16)

        @pl.when(pl.program_id(0) == nb - 1)
        def _():
            for gi in range(GM_GROUPS):
                dw_ref[gi] = _masked_ws(dw_ref[gi])

    blk = lambda c: pl.BlockSpec((GM_BLOCK, H), lambda n, c=c: (n, c))
    vec = pl.BlockSpec((1, H), lambda n: (0, 0))
    wspec = pl.BlockSpec((GM_GROUPS, GM_BLOCK, GM_BLOCK), lambda n: (0, 0, 0))
    return _ticked(_call(body, name=name, grid=(nb,),
                 in_specs=[blk(0), blk(0), blk(1), blk(0), blk(0), pl.BlockSpec((GM_BLOCK, 1), lambda n: (n, 0)), vec, vec, wspec],
                 out_specs=[pl.BlockSpec((GM_BLOCK, H2), lambda n: (n, 0)), wspec,
                            pl.BlockSpec((GM_GROUPS, GM_BLOCK, 1), lambda n: (0, 0, 0)), vec, vec],
                 out_shape=[_sds((S, H2), BF16), _sds((GM_GROUPS, GM_BLOCK, GM_BLOCK), F32),
                            _sds((GM_GROUPS, GM_BLOCK, 1), F32), _sds((1, H), F32), _sds((1, H), F32)],
                 scratch=[pltpu.VMEM((GM_BLOCK, H), F32)], dims=("arbitrary",))(dp, zp, zp, s, vhat, rstd, ln_g, ln_b, w_s),
                   280.0 * S / 4096)


def _shift_down(x, k):
    r = pltpu.roll(x, k, 0)
    rows = lax.broadcasted_iota(jnp.int32, (SUBLANES, x.shape[1]), 0)
    return jnp.concatenate([jnp.where(rows >= k, r[:SUBLANES], 0.0), r[SUBLANES:]], axis=0)


def _shift_up(x, k):
    n = x.shape[0]
    r = pltpu.roll(x, n - k, 0)
    rows = lax.broadcasted_iota(jnp.int32, (SUBLANES, x.shape[1]), 0)
    return jnp.concatenate([r[:n - SUBLANES], jnp.where(rows < SUBLANES - k, r[n - SUBLANES:], 0.0)], axis=0)


def _sigmoid(x):
    return 0.5 * (1.0 + jnp.tanh(0.5 * x))


def _conv(h, w, b):
    return w[0:1, :] * _shift_down(h, 2) + w[1:2, :] * _shift_down(h, 1) + w[2:3, :] * h + b


def _ffn_mid_fwd(h, cw, cb, *, name):
    S, F2 = h.shape
    F = F2 // 2
    nf = F // LANES

    def body(ha_ref, hg_ref, wa_ref, wg_ref, ba_ref, bg_ref, act_ref):
        a = _conv(ha_ref[...].astype(F32), wa_ref[...], ba_ref[...])
        g = _conv(hg_ref[...].astype(F32), wg_ref[...], bg_ref[...])
        act_ref[...] = (g * _sigmoid(g) * a).astype(BF16)

    sl = lambda off, r: pl.BlockSpec((r, LANES), lambda j, off=off: (0, j + off))
    return _call(body, name=name, grid=(nf,),
                 in_specs=[sl(0, S), sl(nf, S), sl(0, 3), sl(nf, 3), sl(0, 1), sl(nf, 1)],
                 out_specs=sl(0, S), out_shape=_sds((S, F), BF16), dims=("parallel",))(h, h, cw, cw, cb, cb)


def _ffn_mid_bwd(dact, h, cw, cb, *, name):
    S, F2 = h.shape
    F = F2 // 2
    nf = F // LANES

    R, HB, HF = FFN_CHUNK, 16, SUBLANES
    nc = S // R

    def body(da_ref, ha_ref, hg_ref, wa_ref, wg_ref, ba_ref, bg_ref, dh_ref, dw_ref, db_ref, a_s, g_s, d_s):
        wa, wg, ba, bg = wa_ref[...], wg_ref[...], ba_ref[...], bg_ref[...]
        zero_tail = jnp.zeros((HF, LANES), F32)
        a_s[S:S + HF, :] = zero_tail
        g_s[S:S + HF, :] = zero_tail
        d_s[S:S + HF, :] = zero_tail

        def conv_chunk(win, w, b):
            return (w[0:1, :] * pltpu.roll(win, 2, 0)[HB:] + w[1:2, :] * pltpu.roll(win, 1, 0)[HB:] + w[2:3, :] * win[HB:] + b)

        def pass1(r0, win_a, win_g):
            rows = pl.ds(r0, R)
            a_s[rows, :] = conv_chunk(win_a, wa, ba)
            g_s[rows, :] = conv_chunk(win_g, wg, bg)
            d_s[rows, :] = da_ref[rows, :].astype(F32)

        lead = jnp.zeros((HB, LANES), F32)
        pass1(0, jnp.concatenate([lead, ha_ref[0:R, :].astype(F32)], axis=0),
              jnp.concatenate([lead, hg_ref[0:R, :].astype(F32)], axis=0))

        @pl.loop(1, nc)
        def _(c):
            r0 = pl.multiple_of(c * R, R)
            win = pl.ds(pl.multiple_of(c * R - HB, HB), R + HB)
            pass1(r0, ha_ref[win, :].astype(F32), hg_ref[win, :].astype(F32))

        def fold(x):
            return jnp.sum(x.reshape(R // SUBLANES, SUBLANES, LANES), axis=0)

        def pass2(c, acc):
            r0 = pl.multiple_of(c * R, R)
            rows, win = pl.ds(r0, R), pl.ds(r0, R + HF)
            a, g, d = a_s[win, :], g_s[win, :], d_s[win, :]
            sg = _sigmoid(g)
            out = []
            for p, dc, w, h_ref in ((0, d * (g * sg), wa, ha_ref), (1, d * a * (sg * (1.0 + g * (1.0 - sg))), wg, hg_ref)):
                up1 = pltpu.roll(dc, R + HF - 1, 0)[0:R]
                up2 = pltpu.roll(dc, R + HF - 2, 0)[0:R]
                dc = dc[0:R]
                dh_ref[p, rows, :] = (w[2:3, :] * dc + w[1:2, :] * up1 + w[0:1, :] * up2).astype(BF16)
                hin = h_ref[rows, :].astype(F32)
                out.append((acc[p][0] + fold(up2 * hin), acc[p][1] + fold(up1 * hin), acc[p][2] + fold(dc * hin), acc[p][3] + fold(dc)))
            return tuple(out)

        z = jnp.zeros((SUBLANES, LANES), F32)
        acc = lax.fori_loop(0, nc, pass2, ((z, z, z, z), (z, z, z, z)))
        for p in range(2):
            for k in range(3):
                dw_ref[p, k:k + 1, :] = jnp.sum(acc[p][k], axis=0, keepdims=True)
            db_ref[p] = jnp.sum(acc[p][3], axis=0, keepdims=True)

    sl = lambda off, r: pl.BlockSpec((r, LANES), lambda j, off=off: (0, j + off))
    out = lambda r: pl.BlockSpec((2, r, LANES), lambda j: (0, 0, j))
    pad = pltpu.VMEM((S + HF, LANES), F32)
    return _ticked(_call(body, name=name, grid=(nf,),
                 in_specs=[sl(0, S), sl(0, S), sl(nf, S), sl(0, 3), sl(nf, 3), sl(0, 1), sl(nf, 1)],
                 out_specs=[out(S), out(3), out(1)],
                 out_shape=[_sds((2, S, F), BF16), _sds((2, 3, F), F32), _sds((2, 1, F), F32)],
                 scratch=[pad, pad, pad], dims=("parallel",))(dact, h, h, cw, cw, cb, cb), 320.0 * S / 4096)


def _swap_half(x):
    lane = lax.broadcasted_iota(jnp.int32, x.shape, 1)
    return jnp.where((lane % ROPE) < ROPE // 2, pltpu.roll(x, LANES - ROPE // 2, 1), pltpu.roll(x, ROPE // 2, 1))


def _rope(x, cos, sin_s):
    return x * cos + _swap_half(x) * sin_s


def _mla_prep(h, gq, gkv, cos, sin_s, *, name):
    S, W = h.shape
    tr = _tile(S, 512, 8)

    def body(h_ref, gq_ref, gkv_ref, cos_ref, sin_ref, cq_ref, ckv_ref, kr_ref, rq_ref, rkv_ref):
        cq = h_ref[:, 0:QRANK]
        rq = lax.rsqrt(jnp.mean(cq * cq, axis=-1, keepdims=True) + RMS_EPS)
        cq_ref[...] = (cq * rq * gq_ref[...]).astype(BF16)
        rq_ref[...] = rq
        ckv = h_ref[:, QRANK:QRANK + KVRANK]
        rkv = lax.rsqrt(jnp.mean(ckv * ckv, axis=-1, keepdims=True) + RMS_EPS)
        ckv_ref[...] = (ckv * rkv * gkv_ref[...]).astype(BF16)
        rkv_ref[...] = rkv
        kr = _rope(h_ref[:, QRANK + KVRANK:W], cos_ref[...], sin_ref[...])
        lane = lax.broadcasted_iota(jnp.int32, kr.shape, 1)
        kr = jnp.where(lane < ROPE, kr, 0.0)
        kr_ref[...] = (kr + pltpu.roll(kr, ROPE, 1)).astype(BF16)

    row = lambda w: pl.BlockSpec((tr, w), lambda i: (i, 0))
    vec = lambda w: pl.BlockSpec((1, w), lambda i: (0, 0))
    return _call(body, name=name, grid=(S // tr,),
                 in_specs=[row(W), vec(QRANK), vec(KVRANK), row(LANES), row(LANES)],
                 out_specs=[row(QRANK), row(KVRANK), row(LANES), row(1), row(1)],
                 out_shape=[_sds((S, QRANK), BF16), _sds((S, KVRANK), BF16), _sds((S, LANES), BF16), _sds((S, 1), F32), _sds((S, 1), F32)],
                 dims=("parallel",))(h, gq, gkv, cos, sin_s)


def _mla_prep_bwd(dcqn, dckvn, dkr2, h, rq, rkv, gq, gkv, cos, sin_neg, *, name):
    S, W = h.shape
    tr = _tile(S, 512, 8)

    def rms_bwd(dy, c, r, g):
        n = c * r
        dn = dy * g
        return r * (dn - n * jnp.mean(dn * n, axis=-1, keepdims=True)), jnp.sum(dy * n, axis=0, keepdims=True)

    def body(dq_ref, dkv_ref, dkr_ref, h_ref, rq_ref, rkv_ref, gq_ref, gkv_ref, cos_ref, sin_ref, dh_ref, dgq_ref, dgkv_ref):
        @pl.when(pl.program_id(0) == 0)
        def _():
            dgq_ref[...] = jnp.zeros_like(dgq_ref)
            dgkv_ref[...] = jnp.zeros_like(dgkv_ref)

        dcq, dg = rms_bwd(dq_ref[...], h_ref[:, 0:QRANK], rq_ref[...], gq_ref[...])
        dgq_ref[...] += dg
        dh_ref[:, 0:QRANK] = dcq.astype(BF16)
        dckv, dg = rms_bwd(dkv_ref[...], h_ref[:, QRANK:QRANK + KVRANK], rkv_ref[...], gkv_ref[...])
        dgkv_ref[...] += dg
        dh_ref[:, QRANK:QRANK + KVRANK] = dckv.astype(BF16)
        dk = dkr_ref[...]
        dk = dk + pltpu.roll(dk, ROPE, 1)
        dk = _rope(dk, cos_ref[...], sin_ref[...])
        lane = lax.broadcasted_iota(jnp.int32, dk.shape, 1)
        dh_ref[:, QRANK + KVRANK:W] = jnp.where(lane < ROPE, dk, 0.0).astype(BF16)

    row = lambda w: pl.BlockSpec((tr, w), lambda i: (i, 0))
    vec = lambda w: pl.BlockSpec((1, w), lambda i: (0, 0))
    return _call(body, name=name, grid=(S // tr,),
                 in_specs=[row(QRANK), row(KVRANK), row(LANES), row(W), row(1), row(1), vec(QRANK), vec(KVRANK), row(LANES), row(LANES)],
                 out_specs=[row(W), vec(QRANK), vec(KVRANK)],
                 out_shape=[_sds((S, W), BF16), _sds((1, QRANK), F32), _sds((1, KVRANK), F32)],
                 dims=("arbitrary",))(dcqn, dckvn, dkr2, h, rq, rkv, gq, gkv, cos, sin_neg)


QPAIR = 2 * NOPE + 2 * ROPE
KVPAIR = 2 * (NOPE + VDIM)


ATT_T = 512
LOG2_E = 1.4426950408889634


def _att_scores(kc, qc, allowed):
    s = lax.dot_general(kc, qc, _DN["nt"], preferred_element_type=F32) * (SM_SCALE * LOG2_E)
    return s if allowed is None else jnp.where(allowed, s, -jnp.inf)


def _att_allowed(k0, q0, t):
    kpos = k0 + lax.broadcasted_iota(jnp.int32, (t, t), 0)
    qpos = q0 + lax.broadcasted_iota(jnp.int32, (t, t), 1)
    return (kpos // CHUNK) <= (qpos // CHUNK)


def _head_mask(r, hh):
    lane = lax.broadcasted_iota(jnp.int32, r.shape, 1)
    keep = (lane < ROPE) if hh == 0 else (lane >= ROPE)
    return jnp.where(keep, r, jnp.zeros_like(r))


def _att_q(q_ref):
    r = q_ref[:, 2 * NOPE:QPAIR]
    return [jnp.concatenate([q_ref[:, hh * NOPE:(hh + 1) * NOPE], _head_mask(r, hh)], axis=1) for hh in range(2)]


def _attn_fwd(q, kv, kr2, cos, sin_s, *, name):
    S = q.shape[0]
    t = _tile(S, ATT_T, 8)
    nq = S // t
    npair = HEADS // 2

    def body(qf_ref, kv_ref, kr_ref, cos_ref, sin_ref, o_ref, lse_ref, q_ref):
        i = pl.program_id(1)
        q0 = i * t
        q_ref[:, 0:2 * NOPE] = qf_ref[:, 0:2 * NOPE].astype(BF16)
        q_ref[:, 2 * NOPE:QPAIR] = _rope(qf_ref[:, 2 * NOPE:QPAIR], cos_ref[...], sin_ref[...]).astype(BF16)
        qc = _att_q(q_ref)

        def step(j, carry, masked):
            k0 = pl.multiple_of(j * t, t)
            rows = pl.ds(k0, t)
            kr = kr_ref[rows, :]
            allowed = _att_allowed(k0, q0, t) if masked else None
            out = []
            for hh in range(2):
                m, l, acc = carry[hh]
                c0 = hh * (NOPE + VDIM)
                kc = jnp.concatenate([kv_ref[rows, c0:c0 + NOPE], kr], axis=1)
                v = kv_ref[rows, c0 + NOPE:c0 + NOPE + VDIM]
                s = _att_scores(kc, qc[hh], allowed)
                m_new = jnp.maximum(m, jnp.max(s, axis=0, keepdims=True))
                p = jnp.exp2(s - m_new)
                a = jnp.exp2(m - m_new)
                l = a * l + jnp.sum(p, axis=0, keepdims=True)
                acc = a * acc + lax.dot_general(v, p.astype(BF16), _DN["tn"], preferred_element_type=F32)
                out.append((m_new, l, acc))
            return tuple(out)

        one = (jnp.full((1, t), -jnp.inf, F32), jnp.zeros((1, t), F32), jnp.zeros((VDIM, t), F32))
        carry = lax.fori_loop(0, i, lambda j, c: step(j, c, False), (one, one))
        carry = step(i, carry, True)
        for hh in range(2):
            m, l, acc = carry[hh]
            o_ref[:, hh * VDIM:(hh + 1) * VDIM] = jnp.transpose(acc / l).astype(BF16)
            lse_ref[hh:hh + 1, :] = m + jnp.log2(l)

    qspec = pl.BlockSpec((t, QPAIR), lambda p, i: (i, p))
    tab = pl.BlockSpec((t, LANES), lambda p, i: (i, 0))
    return _call(body, name=name, grid=(npair, nq),
                 in_specs=[qspec, pl.BlockSpec((S, KVPAIR), lambda p, i: (0, p)), pl.BlockSpec((S, LANES), lambda p, i: (0, 0)), tab, tab],
                 out_specs=[pl.BlockSpec((t, 2 * VDIM), lambda p, i: (i, p)), pl.BlockSpec((None, 2, t), lambda p, i: (p, 0, i)), qspec],
                 out_shape=[_sds((S, HEADS * VDIM), BF16), _sds((npair, 2, S), F32), _sds(q.shape, BF16)],
                 dims=("parallel", "arbitrary"))(q, kv, kr2, cos, sin_s)


def _attn_bwd(qb, kv, kr2, o, do, lse, cos, sin_neg, *, name):
    S = qb.shape[0]
    t = _tile(S, ATT_T, 8)
    nq = S // t
    npair = HEADS // 2

    def body(q_ref, kv_ref, kr_ref, o_ref, do_ref, lse_ref, cos_ref, sin_ref, dq_ref, dkv_ref, dkr_ref, acc_ref):
        pid = pl.program_id(0)
        i = pl.program_id(1)
        q0 = i * t

        @pl.when(i == 0)
        def _():
            acc_ref[...] = jnp.zeros_like(acc_ref)

        @pl.when((i == 0) & (pid == 0))
        def _():
            dkr_ref[...] = jnp.zeros_like(dkr_ref)

        qc = _att_q(q_ref)
        dov = [do_ref[:, hh * VDIM:(hh + 1) * VDIM] for hh in range(2)]
        dsum = [jnp.sum(jnp.transpose(dov[hh].astype(F32) * o_ref[:, hh * VDIM:(hh + 1) * VDIM].astype(F32)), axis=0, keepdims=True)
                for hh in range(2)]
        lse_v = [lse_ref[hh:hh + 1, :] for hh in range(2)]

        def step(j, carry, masked):
            k0 = pl.multiple_of(j * t, t)
            rows = pl.ds(k0, t)
            kr = kr_ref[rows, :]
            allowed = _att_allowed(k0, q0, t) if masked else None
            out = []
            dkr = jnp.zeros((t, LANES), F32)
            for hh in range(2):
                c0 = hh * (NOPE + VDIM)
                kc = jnp.concatenate([kv_ref[rows, c0:c0 + NOPE], kr], axis=1)
                v = kv_ref[rows, c0 + NOPE:c0 + NOPE + VDIM]
                p = jnp.exp2(_att_scores(kc, qc[hh], allowed) - lse_v[hh])
                dp = lax.dot_general(v, dov[hh], _DN["nt"], preferred_element_type=F32)
                ds = (p * (dp - dsum[hh]) * SM_SCALE).astype(BF16)
                acc_ref[rows, c0 + NOPE:c0 + NOPE + VDIM] += jnp.dot(p.astype(BF16), dov[hh], preferred_element_type=F32)
                dkc = jnp.dot(ds, qc[hh], preferred_element_type=F32)
                acc_ref[rows, c0:c0 + NOPE] += dkc[:, 0:NOPE]
                dkr = dkr + dkc[:, NOPE:]
                out.append(carry[hh] + lax.dot_general(ds, kc, _DN["tn"], preferred_element_type=F32))
            dkr_ref[rows, :] += dkr
            return tuple(out)

        zero = jnp.zeros((t, NOPE + LANES), F32)
        carry = lax.fori_loop(0, i, lambda j, c: step(j, c, False), (zero, zero))
        dqc = step(i, carry, True)
        for hh in range(2):
            dq_ref[:, hh * NOPE:(hh + 1) * NOPE] = dqc[hh][:, 0:NOPE].astype(BF16)
        dr = _head_mask(dqc[0][:, NOPE:], 0) + _head_mask(dqc[1][:, NOPE:], 1)
        dq_ref[:, 2 * NOPE:QPAIR] = _rope(dr, cos_ref[...], sin_ref[...]).astype(BF16)

        @pl.when(i == nq - 1)
        def _():
            dkv_ref[...] = acc_ref[...].astype(BF16)

    qspec = pl.BlockSpec((t, QPAIR), lambda p, i: (i, p))
    hspec = pl.BlockSpec((t, 2 * VDIM), lambda p, i: (i, p))
    kvspec = pl.BlockSpec((S, KVPAIR), lambda p, i: (0, p))
    krspec = pl.BlockSpec((S, LANES), lambda p, i: (0, 0))
    tab = pl.BlockSpec((t, LANES), lambda p, i: (i, 0))
    return _ticked(_call(body, name=name, grid=(npair, nq),
                 in_specs=[qspec, kvspec, krspec, hspec, hspec, pl.BlockSpec((None, 2, t), lambda p, i: (p, 0, i)), tab, tab],
                 out_specs=[qspec, kvspec, krspec],
                 out_shape=[_sds(qb.shape, BF16), _sds(kv.shape, BF16), _sds((S, LANES), F32)],
                 scratch=[pltpu.VMEM((S, KVPAIR), F32)], dims=("arbitrary", "arbitrary"))(qb, kv, kr2, o, do, lse, cos, sin_neg),
                   640.0 * (S / 4096) ** 2)


def _coords():
    return lax.axis_index("x"), lax.axis_index("y"), lax.axis_index("c")


def _all_gather_body(n, shake):
    def body(*refs):
        ins, outs = refs[:n], refs[n:2 * n]
        send_sems, recv_sems, local_sems = refs[2 * n:]
        x, y, c = _coords()
        me, sib = (x, y, c), (x, y, 1 - c)
        chips = [(1 - x, y), (x, 1 - y), (1 - x, 1 - y)]
        if shake:
            _handshake([sib] + [(*chip, c) for chip in chips])

        def copy(a, k, block, to, src=None):
            dst = outs[a].at[4 * block[0] + 2 * block[1] + block[2]]
            return pltpu.make_async_remote_copy(src_ref=dst if src is None else src, dst_ref=dst,
                                                send_sem=send_sems.at[a, k], recv_sem=recv_sems.at[a, k],
                                                device_id=to, device_id_type=MESH)

        mine = [pltpu.make_async_copy(ins[a], outs[a].at[4 * x + 2 * y + c], local_sems.at[a]) for a in range(n)]
        for cp in mine:
            cp.start()
        sends = []
        for a in range(n):
            sends.append(copy(a, 0, me, sib, src=ins[a]))
            sends += [copy(a, 1 + j, me, (*chip, c), src=ins[a]) for j, chip in enumerate(chips)]
        for cp in sends:
            cp.start()
        for j, chip in enumerate(chips):
            for a in range(n):
                copy(a, 1 + j, (*chip, c), me).wait_recv()
                fwd = copy(a, 4 + j, (*chip, c), sib)
                fwd.start()
                sends.append(fwd)
        for a in range(n):
            copy(a, 0, sib, me).wait_recv()
            for j, chip in enumerate(chips):
                copy(a, 4 + j, (*chip, 1 - c), me).wait_recv()
        for cp in sends:
            cp.wait_send()
        for cp in mine:
            cp.wait()

    return body


def _all_gather(arrs, *, name):
    n = len(arrs)
    outs = _call(_all_gather_body(n, False), name=name, in_specs=[HBM_SPEC] * n, out_specs=[HBM_SPEC] * n,
                 out_shape=[_sds((N_DEV,) + a.shape, a.dtype) for a in arrs],
                 scratch=[pltpu.SemaphoreType.DMA((n, 7)), pltpu.SemaphoreType.DMA((n, 7)), pltpu.SemaphoreType.DMA((n,))])(*arrs)
    return list(outs)


def _sequencer_call(body, *, name, out_type, n_sems, collective_id):
    return pl.kernel(body, out_type=out_type, mesh=plsc.ScalarSubcoreMesh(axis_name="sq", num_cores=1), name=name,
                     scratch_types=[pltpu.SemaphoreType.DMA(n_sems), pltpu.SemaphoreType.DMA(n_sems), pltpu.SemaphoreType.DMA((n_sems[0],))],
                     compiler_params=pltpu.CompilerParams(collective_id=collective_id))


def _handshake(peers):
    barrier = pltpu.get_barrier_semaphore()
    for p in peers:
        pl.semaphore_signal(barrier, inc=1, device_id=p, device_id_type=MESH)
    pl.semaphore_wait(barrier, len(peers))


AG_ID, PAIR_ID, CHIP_ID = 1, 2, 3


def _all_gather_sc(arrs, *, name):
    n = len(arrs)
    outs = _sequencer_call(_all_gather_body(n, True), name=name, out_type=[_sds((N_DEV,) + a.shape, a.dtype) for a in arrs],
                           n_sems=(n, 7), collective_id=AG_ID)(*arrs)
    return list(outs)


def _pair_exchange_sc(gs, *, name):
    n = len(gs)

    def body(*refs):
        ins, outs = refs[:n], refs[n:2 * n]
        send_sems, recv_sems, _ = refs[2 * n:]
        x, y, c = _coords()
        sib = (x, y, 1 - c)
        _handshake([sib])
        cps = []
        for a in range(n):
            for j in range(4):
                cps.append(pltpu.make_async_remote_copy(src_ref=ins[a].at[2 * j + (1 - c)], dst_ref=outs[a].at[j],
                                                        send_sem=send_sems.at[a, j], recv_sem=recv_sems.at[a, j],
                                                        device_id=sib, device_id_type=MESH))
        for cp in cps:
            cp.start()
        for cp in cps:
            cp.wait()

    outs = _sequencer_call(body, name=name, out_type=[_sds((4,) + g.shape[1:], g.dtype) for g in gs],
                           n_sems=(n, 4), collective_id=PAIR_ID)(*gs)
    return list(outs)


def _chip_exchange_sc(ps, *, name):
    n = len(ps)

    def body(*refs):
        ins, outs = refs[:n], refs[n:2 * n]
        send_sems, recv_sems, _ = refs[2 * n:]
        x, y, c = _coords()
        chips = [(1 - x, y), (x, 1 - y), (1 - x, 1 - y)]
        _handshake([(*chip, c) for chip in chips])
        cps = []
        for a in range(n):
            for r, (px, py) in enumerate(chips):
                cps.append(pltpu.make_async_remote_copy(src_ref=ins[a].at[2 * px + py], dst_ref=outs[a].at[r],
                                                        send_sem=send_sems.at[a, r], recv_sem=recv_sems.at[a, r],
                                                        device_id=(px, py, c), device_id_type=MESH))
        for cp in cps:
            cp.start()
        for cp in cps:
            cp.wait()

    outs = _sequencer_call(body, name=name, out_type=[_sds((3,) + p.shape[1:], p.dtype) for p in ps],
                           n_sems=(n, 3), collective_id=CHIP_ID)(*ps)
    return list(outs)


def _row_tile(R, C, budget=1 << 20, mult=16):
    want = max(mult, budget // (4 * C))
    if R <= want:
        return R
    t = (want // mult) * mult
    while t >= mult:
        if R % t == 0:
            return t
        t -= mult
    return R


RS_TILE_BYTES = 6 << 20


def _pair_sum(g, l1, c_idx, *, name):
    _, R, C = g.shape
    tr = _row_tile(R, C, budget=RS_TILE_BYTES)
    g4 = g.reshape(4, 2, R, C)

    def body(c_ref, g_ref, l_ref, o_ref):
        o_ref[...] = (g_ref[...].astype(F32) + l_ref[...].astype(F32)).astype(o_ref.dtype)

    return _call(body, name=name, n_prefetch=1, grid=(4, R // tr),
                 in_specs=[pl.BlockSpec((None, None, tr, C), lambda j, r, c_ref: (j, c_ref[0], r, 0)),
                           pl.BlockSpec((None, tr, C), lambda j, r, c_ref: (j, r, 0))],
                 out_specs=pl.BlockSpec((None, tr, C), lambda j, r, c_ref: (j, r, 0)),
                 out_shape=_sds((4, R, C), g.dtype))(c_idx, g4, l1)


def _chip_sum(p, l2, chip_idx, *, name):
    _, R, C = p.shape
    tr = _row_tile(R, C, budget=RS_TILE_BYTES)

    def body(j_ref, p_ref, l_ref, o_ref):
        o_ref[...] = ((p_ref[...].astype(F32) + l_ref[0].astype(F32)) + l_ref[1].astype(F32)) + l_ref[2].astype(F32)

    return _call(body, name=name, n_prefetch=1, grid=(R // tr,),
                 in_specs=[pl.BlockSpec((None, tr, C), lambda r, j_ref: (j_ref[0], r, 0)),
                           pl.BlockSpec((3, tr, C), lambda r, j_ref: (0, r, 0))],
                 out_specs=pl.BlockSpec((tr, C), lambda r, j_ref: (r, 0)), out_shape=_sds((R, C), F32))(chip_idx, p, l2)


class _Reducer:
    PAIR_US = (12.0, 7.0)
    CHIP_US = (15.0, 43.0)

    def __init__(self, c_idx, chip_idx):
        self.c_idx, self.chip_idx = c_idx, chip_idx
        self.units, self.out, self.sc_free = [], {}, 0.0

    def _sequencer_done(self, cost):
        self.sc_free = max(self.sc_free, _SCHED.clock) + cost[0] + cost[1] * self._mb
        return self.sc_free

    def submit(self, name, g):
        R = g.shape[0] // N_DEV
        g8 = g.reshape((N_DEV, R, g.shape[1]))
        self._mb = R * g.shape[1] * g.dtype.itemsize / 1e6
        (l1,) = _pair_exchange_sc([g8], name=f"rs_pair_{name}")
        self.units.append(dict(name=name, g8=g8, l1=l1, mb=self._mb, stage=1, ready=self._sequencer_done(self.PAIR_US)))

    def advance(self, force=False):
        for u in self.units:
            if not (force or _SCHED.clock >= u["ready"]):
                continue
            if u["stage"] == 1:
                p = _pair_sum(u.pop("g8"), u.pop("l1"), self.c_idx, name=f"rs_psum_{u['name']}")
                (l2,) = _chip_exchange_sc([p], name=f"rs_chip_{u['name']}")
                self._mb = u["mb"]
                u.update(p=p, l2=l2, stage=2, ready=self._sequencer_done(self.CHIP_US))
            elif u["stage"] == 2:
                self.out[u["name"]] = _chip_sum(u.pop("p"), u.pop("l2"), self.chip_idx, name=f"rs_csum_{u['name']}")
                u["stage"] = 3

    def finish(self):
        self.advance(force=True)
        self.advance(force=True)
        return self.out


def _grad_ready(name, g):
    _SCHED.big[name] = g
    if _SCHED.reducer is not None:
        _SCHED.reducer.submit(name, g)
    return g


def _sum8(a, *, name):
    _, R, C = a.shape
    tr = _row_tile(R, C, budget=1 << 18, mult=8)

    def body(a_ref, o_ref):
        acc = a_ref[0]
        for k in range(1, N_DEV):
            acc = acc + a_ref[k]
        o_ref[...] = acc

    return _call(body, name=name, grid=(R // tr,), in_specs=[pl.BlockSpec((N_DEV, tr, C), lambda r: (0, r, 0))],
                 out_specs=pl.BlockSpec((tr, C), lambda r: (r, 0)), out_shape=_sds((R, C), F32), dims=("parallel",))(a)


def _adamw(w, g, m, v, *, name):
    R, C = w.shape
    tr = _row_tile(R, C, budget=2 << 20, mult=8)
    c1 = 1.0 - ADAM_B1 ** ADAM_STEP
    c2 = 1.0 - ADAM_B2 ** ADAM_STEP

    def body(w_ref, g_ref, m_ref, v_ref, d_ref, mo_ref, vo_ref):
        gv = g_ref[...]
        mn = ADAM_B1 * m_ref[...] + (1.0 - ADAM_B1) * gv
        vn = ADAM_B2 * v_ref[...] + (1.0 - ADAM_B2) * (gv * gv)
        mo_ref[...] = mn
        vo_ref[...] = vn
        d_ref[...] = -ADAM_LR * ((mn / c1) / (jnp.sqrt(vn / c2) + ADAM_EPS) + ADAM_WD * w_ref[...])

    blk = pl.BlockSpec((tr, C), lambda r: (r, 0))
    return _call(body, name=name, grid=(R // tr,), in_specs=[blk] * 4, out_specs=[blk] * 3,
                 out_shape=[_sds((R, C), F32)] * 3, dims=("parallel",))(w, g, m, v)


def _gmlp_fwd(xb, W, sp, tag):
    zp = _mm(xb, W["w_in_t"], "nt", name=f"gm_zp_{tag}", out_dtype=BF16)
    p, s, vhat, rstd = _gm_mid_fwd(zp, sp["ln_g"], sp["ln_b"], sp["w_s"], sp["b_st"], name=f"gm_mid_{tag}")
    m = _mm(p, W["w_out"], "nn", name=f"gm_out_{tag}", out_dtype=F32, tm=512, tk=p.shape[1])
    return m, dict(xb=xb, zp=zp, p=p, s=s, vhat=vhat, rstd=rstd)


def _gmlp_bwd(drb, dr, W, sp, sv, tag, on_small=None):
    _grad_ready(f"{tag}_w_out", _mm(sv["p"], drb, "tn", name=f"gm_dwout_{tag}", out_dtype=BF16, tk=TOKENS_K))
    dp = _mm(drb, W["w_out"], "nt", name=f"gm_dp_{tag}", out_dtype=BF16)
    dzp, dws, dbs, dlg, dlb = _gm_mid_bwd(dp, sv["zp"], sv["s"], sv["vhat"], sv["rstd"], sp["ln_g"], sp["ln_b"], sp["w_s"],
                                          name=f"gm_midb_{tag}")
    small = dict(w_s=dws, b_s=dbs, ln_g=dlg, ln_b=dlb)
    if on_small is not None:
        on_small(small)
    _grad_ready(f"{tag}_w_in_t", _mm(dzp, sv["xb"], "tn", name=f"gm_dwin_{tag}", out_dtype=BF16, tk=TOKENS_K))
    dx = _mm(dzp, W["w_in_t"], "nn", name=f"gm_dx_{tag}", out_dtype=F32, res=dr, res_scale=ALPHA)
    return dx, small


def _mla_fwd(xb, W, sp, rope, tag):
    cos, sin_s, _ = rope
    h = _mm(xb, W["w_in"], "nn", name=f"mla_h_{tag}", out_dtype=F32, tn=W["w_in"].shape[1])
    cqn, ckvn, kr2, rq, rkv = _mla_prep(h, sp["gq"], sp["gkv"], cos, sin_s, name=f"mla_prep_{tag}")
    q = _mm(cqn, W["w_qb_t"], "nt", name=f"mla_q_{tag}", out_dtype=F32)
    kv = _mm(ckvn, W["w_kvb_t"], "nt", name=f"mla_kv_{tag}", out_dtype=BF16)
    o, lse, qb = _attn_fwd(q, kv, kr2, cos, sin_s, name=f"mla_attn_{tag}")
    m = _mm(o, W["w_out"], "nn", name=f"mla_out_{tag}", out_dtype=F32)
    return m, dict(xb=xb, h=h, cqn=cqn, ckvn=ckvn, kr2=kr2, rq=rq, rkv=rkv, qb=qb, kv=kv, o=o, lse=lse)


def _mla_bwd(drb, dr, W, sp, rope, sv, tag):
    cos, _, sin_neg = rope
    _grad_ready(f"{tag}_w_out", _mm(sv["o"], drb, "tn", name=f"mla_dwout_{tag}", out_dtype=BF16, tk=TOKENS_K))
    do = _mm(drb, W["w_out"], "nt", name=f"mla_do_{tag}", out_dtype=BF16)
    dqb, dkv, dkr2 = _attn_bwd(sv["qb"], sv["kv"], sv["kr2"], sv["o"], do, sv["lse"], cos, sin_neg, name=f"mla_attnb_{tag}")
    _grad_ready(f"{tag}_w_qb_t", _mm(dqb, sv["cqn"], "tn", name=f"mla_dwqb_{tag}", out_dtype=BF16, tk=TOKENS_K))
    _grad_ready(f"{tag}_w_kvb_t", _mm(dkv, sv["ckvn"], "tn", name=f"mla_dwkvb_{tag}", out_dtype=BF16, tk=TOKENS_K))
    dcqn = _mm(dqb, W["w_qb_t"], "nn", name=f"mla_dcq_{tag}", out_dtype=F32, tk=dqb.shape[1])
    dckvn = _mm(dkv, W["w_kvb_t"], "nn", name=f"mla_dckv_{tag}", out_dtype=F32, tk=dkv.shape[1])
    dh, dgq, dgkv = _mla_prep_bwd(dcqn, dckvn, dkr2, sv["h"], sv["rq"], sv["rkv"], sp["gq"], sp["gkv"], cos, sin_neg,
                                  name=f"mla_prepb_{tag}")
    _grad_ready(f"{tag}_w_in", _mm(sv["xb"], dh, "tn", name=f"mla_dwin_{tag}", out_dtype=BF16, tn=dh.shape[1], tk=TOKENS_K))
    dx = _mm(dh, W["w_in"], "nt", name=f"mla_dx_{tag}", out_dtype=F32, tk=dh.shape[1], res=dr, res_scale=ALPHA)
    return dx, dict(gq=dgq, gkv=dgkv)


def _ffn_fwd(xb, W, sp, tag):
    S = xb.shape[0]
    F = W["w_down"].shape[0]
    h = _mm(xb, W["w_up_t"], "nt", name=f"ffn_h_{tag}", out_dtype=BF16)
    act = _ffn_mid_fwd(h, sp["cw"], sp["cb"], name=f"ffn_mid_{tag}")
    f = _mm(act, W["w_down"], "nn", name=f"ffn_out_{tag}", out_dtype=F32, tk=F // 2)
    return f, dict(xb=xb, h=h, act=act)


def _ffn_bwd(drb, dr, W, sp, sv, tag):
    F = W["w_down"].shape[0]
    S, D = drb.shape
    tf = _tile(F, FFN_TILE)
    _grad_ready(f"{tag}_w_down", _mm(sv["act"], drb, "tn", name=f"ffn_dwdown_{tag}", out_dtype=BF16, tm=tf, tn=512, tk=TOKENS_K))
    dact = _mm(drb, W["w_down"], "nt", name=f"ffn_dact_{tag}", out_dtype=BF16, tm=512, tn=F, tk=D, b_resident=True)
    dh, dcw, dcb = _ffn_mid_bwd(dact, sv["h"], sp["cw"], sp["cb"], name=f"ffn_midb_{tag}")
    dh = dh.reshape(2 * S, F)
    _grad_ready(f"{tag}_w_up_t", _mm(dh, sv["xb"], "tn", name=f"ffn_dwup_{tag}", out_dtype=BF16, tm=tf, tn=512, tk=TOKENS_K, a_parts=2))
    dx = _mm(dh, W["w_up_t"], "nn", name=f"ffn_dx_{tag}", out_dtype=F32, res=dr, res_scale=ALPHA, a_parts=2, tk=F // 2)
    dcw = jnp.concatenate([dcw[0], dcw[1]], axis=1)
    dcb = jnp.concatenate([dcb[0], dcb[1]], axis=1)
    return dx, dict(cw=dcw, cb=dcb)


def _rope_tables(S):
    half = ROPE // 2
    inv_freq = ROPE_THETA ** (-jnp.arange(half, dtype=F32) / half)
    ang = jnp.arange(S, dtype=F32)[:, None] * inv_freq[None, :]
    cos, sin = jnp.cos(ang), jnp.sin(ang)
    cos128 = jnp.concatenate([cos] * 4, axis=1)
    sin128 = jnp.concatenate([-sin, sin, -sin, sin], axis=1)
    return cos128, sin128, -sin128


def _fwd_bwd(x, tgt, Wm, Wf, spm, spf, ln, on_small=None, late_params=None):
    S, D = x.shape
    _SCHED.last, _SCHED.clock = None, 0.0
    rope = _rope_tables(S)
    xf, xb = x, x.astype(BF16)
    saved = []
    for i in range(DEPTH):
        if i % 2 == 0:
            m, svm = _gmlp_fwd(xb, Wm[i], spm[i], f"l{i}")
        else:
            m, svm = _mla_fwd(xb, Wm[i], spm[i], rope, f"l{i}")
        if i == 0 and late_params is not None:
            spm, spf = late_params(m)
        y, yb, xh1, rs1 = _ln_fwd(xf, m, ln["mix_g"][i], ln["mix_b"][i], name=f"ln_mix_{i}")
        f, svf = _ffn_fwd(yb, Wf[i], spf[i], f"l{i}")
        z, zb, xh2, rs2 = _ln_fwd(y, f, ln["ffn_g"][i], ln["ffn_b"][i], name=f"ln_ffn_{i}")
        saved.append((svm, xh1, rs1, svf, xh2, rs2))
        xf, xb = z, zb
    lp, dy = _loss_kernel(xf, tgt, name="loss")
    gm_small, gf_small, gln = [None] * DEPTH, [None] * DEPTH, [None] * DEPTH
    _SCHED.big = {}

    for i in reversed(range(DEPTH)):
        svm, xh1, rs1, svf, xh2, rs2 = saved[i]
        dr, drb, dg2, db2 = _ln_bwd(dy, xh2, rs2, ln["ffn_g"][i], name=f"lnb_ffn_{i}")
        dy, gf_small[i] = _ffn_bwd(drb, dr, Wf[i], spf[i], svf, f"l{i}")
        dr, drb, dg1, db1 = _ln_bwd(dy, xh1, rs1, ln["mix_g"][i], name=f"lnb_mix_{i}")
        gln[i] = dict(mix_g=dg1, mix_b=db1, ffn_g=dg2, ffn_b=db2)
        if i % 2 == 0:
            hook = None if (i > 0 or on_small is None) else (lambda ms: on_small(lp, [ms] + gm_small[1:], gf_small, gln))
            dy, gm_small[i] = _gmlp_bwd(drb, dr, Wm[i], spm[i], svm, f"l{i}", hook)
        else:
            dy, gm_small[i] = _mla_bwd(drb, dr, Wm[i], spm[i], rope, svm, f"l{i}")
    return lp, dy, _SCHED.big, gm_small, gf_small, gln


def _perm_q_rows(wt):
    hd = NOPE + ROPE
    return jnp.concatenate([wt[0:NOPE], wt[hd:hd + NOPE], wt[NOPE:hd], wt[hd + NOPE:2 * hd]], axis=0)


def _unperm_q_rows(wt):
    return jnp.concatenate([wt[0:NOPE], wt[2 * NOPE:2 * NOPE + ROPE], wt[NOPE:2 * NOPE], wt[2 * NOPE + ROPE:]], axis=0)


def _pad_cols(w, to):
    return jnp.pad(w, ((0, 0), (0, to - w.shape[1])))


def _pad_pieces(a, axis, piece, piece_pad):
    n = a.shape[axis] // piece
    sh = a.shape[:axis] + (n, piece) + a.shape[axis + 1:]
    pad = [(0, 0)] * (len(sh))
    pad[axis + 1] = (0, piece_pad - piece)
    return jnp.pad(a.reshape(sh), pad).reshape(a.shape[:axis] + (n * piece_pad,) + a.shape[axis + 1:])


def _unpad_pieces(a, axis, piece, piece_pad):
    n = a.shape[axis] // piece_pad
    sh = a.shape[:axis] + (n, piece_pad) + a.shape[axis + 1:]
    return lax.slice_in_dim(a.reshape(sh), 0, piece, axis=axis + 1).reshape(a.shape[:axis] + (n * piece,) + a.shape[axis + 1:])


def _pack(parts, rows_mult=8):
    flat = jnp.concatenate([p.reshape(-1).astype(F32) for p in parts])
    n = flat.shape[0]
    per = LANES * rows_mult
    tot = ((n + per - 1) // per) * per
    return jnp.pad(flat, (0, tot - n)).reshape(tot // LANES, LANES)


def _unpack(buf, shapes):
    flat = buf.reshape(-1)
    out, off = [], 0
    for s in shapes:
        n = 1
        for d in s:
            n *= d
        out.append(flat[off:off + n].reshape(s))
        off += n
    return out


def kernel(x, gm_w_in, gm_ln_g, gm_ln_b, gm_w_s, gm_b_s, gm_w_out, mla_w_in, mla_q_norm_g, mla_kv_norm_g, mla_w_q_b, mla_w_kv_b, mla_w_out, ffn_w_up, ffn_conv_w, ffn_conv_b, ffn_w_down, ln_mix_g, ln_mix_b, ln_ffn_g, ln_ffn_b, loss_target, m_gm_w_in, m_gm_ln_g, m_gm_ln_b, m_gm_w_s, m_gm_b_s, m_gm_w_out, m_mla_w_in, m_mla_q_norm_g, m_mla_kv_norm_g, m_mla_w_q_b, m_mla_w_kv_b, m_mla_w_out, m_ffn_w_up, m_ffn_conv_w, m_ffn_conv_b, m_ffn_w_down, m_ln_mix_g, m_ln_mix_b, m_ln_ffn_g, m_ln_ffn_b, v_gm_w_in, v_gm_ln_g, v_gm_ln_b, v_gm_w_s, v_gm_b_s, v_gm_w_out, v_mla_w_in, v_mla_q_norm_g, v_mla_kv_norm_g, v_mla_w_q_b, v_mla_w_kv_b, v_mla_w_out, v_ffn_w_up, v_ffn_conv_w, v_ffn_conv_b, v_ffn_w_down, v_ln_mix_g, v_ln_mix_b, v_ln_ffn_g, v_ln_ffn_b):
    S, D = x.shape[1], x.shape[2]
    xi, yi, ci = _coords()
    dev = 4 * xi + 2 * yi + ci
    c_idx = jnp.reshape(ci, (1,)).astype(jnp.int32)
    chip_idx = jnp.reshape(2 * xi + yi, (1,)).astype(jnp.int32)
    w_in_cols = mla_w_in.shape[2]
    w_in_pad = ((w_in_cols + LANES - 1) // LANES) * LANES
    n_gm, n_mla = gm_w_in.shape[0], mla_w_in.shape[0]

    cw_l = ffn_conv_w.shape[2]
    small_in = _pack([mla_q_norm_g, mla_kv_norm_g, ffn_conv_w])
    r_down = ffn_w_down.shape[1]
    r_pad = ((r_down + FFN_SHARD_MULT - 1) // FFN_SHARD_MULT) * FFN_SHARD_MULT

    def gather(shards, name):
        full = _all_gather_sc(list(shards.values()), name=name)
        return {k: f.reshape((-1, f.shape[2])) for k, f in zip(shards.keys(), full)}

    Wm, Wf = [None] * DEPTH, [None] * DEPTH
    for i in range(DEPTH):
        s = i // 2
        if i == 0:
            Wm[i] = gather(dict(w_in_t=gm_w_in[s].T.astype(BF16)), "ag_first")
            Wm[i].update(gather(dict(w_out=gm_w_out[s].astype(BF16), small=small_in), f"ag_gm_{i}_rest"))
            small_all = Wm[i].pop("small").reshape((N_DEV,) + small_in.shape)
        elif i % 2 == 0:
            Wm[i] = gather(dict(w_in_t=gm_w_in[s].T.astype(BF16), w_out=gm_w_out[s].astype(BF16)), f"ag_gm_{i}")
        else:
            Wm[i] = gather(dict(w_in=_pad_cols(mla_w_in[s], w_in_pad).astype(BF16),
                                w_qb_t=_perm_q_rows(mla_w_q_b[s].T).astype(BF16),
                                w_kvb_t=mla_w_kv_b[s].T.astype(BF16), w_out=mla_w_out[s].astype(BF16)), f"ag_mla_{i}")
        Wf[i] = gather(dict(w_up_t=_pad_pieces(ffn_w_up[i].T, 0, r_down, r_pad).astype(BF16),
                            w_down=_pad_pieces(ffn_w_down[i], 0, r_down, r_pad).astype(BF16)), f"ag_ffn_{i}")

    def small_params(anchor):
        spm, spf = [None] * DEPTH, [None] * DEPTH
        if anchor is not None:
            pack, _ = lax.optimization_barrier((small_all, anchor))
            parts = [_unpack(pack[k], [mla_q_norm_g.shape, mla_kv_norm_g.shape, ffn_conv_w.shape]) for k in range(N_DEV)]
            gq_full = jnp.concatenate([p[0] for p in parts], axis=1)
            gkv_full = jnp.concatenate([p[1] for p in parts], axis=1)
            cw_full = jnp.concatenate([p[2] for p in parts], axis=2)
        for i in range(DEPTH):
            s = i // 2
            if i % 2 == 0:
                spm[i] = dict(ln_g=gm_ln_g[s][None], ln_b=gm_ln_b[s][None], w_s=gm_w_s[s], b_st=gm_b_s[s].T)
            elif anchor is not None:
                spm[i] = dict(gq=gq_full[s][None], gkv=gkv_full[s][None])
            if anchor is not None:
                spf[i] = dict(cw=_pad_pieces(cw_full[i], 1, r_down, r_pad), cb=_pad_pieces(ffn_conv_b[i][None], 1, r_down, r_pad))
        return spm, spf

    spm, spf = small_params(None)
    ln = dict(mix_g=ln_mix_g[:, None], mix_b=ln_mix_b[:, None], ffn_g=ln_ffn_g[:, None], ffn_b=ln_ffn_b[:, None])

    small_box = {}

    def on_small(lp, gm_small, gf_small, gln):
        small_g = [
            jnp.stack([gm_small[2 * s]["ln_g"][0] for s in range(n_gm)]),
            jnp.stack([gm_small[2 * s]["ln_b"][0] for s in range(n_gm)]),
            jnp.stack([gm_small[2 * s]["w_s"] for s in range(n_gm)]),
            jnp.stack([gm_small[2 * s]["b_s"][:, :, 0] for s in range(n_gm)]),
            jnp.stack([_unpad_pieces(gf_small[i]["cb"], 1, r_down, r_pad)[0] for i in range(DEPTH)]),
            jnp.stack([gln[i]["mix_g"][0] for i in range(DEPTH)]),
            jnp.stack([gln[i]["mix_b"][0] for i in range(DEPTH)]),
            jnp.stack([gln[i]["ffn_g"][0] for i in range(DEPTH)]),
            jnp.stack([gln[i]["ffn_b"][0] for i in range(DEPTH)]),
            jnp.stack([gm_small[2 * s + 1]["gq"][0] for s in range(n_mla)]),
            jnp.stack([gm_small[2 * s + 1]["gkv"][0] for s in range(n_mla)]),
            jnp.stack([_unpad_pieces(gf_small[i]["cw"], 1, r_down, r_pad) for i in range(DEPTH)]),
        ]
        small_box["shapes"] = [g.shape for g in small_g]
        (small_box["all"],) = _all_gather_sc([_pack(small_g)], name="ag_small_grads")

    _SCHED.reducer = _Reducer(c_idx, chip_idx)
    lp, grad_x, _, _, _, _ = _fwd_bwd(x[0], loss_target[0], Wm, Wf, spm, spf, ln, on_small, small_params)
    red = _SCHED.reducer.finish()
    _SCHED.reducer = None
    rm = [{k: red[f"l{i}_{k}"] for k in Wm[i]} for i in range(DEPTH)]
    rf = [{k: red[f"l{i}_{k}"] for k in Wf[i]} for i in range(DEPTH)]

    g_gm_w_in = jnp.stack([rm[2 * s]["w_in_t"].T for s in range(n_gm)])
    g_gm_w_out = jnp.stack([rm[2 * s]["w_out"] for s in range(n_gm)])
    g_mla_w_in = jnp.stack([rm[2 * s + 1]["w_in"][:, :w_in_cols] for s in range(n_mla)])
    g_mla_w_q_b = jnp.stack([_unperm_q_rows(rm[2 * s + 1]["w_qb_t"]).T for s in range(n_mla)])
    g_mla_w_kv_b = jnp.stack([rm[2 * s + 1]["w_kvb_t"].T for s in range(n_mla)])
    g_mla_w_out = jnp.stack([rm[2 * s + 1]["w_out"] for s in range(n_mla)])
    g_ffn_w_up = jnp.stack([_unpad_pieces(rf[i]["w_up_t"], 0, r_down, r_pad).T for i in range(DEPTH)])
    g_ffn_w_down = jnp.stack([_unpad_pieces(rf[i]["w_down"], 0, r_down, r_pad) for i in range(DEPTH)])

    small_sum = _unpack(_sum8(small_box["all"], name="small_grad_sum"), small_box["shapes"])
    (g_gm_ln_g, g_gm_ln_b, g_gm_w_s, g_gm_b_s, g_ffn_conv_b, g_ln_mix_g, g_ln_mix_b, g_ln_ffn_g, g_ln_ffn_b,
     gq_all, gkv_all, cw_all) = small_sum
    (loss_all,) = _all_gather([_pack([jnp.reshape(0.5 * jnp.sum(lp) / D, (1,))])], name="ag_loss")
    loss = jnp.reshape(_sum8(loss_all, name="loss_sum")[0, 0], ())
    qn_l = mla_q_norm_g.shape[1]
    g_mla_q_norm_g = lax.dynamic_slice_in_dim(gq_all, dev * qn_l, qn_l, axis=1)
    g_mla_kv_norm_g = lax.dynamic_slice_in_dim(gkv_all, dev * qn_l, qn_l, axis=1)
    g_ffn_conv_w = lax.dynamic_slice_in_dim(cw_all, dev * cw_l, cw_l, axis=2)

    def adam_big(w, g, m, v, tag):
        sh = w.shape
        two = lambda a: a.reshape((-1, sh[-1]))
        d, mn, vn = _adamw(two(w), two(g), two(m), two(v), name=f"adamw_{tag}")
        return d.reshape(sh), mn.reshape(sh), vn.reshape(sh)

    big = [("gm_w_in", gm_w_in, g_gm_w_in, m_gm_w_in, v_gm_w_in), ("gm_w_out", gm_w_out, g_gm_w_out, m_gm_w_out, v_gm_w_out),
           ("mla_w_in", mla_w_in, g_mla_w_in, m_mla_w_in, v_mla_w_in), ("mla_w_q_b", mla_w_q_b, g_mla_w_q_b, m_mla_w_q_b, v_mla_w_q_b),
           ("mla_w_kv_b", mla_w_kv_b, g_mla_w_kv_b, m_mla_w_kv_b, v_mla_w_kv_b), ("mla_w_out", mla_w_out, g_mla_w_out, m_mla_w_out, v_mla_w_out),
           ("ffn_w_up", ffn_w_up, g_ffn_w_up, m_ffn_w_up, v_ffn_w_up), ("ffn_w_down", ffn_w_down, g_ffn_w_down, m_ffn_w_down, v_ffn_w_down)]
    res = {}
    for tag, w, g, m, v in big:
        res[tag] = (g,) + adam_big(w, g, m, v, tag)

    small = [("gm_ln_g", gm_ln_g, g_gm_ln_g, m_gm_ln_g, v_gm_ln_g), ("gm_ln_b", gm_ln_b, g_gm_ln_b, m_gm_ln_b, v_gm_ln_b),
             ("gm_w_s", gm_w_s, g_gm_w_s, m_gm_w_s, v_gm_w_s), ("gm_b_s", gm_b_s, g_gm_b_s, m_gm_b_s, v_gm_b_s),
             ("mla_q_norm_g", mla_q_norm_g, g_mla_q_norm_g, m_mla_q_norm_g, v_mla_q_norm_g),
             ("mla_kv_norm_g", mla_kv_norm_g, g_mla_kv_norm_g, m_mla_kv_norm_g, v_mla_kv_norm_g),
             ("ffn_conv_w", ffn_conv_w, g_ffn_conv_w, m_ffn_conv_w, v_ffn_conv_w), ("ffn_conv_b", ffn_conv_b, g_ffn_conv_b, m_ffn_conv_b, v_ffn_conv_b),
             ("ln_mix_g", ln_mix_g, g_ln_mix_g, m_ln_mix_g, v_ln_mix_g), ("ln_mix_b", ln_mix_b, g_ln_mix_b, m_ln_mix_b, v_ln_mix_b),
             ("ln_ffn_g", ln_ffn_g, g_ln_ffn_g, m_ln_ffn_g, v_ln_ffn_g), ("ln_ffn_b", ln_ffn_b, g_ln_ffn_b, m_ln_ffn_b, v_ln_ffn_b)]
    shapes = [t[1].shape for t in small]
    d_s, m_s, v_s = _adamw(_pack([t[1] for t in small]), _pack([t[2] for t in small]), _pack([t[3] for t in small]),
                           _pack([t[4] for t in small]), name="adamw_small")
    d_l, m_l, v_l = _unpack(d_s, shapes), _unpack(m_s, shapes), _unpack(v_s, shapes)
    for (tag, _, g, _, _), d, mn, vn in zip(small, d_l, m_l, v_l):
        res[tag] = (g, d, mn, vn)

    order = ["gm_w_in", "gm_ln_g", "gm_ln_b", "gm_w_s", "gm_b_s", "gm_w_out", "mla_w_in", "mla_q_norm_g", "mla_kv_norm_g",
             "mla_w_q_b", "mla_w_kv_b", "mla_w_out", "ffn_w_up", "ffn_conv_w", "ffn_conv_b", "ffn_w_down",
             "ln_mix_g", "ln_mix_b", "ln_ffn_g", "ln_ffn_b"]
    out = [loss, grad_x[None]]
    for q in range(4):
        out += [res[k][q] for k in order]
    return tuple(out)
```
